```python
import jax, jax.numpy as jnp
from jax import lax
import numpy as np

D_MODEL = 1024
BATCH = 8
SEQ = 4096
DEPTH = 4

GRID_W = 64
CTX_LEN = 256
N_MIXERS = 3
CHUNK = 128
A_WIDTH = 2 * D_MODEL
A_GROUPS = 8
B_GROUPS = 4
HEAD_DIM = 128
N_HEADS = D_MODEL // HEAD_DIM
N_KV_HEADS = 2
Q_BLOCK = 128
ROPE_THETA = 10000.0
D_FF = 4 * D_MODEL
NORM_EPS = 1e-6
LN_EPS = 1e-5
N_A = len(range(0, DEPTH, N_MIXERS))
N_B = len(range(1, DEPTH, N_MIXERS))
N_C = len(range(2, DEPTH, N_MIXERS))

kernel_name = "hybrid_gmlp_fnet_gqa_prefix_dit"


def rms_norm(x, g):
    xf = x.astype(jnp.float32)
    y = xf * lax.rsqrt(jnp.mean(xf * xf, axis=-1, keepdims=True) + NORM_EPS)
    return (y * g.astype(jnp.float32)).astype(x.dtype)


def layer_norm(x, g):
    xf = x.astype(jnp.float32)
    mu = jnp.mean(xf, axis=-1, keepdims=True)
    var = jnp.mean(jnp.square(xf - mu), axis=-1, keepdims=True)
    return ((xf - mu) * lax.rsqrt(var + LN_EPS) * g.astype(jnp.float32)).astype(x.dtype)


def ada_mod(cond, w, b):
    return jnp.split(jax.nn.silu(cond) @ w + b, 6, axis=-1)


def modulate(h, shift, scale):
    return h * (1.0 + scale) + shift


def chunk_gmlp(h, w_in, ln_g, w_s, b_s, w_out):
    bsz, length, _ = h.shape
    u, v = jnp.split(jax.nn.gelu(h @ w_in), 2, axis=-1)
    v = layer_norm(v, ln_g)
    v = v.reshape(bsz, length // CHUNK, CHUNK, A_GROUPS, A_WIDTH // A_GROUPS)
    sv = jnp.einsum('gpq,bnqgc->bnpgc', w_s, v) + b_s.T[:, :, None]
    return (u * sv.reshape(bsz, length, A_WIDTH)) @ w_out


def fourier_mix(h, w_out):
    bsz, length, d = h.shape
    hg = h.astype(jnp.float32).reshape(bsz, length, B_GROUPS, d // B_GROUPS)
    y = jnp.fft.fftn(hg, axes=(1, 3), norm="ortho").real
    return y.reshape(bsz, length, d).astype(h.dtype) @ w_out


def axial_rope(rows):
    t = jnp.arange(rows * GRID_W)
    row = (t // GRID_W).astype(jnp.float32)
    col = (t % GRID_W).astype(jnp.float32)
    n_freq = HEAD_DIM // 4
    inv = ROPE_THETA ** (-jnp.arange(n_freq, dtype=jnp.float32) / n_freq)
    ang = jnp.concatenate([row[:, None] * inv, col[:, None] * inv], axis=-1)
    return jnp.cos(ang), jnp.sin(ang)


def apply_rope(x, cos, sin):
    x1, x2 = jnp.split(x, 2, axis=-1)
    cs = cos[None, :, None, :].astype(x.dtype)
    sn = sin[None, :, None, :].astype(x.dtype)
    return jnp.concatenate([x1 * cs - x2 * sn, x1 * sn + x2 * cs], axis=-1)


def attend(q, keys, vals):
    bsz, lq = q.shape[:2]
    qg = q.reshape(bsz, lq, N_KV_HEADS, N_HEADS // N_KV_HEADS, HEAD_DIM)
    s = jnp.einsum('bqkrd,bskd->bkrqs', qg, keys).astype(jnp.float32) * (HEAD_DIM ** -0.5)
    p = jax.nn.softmax(s, axis=-1).astype(vals.dtype)
    o = jnp.einsum('bkrqs,bskd->bqkrd', p, vals)
    return o.reshape(bsz, lq, N_HEADS * HEAD_DIM)


def gqa_attention(h, hc, w_qkv, q_g, k_g, w_o, cos, sin, need_ctx_out):
    hq = N_HEADS * HEAD_DIM
    hk = N_KV_HEADS * HEAD_DIM
    w_q, w_k, w_v = w_qkv[:, :hq], w_qkv[:, hq:hq + hk], w_qkv[:, hq + hk:]

    def q_proj(z):
        bsz, length, _ = z.shape
        return rms_norm((z @ w_q).reshape(bsz, length, N_HEADS, HEAD_DIM), q_g)

    def kv_proj(z):
        bsz, length, _ = z.shape
        k = rms_norm((z @ w_k).reshape(bsz, length, N_KV_HEADS, HEAD_DIM), k_g)
        v = (z @ w_v).reshape(bsz, length, N_KV_HEADS, HEAD_DIM)
        return k, v

    q = apply_rope(q_proj(h), cos, sin)
    k, v = kv_proj(h)
    k = apply_rope(k, cos, sin)
    kc, vc = kv_proj(hc)
    keys = jnp.concatenate([kc, k], axis=1)
    vals = jnp.concatenate([vc, v], axis=1)

    bsz, length = q.shape[:2]
    nb = length // Q_BLOCK
    qb = jnp.moveaxis(q.reshape(bsz, nb, Q_BLOCK, N_HEADS, HEAD_DIM), 1, 0)
    ob = lax.map(lambda blk: attend(blk, keys, vals), qb)
    y = jnp.moveaxis(ob, 0, 1).reshape(bsz, length, hq) @ w_o

    yc = attend(q_proj(hc), kc, vc) @ w_o if need_ctx_out else None
    return y, yc


def channel_mlp(h, w1, w2):
    return jnp.square(jax.nn.relu(h @ w1)) @ w2


def setup_inputs(seed: int = 0) -> dict:
    key = jax.random.key(seed)
    ks = jax.random.split(key, 20)
    f32 = jnp.float32
    D = D_MODEL

    def nrm(k, shape, scale):
        return jax.random.normal(k, shape, f32) * scale

    return {
        "x": nrm(ks[0], (BATCH, SEQ, D), 1.0),
        "c": nrm(ks[1], (BATCH, D), 1.0),
        "ctx": nrm(ks[2], (BATCH, CTX_LEN, D), 1.0),
        "c_ctx": nrm(ks[3], (D,), 1.0),
        "ada_w": nrm(ks[4], (DEPTH, D, 6 * D), 0.5 * D ** -0.5),
        "ada_b": nrm(ks[5], (DEPTH, 6 * D), 0.02),
        "norm_g": 1.0 + nrm(ks[6], (DEPTH, 4, D), 0.02),
        "mlp_w1": nrm(ks[7], (DEPTH, D, D_FF), D ** -0.5),
        "mlp_w2": nrm(ks[8], (DEPTH, D_FF, D), D_FF ** -0.5),
        "a_w_in": nrm(ks[9], (N_A, D, 2 * A_WIDTH), D ** -0.5),
        "a_ln_g": 1.0 + nrm(ks[10], (N_A, A_WIDTH), 0.02),
        "a_w_s": nrm(ks[11], (N_A, A_GROUPS, CHUNK, CHUNK), CHUNK ** -0.5),
        "a_b_s": 1.0 + nrm(ks[12], (N_A, A_GROUPS, CHUNK), 0.02),
        "a_w_out": nrm(ks[13], (N_A, A_WIDTH, D), A_WIDTH ** -0.5),
        "b_w_out": nrm(ks[14], (N_B, D, D), D ** -0.5),
        "c_w_qkv": nrm(ks[15], (N_C, D, (N_HEADS + 2 * N_KV_HEADS) * HEAD_DIM), D ** -0.5),
        "c_q_g": 1.0 + nrm(ks[16], (N_C, HEAD_DIM), 0.02),
        "c_k_g": 1.0 + nrm(ks[17], (N_C, HEAD_DIM), 0.02),
        "c_w_o": nrm(ks[18], (N_C, N_HEADS * HEAD_DIM, D), (N_HEADS * HEAD_DIM) ** -0.5),
    }


def reference(x, c, ctx, c_ctx, ada_w, ada_b, norm_g, mlp_w1, mlp_w2, a_w_in, a_ln_g, a_w_s, a_b_s, a_w_out,
              b_w_out, c_w_qkv, c_q_g, c_k_g, c_w_o):
    rows = x.shape[1] // GRID_W
    cos, sin = axial_rope(rows)
    attn_layers = [i for i in range(DEPTH) if i % N_MIXERS == 2]
    last_ctx_read = attn_layers[-1] if attn_layers else -1

    for i in range(DEPTH):
        kind, j = i % N_MIXERS, i // N_MIXERS
        ctx_in = i <= last_ctx_read
        ctx_out = i < last_ctx_read

        sh1, sc1, g1, sh2, sc2, g2 = [m[:, None, :] for m in ada_mod(c, ada_w[i], ada_b[i])]
        h = modulate(rms_norm(x, norm_g[i, 0]), sh1, sc1)
        hc = None
        if ctx_in:
            csh1, csc1, cg1, csh2, csc2, cg2 = ada_mod(c_ctx, ada_w[i], ada_b[i])
            hc = modulate(rms_norm(ctx, norm_g[i, 0]), csh1, csc1)

        if kind == 0:
            y = chunk_gmlp(h, a_w_in[j], a_ln_g[j], a_w_s[j], a_b_s[j], a_w_out[j])
            yc = chunk_gmlp(hc, a_w_in[j], a_ln_g[j], a_w_s[j], a_b_s[j], a_w_out[j]) if ctx_out else None
        elif kind == 1:
            y = fourier_mix(h, b_w_out[j])
            yc = fourier_mix(hc, b_w_out[j]) if ctx_out else None
        else:
            y, yc = gqa_attention(h, hc, c_w_qkv[j], c_q_g[j], c_k_g[j], c_w_o[j], cos, sin, ctx_out)

        x = x + g1 * rms_norm(y, norm_g[i, 1])
        hm = modulate(rms_norm(x, norm_g[i, 2]), sh2, sc2)
        x = x + g2 * rms_norm(channel_mlp(hm, mlp_w1[i], mlp_w2[i]), norm_g[i, 3])

        if ctx_out:
            ctx = ctx + cg1 * rms_norm(yc, norm_g[i, 1])
            hcm = modulate(rms_norm(ctx, norm_g[i, 2]), csh2, csc2)
            ctx = ctx + cg2 * rms_norm(channel_mlp(hcm, mlp_w1[i], mlp_w2[i]), norm_g[i, 3])

    return x
```

```python
import functools

import jax
import jax.numpy as jnp
from jax import lax
from jax.experimental import pallas as pl
from jax.experimental.pallas import tpu as pltpu

D_MODEL = 1024
DEPTH = 4
GRID_W = 64
N_MIXERS = 3
CHUNK = 128
A_WIDTH = 2 * D_MODEL
A_GROUPS = 8
A_GROUP_W = A_WIDTH // A_GROUPS
B_GROUPS = 4
B_GROUP_W = D_MODEL // B_GROUPS
HEAD_DIM = 128
N_HEADS = D_MODEL // HEAD_DIM
N_KV_HEADS = 2
HEADS_PER_KV = N_HEADS // N_KV_HEADS
ROPE_THETA = 10000.0
D_FF = 4 * D_MODEL
NORM_EPS = 1e-6
LN_EPS = 1e-5

F32 = jnp.float32
BF16 = jnp.bfloat16

MXU_COLS = 256
ROW_BLOCK = 64
COND_ROWS = 16
MIB = 1024 * 1024


def _dot(a, b):
    return jnp.dot(a, b, preferred_element_type=F32)


def _rms(x, g):
    ms = jnp.mean(x * x, axis=-1, keepdims=True)
    return x * lax.rsqrt(ms + NORM_EPS) * g


def _gelu_tanh(x):
    c = 0.7978845608028654
    return 0.5 * x * (1.0 + jnp.tanh(c * (x + 0.044715 * (x * x * x))))


def _const_spec(shape):
    zeros = (0,) * len(shape)
    return pl.BlockSpec(shape, lambda b, t: zeros, pipeline_mode=pl.Buffered(1))


def _mod_spec(is_ctx, batch):
    if is_ctx:
        return pl.BlockSpec((None, 6, D_MODEL), lambda b, t: (batch, 0, 0))
    return pl.BlockSpec((None, 6, D_MODEL), lambda b, t: (b, 0, 0))


def _params(vmem_mib):
    return pltpu.CompilerParams(
        dimension_semantics=("arbitrary", "arbitrary"),
        vmem_limit_bytes=vmem_mib * MIB)


def _modulated_norm_to(h_s, x_ref, mod_ref, ng_ref):
    sh1 = mod_ref[0:1, :]
    sc1p = 1.0 + mod_ref[1:2, :]
    g0 = ng_ref[0:1, :]
    for r in range(0, x_ref.shape[0], ROW_BLOCK):
        rows = slice(r, r + ROW_BLOCK)
        h_s[rows, :] = (_rms(x_ref[rows, :], g0) * sc1p + sh1).astype(BF16)


def _ada_kernel(cond_ref, w_ref, b_ref, o_ref):
    a = cond_ref[...]
    a = a * jax.nn.sigmoid(a)
    o_ref[...] = _dot(a.astype(BF16), w_ref[...].astype(BF16)) + b_ref[...]


def _ada_all(cond, ada_w, ada_b):
    tn = 1536
    n_out = 6 * D_MODEL
    return pl.pallas_call(
        _ada_kernel,
        grid=(DEPTH, n_out // tn),
        in_specs=[
            pl.BlockSpec((COND_ROWS, D_MODEL), lambda i, n: (0, 0)),
            pl.BlockSpec((None, D_MODEL, tn), lambda i, n: (i, 0, n)),
            pl.BlockSpec((None, 1, tn), lambda i, n: (i, 0, n)),
        ],
        out_specs=pl.BlockSpec((None, COND_ROWS, tn), lambda i, n: (i, 0, n)),
        out_shape=jax.ShapeDtypeStruct((DEPTH, COND_ROWS, n_out), F32),
        compiler_params=_params(32),
        name="ada_mod",
    )(cond, ada_w, ada_b.reshape(DEPTH, 1, n_out))


def _gmlp_head_kernel(x_ref, mod_ref, ng_ref, win_ref, lng_ref, ws_ref, bs_ref, z_ref,
                      h_s, u_s, v_s, vb_s):
    tm = x_ref.shape[0]
    _modulated_norm_to(h_s, x_ref, mod_ref, ng_ref)
    for j in range(A_WIDTH // MXU_COLS):
        cols = slice(j * MXU_COLS, (j + 1) * MXU_COLS)
        u_s[:, cols] = _gelu_tanh(_dot(h_s[...], win_ref[:, cols]))
    for j in range(A_WIDTH // MXU_COLS):
        cols = slice(j * MXU_COLS, (j + 1) * MXU_COLS)
        wcols = slice(A_WIDTH + j * MXU_COLS, A_WIDTH + (j + 1) * MXU_COLS)
        v_s[:, cols] = _gelu_tanh(_dot(h_s[...], win_ref[:, wcols]))
    lng = lng_ref[...]
    for r in range(0, tm, ROW_BLOCK // 2):
        rows = slice(r, r + ROW_BLOCK // 2)
        v = v_s[rows, :]
        d = v - jnp.mean(v, axis=-1, keepdims=True)
        var = jnp.mean(d * d, axis=-1, keepdims=True)
        vb_s[rows, :] = (d * lax.rsqrt(var + LN_EPS) * lng).astype(BF16)
    for g in range(A_GROUPS):
        w = ws_ref[g]
        b = bs_ref[g]
        cols = slice(g * A_GROUP_W, (g + 1) * A_GROUP_W)
        for n in range(tm // CHUNK):
            rows = slice(n * CHUNK, (n + 1) * CHUNK)
            sv = _dot(w, vb_s[rows, cols]) + b
            z_ref[rows, cols] = (u_s[rows, cols] * sv).astype(BF16)


def _gmlp_head(x, mods, ng, w_in, ln_g, w_s, b_s, is_ctx):
    bsz, length, _ = x.shape
    tm = min(512, length)
    return pl.pallas_call(
        _gmlp_head_kernel,
        grid=(bsz, length // tm),
        in_specs=[
            pl.BlockSpec((None, tm, D_MODEL), lambda b, t: (b, t, 0)),
            _mod_spec(is_ctx, bsz),
            _const_spec((4, D_MODEL)),
            _const_spec((D_MODEL, 2 * A_WIDTH)),
            _const_spec((1, A_WIDTH)),
            _const_spec((A_GROUPS, CHUNK, CHUNK)),
            _const_spec((A_GROUPS, CHUNK, 1)),
        ],
        out_specs=pl.BlockSpec((None, tm, A_WIDTH), lambda b, t: (b, t, 0)),
        out_shape=jax.ShapeDtypeStruct((bsz, length, A_WIDTH), BF16),
        scratch_shapes=[
            pltpu.VMEM((tm, D_MODEL), BF16),
            pltpu.VMEM((tm, A_WIDTH), F32),
            pltpu.VMEM((tm, A_WIDTH), F32),
            pltpu.VMEM((tm, A_WIDTH), BF16),
        ],
        compiler_params=_params(48),
        name="gmlp_head",
    )(x, mods, ng, w_in, ln_g, w_s, b_s)


def _tail_kernel(x_ref, z_ref, mod_ref, ng_ref, wp_ref, w1_ref, w2_ref, o_ref,
                 y_s, hm_s, hid_s):
    tm = x_ref.shape[0]
    for n in range(D_MODEL // MXU_COLS):
        cols = slice(n * MXU_COLS, (n + 1) * MXU_COLS)
        y_s[:, cols] = _dot(z_ref[...], wp_ref[:, cols])
    g1 = mod_ref[2:3, :]
    sh2 = mod_ref[3:4, :]
    sc2p = 1.0 + mod_ref[4:5, :]
    g2 = mod_ref[5:6, :]
    for r in range(0, tm, ROW_BLOCK):
        rows = slice(r, r + ROW_BLOCK)
        x1 = x_ref[rows, :] + g1 * _rms(y_s[rows, :], ng_ref[1:2, :])
        o_ref[rows, :] = x1
        hm_s[rows, :] = (_rms(x1, ng_ref[2:3, :]) * sc2p + sh2).astype(BF16)
    ff_cols = 2 * MXU_COLS
    for c in range(D_FF // ff_cols):
        cols = slice(c * ff_cols, (c + 1) * ff_cols)
        t = jnp.maximum(_dot(hm_s[...], w1_ref[:, cols]), 0.0)
        hid_s[:, cols] = (t * t).astype(BF16)
    for n in range(D_MODEL // MXU_COLS):
        cols = slice(n * MXU_COLS, (n + 1) * MXU_COLS)
        y_s[:, cols] = _dot(hid_s[...], w2_ref[:, cols])
    for r in range(0, tm, ROW_BLOCK):
        rows = slice(r, r + ROW_BLOCK)
        o_ref[rows, :] = o_ref[rows, :] + g2 * _rms(y_s[rows, :], ng_ref[3:4, :])


def _tail(x, z, mods, ng, w_post, w1, w2, is_ctx):
    bsz, length, _ = x.shape
    kz = z.shape[-1]
    tm = min(512, length)
    return pl.pallas_call(
        _tail_kernel,
        grid=(bsz, length // tm),
        in_specs=[
            pl.BlockSpec((None, tm, D_MODEL), lambda b, t: (b, t, 0)),
            pl.BlockSpec((None, tm, kz), lambda b, t: (b, t, 0)),
            _mod_spec(is_ctx, bsz),
            _const_spec((4, D_MODEL)),
            _const_spec((kz, D_MODEL)),
            _const_spec((D_MODEL, D_FF)),
            _const_spec((D_FF, D_MODEL)),
        ],
        out_specs=pl.BlockSpec((None, tm, D_MODEL), lambda b, t: (b, t, 0)),
        out_shape=jax.ShapeDtypeStruct((bsz, length, D_MODEL), F32),
        scratch_shapes=[
            pltpu.VMEM((tm, D_MODEL), F32),
            pltpu.VMEM((tm, D_MODEL), BF16),
            pltpu.VMEM((tm, D_FF), BF16),
        ],
        compiler_params=_params(54),
        name="tail",
    )(x, z, mods, ng, w_post, w1, w2)


def _fourier_chan_kernel(x_ref, mod_ref, ng_ref, cs_ref, ab_ref, h_s):
    _modulated_norm_to(h_s, x_ref, mod_ref, ng_ref)
    for g in range(B_GROUPS):
        cols = slice(g * B_GROUP_W, (g + 1) * B_GROUP_W)
        t = _dot(h_s[:, cols], cs_ref[...])
        ab_ref[0, :, cols] = t[:, :B_GROUP_W].astype(BF16)
        ab_ref[1, :, cols] = t[:, B_GROUP_W:].astype(BF16)


def _fourier_seq_kernel(cs_ref, ab_ref, y_ref):
    for n in range(D_MODEL // MXU_COLS):
        cols = slice(n * MXU_COLS, (n + 1) * MXU_COLS)
        y_ref[:, cols] = _dot(cs_ref[...], ab_ref[:, cols]).astype(BF16)


def _dft_tables(length):
    m = jnp.arange(B_GROUP_W, dtype=jnp.int32)
    ang_c = (2.0 * jnp.pi / B_GROUP_W) * ((m[:, None] * m[None, :]) % B_GROUP_W).astype(F32)
    s_c = B_GROUP_W ** -0.5
    cs_chan = jnp.concatenate([jnp.cos(ang_c), jnp.sin(ang_c)], axis=1) * s_c
    k = jnp.arange(length, dtype=jnp.int32)
    ang_l = (2.0 * jnp.pi / length) * ((k[:, None] * k[None, :]) % length).astype(F32)
    s_l = length ** -0.5
    cs_seq = jnp.concatenate([jnp.cos(ang_l), -jnp.sin(ang_l)], axis=1) * s_l
    return cs_chan.astype(BF16), cs_seq.astype(BF16)


def _fourier_head(x, mods, ng, is_ctx):
    bsz, length, _ = x.shape
    tm = min(512, length)
    cs_chan, cs_seq = _dft_tables(length)
    ab = pl.pallas_call(
        _fourier_chan_kernel,
        grid=(bsz, length // tm),
        in_specs=[
            pl.BlockSpec((None, tm, D_MODEL), lambda b, t: (b, t, 0)),
            _mod_spec(is_ctx, bsz),
            _const_spec((4, D_MODEL)),
            _const_spec((B_GROUP_W, 2 * B_GROUP_W)),
        ],
        out_specs=pl.BlockSpec((None, 2, tm, D_MODEL), lambda b, t: (b, 0, t, 0)),
        out_shape=jax.ShapeDtypeStruct((bsz, 2, length, D_MODEL), BF16),
        scratch_shapes=[pltpu.VMEM((tm, D_MODEL), BF16)],
        compiler_params=_params(32),
        name="fourier_chan",
    )(x, mods, ng, cs_chan)
    ab = ab.reshape(bsz, 2 * length, D_MODEL)
    return pl.pallas_call(
        _fourier_seq_kernel,
        grid=(bsz, length // tm),
        in_specs=[
            pl.BlockSpec((tm, 2 * length), lambda b, t: (t, 0)),
            pl.BlockSpec((None, 2 * length, D_MODEL), lambda b, t: (b, 0, 0),
                         pipeline_mode=pl.Buffered(1)),
        ],
        out_specs=pl.BlockSpec((None, tm, D_MODEL), lambda b, t: (b, t, 0)),
        out_shape=jax.ShapeDtypeStruct((bsz, length, D_MODEL), BF16),
        compiler_params=_params(48),
        name="fourier_seq",
    )(cs_seq, ab)


def _head_rms(r, g):
    return r * lax.rsqrt(jnp.mean(r * r, axis=-1, keepdims=True) + NORM_EPS) * g


def _qkv_kernel(x_ref, mod_ref, ng_ref, w_ref, qg_ref, kg_ref, cos_ref, sin_ref,
                q_ref, k_ref, v_ref, h_s):
    _modulated_norm_to(h_s, x_ref, mod_ref, ng_ref)
    cosf = cos_ref[...]
    sinf = sin_ref[...]

    def rope(r):
        return r * cosf + pltpu.roll(r, HEAD_DIM // 2, 1) * sinf

    qg = qg_ref[...] * (HEAD_DIM ** -0.5)
    for p in range(N_HEADS // 2):
        t = _dot(h_s[...], w_ref[:, p * MXU_COLS:(p + 1) * MXU_COLS])
        for e in range(2):
            q_ref[2 * p + e] = rope(_head_rms(t[:, e * HEAD_DIM:(e + 1) * HEAD_DIM], qg)).astype(BF16)
    kcol = N_HEADS * HEAD_DIM
    t = _dot(h_s[...], w_ref[:, kcol:kcol + MXU_COLS])
    for e in range(N_KV_HEADS):
        k_ref[e] = rope(_head_rms(t[:, e * HEAD_DIM:(e + 1) * HEAD_DIM], kg_ref[...])).astype(BF16)
    t = _dot(h_s[...], w_ref[:, kcol + MXU_COLS:kcol + 2 * MXU_COLS])
    for e in range(N_KV_HEADS):
        v_ref[e] = t[:, e * HEAD_DIM:(e + 1) * HEAD_DIM].astype(BF16)


def _kv_ctx_kernel(x_ref, mod_ref, ng_ref, w_ref, kg_ref, k_ref, v_ref, h_s):
    _modulated_norm_to(h_s, x_ref, mod_ref, ng_ref)
    t = _dot(h_s[...], w_ref[:, 0:MXU_COLS])
    for e in range(N_KV_HEADS):
        k_ref[e] = _head_rms(t[:, e * HEAD_DIM:(e + 1) * HEAD_DIM], kg_ref[...]).astype(BF16)
    t = _dot(h_s[...], w_ref[:, MXU_COLS:2 * MXU_COLS])
    for e in range(N_KV_HEADS):
        v_ref[e] = t[:, e * HEAD_DIM:(e + 1) * HEAD_DIM].astype(BF16)


def _rope_tables(length):
    t = jnp.arange(length)
    row = (t // GRID_W).astype(F32)
    col = (t % GRID_W).astype(F32)
    n_freq = HEAD_DIM // 4
    inv = ROPE_THETA ** (-jnp.arange(n_freq, dtype=F32) / n_freq)
    ang = jnp.concatenate([row[:, None] * inv, col[:, None] * inv], axis=-1)
    cos, sin = jnp.cos(ang), jnp.sin(ang)
    return jnp.concatenate([cos, cos], axis=-1), jnp.concatenate([-sin, sin], axis=-1)


def _qkv_latent(x, mods, ng, w_qkv, q_g, k_g):
    bsz, length, _ = x.shape
    tm = 512
    cosf, sinf = _rope_tables(length)
    n_qkv = w_qkv.shape[-1]
    kv_shape = jax.ShapeDtypeStruct((bsz, N_KV_HEADS, length, HEAD_DIM), BF16)
    kv_spec = pl.BlockSpec((None, N_KV_HEADS, tm, HEAD_DIM), lambda b, t: (b, 0, t, 0))
    return pl.pallas_call(
        _qkv_kernel,
        grid=(bsz, length // tm),
        in_specs=[
            pl.BlockSpec((None, tm, D_MODEL), lambda b, t: (b, t, 0)),
            _mod_spec(False, bsz),
            _const_spec((4, D_MODEL)),
            _const_spec((D_MODEL, n_qkv)),
            _const_spec((1, HEAD_DIM)),
            _const_spec((1, HEAD_DIM)),
            pl.BlockSpec((tm, HEAD_DIM), lambda b, t: (t, 0)),
            pl.BlockSpec((tm, HEAD_DIM), lambda b, t: (t, 0)),
        ],
        out_specs=[
            pl.BlockSpec((None, N_HEADS, tm, HEAD_DIM), lambda b, t: (b, 0, t, 0)),
            kv_spec, kv_spec,
        ],
        out_shape=[
            jax.ShapeDtypeStruct((bsz, N_HEADS, length, HEAD_DIM), BF16),
            kv_shape, kv_shape,
        ],
        scratch_shapes=[pltpu.VMEM((tm, D_MODEL), BF16)],
        compiler_params=_params(32),
        name="qkv_latent",
    )(x, mods, ng, w_qkv, q_g, k_g, cosf, sinf)


def _kv_ctx(ctx, mods, ng, w_kv, k_g):
    bsz, length, _ = ctx.shape
    tm = length
    kv_shape = jax.ShapeDtypeStruct((bsz, N_KV_HEADS, length, HEAD_DIM), BF16)
    kv_spec = pl.BlockSpec((None, N_KV_HEADS, tm, HEAD_DIM), lambda b, t: (b, 0, t, 0))
    return pl.pallas_call(
        _kv_ctx_kernel,
        grid=(bsz, length // tm),
        in_specs=[
            pl.BlockSpec((None, tm, D_MODEL), lambda b, t: (b, t, 0)),
            _mod_spec(True, bsz),
            _const_spec((4, D_MODEL)),
            _const_spec((D_MODEL, 2 * MXU_COLS)),
            _const_spec((1, HEAD_DIM)),
        ],
        out_specs=[kv_spec, kv_spec],
        out_shape=[kv_shape, kv_shape],
        scratch_shapes=[pltpu.VMEM((tm, D_MODEL), BF16)],
        compiler_params=_params(32),
        name="kv_ctx",
    )(ctx, mods, ng, w_kv, k_g)


def _key_chunks(n_keys, width):
    chunks = [(c, width) for c in range(0, n_keys - width + 1, width)]
    done = len(chunks) * width
    if done < n_keys:
        chunks.append((done, n_keys - done))
    return chunks


def _attn_kernel(q_ref, kt_ref, v_ref, o_ref, q4_s, s_s, p_s):
    tq = q_ref.shape[1]
    n_keys = kt_ref.shape[-1]
    chunks = _key_chunks(n_keys, 2 * MXU_COLS)
    for j in range(N_KV_HEADS):
        for h in range(HEADS_PER_KV):
            q4_s[h * tq:(h + 1) * tq, :] = q_ref[HEADS_PER_KV * j + h]
        m = None
        for c0, cw in chunks:
            s = _dot(q4_s[...], kt_ref[j, :, c0:c0 + cw])
            s_s[:, c0:c0 + cw] = s
            cm = jnp.max(s, axis=-1, keepdims=True)
            m = cm if m is None else jnp.maximum(m, cm)
        l = None
        for c0, cw in chunks:
            p = jnp.exp(s_s[:, c0:c0 + cw] - m)
            cl = jnp.sum(p, axis=-1, keepdims=True)
            l = cl if l is None else l + cl
            p_s[:, c0:c0 + cw] = p.astype(BF16)
        o = _dot(p_s[...], v_ref[j]) / l
        for h in range(HEADS_PER_KV):
            head = HEADS_PER_KV * j + h
            o_ref[:, head * HEAD_DIM:(head + 1) * HEAD_DIM] = o[h * tq:(h + 1) * tq, :].astype(BF16)


def _attention(q, kt, v):
    bsz, _, length, _ = q.shape
    n_keys = kt.shape[-1]
    tq = 128
    return pl.pallas_call(
        _attn_kernel,
        grid=(bsz, length // tq),
        in_specs=[
            pl.BlockSpec((None, N_HEADS, tq, HEAD_DIM), lambda b, t: (b, 0, t, 0)),
            pl.BlockSpec((None, N_KV_HEADS, HEAD_DIM, n_keys), lambda b, t: (b, 0, 0, 0)),
            pl.BlockSpec((None, N_KV_HEADS, n_keys, HEAD_DIM), lambda b, t: (b, 0, 0, 0)),
        ],
        out_specs=pl.BlockSpec((None, tq, D_MODEL), lambda b, t: (b, t, 0)),
        out_shape=jax.ShapeDtypeStruct((bsz, length, D_MODEL), BF16),
        scratch_shapes=[
            pltpu.VMEM((HEADS_PER_KV * tq, HEAD_DIM), BF16),
            pltpu.VMEM((HEADS_PER_KV * tq, n_keys), F32),
            pltpu.VMEM((HEADS_PER_KV * tq, n_keys), BF16),
        ],
        compiler_params=_params(48),
        name="attention",
    )(q, kt, v)


def kernel(x, c, ctx, c_ctx, ada_w, ada_b, norm_g, mlp_w1, mlp_w2, a_w_in, a_ln_g, a_w_s, a_b_s, a_w_out,
           b_w_out, c_w_qkv, c_q_g, c_k_g, c_w_o):
    bsz = x.shape[0]
    attn_layers = [i for i in range(DEPTH) if i % N_MIXERS == 2]
    last_ctx_read = attn_layers[-1] if attn_layers else -1

    cond = jnp.zeros((COND_ROWS, D_MODEL), F32).at[:bsz].set(c).at[bsz].set(c_ctx)
    mods_all = _ada_all(cond, ada_w, ada_b).reshape(DEPTH, COND_ROWS, 6, D_MODEL)

    w1 = mlp_w1.astype(BF16)
    w2 = mlp_w2.astype(BF16)

    for i in range(DEPTH):
        kind, j = i % N_MIXERS, i // N_MIXERS
        ctx_in = i <= last_ctx_read
        ctx_out = i < last_ctx_read
        mods = mods_all[i]
        ng = norm_g[i]

        if kind == 0:
            w_in = a_w_in[j].astype(BF16)
            ln_g = a_ln_g[j].reshape(1, A_WIDTH)
            w_s = a_w_s[j].astype(BF16)
            b_s = a_b_s[j].reshape(A_GROUPS, CHUNK, 1)
            w_post = a_w_out[j].astype(BF16)
            z = _gmlp_head(x, mods, ng, w_in, ln_g, w_s, b_s, False)
            zc = _gmlp_head(ctx, mods, ng, w_in, ln_g, w_s, b_s, True) if ctx_out else None
        elif kind == 1:
            w_post = b_w_out[j].astype(BF16)
            z = _fourier_head(x, mods, ng, False)
            zc = _fourier_head(ctx, mods, ng, True) if ctx_out else None
        else:
            w_qkv = c_w_qkv[j].astype(BF16)
            w_post = c_w_o[j].astype(BF16)
            q_g = c_q_g[j].reshape(1, HEAD_DIM)
            k_g = c_k_g[j].reshape(1, HEAD_DIM)
            q, k, v = _qkv_latent(x, mods, ng, w_qkv, q_g, k_g)
            if ctx_in:
                kc, vc = _kv_ctx(ctx, mods, ng, w_qkv[:, N_HEADS * HEAD_DIM:], k_g)
                k = jnp.concatenate([kc, k], axis=2)
                v = jnp.concatenate([vc, v], axis=2)
            z = _attention(q, jnp.swapaxes(k, 2, 3), v)
            zc = None
            assert not ctx_out

        x = _tail(x, z, mods, ng, w_post, w1[i], w2[i], False)
        if ctx_out:
            ctx = _tail(ctx, zc, mods, ng, w_post, w1[i], w2[i], True)

    return x
```

```python
import functools

import jax
import jax.numpy as jnp
from jax import lax
from jax.experimental import pallas as pl
from jax.experimental.pallas import tpu as pltpu

D_MODEL = 1024
DEPTH = 4
GRID_W = 64
N_MIXERS = 3
CHUNK = 128
A_WIDTH = 2 * D_MODEL
A_GROUPS = 8
A_GROUP_W = A_WIDTH // A_GROUPS
B_GROUPS = 4
B_GROUP_W = D_MODEL // B_GROUPS
HEAD_DIM = 128
N_HEADS = D_MODEL // HEAD_DIM
N_KV_HEADS = 2
HEADS_PER_KV = N_HEADS // N_KV_HEADS
ROPE_THETA = 10000.0
D_FF = 4 * D_MODEL
NORM_EPS = 1e-6
LN_EPS = 1e-5
LOG2_E = 1.4426950408889634

F32 = jnp.float32
BF16 = jnp.bfloat16

MXU_COLS = 256
ROW_BLOCK = 64
BF16_ROWS = 16
COND_ROWS = 16
MIB = 1024 * 1024


def _dot(a, b):
    return jnp.dot(a, b, preferred_element_type=F32)


def _rms(x, g):
    ms = jnp.mean(x * x, axis=-1, keepdims=True)
    return x * lax.rsqrt(ms + NORM_EPS) * g


def _gelu_tanh(x):
    c = 0.7978845608028654
    return 0.5 * x * (1.0 + jnp.tanh(c * (x + 0.044715 * (x * x * x))))


def _const_spec(shape):
    zeros = (0,) * len(shape)
    return pl.BlockSpec(shape, lambda b, t: zeros, pipeline_mode=pl.Buffered(1))


def _mod_spec(is_ctx, batch):
    if is_ctx:
        return pl.BlockSpec((None, 6, D_MODEL), lambda b, t: (batch, 0, 0))
    return pl.BlockSpec((None, 6, D_MODEL), lambda b, t: (b, 0, 0))


def _params(vmem_mib):
    return pltpu.CompilerParams(
        dimension_semantics=("arbitrary", "arbitrary"),
        vmem_limit_bytes=vmem_mib * MIB)


def _modulated_norm_to(h_s, x_ref, mod_ref, ng_ref):
    sh1 = mod_ref[0:1, :]
    sc1p = 1.0 + mod_ref[1:2, :]
    g0 = ng_ref[0:1, :]
    for r in range(0, x_ref.shape[0], ROW_BLOCK):
        rows = slice(r, r + ROW_BLOCK)
        h_s[rows, :] = (_rms(x_ref[rows, :], g0) * sc1p + sh1).astype(BF16)


def _ada_kernel(cond_ref, w_ref, b_ref, o_ref):
    a = cond_ref[...]
    a = a * jax.nn.sigmoid(a)
    o_ref[...] = _dot(a.astype(BF16), w_ref[...].astype(BF16)) + b_ref[...]


def _ada_all(cond, ada_w, ada_b):
    tn = 1536
    n_out = 6 * D_MODEL
    return pl.pallas_call(
        _ada_kernel,
        grid=(DEPTH, n_out // tn),
        in_specs=[
            pl.BlockSpec((COND_ROWS, D_MODEL), lambda i, n: (0, 0)),
            pl.BlockSpec((None, D_MODEL, tn), lambda i, n: (i, 0, n)),
            pl.BlockSpec((None, 1, tn), lambda i, n: (i, 0, n)),
        ],
        out_specs=pl.BlockSpec((None, COND_ROWS, tn), lambda i, n: (i, 0, n)),
        out_shape=jax.ShapeDtypeStruct((DEPTH, COND_ROWS, n_out), F32),
        compiler_params=_params(32),
        name="ada_mod",
    )(cond, ada_w, ada_b.reshape(DEPTH, 1, n_out))


def _gmlp_head_kernel(x_ref, mod_ref, ng_ref, win_ref, lng_ref, ws_ref, bs_ref, z_ref,
                      h_s, u_s, v_s, vb_s):
    tm = x_ref.shape[0]
    _modulated_norm_to(h_s, x_ref, mod_ref, ng_ref)
    for j in range(A_WIDTH // MXU_COLS):
        cols = slice(j * MXU_COLS, (j + 1) * MXU_COLS)
        u_s[:, cols] = _gelu_tanh(_dot(h_s[...], win_ref[:, cols]))
    for j in range(A_WIDTH // MXU_COLS):
        cols = slice(j * MXU_COLS, (j + 1) * MXU_COLS)
        wcols = slice(A_WIDTH + j * MXU_COLS, A_WIDTH + (j + 1) * MXU_COLS)
        v_s[:, cols] = _gelu_tanh(_dot(h_s[...], win_ref[:, wcols]))
    lng = lng_ref[...]
    for r in range(0, tm, ROW_BLOCK // 2):
        rows = slice(r, r + ROW_BLOCK // 2)
        v = v_s[rows, :]
        d = v - jnp.mean(v, axis=-1, keepdims=True)
        var = jnp.mean(d * d, axis=-1, keepdims=True)
        vb_s[rows, :] = (d * lax.rsqrt(var + LN_EPS) * lng).astype(BF16)
    for g in range(A_GROUPS):
        w = ws_ref[g]
        b = bs_ref[g]
        cols = slice(g * A_GROUP_W, (g + 1) * A_GROUP_W)
        for n in range(tm // CHUNK):
            rows = slice(n * CHUNK, (n + 1) * CHUNK)
            sv = _dot(w, vb_s[rows, cols]) + b
            z_ref[rows, cols] = (u_s[rows, cols] * sv).astype(BF16)


def _gmlp_head(x, mods, ng, w_in, ln_g, w_s, b_s, is_ctx):
    bsz, length, _ = x.shape
    tm = min(512, length)
    return pl.pallas_call(
        _gmlp_head_kernel,
        grid=(bsz, length // tm),
        in_specs=[
            pl.BlockSpec((None, tm, D_MODEL), lambda b, t: (b, t, 0)),
            _mod_spec(is_ctx, bsz),
            _const_spec((4, D_MODEL)),
            _const_spec((D_MODEL, 2 * A_WIDTH)),
            _const_spec((1, A_WIDTH)),
            _const_spec((A_GROUPS, CHUNK, CHUNK)),
            _const_spec((A_GROUPS, CHUNK, 1)),
        ],
        out_specs=pl.BlockSpec((None, tm, A_WIDTH), lambda b, t: (b, t, 0)),
        out_shape=jax.ShapeDtypeStruct((bsz, length, A_WIDTH), BF16),
        scratch_shapes=[
            pltpu.VMEM((tm, D_MODEL), BF16),
            pltpu.VMEM((tm, A_WIDTH), F32),
            pltpu.VMEM((tm, A_WIDTH), F32),
            pltpu.VMEM((tm, A_WIDTH), BF16),
        ],
        compiler_params=_params(48),
        name="gmlp_head",
    )(x, mods, ng, w_in, ln_g, w_s, b_s)


def _tail_kernel(x_ref, z_ref, mod_ref, ng_ref, wp_ref, w1_ref, w2_ref, o_ref,
                 y_s, hm_s, hid_s):
    tm = x_ref.shape[0]
    for n in range(D_MODEL // MXU_COLS):
        cols = slice(n * MXU_COLS, (n + 1) * MXU_COLS)
        y_s[:, cols] = _dot(z_ref[...], wp_ref[:, cols])
    g1 = mod_ref[2:3, :]
    sh2 = mod_ref[3:4, :]
    sc2p = 1.0 + mod_ref[4:5, :]
    g2 = mod_ref[5:6, :]
    for r in range(0, tm, ROW_BLOCK):
        rows = slice(r, r + ROW_BLOCK)
        x1 = x_ref[rows, :] + g1 * _rms(y_s[rows, :], ng_ref[1:2, :])
        o_ref[rows, :] = x1
        hm_s[rows, :] = (_rms(x1, ng_ref[2:3, :]) * sc2p + sh2).astype(BF16)
    ff_cols = 2 * MXU_COLS
    for c in range(D_FF // ff_cols):
        cols = slice(c * ff_cols, (c + 1) * ff_cols)
        t = jnp.maximum(_dot(hm_s[...], w1_ref[:, cols]), 0.0)
        hid_s[:, cols] = (t * t).astype(BF16)
    for n in range(D_MODEL // MXU_COLS):
        cols = slice(n * MXU_COLS, (n + 1) * MXU_COLS)
        y_s[:, cols] = _dot(hid_s[...], w2_ref[:, cols])
    for r in range(0, tm, ROW_BLOCK):
        rows = slice(r, r + ROW_BLOCK)
        o_ref[rows, :] = o_ref[rows, :] + g2 * _rms(y_s[rows, :], ng_ref[3:4, :])


def _tail(x, z, mods, ng, w_post, w1, w2, is_ctx, z_mirrored=False):
    bsz, length, _ = x.shape
    kz = z.shape[-1]
    tm = min(512, length)
    if z_mirrored:
        nt = length // 2 // tm
        z_spec = pl.BlockSpec(
            (None, None, tm, kz),
            lambda b, t: (b, t // nt, jnp.where(t < nt, t, 2 * nt - 1 - t), 0))
    else:
        z_spec = pl.BlockSpec((None, tm, kz), lambda b, t: (b, t, 0))
    return pl.pallas_call(
        _tail_kernel,
        grid=(bsz, length // tm),
        in_specs=[
            pl.BlockSpec((None, tm, D_MODEL), lambda b, t: (b, t, 0)),
            z_spec,
            _mod_spec(is_ctx, bsz),
            _const_spec((4, D_MODEL)),
            _const_spec((kz, D_MODEL)),
            _const_spec((D_MODEL, D_FF)),
            _const_spec((D_FF, D_MODEL)),
        ],
        out_specs=pl.BlockSpec((None, tm, D_MODEL), lambda b, t: (b, t, 0)),
        out_shape=jax.ShapeDtypeStruct((bsz, length, D_MODEL), F32),
        scratch_shapes=[
            pltpu.VMEM((tm, D_MODEL), F32),
            pltpu.VMEM((tm, D_MODEL), BF16),
            pltpu.VMEM((tm, D_FF), BF16),
        ],
        compiler_params=_params(54),
        name="tail",
    )(x, z, mods, ng, w_post, w1, w2)


def _fourier_chan_kernel(x_ref, mod_ref, ng_ref, cs_ref, ab_ref, h_s):
    _modulated_norm_to(h_s, x_ref, mod_ref, ng_ref)
    for g in range(B_GROUPS):
        cols = slice(g * B_GROUP_W, (g + 1) * B_GROUP_W)
        t = _dot(h_s[:, cols], cs_ref[...])
        ab_ref[0, :, cols] = t[:, :B_GROUP_W].astype(BF16)
        ab_ref[1, :, cols] = t[:, B_GROUP_W:].astype(BF16)


def _fourier_seq_kernel(cs_ref, ab_ref, y_ref):
    for n in range(D_MODEL // MXU_COLS):
        cols = slice(n * MXU_COLS, (n + 1) * MXU_COLS)
        y_ref[:, cols] = _dot(cs_ref[...], ab_ref[:, cols]).astype(BF16)


def _dft_tables(length):
    m = jnp.arange(B_GROUP_W, dtype=jnp.int32)
    ang_c = (2.0 * jnp.pi / B_GROUP_W) * ((m[:, None] * m[None, :]) % B_GROUP_W).astype(F32)
    s_c = B_GROUP_W ** -0.5
    cs_chan = jnp.concatenate([jnp.cos(ang_c), jnp.sin(ang_c)], axis=1) * s_c
    k = jnp.arange(length, dtype=jnp.int32)
    ang_l = (2.0 * jnp.pi / length) * ((k[:, None] * k[None, :]) % length).astype(F32)
    s_l = length ** -0.5
    cs_seq = jnp.concatenate([jnp.cos(ang_l), -jnp.sin(ang_l)], axis=1) * s_l
    return cs_chan.astype(BF16), cs_seq.astype(BF16)


def _fourier_seq_sym_kernel(ct_ref, st_ref, ab_ref, perm_ref, y_ref, w_s):
    tk = y_ref.shape[1]
    length = ct_ref.shape[1]
    for n in range(D_MODEL // MXU_COLS):
        cols = slice(n * MXU_COLS, (n + 1) * MXU_COLS)
        yc = _dot(ct_ref[...], ab_ref[0:length, cols])
        ys = _dot(st_ref[...], ab_ref[length:2 * length, cols])
        y_ref[0, :, cols] = (yc[:tk] - ys[:tk]).astype(BF16)
        w_s[:, cols] = (yc + ys).astype(BF16)
    for n in range(D_MODEL // MXU_COLS):
        cols = slice(n * MXU_COLS, (n + 1) * MXU_COLS)
        y_ref[1, :, cols] = _dot(perm_ref[...], w_s[:, cols]).astype(BF16)


def _seq_tables(length, tk):
    nt = length // 2 // tk
    rows = tk + BF16_ROWS
    fine = 64
    n_coarse = -(-(nt * tk + BF16_ROWS) // fine)
    t = jnp.arange(length, dtype=jnp.int32)

    def cos_sin(k):
        ang = (2.0 * jnp.pi / length) * ((k[:, None] * t[None, :]) % length).astype(F32)
        return jnp.cos(ang), jnp.sin(ang)

    ch, sh = cos_sin(jnp.arange(n_coarse, dtype=jnp.int32) * fine)
    cl, sl = cos_sin(jnp.arange(fine, dtype=jnp.int32))
    scale = length ** -0.5
    c = (ch[:, None, :] * cl[None] - sh[:, None, :] * sl[None]).reshape(n_coarse * fine, length)
    s = (sh[:, None, :] * cl[None] + ch[:, None, :] * sl[None]).reshape(n_coarse * fine, length)
    ct = jnp.stack([c[i * tk:i * tk + rows] for i in range(nt)]) * scale
    st = jnp.stack([s[i * tk:i * tk + rows] for i in range(nt)]) * scale
    return ct.astype(BF16), st.astype(BF16)


def _fourier_seq_sym(ab, length, tk):
    bsz = ab.shape[0]
    nt = length // 2 // tk
    rows = tk + BF16_ROWS
    ct, st = _seq_tables(length, tk)
    perm = (jnp.arange(tk)[:, None] + jnp.arange(rows)[None, :] == tk).astype(BF16)
    return pl.pallas_call(
        _fourier_seq_sym_kernel,
        grid=(bsz, nt),
        in_specs=[
            pl.BlockSpec((None, rows, length), lambda b, t: (t, 0, 0)),
            pl.BlockSpec((None, rows, length), lambda b, t: (t, 0, 0)),
            pl.BlockSpec((None, 2 * length, D_MODEL), lambda b, t: (b, 0, 0),
                         pipeline_mode=pl.Buffered(1)),
            _const_spec((tk, rows)),
        ],
        out_specs=pl.BlockSpec((None, 2, tk, D_MODEL), lambda b, t: (b, 0, t, 0)),
        out_shape=jax.ShapeDtypeStruct((bsz, 2, length // 2, D_MODEL), BF16),
        scratch_shapes=[pltpu.VMEM((rows, D_MODEL), BF16)],
        compiler_params=_params(50),
        name="fourier_seq_sym",
    )(ct, st, ab, perm)


def _fourier_head(x, mods, ng, is_ctx):
    bsz, length, _ = x.shape
    tm = min(512, length)
    cs_chan, cs_seq = _dft_tables(length if is_ctx else B_GROUP_W)
    ab = pl.pallas_call(
        _fourier_chan_kernel,
        grid=(bsz, length // tm),
        in_specs=[
            pl.BlockSpec((None, tm, D_MODEL), lambda b, t: (b, t, 0)),
            _mod_spec(is_ctx, bsz),
            _const_spec((4, D_MODEL)),
            _const_spec((B_GROUP_W, 2 * B_GROUP_W)),
        ],
        out_specs=pl.BlockSpec((None, 2, tm, D_MODEL), lambda b, t: (b, 0, t, 0)),
        out_shape=jax.ShapeDtypeStruct((bsz, 2, length, D_MODEL), BF16),
        scratch_shapes=[pltpu.VMEM((tm, D_MODEL), BF16)],
        compiler_params=_params(32),
        name="fourier_chan",
    )(x, mods, ng, cs_chan)
    ab = ab.reshape(bsz, 2 * length, D_MODEL)
    if not is_ctx:
        return _fourier_seq_sym(ab, length, tm)
    return pl.pallas_call(
        _fourier_seq_kernel,
        grid=(bsz, length // tm),
        in_specs=[
            pl.BlockSpec((tm, 2 * length), lambda b, t: (t, 0)),
            pl.BlockSpec((None, 2 * length, D_MODEL), lambda b, t: (b, 0, 0),
                         pipeline_mode=pl.Buffered(1)),
        ],
        out_specs=pl.BlockSpec((None, tm, D_MODEL), lambda b, t: (b, t, 0)),
        out_shape=jax.ShapeDtypeStruct((bsz, length, D_MODEL), BF16),
        compiler_params=_params(48),
        name="fourier_seq",
    )(cs_seq, ab)


def _head_rms(r, g):
    return r * lax.rsqrt(jnp.mean(r * r, axis=-1, keepdims=True) + NORM_EPS) * g


def _qkv_kernel(x_ref, mod_ref, ng_ref, w_ref, qg_ref, kg_ref, cos_ref, sin_ref,
                q_ref, k_ref, v_ref, h_s):
    _modulated_norm_to(h_s, x_ref, mod_ref, ng_ref)
    cosf = cos_ref[...]
    sinf = sin_ref[...]

    def rope(r):
        return r * cosf + pltpu.roll(r, HEAD_DIM // 2, 1) * sinf

    qg = qg_ref[...] * (HEAD_DIM ** -0.5 * LOG2_E)
    for p in range(N_HEADS // 2):
        t = _dot(h_s[...], w_ref[:, p * MXU_COLS:(p + 1) * MXU_COLS])
        for e in range(2):
            q_ref[2 * p + e] = rope(_head_rms(t[:, e * HEAD_DIM:(e + 1) * HEAD_DIM], qg)).astype(BF16)
    kcol = N_HEADS * HEAD_DIM
    t = _dot(h_s[...], w_ref[:, kcol:kcol + MXU_COLS])
    for e in range(N_KV_HEADS):
        k_ref[e] = rope(_head_rms(t[:, e * HEAD_DIM:(e + 1) * HEAD_DIM], kg_ref[...])).astype(BF16)
    t = _dot(h_s[...], w_ref[:, kcol + MXU_COLS:kcol + 2 * MXU_COLS])
    for e in range(N_KV_HEADS):
        v_ref[e] = t[:, e * HEAD_DIM:(e + 1) * HEAD_DIM].astype(BF16)


def _kv_ctx_kernel(x_ref, mod_ref, ng_ref, w_ref, kg_ref, k_ref, v_ref, h_s):
    _modulated_norm_to(h_s, x_ref, mod_ref, ng_ref)
    t = _dot(h_s[...], w_ref[:, 0:MXU_COLS])
    for e in range(N_KV_HEADS):
        k_ref[e] = _head_rms(t[:, e * HEAD_DIM:(e + 1) * HEAD_DIM], kg_ref[...]).astype(BF16)
    t = _dot(h_s[...], w_ref[:, MXU_COLS:2 * MXU_COLS])
    for e in range(N_KV_HEADS):
        v_ref[e] = t[:, e * HEAD_DIM:(e + 1) * HEAD_DIM].astype(BF16)


def _rope_tables(length):
    t = jnp.arange(length)
    row = (t // GRID_W).astype(F32)
    col = (t % GRID_W).astype(F32)
    n_freq = HEAD_DIM // 4
    inv = ROPE_THETA ** (-jnp.arange(n_freq, dtype=F32) / n_freq)
    ang = jnp.concatenate([row[:, None] * inv, col[:, None] * inv], axis=-1)
    cos, sin = jnp.cos(ang), jnp.sin(ang)
    return jnp.concatenate([cos, cos], axis=-1), jnp.concatenate([-sin, sin], axis=-1)


def _qkv_latent(x, mods, ng, w_qkv, q_g, k_g):
    bsz, length, _ = x.shape
    tm = 512
    cosf, sinf = _rope_tables(length)
    n_qkv = w_qkv.shape[-1]
    kv_shape = jax.ShapeDtypeStruct((bsz, N_KV_HEADS, length, HEAD_DIM), BF16)
    kv_spec = pl.BlockSpec((None, N_KV_HEADS, tm, HEAD_DIM), lambda b, t: (b, 0, t, 0))
    return pl.pallas_call(
        _qkv_kernel,
        grid=(bsz, length // tm),
        in_specs=[
            pl.BlockSpec((None, tm, D_MODEL), lambda b, t: (b, t, 0)),
            _mod_spec(False, bsz),
            _const_spec((4, D_MODEL)),
            _const_spec((D_MODEL, n_qkv)),
            _const_spec((1, HEAD_DIM)),
            _const_spec((1, HEAD_DIM)),
            pl.BlockSpec((tm, HEAD_DIM), lambda b, t: (t, 0)),
            pl.BlockSpec((tm, HEAD_DIM), lambda b, t: (t, 0)),
        ],
        out_specs=[
            pl.BlockSpec((None, N_HEADS, tm, HEAD_DIM), lambda b, t: (b, 0, t, 0)),
            kv_spec, kv_spec,
        ],
        out_shape=[
            jax.ShapeDtypeStruct((bsz, N_HEADS, length, HEAD_DIM), BF16),
            kv_shape, kv_shape,
        ],
        scratch_shapes=[pltpu.VMEM((tm, D_MODEL), BF16)],
        compiler_params=_params(32),
        name="qkv_latent",
    )(x, mods, ng, w_qkv, q_g, k_g, cosf, sinf)


def _kv_ctx(ctx, mods, ng, w_kv, k_g):
    bsz, length, _ = ctx.shape
    tm = length
    kv_shape = jax.ShapeDtypeStruct((bsz, N_KV_HEADS, length, HEAD_DIM), BF16)
    kv_spec = pl.BlockSpec((None, N_KV_HEADS, tm, HEAD_DIM), lambda b, t: (b, 0, t, 0))
    return pl.pallas_call(
        _kv_ctx_kernel,
        grid=(bsz, length // tm),
        in_specs=[
            pl.BlockSpec((None, tm, D_MODEL), lambda b, t: (b, t, 0)),
            _mod_spec(True, bsz),
            _const_spec((4, D_MODEL)),
            _const_spec((D_MODEL, 2 * MXU_COLS)),
            _const_spec((1, HEAD_DIM)),
        ],
        out_specs=[kv_spec, kv_spec],
        out_shape=[kv_shape, kv_shape],
        scratch_shapes=[pltpu.VMEM((tm, D_MODEL), BF16)],
        compiler_params=_params(32),
        name="kv_ctx",
    )(ctx, mods, ng, w_kv, k_g)


def _region(index, fn):
    pl.when(pl.program_id(0) >= -index)(fn)


def _attn_kernel(q_ref, k_ref, vt_ref, o_ref, *scratch):
    tq = q_ref.shape[1]
    n_keys = k_ref.shape[1]
    nq = HEADS_PER_KV * tq
    q4_s, s_s, p_s, m_s = (scratch[i * N_KV_HEADS:(i + 1) * N_KV_HEADS] for i in range(4))

    key_chunks = [slice(r, r + MXU_COLS) for r in range(0, n_keys, MXU_COLS)]

    def scores(j, other):
        for h in range(HEADS_PER_KV):
            q4_s[j][h * tq:(h + 1) * tq, :] = q_ref[HEADS_PER_KV * j + h]
        m = None
        for rows in key_chunks:
            s = lax.dot_general(k_ref[j, rows, :], q4_s[j][...], (((1,), (1,)), ((), ())),
                                preferred_element_type=F32)
            s_s[j][rows, :] = s
            cm = jnp.max(s, axis=0, keepdims=True)
            m = cm if m is None else jnp.maximum(m, cm)
            other(rows)
        m_s[j][...] = jnp.broadcast_to(m, (BF16_ROWS, nq))

    def probs(j):
        m = m_s[j][...]

        def chunk(rows):
            for r in range(rows.start, rows.stop, BF16_ROWS):
                tile = slice(r, r + BF16_ROWS)
                p_s[j][tile, :] = jnp.exp2(s_s[j][tile, :] - m).astype(BF16)
        return chunk

    def finish(j, other):
        acc = None
        for rows in key_chunks:
            d = _dot(vt_ref[j, :, rows], p_s[j][rows, :])
            acc = d if acc is None else acc + d
            other(rows)
        o = acc[:HEAD_DIM, :] / acc[HEAD_DIM:HEAD_DIM + 1, :]
        for h in range(HEADS_PER_KV):
            head = HEADS_PER_KV * j + h
            o_ref[:, head * HEAD_DIM:(head + 1) * HEAD_DIM] = o[:, h * tq:(h + 1) * tq].T.astype(BF16)

    def nothing(rows):
        del rows

    _region(0, lambda: scores(0, nothing))
    _region(1, lambda: scores(1, probs(0)))
    _region(2, lambda: finish(0, probs(1)))
    _region(3, lambda: finish(1, nothing))


def _attention(q, k, vt_ext):
    bsz, _, length, _ = q.shape
    n_keys = k.shape[2]
    vt_rows = vt_ext.shape[2]
    tq = 128
    nq = HEADS_PER_KV * tq
    return pl.pallas_call(
        _attn_kernel,
        grid=(bsz, length // tq),
        in_specs=[
            pl.BlockSpec((None, N_HEADS, tq, HEAD_DIM), lambda b, t: (b, 0, t, 0)),
            pl.BlockSpec((None, N_KV_HEADS, n_keys, HEAD_DIM), lambda b, t: (b, 0, 0, 0)),
            pl.BlockSpec((None, N_KV_HEADS, vt_rows, n_keys), lambda b, t: (b, 0, 0, 0)),
        ],
        out_specs=pl.BlockSpec((None, tq, D_MODEL), lambda b, t: (b, t, 0)),
        out_shape=jax.ShapeDtypeStruct((bsz, length, D_MODEL), BF16),
        scratch_shapes=(
            [pltpu.VMEM((nq, HEAD_DIM), BF16)] * N_KV_HEADS
            + [pltpu.VMEM((n_keys, nq), F32)] * N_KV_HEADS
            + [pltpu.VMEM((n_keys, nq), BF16)] * N_KV_HEADS
            + [pltpu.VMEM((BF16_ROWS, nq), F32)] * N_KV_HEADS),
        compiler_params=_params(54),
        name="attention",
    )(q, k, vt_ext)


def kernel(x, c, ctx, c_ctx, ada_w, ada_b, norm_g, mlp_w1, mlp_w2, a_w_in, a_ln_g, a_w_s, a_b_s, a_w_out,
           b_w_out, c_w_qkv, c_q_g, c_k_g, c_w_o):
    bsz = x.shape[0]
    attn_layers = [i for i in range(DEPTH) if i % N_MIXERS == 2]
    last_ctx_read = attn_layers[-1] if attn_layers else -1

    cond = jnp.zeros((COND_ROWS, D_MODEL), F32).at[:bsz].set(c).at[bsz].set(c_ctx)
    mods_all = _ada_all(cond, ada_w, ada_b).reshape(DEPTH, COND_ROWS, 6, D_MODEL)

    w1 = mlp_w1.astype(BF16)
    w2 = mlp_w2.astype(BF16)

    for i in range(DEPTH):
        kind, j = i % N_MIXERS, i // N_MIXERS
        ctx_in = i <= last_ctx_read
        ctx_out = i < last_ctx_read
        mods = mods_all[i]
        ng = norm_g[i]

        if kind == 0:
            w_in = a_w_in[j].astype(BF16)
            ln_g = a_ln_g[j].reshape(1, A_WIDTH)
            w_s = a_w_s[j].astype(BF16)
            b_s = a_b_s[j].reshape(A_GROUPS, CHUNK, 1)
            w_post = a_w_out[j].astype(BF16)
            z = _gmlp_head(x, mods, ng, w_in, ln_g, w_s, b_s, False)
            zc = _gmlp_head(ctx, mods, ng, w_in, ln_g, w_s, b_s, True) if ctx_out else None
        elif kind == 1:
            w_post = b_w_out[j].astype(BF16)
            z = _fourier_head(x, mods, ng, False)
            zc = _fourier_head(ctx, mods, ng, True) if ctx_out else None
        else:
            w_qkv = c_w_qkv[j].astype(BF16)
            w_post = c_w_o[j].astype(BF16)
            q_g = c_q_g[j].reshape(1, HEAD_DIM)
            k_g = c_k_g[j].reshape(1, HEAD_DIM)
            q, k, v = _qkv_latent(x, mods, ng, w_qkv, q_g, k_g)
            if ctx_in:
                kc, vc = _kv_ctx(ctx, mods, ng, w_qkv[:, N_HEADS * HEAD_DIM:], k_g)
                k = jnp.concatenate([kc, k], axis=2)
                v = jnp.concatenate([vc, v], axis=2)
            ones_row = (jnp.arange(BF16_ROWS) == 0).astype(BF16)[:, None]
            pad = jnp.broadcast_to(ones_row, v.shape[:2] + (BF16_ROWS, v.shape[2]))
            vt_ext = jnp.concatenate([jnp.swapaxes(v, 2, 3), pad], axis=2)
            z = _attention(q, k, vt_ext)
            zc = None
            assert not ctx_out

        x = _tail(x, z, mods, ng, w_post, w1[i], w2[i], False, z_mirrored=(kind == 1))
        if ctx_out:
            ctx = _tail(ctx, zc, mods, ng, w_post, w1[i], w2[i], True)

    return x
```

```python
import functools

import jax
import jax.numpy as jnp
from jax import lax
from jax.experimental import pallas as pl
from jax.experimental.pallas import tpu as pltpu

D_MODEL = 1024
DEPTH = 4
GRID_W = 64
N_MIXERS = 3
CHUNK = 128
A_WIDTH = 2 * D_MODEL
A_GROUPS = 8
A_GROUP_W = A_WIDTH // A_GROUPS
B_GROUPS = 4
B_GROUP_W = D_MODEL // B_GROUPS
HEAD_DIM = 128
N_HEADS = D_MODEL // HEAD_DIM
N_KV_HEADS = 2
HEADS_PER_KV = N_HEADS // N_KV_HEADS
ROPE_THETA = 10000.0
D_FF = 4 * D_MODEL
NORM_EPS = 1e-6
LN_EPS = 1e-5
LOG2_E = 1.4426950408889634

F32 = jnp.float32
BF16 = jnp.bfloat16

MXU_COLS = 256
ROW_BLOCK = 64
BF16_ROWS = 16
COND_ROWS = 16
MIB = 1024 * 1024


def _dot(a, b):
    return jnp.dot(a, b, preferred_element_type=F32)


def _rms(x, g):
    ms = jnp.mean(x * x, axis=-1, keepdims=True)
    return x * lax.rsqrt(ms + NORM_EPS) * g


def _gelu_tanh(x):
    c = 2.0 * 0.7978845608028654 * LOG2_E
    z = x * (-c - (c * 0.044715) * (x * x))
    return x / (1.0 + jnp.exp2(z))


def _const_spec(shape):
    zeros = (0,) * len(shape)
    return pl.BlockSpec(shape, lambda b, t: zeros, pipeline_mode=pl.Buffered(1))


def _mod_spec(is_ctx, batch):
    if is_ctx:
        return pl.BlockSpec((None, 6, D_MODEL), lambda b, t: (batch, 0, 0))
    return pl.BlockSpec((None, 6, D_MODEL), lambda b, t: (b, 0, 0))


def _params(vmem_mib):
    return pltpu.CompilerParams(
        dimension_semantics=("arbitrary", "arbitrary"),
        vmem_limit_bytes=vmem_mib * MIB)


def _modulated_norm_to(h_s, x_ref, mod_ref, ng_ref):
    sh1 = mod_ref[0:1, :]
    sc1p = 1.0 + mod_ref[1:2, :]
    g0 = ng_ref[0:1, :]
    for r in range(0, x_ref.shape[0], ROW_BLOCK):
        rows = slice(r, r + ROW_BLOCK)
        h_s[rows, :] = (_rms(x_ref[rows, :], g0) * sc1p + sh1).astype(BF16)


def _ada_kernel(cond_ref, w_ref, b_ref, o_ref):
    a = cond_ref[...]
    a = a * jax.nn.sigmoid(a)
    o_ref[...] = _dot(a.astype(BF16), w_ref[...].astype(BF16)) + b_ref[...]


def _ada_all(cond, ada_w, ada_b):
    tn = 1536
    n_out = 6 * D_MODEL
    return pl.pallas_call(
        _ada_kernel,
        grid=(DEPTH, n_out // tn),
        in_specs=[
            pl.BlockSpec((COND_ROWS, D_MODEL), lambda i, n: (0, 0)),
            pl.BlockSpec((None, D_MODEL, tn), lambda i, n: (i, 0, n)),
            pl.BlockSpec((None, 1, tn), lambda i, n: (i, 0, n)),
        ],
        out_specs=pl.BlockSpec((None, COND_ROWS, tn), lambda i, n: (i, 0, n)),
        out_shape=jax.ShapeDtypeStruct((DEPTH, COND_ROWS, n_out), F32),
        compiler_params=_params(32),
        name="ada_mod",
    )(cond, ada_w, ada_b.reshape(DEPTH, 1, n_out))


def _gmlp_head_kernel(x_ref, mod_ref, ng_ref, win_ref, lng_ref, ws_ref, bs_ref, z_ref,
                      h_s, u_s, v_s, vb_s):
    tm = x_ref.shape[0]
    _modulated_norm_to(h_s, x_ref, mod_ref, ng_ref)
    width = 2 * MXU_COLS
    n_half = A_WIDTH // width
    lng = lng_ref[...]

    def proj(c):
        return _dot(h_s[...], win_ref[:, c * width:(c + 1) * width])

    def gelu_store(d, c):
        dst, c = (u_s, c) if c < n_half else (v_s, c - n_half)
        for r in range(0, tm, ROW_BLOCK):
            dst[r:r + ROW_BLOCK, c * width:(c + 1) * width] = _gelu_tanh(d[r:r + ROW_BLOCK, :])

    def layer_norm_rows(part):
        for r in range(part * tm // n_half, (part + 1) * tm // n_half, ROW_BLOCK // 2):
            rows = slice(r, r + ROW_BLOCK // 2)
            v = v_s[rows, :]
            d = v - jnp.mean(v, axis=-1, keepdims=True)
            var = jnp.mean(d * d, axis=-1, keepdims=True)
            vb_s[rows, :] = (d * lax.rsqrt(var + LN_EPS) * lng).astype(BF16)

    order = list(range(n_half, 2 * n_half)) + list(range(n_half))
    d = proj(order[0])
    for i in range(1, len(order)):
        d_next = proj(order[i])
        gelu_store(d, order[i - 1])
        if i > n_half:
            layer_norm_rows(i - n_half - 1)
        d = d_next
    gelu_store(d, order[-1])
    layer_norm_rows(n_half - 1)
    for g in range(A_GROUPS):
        w = ws_ref[g]
        b = bs_ref[g]
        cols = slice(g * A_GROUP_W, (g + 1) * A_GROUP_W)
        for n in range(tm // CHUNK):
            rows = slice(n * CHUNK, (n + 1) * CHUNK)
            sv = _dot(w, vb_s[rows, cols]) + b
            z_ref[rows, cols] = (u_s[rows, cols] * sv).astype(BF16)


def _gmlp_head(x, mods, ng, w_in, ln_g, w_s, b_s, is_ctx):
    bsz, length, _ = x.shape
    tm = min(512, length)
    return pl.pallas_call(
        _gmlp_head_kernel,
        grid=(bsz, length // tm),
        in_specs=[
            pl.BlockSpec((None, tm, D_MODEL), lambda b, t: (b, t, 0)),
            _mod_spec(is_ctx, bsz),
            _const_spec((4, D_MODEL)),
            _const_spec((D_MODEL, 2 * A_WIDTH)),
            _const_spec((1, A_WIDTH)),
            _const_spec((A_GROUPS, CHUNK, CHUNK)),
            _const_spec((A_GROUPS, CHUNK, 1)),
        ],
        out_specs=pl.BlockSpec((None, tm, A_WIDTH), lambda b, t: (b, t, 0)),
        out_shape=jax.ShapeDtypeStruct((bsz, length, A_WIDTH), BF16),
        scratch_shapes=[
            pltpu.VMEM((tm, D_MODEL), BF16),
            pltpu.VMEM((tm, A_WIDTH), F32),
            pltpu.VMEM((tm, A_WIDTH), F32),
            pltpu.VMEM((tm, A_WIDTH), BF16),
        ],
        compiler_params=_params(48),
        name="gmlp_head",
    )(x, mods, ng, w_in, ln_g, w_s, b_s)


def _tail_kernel(x_ref, z_ref, mod_ref, ng_ref, wp_ref, w1_ref, w2_ref, o_ref,
                 y_s, hm_s, hid_s):
    tm = x_ref.shape[0]
    for n in range(D_MODEL // MXU_COLS):
        cols = slice(n * MXU_COLS, (n + 1) * MXU_COLS)
        y_s[:, cols] = _dot(z_ref[...], wp_ref[:, cols])
    g1 = mod_ref[2:3, :]
    sh2 = mod_ref[3:4, :]
    sc2p = 1.0 + mod_ref[4:5, :]
    g2 = mod_ref[5:6, :]
    for r in range(0, tm, ROW_BLOCK):
        rows = slice(r, r + ROW_BLOCK)
        x1 = x_ref[rows, :] + g1 * _rms(y_s[rows, :], ng_ref[1:2, :])
        o_ref[rows, :] = x1
        hm_s[rows, :] = (_rms(x1, ng_ref[2:3, :]) * sc2p + sh2).astype(BF16)
    ff_cols = 2 * MXU_COLS
    for c in range(D_FF // ff_cols):
        cols = slice(c * ff_cols, (c + 1) * ff_cols)
        t = jnp.maximum(_dot(hm_s[...], w1_ref[:, cols]), 0.0)
        hid_s[:, cols] = (t * t).astype(BF16)
    for n in range(D_MODEL // MXU_COLS):
        cols = slice(n * MXU_COLS, (n + 1) * MXU_COLS)
        y_s[:, cols] = _dot(hid_s[...], w2_ref[:, cols])
    for r in range(0, tm, ROW_BLOCK):
        rows = slice(r, r + ROW_BLOCK)
        o_ref[rows, :] = o_ref[rows, :] + g2 * _rms(y_s[rows, :], ng_ref[3:4, :])


def _tail(x, z, mods, ng, w_post, w1, w2, is_ctx, z_mirrored=False):
    bsz, length, _ = x.shape
    kz = z.shape[-1]
    tm = min(512, length)
    if z_mirrored:
        nt = length // 2 // tm
        z_spec = pl.BlockSpec(
            (None, None, tm, kz),
            lambda b, t: (b, t // nt, jnp.where(t < nt, t, 2 * nt - 1 - t), 0))
    else:
        z_spec = pl.BlockSpec((None, tm, kz), lambda b, t: (b, t, 0))
    return pl.pallas_call(
        _tail_kernel,
        grid=(bsz, length // tm),
        in_specs=[
            pl.BlockSpec((None, tm, D_MODEL), lambda b, t: (b, t, 0)),
            z_spec,
            _mod_spec(is_ctx, bsz),
            _const_spec((4, D_MODEL)),
            _const_spec((kz, D_MODEL)),
            _const_spec((D_MODEL, D_FF)),
            _const_spec((D_FF, D_MODEL)),
        ],
        out_specs=pl.BlockSpec((None, tm, D_MODEL), lambda b, t: (b, t, 0)),
        out_shape=jax.ShapeDtypeStruct((bsz, length, D_MODEL), F32),
        scratch_shapes=[
            pltpu.VMEM((tm, D_MODEL), F32),
            pltpu.VMEM((tm, D_MODEL), BF16),
            pltpu.VMEM((tm, D_FF), BF16),
        ],
        compiler_params=_params(54),
        name="tail",
    )(x, z, mods, ng, w_post, w1, w2)


def _fourier_chan_kernel(x_ref, mod_ref, ng_ref, cs_ref, ab_ref, h_s):
    _modulated_norm_to(h_s, x_ref, mod_ref, ng_ref)
    for g in range(B_GROUPS):
        cols = slice(g * B_GROUP_W, (g + 1) * B_GROUP_W)
        t = _dot(h_s[:, cols], cs_ref[...])
        ab_ref[0, :, cols] = t[:, :B_GROUP_W].astype(BF16)
        ab_ref[1, :, cols] = t[:, B_GROUP_W:].astype(BF16)


def _fourier_seq_kernel(cs_ref, ab_ref, y_ref):
    for n in range(D_MODEL // MXU_COLS):
        cols = slice(n * MXU_COLS, (n + 1) * MXU_COLS)
        y_ref[:, cols] = _dot(cs_ref[...], ab_ref[:, cols]).astype(BF16)


def _dft_tables(length):
    m = jnp.arange(B_GROUP_W, dtype=jnp.int32)
    ang_c = (2.0 * jnp.pi / B_GROUP_W) * ((m[:, None] * m[None, :]) % B_GROUP_W).astype(F32)
    s_c = B_GROUP_W ** -0.5
    cs_chan = jnp.concatenate([jnp.cos(ang_c), jnp.sin(ang_c)], axis=1) * s_c
    k = jnp.arange(length, dtype=jnp.int32)
    ang_l = (2.0 * jnp.pi / length) * ((k[:, None] * k[None, :]) % length).astype(F32)
    s_l = length ** -0.5
    cs_seq = jnp.concatenate([jnp.cos(ang_l), -jnp.sin(ang_l)], axis=1) * s_l
    return cs_chan.astype(BF16), cs_seq.astype(BF16)


def _fourier_fold_kernel(xp_ref, xm_ref, x0_ref, mod_ref, ng_ref, cc_ref, sc_ref, rev_ref,
                         ab_ref, a0_ref, hp_s, hm_s):
    tm = xm_ref.shape[0]
    sh1 = mod_ref[0:1, :]
    sc1p = 1.0 + mod_ref[1:2, :]
    g0 = ng_ref[0:1, :]
    for r in range(0, tm, ROW_BLOCK):
        rows = slice(r, r + ROW_BLOCK)
        hp_s[rows, :] = _rms(xp_ref[0, r + 1:r + 1 + ROW_BLOCK, :], g0) * sc1p + sh1
        hm_s[rows, :] = (_rms(xm_ref[rows, :], g0) * sc1p + sh1).astype(BF16)
    h0 = (_rms(x0_ref[...], g0) * sc1p + sh1).astype(BF16)
    for g in range(B_GROUPS):
        cols = slice(g * B_GROUP_W, (g + 1) * B_GROUP_W)
        hm_rev = _dot(rev_ref[...], hm_s[:, cols])
        hp = hp_s[:, cols]
        ab_ref[0, :, cols] = _dot((hp + hm_rev).astype(BF16), cc_ref[...]).astype(BF16)
        ab_ref[1, :, cols] = _dot((hp - hm_rev).astype(BF16), sc_ref[...]).astype(BF16)
        a0_ref[:, cols] = _dot(h0[:, cols], cc_ref[...])


def _fourier_seq_fold_kernel(ct_ref, st_ref, ab_ref, a0_ref, perm_ref, y_ref, w_s, *, scale):
    tk = y_ref.shape[1]
    half = ct_ref.shape[1]
    for n in range(D_MODEL // MXU_COLS):
        cols = slice(n * MXU_COLS, (n + 1) * MXU_COLS)
        yc = _dot(ct_ref[...], ab_ref[0:half, cols]) + a0_ref[0:1, cols] * scale
        ys = _dot(st_ref[...], ab_ref[half:2 * half, cols])
        y_ref[0, :, cols] = (yc[:tk] - ys[:tk]).astype(BF16)
        w_s[:, cols] = (yc + ys).astype(BF16)
    for n in range(D_MODEL // MXU_COLS):
        cols = slice(n * MXU_COLS, (n + 1) * MXU_COLS)
        y_ref[1, :, cols] = _dot(perm_ref[...], w_s[:, cols]).astype(BF16)


def _seq_fold_tables(length, tk):
    half = length // 2
    nt = half // tk
    rows = tk + BF16_ROWS
    fine = 64
    n_coarse = -(-(nt * tk + BF16_ROWS) // fine)
    t = jnp.arange(1, half + 1, dtype=jnp.int32)

    def cos_sin(k):
        ang = (2.0 * jnp.pi / length) * ((k[:, None] * t[None, :]) % length).astype(F32)
        return jnp.cos(ang), jnp.sin(ang)

    ch, sh = cos_sin(jnp.arange(n_coarse, dtype=jnp.int32) * fine)
    cl, sl = cos_sin(jnp.arange(fine, dtype=jnp.int32))
    weight = jnp.where(t == half, 0.5, 1.0) * length ** -0.5
    c = (ch[:, None, :] * cl[None] - sh[:, None, :] * sl[None]).reshape(n_coarse * fine, half)
    s = (sh[:, None, :] * cl[None] + ch[:, None, :] * sl[None]).reshape(n_coarse * fine, half)
    ct = jnp.stack([c[i * tk:i * tk + rows] for i in range(nt)]) * weight
    st = jnp.stack([s[i * tk:i * tk + rows] for i in range(nt)]) * weight
    return ct.astype(BF16), st.astype(BF16)


def _fourier_head_folded(x, mods, ng):
    bsz, length, _ = x.shape
    tm = 512
    half = length // 2
    nt = half // tm
    rows = tm + BF16_ROWS
    m = jnp.arange(B_GROUP_W, dtype=jnp.int32)
    ang_c = (2.0 * jnp.pi / B_GROUP_W) * ((m[:, None] * m[None, :]) % B_GROUP_W).astype(F32)
    cc = (jnp.cos(ang_c) * B_GROUP_W ** -0.5).astype(BF16)
    sc = (jnp.sin(ang_c) * B_GROUP_W ** -0.5).astype(BF16)
    rev = (jnp.arange(tm)[:, None] + jnp.arange(tm)[None, :] == tm - 1).astype(BF16)
    ab, a0 = pl.pallas_call(
        _fourier_fold_kernel,
        grid=(bsz, nt),
        in_specs=[
            pl.BlockSpec((pl.Element(1), pl.Element(tm + 8), pl.Element(D_MODEL)),
                         lambda b, t: (b, t * tm, 0)),
            pl.BlockSpec((None, tm, D_MODEL), lambda b, t: (b, 2 * nt - 1 - t, 0)),
            pl.BlockSpec((None, 8, D_MODEL), lambda b, t: (b, 0, 0)),
            _mod_spec(False, bsz),
            _const_spec((4, D_MODEL)),
            _const_spec((B_GROUP_W, B_GROUP_W)),
            _const_spec((B_GROUP_W, B_GROUP_W)),
            _const_spec((tm, tm)),
        ],
        out_specs=[
            pl.BlockSpec((None, 2, tm, D_MODEL), lambda b, t: (b, 0, t, 0)),
            pl.BlockSpec((None, 8, D_MODEL), lambda b, t: (b, 0, 0)),
        ],
        out_shape=[
            jax.ShapeDtypeStruct((bsz, 2, half, D_MODEL), BF16),
            jax.ShapeDtypeStruct((bsz, 8, D_MODEL), F32),
        ],
        scratch_shapes=[pltpu.VMEM((tm, D_MODEL), F32), pltpu.VMEM((tm, D_MODEL), BF16)],
        compiler_params=_params(32),
        name="fourier_fold",
    )(x, x, x, mods, ng, cc, sc, rev)
    ct, st = _seq_fold_tables(length, tm)
    perm = (jnp.arange(tm)[:, None] + jnp.arange(rows)[None, :] == tm).astype(BF16)
    return pl.pallas_call(
        functools.partial(_fourier_seq_fold_kernel, scale=length ** -0.5),
        grid=(bsz, nt),
        in_specs=[
            pl.BlockSpec((None, rows, half), lambda b, t: (t, 0, 0)),
            pl.BlockSpec((None, rows, half), lambda b, t: (t, 0, 0)),
            pl.BlockSpec((None, 2 * half, D_MODEL), lambda b, t: (b, 0, 0),
                         pipeline_mode=pl.Buffered(1)),
            pl.BlockSpec((None, 8, D_MODEL), lambda b, t: (b, 0, 0)),
            _const_spec((tm, rows)),
        ],
        out_specs=pl.BlockSpec((None, 2, tm, D_MODEL), lambda b, t: (b, 0, t, 0)),
        out_shape=jax.ShapeDtypeStruct((bsz, 2, half, D_MODEL), BF16),
        scratch_shapes=[pltpu.VMEM((rows, D_MODEL), BF16)],
        compiler_params=_params(40),
        name="fourier_seq_fold",
    )(ct, st, ab.reshape(bsz, 2 * half, D_MODEL), a0, perm)


def _fourier_head(x, mods, ng, is_ctx):
    bsz, length, _ = x.shape
    tm = min(512, length)
    cs_chan, cs_seq = _dft_tables(length)
    ab = pl.pallas_call(
        _fourier_chan_kernel,
        grid=(bsz, length // tm),
        in_specs=[
            pl.BlockSpec((None, tm, D_MODEL), lambda b, t: (b, t, 0)),
            _mod_spec(is_ctx, bsz),
            _const_spec((4, D_MODEL)),
            _const_spec((B_GROUP_W, 2 * B_GROUP_W)),
        ],
        out_specs=pl.BlockSpec((None, 2, tm, D_MODEL), lambda b, t: (b, 0, t, 0)),
        out_shape=jax.ShapeDtypeStruct((bsz, 2, length, D_MODEL), BF16),
        scratch_shapes=[pltpu.VMEM((tm, D_MODEL), BF16)],
        compiler_params=_params(32),
        name="fourier_chan",
    )(x, mods, ng, cs_chan)
    ab = ab.reshape(bsz, 2 * length, D_MODEL)
    return pl.pallas_call(
        _fourier_seq_kernel,
        grid=(bsz, length // tm),
        in_specs=[
            pl.BlockSpec((tm, 2 * length), lambda b, t: (t, 0)),
            pl.BlockSpec((None, 2 * length, D_MODEL), lambda b, t: (b, 0, 0),
                         pipeline_mode=pl.Buffered(1)),
        ],
        out_specs=pl.BlockSpec((None, tm, D_MODEL), lambda b, t: (b, t, 0)),
        out_shape=jax.ShapeDtypeStruct((bsz, length, D_MODEL), BF16),
        compiler_params=_params(48),
        name="fourier_seq",
    )(cs_seq, ab)


def _head_rms(r, g):
    return r * lax.rsqrt(jnp.mean(r * r, axis=-1, keepdims=True) + NORM_EPS) * g


def _qkv_kernel(x_ref, mod_ref, ng_ref, w_ref, qg_ref, kg_ref, mean_ref, cos_ref, sin_ref,
                q_ref, k_ref, v_ref, h_s):
    _modulated_norm_to(h_s, x_ref, mod_ref, ng_ref)
    cosf = cos_ref[...]
    sinf = sin_ref[...]

    qg = qg_ref[...] * (HEAD_DIM ** -0.5 * LOG2_E)
    gains = [jnp.concatenate([qg, qg], axis=1)] * (N_HEADS // 2)
    gains.append(jnp.concatenate([kg_ref[...], kg_ref[...]], axis=1))
    outs = [(q_ref, 2 * p) for p in range(N_HEADS // 2)] + [(k_ref, 0)]
    n_pairs = len(outs)

    def project(p):
        return _dot(h_s[...], w_ref[:, p * MXU_COLS:(p + 1) * MXU_COLS])

    def normalise(t, p):
        sq = t * t
        hi = sq.astype(BF16)
        lo = (sq - hi.astype(F32)).astype(BF16)
        ms = _dot(hi, mean_ref[...]) + _dot(lo, mean_ref[...])
        return t * lax.rsqrt(ms + NORM_EPS) * gains[p]

    def rope_store(rn, p):
        ref, first = outs[p]
        for e in range(2):
            r = rn[:, e * HEAD_DIM:(e + 1) * HEAD_DIM]
            ref[first + e] = (r * cosf + pltpu.roll(r, HEAD_DIM // 2, 1) * sinf).astype(BF16)

    t_next = project(0)
    rn_prev = None
    for p in range(n_pairs + 1):
        t = t_next
        if p < n_pairs:
            t_next = project(p + 1)
        rn = normalise(t, p) if p < n_pairs else None
        if rn_prev is not None:
            rope_store(rn_prev, p - 1)
        rn_prev = rn
    for e in range(N_KV_HEADS):
        v_ref[e] = t[:, e * HEAD_DIM:(e + 1) * HEAD_DIM].astype(BF16)


def _kv_ctx_kernel(x_ref, mod_ref, ng_ref, w_ref, kg_ref, k_ref, v_ref, h_s):
    _modulated_norm_to(h_s, x_ref, mod_ref, ng_ref)
    t = _dot(h_s[...], w_ref[:, 0:MXU_COLS])
    for e in range(N_KV_HEADS):
        k_ref[e] = _head_rms(t[:, e * HEAD_DIM:(e + 1) * HEAD_DIM], kg_ref[...]).astype(BF16)
    t = _dot(h_s[...], w_ref[:, MXU_COLS:2 * MXU_COLS])
    for e in range(N_KV_HEADS):
        v_ref[e] = t[:, e * HEAD_DIM:(e + 1) * HEAD_DIM].astype(BF16)


def _rope_tables(length):
    t = jnp.arange(length)
    row = (t // GRID_W).astype(F32)
    col = (t % GRID_W).astype(F32)
    n_freq = HEAD_DIM // 4
    inv = ROPE_THETA ** (-jnp.arange(n_freq, dtype=F32) / n_freq)
    ang = jnp.concatenate([row[:, None] * inv, col[:, None] * inv], axis=-1)
    cos, sin = jnp.cos(ang), jnp.sin(ang)
    return jnp.concatenate([cos, cos], axis=-1), jnp.concatenate([-sin, sin], axis=-1)


def _qkv_latent(x, mods, ng, w_qkv, q_g, k_g):
    bsz, length, _ = x.shape
    tm = 512
    cosf, sinf = _rope_tables(length)
    n_qkv = w_qkv.shape[-1]
    lane_head = jnp.arange(MXU_COLS) // HEAD_DIM
    head_mean = ((lane_head[:, None] == lane_head[None, :]) * (1.0 / HEAD_DIM)).astype(BF16)
    kv_shape = jax.ShapeDtypeStruct((bsz, N_KV_HEADS, length, HEAD_DIM), BF16)
    kv_spec = pl.BlockSpec((None, N_KV_HEADS, tm, HEAD_DIM), lambda b, t: (b, 0, t, 0))
    return pl.pallas_call(
        _qkv_kernel,
        grid=(bsz, length // tm),
        in_specs=[
            pl.BlockSpec((None, tm, D_MODEL), lambda b, t: (b, t, 0)),
            _mod_spec(False, bsz),
            _const_spec((4, D_MODEL)),
            _const_spec((D_MODEL, n_qkv)),
            _const_spec((1, HEAD_DIM)),
            _const_spec((1, HEAD_DIM)),
            _const_spec((MXU_COLS, MXU_COLS)),
            pl.BlockSpec((tm, HEAD_DIM), lambda b, t: (t, 0)),
            pl.BlockSpec((tm, HEAD_DIM), lambda b, t: (t, 0)),
        ],
        out_specs=[
            pl.BlockSpec((None, N_HEADS, tm, HEAD_DIM), lambda b, t: (b, 0, t, 0)),
            kv_spec, kv_spec,
        ],
        out_shape=[
            jax.ShapeDtypeStruct((bsz, N_HEADS, length, HEAD_DIM), BF16),
            kv_shape, kv_shape,
        ],
        scratch_shapes=[pltpu.VMEM((tm, D_MODEL), BF16)],
        compiler_params=_params(32),
        name="qkv_latent",
    )(x, mods, ng, w_qkv, q_g, k_g, head_mean, cosf, sinf)


def _kv_ctx(ctx, mods, ng, w_kv, k_g):
    bsz, length, _ = ctx.shape
    tm = length
    kv_shape = jax.ShapeDtypeStruct((bsz, N_KV_HEADS, length, HEAD_DIM), BF16)
    kv_spec = pl.BlockSpec((None, N_KV_HEADS, tm, HEAD_DIM), lambda b, t: (b, 0, t, 0))
    return pl.pallas_call(
        _kv_ctx_kernel,
        grid=(bsz, length // tm),
        in_specs=[
            pl.BlockSpec((None, tm, D_MODEL), lambda b, t: (b, t, 0)),
            _mod_spec(True, bsz),
            _const_spec((4, D_MODEL)),
            _const_spec((D_MODEL, 2 * MXU_COLS)),
            _const_spec((1, HEAD_DIM)),
        ],
        out_specs=[kv_spec, kv_spec],
        out_shape=[kv_shape, kv_shape],
        scratch_shapes=[pltpu.VMEM((tm, D_MODEL), BF16)],
        compiler_params=_params(32),
        name="kv_ctx",
    )(ctx, mods, ng, w_kv, k_g)


def _region(index, fn):
    pl.when(pl.program_id(0) >= -index)(fn)


def _attn_kernel(q_ref, k_ref, vt_ref, o_ref, *scratch):
    tq = q_ref.shape[1]
    n_keys = k_ref.shape[1]
    nq = HEADS_PER_KV * tq
    q4_s, s_s, p_s, m_s = (scratch[i * N_KV_HEADS:(i + 1) * N_KV_HEADS] for i in range(4))

    key_chunks = [slice(r, r + MXU_COLS) for r in range(0, n_keys, MXU_COLS)]

    def scores(j, other):
        for h in range(HEADS_PER_KV):
            q4_s[j][h * tq:(h + 1) * tq, :] = q_ref[HEADS_PER_KV * j + h]
        m = None
        for rows in key_chunks:
            s = lax.dot_general(k_ref[j, rows, :], q4_s[j][...], (((1,), (1,)), ((), ())),
                                preferred_element_type=F32)
            s_s[j][rows, :] = s
            cm = jnp.max(s, axis=0, keepdims=True)
            m = cm if m is None else jnp.maximum(m, cm)
            other(rows)
        m_s[j][...] = jnp.broadcast_to(m, (BF16_ROWS, nq))

    def probs(j):
        m = m_s[j][...]

        def chunk(rows):
            for r in range(rows.start, rows.stop, BF16_ROWS):
                tile = slice(r, r + BF16_ROWS)
                p_s[j][tile, :] = jnp.exp2(s_s[j][tile, :] - m).astype(BF16)
        return chunk

    def finish(j, other):
        acc = None
        for rows in key_chunks:
            d = _dot(vt_ref[j, :, rows], p_s[j][rows, :])
            acc = d if acc is None else acc + d
            other(rows)
        o = acc[:HEAD_DIM, :] / acc[HEAD_DIM:HEAD_DIM + 1, :]
        for h in range(HEADS_PER_KV):
            head = HEADS_PER_KV * j + h
            o_ref[:, head * HEAD_DIM:(head + 1) * HEAD_DIM] = o[:, h * tq:(h + 1) * tq].T.astype(BF16)

    def nothing(rows):
        del rows

    _region(0, lambda: scores(0, nothing))
    _region(1, lambda: scores(1, probs(0)))
    _region(2, lambda: finish(0, probs(1)))
    _region(3, lambda: finish(1, nothing))


def _attention(q, k, vt_ext):
    bsz, _, length, _ = q.shape
    n_keys = k.shape[2]
    vt_rows = vt_ext.shape[2]
    tq = 128
    nq = HEADS_PER_KV * tq
    return pl.pallas_call(
        _attn_kernel,
        grid=(bsz, length // tq),
        in_specs=[
            pl.BlockSpec((None, N_HEADS, tq, HEAD_DIM), lambda b, t: (b, 0, t, 0)),
            pl.BlockSpec((None, N_KV_HEADS, n_keys, HEAD_DIM), lambda b, t: (b, 0, 0, 0)),
            pl.BlockSpec((None, N_KV_HEADS, vt_rows, n_keys), lambda b, t: (b, 0, 0, 0)),
        ],
        out_specs=pl.BlockSpec((None, tq, D_MODEL), lambda b, t: (b, t, 0)),
        out_shape=jax.ShapeDtypeStruct((bsz, length, D_MODEL), BF16),
        scratch_shapes=(
            [pltpu.VMEM((nq, HEAD_DIM), BF16)] * N_KV_HEADS
            + [pltpu.VMEM((n_keys, nq), F32)] * N_KV_HEADS
            + [pltpu.VMEM((n_keys, nq), BF16)] * N_KV_HEADS
            + [pltpu.VMEM((BF16_ROWS, nq), F32)] * N_KV_HEADS),
        compiler_params=_params(54),
        name="attention",
    )(q, k, vt_ext)


def kernel(x, c, ctx, c_ctx, ada_w, ada_b, norm_g, mlp_w1, mlp_w2, a_w_in, a_ln_g, a_w_s, a_b_s, a_w_out,
           b_w_out, c_w_qkv, c_q_g, c_k_g, c_w_o):
    bsz = x.shape[0]
    attn_layers = [i for i in range(DEPTH) if i % N_MIXERS == 2]
    last_ctx_read = attn_layers[-1] if attn_layers else -1

    cond = jnp.zeros((COND_ROWS, D_MODEL), F32).at[:bsz].set(c).at[bsz].set(c_ctx)
    mods_all = _ada_all(cond, ada_w, ada_b).reshape(DEPTH, COND_ROWS, 6, D_MODEL)


    for i in range(DEPTH):
        kind, j = i % N_MIXERS, i // N_MIXERS
        ctx_in = i <= last_ctx_read
        ctx_out = i < last_ctx_read
        mods = mods_all[i]
        ng = norm_g[i]

        if kind == 0:
            w_in = a_w_in[j].astype(BF16)
            ln_g = a_ln_g[j].reshape(1, A_WIDTH)
            w_s = a_w_s[j].astype(BF16)
            b_s = a_b_s[j].reshape(A_GROUPS, CHUNK, 1)
            w_post = a_w_out[j].astype(BF16)
            z = _gmlp_head(x, mods, ng, w_in, ln_g, w_s, b_s, False)
            zc = _gmlp_head(ctx, mods, ng, w_in, ln_g, w_s, b_s, True) if ctx_out else None
        elif kind == 1:
            w_post = b_w_out[j].astype(BF16)
            z = _fourier_head_folded(x, mods, ng)
            zc = _fourier_head(ctx, mods, ng, True) if ctx_out else None
        else:
            w_qkv = c_w_qkv[j].astype(BF16)
            w_post = c_w_o[j].astype(BF16)
            q_g = c_q_g[j].reshape(1, HEAD_DIM)
            k_g = c_k_g[j].reshape(1, HEAD_DIM)
            q, k, v = _qkv_latent(x, mods, ng, w_qkv, q_g, k_g)
            if ctx_in:
                kc, vc = _kv_ctx(ctx, mods, ng, w_qkv[:, N_HEADS * HEAD_DIM:], k_g)
                k = jnp.concatenate([kc, k], axis=2)
                v = jnp.concatenate([vc, v], axis=2)
            ones_row = (jnp.arange(BF16_ROWS) == 0).astype(BF16)[:, None]
            pad = jnp.broadcast_to(ones_row, v.shape[:2] + (BF16_ROWS, v.shape[2]))
            vt_ext = jnp.concatenate([jnp.swapaxes(v, 2, 3), pad], axis=2)
            z = _attention(q, k, vt_ext)
            zc = None
            assert not ctx_out

        w1 = mlp_w1[i].astype(BF16)
        w2 = mlp_w2[i].astype(BF16)
        x = _tail(x, z, mods, ng, w_post, w1, w2, False, z_mirrored=(kind == 1))
        if ctx_out:
            ctx = _tail(ctx, zc, mods, ng, w_post, w1, w2, True)

    return x
```

```python
import functools

import jax
import jax.numpy as jnp
from jax import lax
from jax.experimental import pallas as pl
from jax.experimental.pallas import tpu as pltpu

D_MODEL = 1024
DEPTH = 4
GRID_W = 64
N_MIXERS = 3
CHUNK = 128
A_WIDTH = 2 * D_MODEL
A_GROUPS = 8
A_GROUP_W = A_WIDTH // A_GROUPS
B_GROUPS = 4
B_GROUP_W = D_MODEL // B_GROUPS
HEAD_DIM = 128
N_HEADS = D_MODEL // HEAD_DIM
N_KV_HEADS = 2
HEADS_PER_KV = N_HEADS // N_KV_HEADS
ROPE_THETA = 10000.0
D_FF = 4 * D_MODEL
NORM_EPS = 1e-6
LN_EPS = 1e-5
LOG2_E = 1.4426950408889634

F32 = jnp.float32
BF16 = jnp.bfloat16

MXU_COLS = 256
ROW_BLOCK = 64
BF16_ROWS = 16
COND_ROWS = 16
MIB = 1024 * 1024


def _dot(a, b):
    return jnp.dot(a, b, preferred_element_type=F32)


def _rms(x, g):
    ms = jnp.mean(x * x, axis=-1, keepdims=True)
    return x * lax.rsqrt(ms + NORM_EPS) * g


def _gelu_tanh(x):
    c = 2.0 * 0.7978845608028654 * LOG2_E
    z = x * (-c - (c * 0.044715) * (x * x))
    return x / (1.0 + jnp.exp2(z))


def _const_spec(shape):
    zeros = (0,) * len(shape)
    return pl.BlockSpec(shape, lambda b, t: zeros, pipeline_mode=pl.Buffered(1))


def _mod_spec(is_ctx, batch):
    if is_ctx:
        return pl.BlockSpec((None, 6, D_MODEL), lambda b, t: (batch, 0, 0))
    return pl.BlockSpec((None, 6, D_MODEL), lambda b, t: (b, 0, 0))


def _params(vmem_mib):
    return pltpu.CompilerParams(
        dimension_semantics=("arbitrary", "arbitrary"),
        vmem_limit_bytes=vmem_mib * MIB)


def _modulated_norm_to(h_s, x_ref, mod_ref, ng_ref):
    sh1 = mod_ref[0:1, :]
    sc1p = 1.0 + mod_ref[1:2, :]
    g0 = ng_ref[0:1, :]
    for r in range(0, x_ref.shape[0], ROW_BLOCK):
        rows = slice(r, r + ROW_BLOCK)
        h_s[rows, :] = (_rms(x_ref[rows, :], g0) * sc1p + sh1).astype(BF16)


def _ada_kernel(cond_ref, w_ref, b_ref, o_ref):
    a = cond_ref[...]
    a = a * jax.nn.sigmoid(a)
    o_ref[...] = _dot(a.astype(BF16), w_ref[...].astype(BF16)) + b_ref[...]


def _ada_all(cond, ada_w, ada_b):
    tn = 1536
    n_out = 6 * D_MODEL
    return pl.pallas_call(
        _ada_kernel,
        grid=(DEPTH, n_out // tn),
        in_specs=[
            pl.BlockSpec((COND_ROWS, D_MODEL), lambda i, n: (0, 0)),
            pl.BlockSpec((None, D_MODEL, tn), lambda i, n: (i, 0, n)),
            pl.BlockSpec((None, 1, tn), lambda i, n: (i, 0, n)),
        ],
        out_specs=pl.BlockSpec((None, COND_ROWS, tn), lambda i, n: (i, 0, n)),
        out_shape=jax.ShapeDtypeStruct((DEPTH, COND_ROWS, n_out), F32),
        compiler_params=_params(32),
        name="ada_mod",
    )(cond, ada_w, ada_b.reshape(DEPTH, 1, n_out))


def _gmlp_head_kernel(x_ref, mod_ref, ng_ref, win_ref, lng_ref, ws_ref, bs_ref, z_ref,
                      h_s, u_s, v_s, vb_s):
    tm = x_ref.shape[0]
    _modulated_norm_to(h_s, x_ref, mod_ref, ng_ref)
    width = 2 * MXU_COLS
    n_half = A_WIDTH // width
    lng = lng_ref[...]

    def proj(c):
        return _dot(h_s[...], win_ref[:, c * width:(c + 1) * width])

    def gelu_store(d, c):
        dst, c = (u_s, c) if c < n_half else (v_s, c - n_half)
        for r in range(0, tm, ROW_BLOCK):
            dst[r:r + ROW_BLOCK, c * width:(c + 1) * width] = _gelu_tanh(d[r:r + ROW_BLOCK, :])

    def layer_norm_rows(part):
        for r in range(part * tm // n_half, (part + 1) * tm // n_half, ROW_BLOCK // 2):
            rows = slice(r, r + ROW_BLOCK // 2)
            v = v_s[rows, :]
            d = v - jnp.mean(v, axis=-1, keepdims=True)
            var = jnp.mean(d * d, axis=-1, keepdims=True)
            vb_s[rows, :] = (d * lax.rsqrt(var + LN_EPS) * lng).astype(BF16)

    order = list(range(n_half, 2 * n_half)) + list(range(n_half))
    d = proj(order[0])
    for i in range(1, len(order)):
        d_next = proj(order[i])
        gelu_store(d, order[i - 1])
        if i > n_half:
            layer_norm_rows(i - n_half - 1)
        d = d_next
    gelu_store(d, order[-1])
    layer_norm_rows(n_half - 1)
    for g in range(A_GROUPS):
        w = ws_ref[g]
        b = bs_ref[g]
        cols = slice(g * A_GROUP_W, (g + 1) * A_GROUP_W)
        for n in range(tm // CHUNK):
            rows = slice(n * CHUNK, (n + 1) * CHUNK)
            sv = _dot(w, vb_s[rows, cols]) + b
            z_ref[rows, cols] = (u_s[rows, cols] * sv).astype(BF16)


def _gmlp_head(x, mods, ng, w_in, ln_g, w_s, b_s, is_ctx):
    bsz, length, _ = x.shape
    tm = min(512, length)
    return pl.pallas_call(
        _gmlp_head_kernel,
        grid=(bsz, length // tm),
        in_specs=[
            pl.BlockSpec((None, tm, D_MODEL), lambda b, t: (b, t, 0)),
            _mod_spec(is_ctx, bsz),
            _const_spec((4, D_MODEL)),
            _const_spec((D_MODEL, 2 * A_WIDTH)),
            _const_spec((1, A_WIDTH)),
            _const_spec((A_GROUPS, CHUNK, CHUNK)),
            _const_spec((A_GROUPS, CHUNK, 1)),
        ],
        out_specs=pl.BlockSpec((None, tm, A_WIDTH), lambda b, t: (b, t, 0)),
        out_shape=jax.ShapeDtypeStruct((bsz, length, A_WIDTH), BF16),
        scratch_shapes=[
            pltpu.VMEM((tm, D_MODEL), BF16),
            pltpu.VMEM((tm, A_WIDTH), F32),
            pltpu.VMEM((tm, A_WIDTH), F32),
            pltpu.VMEM((tm, A_WIDTH), BF16),
        ],
        compiler_params=_params(48),
        name="gmlp_head",
    )(x, mods, ng, w_in, ln_g, w_s, b_s)


def _interleave(dots, fillers):
    per = -(-len(fillers) // len(dots))
    for i, dot_fn in enumerate(dots):
        dot_fn()
        for filler in fillers[i * per:(i + 1) * per]:
            filler()


def _tail_kernel(x_ref, z_ref, mod_ref, ng_ref, wp_ref, w1_ref, w2_ref, o_ref, *scratch):
    half = x_ref.shape[0] // 2
    y_s, hm_s, hid_s = scratch[0:2], scratch[2:4], scratch[4:6]
    g1 = mod_ref[2:3, :]
    sh2 = mod_ref[3:4, :]
    sc2p = 1.0 + mod_ref[4:5, :]
    g2 = mod_ref[5:6, :]
    ff_cols = 2 * MXU_COLS
    out_chunks = [slice(n * MXU_COLS, (n + 1) * MXU_COLS) for n in range(D_MODEL // MXU_COLS)]
    ff_chunks = [slice(c * ff_cols, (c + 1) * ff_cols) for c in range(D_FF // ff_cols)]
    blocks = list(range(0, half, ROW_BLOCK))

    def post_dot(a, cols):
        y_s[a][:, cols] = _dot(z_ref[a * half:(a + 1) * half, :], wp_ref[:, cols])

    def norm1(a, r):
        rows = slice(a * half + r, a * half + r + ROW_BLOCK)
        x1 = x_ref[rows, :] + g1 * _rms(y_s[a][r:r + ROW_BLOCK, :], ng_ref[1:2, :])
        o_ref[rows, :] = x1
        hm_s[a][r:r + ROW_BLOCK, :] = (_rms(x1, ng_ref[2:3, :]) * sc2p + sh2).astype(BF16)

    def up_dot(a, cols):
        t = jnp.maximum(_dot(hm_s[a][...], w1_ref[:, cols]), 0.0)
        hid_s[a][:, cols] = (t * t).astype(BF16)

    def down_dot(a, cols):
        y_s[a][:, cols] = _dot(hid_s[a][...], w2_ref[:, cols])

    def norm3(a, r):
        rows = slice(a * half + r, a * half + r + ROW_BLOCK)
        o_ref[rows, :] = o_ref[rows, :] + g2 * _rms(y_s[a][r:r + ROW_BLOCK, :], ng_ref[3:4, :])

    def each(fn, a, items):
        return [functools.partial(fn, a, item) for item in items]

    _interleave(each(post_dot, 0, out_chunks), [])
    _interleave(each(post_dot, 1, out_chunks), each(norm1, 0, blocks))
    _interleave(each(up_dot, 0, ff_chunks), each(norm1, 1, blocks))
    _interleave(each(up_dot, 1, ff_chunks), [])
    _interleave(each(down_dot, 0, out_chunks), [])
    _interleave(each(down_dot, 1, out_chunks), each(norm3, 0, blocks))
    _interleave(each(norm3, 1, blocks), [])


def _tail(x, z, mods, ng, w_post, w1_all, w2_all, layer, is_ctx, z_mirrored=False):
    bsz, length, _ = x.shape
    kz = z.shape[-1]
    tm = min(512, length)
    if z_mirrored:
        nt = length // 2 // tm
        z_spec = pl.BlockSpec(
            (None, None, tm, kz),
            lambda b, t: (b, t // nt, jnp.where(t < nt, t, 2 * nt - 1 - t), 0))
    else:
        z_spec = pl.BlockSpec((None, tm, kz), lambda b, t: (b, t, 0))
    return pl.pallas_call(
        _tail_kernel,
        grid=(bsz, length // tm),
        in_specs=[
            pl.BlockSpec((None, tm, D_MODEL), lambda b, t: (b, t, 0)),
            z_spec,
            _mod_spec(is_ctx, bsz),
            _const_spec((4, D_MODEL)),
            _const_spec((kz, D_MODEL)),
            pl.BlockSpec((None, D_MODEL, D_FF), lambda b, t: (layer, 0, 0), pipeline_mode=pl.Buffered(1)),
            pl.BlockSpec((None, D_FF, D_MODEL), lambda b, t: (layer, 0, 0), pipeline_mode=pl.Buffered(1)),
        ],
        out_specs=pl.BlockSpec((None, tm, D_MODEL), lambda b, t: (b, t, 0)),
        out_shape=jax.ShapeDtypeStruct((bsz, length, D_MODEL), F32),
        scratch_shapes=(
            [pltpu.VMEM((tm // 2, D_MODEL), F32)] * 2
            + [pltpu.VMEM((tm // 2, D_MODEL), BF16)] * 2
            + [pltpu.VMEM((tm // 2, D_FF), BF16)] * 2),
        compiler_params=_params(54),
        name="tail",
    )(x, z, mods, ng, w_post, w1_all, w2_all)


def _fourier_chan_kernel(x_ref, mod_ref, ng_ref, cs_ref, ab_ref, h_s):
    _modulated_norm_to(h_s, x_ref, mod_ref, ng_ref)
    for g in range(B_GROUPS):
        cols = slice(g * B_GROUP_W, (g + 1) * B_GROUP_W)
        t = _dot(h_s[:, cols], cs_ref[...])
        ab_ref[0, :, cols] = t[:, :B_GROUP_W].astype(BF16)
        ab_ref[1, :, cols] = t[:, B_GROUP_W:].astype(BF16)


def _fourier_seq_kernel(cs_ref, ab_ref, y_ref):
    for n in range(D_MODEL // MXU_COLS):
        cols = slice(n * MXU_COLS, (n + 1) * MXU_COLS)
        y_ref[:, cols] = _dot(cs_ref[...], ab_ref[:, cols]).astype(BF16)


def _dft_tables(length):
    m = jnp.arange(B_GROUP_W, dtype=jnp.int32)
    ang_c = (2.0 * jnp.pi / B_GROUP_W) * ((m[:, None] * m[None, :]) % B_GROUP_W).astype(F32)
    s_c = B_GROUP_W ** -0.5
    cs_chan = jnp.concatenate([jnp.cos(ang_c), jnp.sin(ang_c)], axis=1) * s_c
    k = jnp.arange(length, dtype=jnp.int32)
    ang_l = (2.0 * jnp.pi / length) * ((k[:, None] * k[None, :]) % length).astype(F32)
    s_l = length ** -0.5
    cs_seq = jnp.concatenate([jnp.cos(ang_l), -jnp.sin(ang_l)], axis=1) * s_l
    return cs_chan.astype(BF16), cs_seq.astype(BF16)


def _fourier_fold_kernel(xp_ref, xm_ref, x0_ref, mod_ref, ng_ref, cc_ref, sc_ref, rev_ref,
                         ab_ref, a0_ref, hp_s, hm_s):
    tm = xm_ref.shape[0]
    sh1 = mod_ref[0:1, :]
    sc1p = 1.0 + mod_ref[1:2, :]
    g0 = ng_ref[0:1, :]
    for r in range(0, tm, ROW_BLOCK):
        rows = slice(r, r + ROW_BLOCK)
        hp_s[rows, :] = _rms(xp_ref[0, r + 1:r + 1 + ROW_BLOCK, :], g0) * sc1p + sh1
        hm_s[rows, :] = (_rms(xm_ref[rows, :], g0) * sc1p + sh1).astype(BF16)
    h0 = (_rms(x0_ref[...], g0) * sc1p + sh1).astype(BF16)
    for g in range(B_GROUPS):
        cols = slice(g * B_GROUP_W, (g + 1) * B_GROUP_W)
        hm_rev = _dot(rev_ref[...], hm_s[:, cols])
        hp = hp_s[:, cols]
        ab_ref[0, :, cols] = _dot((hp + hm_rev).astype(BF16), cc_ref[...]).astype(BF16)
        ab_ref[1, :, cols] = _dot((hp - hm_rev).astype(BF16), sc_ref[...]).astype(BF16)
        a0_ref[:, cols] = _dot(h0[:, cols], cc_ref[...])


def _fourier_seq_fold_kernel(ct_ref, st_ref, ab_ref, a0_ref, perm_ref, y_ref, w_s, *, scale):
    tk = y_ref.shape[1]
    half = ct_ref.shape[1]
    for n in range(D_MODEL // MXU_COLS):
        cols = slice(n * MXU_COLS, (n + 1) * MXU_COLS)
        yc = _dot(ct_ref[...], ab_ref[0:half, cols]) + a0_ref[0:1, cols] * scale
        ys = _dot(st_ref[...], ab_ref[half:2 * half, cols])
        y_ref[0, :, cols] = (yc[:tk] - ys[:tk]).astype(BF16)
        w_s[:, cols] = (yc + ys).astype(BF16)
    for n in range(D_MODEL // MXU_COLS):
        cols = slice(n * MXU_COLS, (n + 1) * MXU_COLS)
        y_ref[1, :, cols] = _dot(perm_ref[...], w_s[:, cols]).astype(BF16)


def _seq_fold_tables(length, tk):
    half = length // 2
    nt = half // tk
    rows = tk + BF16_ROWS
    fine = 64
    n_coarse = -(-(nt * tk + BF16_ROWS) // fine)
    t = jnp.arange(1, half + 1, dtype=jnp.int32)

    def cos_sin(k):
        ang = (2.0 * jnp.pi / length) * ((k[:, None] * t[None, :]) % length).astype(F32)
        return jnp.cos(ang), jnp.sin(ang)

    ch, sh = cos_sin(jnp.arange(n_coarse, dtype=jnp.int32) * fine)
    cl, sl = cos_sin(jnp.arange(fine, dtype=jnp.int32))
    weight = jnp.where(t == half, 0.5, 1.0) * length ** -0.5
    c = (ch[:, None, :] * cl[None] - sh[:, None, :] * sl[None]).reshape(n_coarse * fine, half)
    s = (sh[:, None, :] * cl[None] + ch[:, None, :] * sl[None]).reshape(n_coarse * fine, half)
    ct = jnp.stack([c[i * tk:i * tk + rows] for i in range(nt)]) * weight
    st = jnp.stack([s[i * tk:i * tk + rows] for i in range(nt)]) * weight
    return ct.astype(BF16), st.astype(BF16)


def _fourier_head_folded(x, mods, ng):
    bsz, length, _ = x.shape
    tm = 512
    half = length // 2
    nt = half // tm
    rows = tm + BF16_ROWS
    m = jnp.arange(B_GROUP_W, dtype=jnp.int32)
    ang_c = (2.0 * jnp.pi / B_GROUP_W) * ((m[:, None] * m[None, :]) % B_GROUP_W).astype(F32)
    cc = (jnp.cos(ang_c) * B_GROUP_W ** -0.5).astype(BF16)
    sc = (jnp.sin(ang_c) * B_GROUP_W ** -0.5).astype(BF16)
    rev = (jnp.arange(tm)[:, None] + jnp.arange(tm)[None, :] == tm - 1).astype(BF16)
    ab, a0 = pl.pallas_call(
        _fourier_fold_kernel,
        grid=(bsz, nt),
        in_specs=[
            pl.BlockSpec((pl.Element(1), pl.Element(tm + 8), pl.Element(D_MODEL)),
                         lambda b, t: (b, t * tm, 0)),
            pl.BlockSpec((None, tm, D_MODEL), lambda b, t: (b, 2 * nt - 1 - t, 0)),
            pl.BlockSpec((None, 8, D_MODEL), lambda b, t: (b, 0, 0)),
            _mod_spec(False, bsz),
            _const_spec((4, D_MODEL)),
            _const_spec((B_GROUP_W, B_GROUP_W)),
            _const_spec((B_GROUP_W, B_GROUP_W)),
            _const_spec((tm, tm)),
        ],
        out_specs=[
            pl.BlockSpec((None, 2, tm, D_MODEL), lambda b, t: (b, 0, t, 0)),
            pl.BlockSpec((None, 8, D_MODEL), lambda b, t: (b, 0, 0)),
        ],
        out_shape=[
            jax.ShapeDtypeStruct((bsz, 2, half, D_MODEL), BF16),
            jax.ShapeDtypeStruct((bsz, 8, D_MODEL), F32),
        ],
        scratch_shapes=[pltpu.VMEM((tm, D_MODEL), F32), pltpu.VMEM((tm, D_MODEL), BF16)],
        compiler_params=_params(32),
        name="fourier_fold",
    )(x, x, x, mods, ng, cc, sc, rev)
    ct, st = _seq_fold_tables(length, tm)
    perm = (jnp.arange(tm)[:, None] + jnp.arange(rows)[None, :] == tm).astype(BF16)
    return pl.pallas_call(
        functools.partial(_fourier_seq_fold_kernel, scale=length ** -0.5),
        grid=(bsz, nt),
        in_specs=[
            pl.BlockSpec((None, rows, half), lambda b, t: (t, 0, 0)),
            pl.BlockSpec((None, rows, half), lambda b, t: (t, 0, 0)),
            pl.BlockSpec((None, 2 * half, D_MODEL), lambda b, t: (b, 0, 0),
                         pipeline_mode=pl.Buffered(1)),
            pl.BlockSpec((None, 8, D_MODEL), lambda b, t: (b, 0, 0)),
            _const_spec((tm, rows)),
        ],
        out_specs=pl.BlockSpec((None, 2, tm, D_MODEL), lambda b, t: (b, 0, t, 0)),
        out_shape=jax.ShapeDtypeStruct((bsz, 2, half, D_MODEL), BF16),
        scratch_shapes=[pltpu.VMEM((rows, D_MODEL), BF16)],
        compiler_params=_params(40),
        name="fourier_seq_fold",
    )(ct, st, ab.reshape(bsz, 2 * half, D_MODEL), a0, perm)


def _fourier_head(x, mods, ng, is_ctx):
    bsz, length, _ = x.shape
    tm = min(512, length)
    cs_chan, cs_seq = _dft_tables(length)
    ab = pl.pallas_call(
        _fourier_chan_kernel,
        grid=(bsz, length // tm),
        in_specs=[
            pl.BlockSpec((None, tm, D_MODEL), lambda b, t: (b, t, 0)),
            _mod_spec(is_ctx, bsz),
            _const_spec((4, D_MODEL)),
            _const_spec((B_GROUP_W, 2 * B_GROUP_W)),
        ],
        out_specs=pl.BlockSpec((None, 2, tm, D_MODEL), lambda b, t: (b, 0, t, 0)),
        out_shape=jax.ShapeDtypeStruct((bsz, 2, length, D_MODEL), BF16),
        scratch_shapes=[pltpu.VMEM((tm, D_MODEL), BF16)],
        compiler_params=_params(32),
        name="fourier_chan",
    )(x, mods, ng, cs_chan)
    ab = ab.reshape(bsz, 2 * length, D_MODEL)
    return pl.pallas_call(
        _fourier_seq_kernel,
        grid=(bsz, length // tm),
        in_specs=[
            pl.BlockSpec((tm, 2 * length), lambda b, t: (t, 0)),
            pl.BlockSpec((None, 2 * length, D_MODEL), lambda b, t: (b, 0, 0),
                         pipeline_mode=pl.Buffered(1)),
        ],
        out_specs=pl.BlockSpec((None, tm, D_MODEL), lambda b, t: (b, t, 0)),
        out_shape=jax.ShapeDtypeStruct((bsz, length, D_MODEL), BF16),
        compiler_params=_params(48),
        name="fourier_seq",
    )(cs_seq, ab)


def _head_rms(r, g):
    return r * lax.rsqrt(jnp.mean(r * r, axis=-1, keepdims=True) + NORM_EPS) * g


def _qkv_kernel(x_ref, mod_ref, ng_ref, w_ref, qg_ref, kg_ref, mean_ref, cos_ref, sin_ref,
                q_ref, k_ref, v_ref, h_s):
    _modulated_norm_to(h_s, x_ref, mod_ref, ng_ref)
    cosf = cos_ref[...]
    sinf = sin_ref[...]

    qg = qg_ref[...] * (HEAD_DIM ** -0.5 * LOG2_E)
    gains = [jnp.concatenate([qg, qg], axis=1)] * (N_HEADS // 2)
    gains.append(jnp.concatenate([kg_ref[...], kg_ref[...]], axis=1))
    outs = [(q_ref, 2 * p) for p in range(N_HEADS // 2)] + [(k_ref, 0)]
    n_pairs = len(outs)

    def project(p):
        return _dot(h_s[...], w_ref[:, p * MXU_COLS:(p + 1) * MXU_COLS])

    def normalise(t, p):
        sq = t * t
        hi = sq.astype(BF16)
        lo = (sq - hi.astype(F32)).astype(BF16)
        ms = _dot(hi, mean_ref[...]) + _dot(lo, mean_ref[...])
        return t * lax.rsqrt(ms + NORM_EPS) * gains[p]

    def rope_store(rn, p):
        ref, first = outs[p]
        for e in range(2):
            r = rn[:, e * HEAD_DIM:(e + 1) * HEAD_DIM]
            ref[first + e] = (r * cosf + pltpu.roll(r, HEAD_DIM // 2, 1) * sinf).astype(BF16)

    t_next = project(0)
    rn_prev = None
    for p in range(n_pairs + 1):
        t = t_next
        if p < n_pairs:
            t_next = project(p + 1)
        rn = normalise(t, p) if p < n_pairs else None
        if rn_prev is not None:
            rope_store(rn_prev, p - 1)
        rn_prev = rn
    for e in range(N_KV_HEADS):
        v_ref[e] = t[:, e * HEAD_DIM:(e + 1) * HEAD_DIM].astype(BF16)


def _kv_ctx_kernel(x_ref, mod_ref, ng_ref, w_ref, kg_ref, k_ref, v_ref, h_s):
    _modulated_norm_to(h_s, x_ref, mod_ref, ng_ref)
    t = _dot(h_s[...], w_ref[:, 0:MXU_COLS])
    for e in range(N_KV_HEADS):
        k_ref[e] = _head_rms(t[:, e * HEAD_DIM:(e + 1) * HEAD_DIM], kg_ref[...]).astype(BF16)
    t = _dot(h_s[...], w_ref[:, MXU_COLS:2 * MXU_COLS])
    for e in range(N_KV_HEADS):
        v_ref[e] = t[:, e * HEAD_DIM:(e + 1) * HEAD_DIM].astype(BF16)


def _rope_tables(length):
    t = jnp.arange(length)
    row = (t // GRID_W).astype(F32)
    col = (t % GRID_W).astype(F32)
    n_freq = HEAD_DIM // 4
    inv = ROPE_THETA ** (-jnp.arange(n_freq, dtype=F32) / n_freq)
    ang = jnp.concatenate([row[:, None] * inv, col[:, None] * inv], axis=-1)
    cos, sin = jnp.cos(ang), jnp.sin(ang)
    return jnp.concatenate([cos, cos], axis=-1), jnp.concatenate([-sin, sin], axis=-1)


def _qkv_latent(x, mods, ng, w_qkv, q_g, k_g):
    bsz, length, _ = x.shape
    tm = 512
    cosf, sinf = _rope_tables(length)
    n_qkv = w_qkv.shape[-1]
    lane_head = jnp.arange(MXU_COLS) // HEAD_DIM
    head_mean = ((lane_head[:, None] == lane_head[None, :]) * (1.0 / HEAD_DIM)).astype(BF16)
    kv_shape = jax.ShapeDtypeStruct((bsz, N_KV_HEADS, length, HEAD_DIM), BF16)
    kv_spec = pl.BlockSpec((None, N_KV_HEADS, tm, HEAD_DIM), lambda b, t: (b, 0, t, 0))
    return pl.pallas_call(
        _qkv_kernel,
        grid=(bsz, length // tm),
        in_specs=[
            pl.BlockSpec((None, tm, D_MODEL), lambda b, t: (b, t, 0)),
            _mod_spec(False, bsz),
            _const_spec((4, D_MODEL)),
            _const_spec((D_MODEL, n_qkv)),
            _const_spec((1, HEAD_DIM)),
            _const_spec((1, HEAD_DIM)),
            _const_spec((MXU_COLS, MXU_COLS)),
            pl.BlockSpec((tm, HEAD_DIM), lambda b, t: (t, 0)),
            pl.BlockSpec((tm, HEAD_DIM), lambda b, t: (t, 0)),
        ],
        out_specs=[
            pl.BlockSpec((None, N_HEADS, tm, HEAD_DIM), lambda b, t: (b, 0, t, 0)),
            kv_spec, kv_spec,
        ],
        out_shape=[
            jax.ShapeDtypeStruct((bsz, N_HEADS, length, HEAD_DIM), BF16),
            kv_shape, kv_shape,
        ],
        scratch_shapes=[pltpu.VMEM((tm, D_MODEL), BF16)],
        compiler_params=_params(32),
        name="qkv_latent",
    )(x, mods, ng, w_qkv, q_g, k_g, head_mean, cosf, sinf)


def _kv_ctx(ctx, mods, ng, w_kv, k_g):
    bsz, length, _ = ctx.shape
    tm = length
    kv_shape = jax.ShapeDtypeStruct((bsz, N_KV_HEADS, length, HEAD_DIM), BF16)
    kv_spec = pl.BlockSpec((None, N_KV_HEADS, tm, HEAD_DIM), lambda b, t: (b, 0, t, 0))
    return pl.pallas_call(
        _kv_ctx_kernel,
        grid=(bsz, length // tm),
        in_specs=[
            pl.BlockSpec((None, tm, D_MODEL), lambda b, t: (b, t, 0)),
            _mod_spec(True, bsz),
            _const_spec((4, D_MODEL)),
            _const_spec((D_MODEL, 2 * MXU_COLS)),
            _const_spec((1, HEAD_DIM)),
        ],
        out_specs=[kv_spec, kv_spec],
        out_shape=[kv_shape, kv_shape],
        scratch_shapes=[pltpu.VMEM((tm, D_MODEL), BF16)],
        compiler_params=_params(32),
        name="kv_ctx",
    )(ctx, mods, ng, w_kv, k_g)


def _region(index, fn):
    pl.when(pl.program_id(0) >= -index)(fn)


def _attn_kernel(q_ref, k_ref, vt_ref, o_ref, *scratch):
    tq = q_ref.shape[1]
    n_keys = k_ref.shape[1]
    nq = HEADS_PER_KV * tq
    q4_s, s_s, p_s, m_s = (scratch[i * N_KV_HEADS:(i + 1) * N_KV_HEADS] for i in range(4))
    key_chunks = [slice(r, r + MXU_COLS) for r in range(0, n_keys, MXU_COLS)]

    @pl.when(pl.program_id(0) == 0)
    def _():
        s_s[1][...] = jnp.zeros(s_s[1].shape, F32)
        m_s[1][...] = jnp.zeros(m_s[1].shape, F32)
        p_s[0][...] = jnp.ones(p_s[0].shape, BF16)

    def stage(j_scores, j_probs, j_pv):
        for h in range(HEADS_PER_KV):
            q4_s[j_scores][h * tq:(h + 1) * tq, :] = q_ref[HEADS_PER_KV * j_scores + h]
        m_probs = m_s[j_probs][...]
        m = None
        acc = None
        for rows in key_chunks:
            s = lax.dot_general(k_ref[j_scores, rows, :], q4_s[j_scores][...],
                                (((1,), (1,)), ((), ())), preferred_element_type=F32)
            s_s[j_scores][rows, :] = s
            cm = jnp.max(s, axis=0, keepdims=True)
            m = cm if m is None else jnp.maximum(m, cm)
            d = _dot(vt_ref[j_pv, :, rows], p_s[j_pv][rows, :])
            acc = d if acc is None else acc + d
            for r in range(rows.start, rows.stop, BF16_ROWS):
                tile = slice(r, r + BF16_ROWS)
                p_s[j_probs][tile, :] = jnp.exp2(s_s[j_probs][tile, :] - m_probs).astype(BF16)
        m_s[j_scores][...] = jnp.broadcast_to(m, (BF16_ROWS, nq))
        o = acc[:HEAD_DIM, :] / acc[HEAD_DIM:HEAD_DIM + 1, :]
        for h in range(HEADS_PER_KV):
            head = HEADS_PER_KV * j_pv + h
            o_ref[:, head * HEAD_DIM:(head + 1) * HEAD_DIM] = o[:, h * tq:(h + 1) * tq].T.astype(BF16)

    _region(0, lambda: stage(0, 1, 0))
    _region(1, lambda: stage(1, 0, 1))


def _attention(q, k, vt_ext):
    bsz, _, length, _ = q.shape
    n_keys = k.shape[2]
    vt_rows = vt_ext.shape[2]
    tq = 128
    nq = HEADS_PER_KV * tq
    per_batch = length // tq
    n_tiles = bsz * per_batch

    def cur(i):
        return jnp.minimum(i, n_tiles - 1)

    def prev(i):
        return jnp.maximum(i - 1, 0)

    return pl.pallas_call(
        _attn_kernel,
        grid=(n_tiles + 1,),
        in_specs=[
            pl.BlockSpec((None, N_HEADS, tq, HEAD_DIM),
                         lambda i: (cur(i) // per_batch, 0, cur(i) % per_batch, 0)),
            pl.BlockSpec((None, N_KV_HEADS, n_keys, HEAD_DIM), lambda i: (cur(i) // per_batch, 0, 0, 0)),
            pl.BlockSpec((None, N_KV_HEADS, vt_rows, n_keys), lambda i: (prev(i) // per_batch, 0, 0, 0)),
        ],
        out_specs=pl.BlockSpec((None, tq, D_MODEL),
                               lambda i: (prev(i) // per_batch, prev(i) % per_batch, 0)),
        out_shape=jax.ShapeDtypeStruct((bsz, length, D_MODEL), BF16),
        scratch_shapes=(
            [pltpu.VMEM((nq, HEAD_DIM), BF16)] * N_KV_HEADS
            + [pltpu.VMEM((n_keys, nq), F32)] * N_KV_HEADS
            + [pltpu.VMEM((n_keys, nq), BF16)] * N_KV_HEADS
            + [pltpu.VMEM((BF16_ROWS, nq), F32)] * N_KV_HEADS),
        compiler_params=pltpu.CompilerParams(
            dimension_semantics=("arbitrary",), vmem_limit_bytes=54 * MIB),
        name="attention",
    )(q, k, vt_ext)


def kernel(x, c, ctx, c_ctx, ada_w, ada_b, norm_g, mlp_w1, mlp_w2, a_w_in, a_ln_g, a_w_s, a_b_s, a_w_out,
           b_w_out, c_w_qkv, c_q_g, c_k_g, c_w_o):
    bsz = x.shape[0]
    attn_layers = [i for i in range(DEPTH) if i % N_MIXERS == 2]
    last_ctx_read = attn_layers[-1] if attn_layers else -1

    cond = jnp.zeros((COND_ROWS, D_MODEL), F32).at[:bsz].set(c).at[bsz].set(c_ctx)
    mods_all = _ada_all(cond, ada_w, ada_b).reshape(DEPTH, COND_ROWS, 6, D_MODEL)
    w1_all = mlp_w1.astype(BF16)
    w2_all = mlp_w2.astype(BF16)


    for i in range(DEPTH):
        kind, j = i % N_MIXERS, i // N_MIXERS
        ctx_in = i <= last_ctx_read
        ctx_out = i < last_ctx_read
        mods = mods_all[i]
        ng = norm_g[i]

        if kind == 0:
            w_in = a_w_in[j].astype(BF16)
            ln_g = a_ln_g[j].reshape(1, A_WIDTH)
            w_s = a_w_s[j].astype(BF16)
            b_s = a_b_s[j].reshape(A_GROUPS, CHUNK, 1)
            w_post = a_w_out[j].astype(BF16)
            z = _gmlp_head(x, mods, ng, w_in, ln_g, w_s, b_s, False)
            zc = _gmlp_head(ctx, mods, ng, w_in, ln_g, w_s, b_s, True) if ctx_out else None
        elif kind == 1:
            w_post = b_w_out[j].astype(BF16)
            z = _fourier_head_folded(x, mods, ng)
            zc = _fourier_head(ctx, mods, ng, True) if ctx_out else None
        else:
            w_qkv = c_w_qkv[j].astype(BF16)
            w_post = c_w_o[j].astype(BF16)
            q_g = c_q_g[j].reshape(1, HEAD_DIM)
            k_g = c_k_g[j].reshape(1, HEAD_DIM)
            q, k, v = _qkv_latent(x, mods, ng, w_qkv, q_g, k_g)
            if ctx_in:
                kc, vc = _kv_ctx(ctx, mods, ng, w_qkv[:, N_HEADS * HEAD_DIM:], k_g)
                k = jnp.concatenate([kc, k], axis=2)
                v = jnp.concatenate([vc, v], axis=2)
            ones_row = (jnp.arange(BF16_ROWS) == 0).astype(BF16)[:, None]
            pad = jnp.broadcast_to(ones_row, v.shape[:2] + (BF16_ROWS, v.shape[2]))
            vt_ext = jnp.concatenate([jnp.swapaxes(v, 2, 3), pad], axis=2)
            z = _attention(q, k, vt_ext)
            zc = None
            assert not ctx_out

        x = _tail(x, z, mods, ng, w_post, w1_all, w2_all, i, False, z_mirrored=(kind == 1))
        if ctx_out:
            ctx = _tail(ctx, zc, mods, ng, w_post, w1_all, w2_all, i, True)

    return x
```

```python
import functools

import jax
import jax.numpy as jnp
from jax import lax
from jax.experimental import pallas as pl
from jax.experimental.pallas import tpu as pltpu

D_MODEL = 1024
DEPTH = 4
GRID_W = 64
N_MIXERS = 3
CHUNK = 128
A_WIDTH = 2 * D_MODEL
A_GROUPS = 8
A_GROUP_W = A_WIDTH // A_GROUPS
B_GROUPS = 4
B_GROUP_W = D_MODEL // B_GROUPS
HEAD_DIM = 128
N_HEADS = D_MODEL // HEAD_DIM
N_KV_HEADS = 2
HEADS_PER_KV = N_HEADS // N_KV_HEADS
ROPE_THETA = 10000.0
D_FF = 4 * D_MODEL
NORM_EPS = 1e-6
LN_EPS = 1e-5
LOG2_E = 1.4426950408889634

F32 = jnp.float32
BF16 = jnp.bfloat16

MXU_COLS = 256
ROW_BLOCK = 64
BF16_ROWS = 16
LANES = 128
COND_ROWS = 16
MIB = 1024 * 1024


def _dot(a, b):
    return jnp.dot(a, b, preferred_element_type=F32)


def _rms(x, g):
    ms = jnp.mean(x * x, axis=-1, keepdims=True)
    return x * lax.rsqrt(ms + NORM_EPS) * g


def _gelu_tanh(x):
    c = 2.0 * 0.7978845608028654 * LOG2_E
    z = x * (-c - (c * 0.044715) * (x * x))
    return x / (1.0 + jnp.exp2(z))


def _const_spec(shape):
    zeros = (0,) * len(shape)
    return pl.BlockSpec(shape, lambda b, t: zeros, pipeline_mode=pl.Buffered(1))


def _mod_spec(is_ctx, batch):
    if is_ctx:
        return pl.BlockSpec((None, 6, D_MODEL), lambda b, t: (batch, 0, 0))
    return pl.BlockSpec((None, 6, D_MODEL), lambda b, t: (b, 0, 0))


def _params(vmem_mib):
    return pltpu.CompilerParams(
        dimension_semantics=("arbitrary", "arbitrary"),
        vmem_limit_bytes=vmem_mib * MIB)


def _modulated_norm_to(h_s, x_ref, mod_ref, ng_ref, start=0, stop=None):
    sh1 = mod_ref[0:1, :]
    sc1p = 1.0 + mod_ref[1:2, :]
    g0 = ng_ref[0:1, :]
    stop = x_ref.shape[0] if stop is None else stop
    for r in range(start, stop, ROW_BLOCK):
        rows = slice(r, r + ROW_BLOCK)
        h_s[rows, :] = (_rms(x_ref[rows, :], g0) * sc1p + sh1).astype(BF16)


def _ada_kernel(cond_ref, w_ref, b_ref, o_ref):
    a = cond_ref[...]
    a = a * jax.nn.sigmoid(a)
    o_ref[...] = _dot(a.astype(BF16), w_ref[...].astype(BF16)) + b_ref[...]


def _ada_all(cond, ada_w, ada_b):
    tn = 1536
    n_out = 6 * D_MODEL
    return pl.pallas_call(
        _ada_kernel,
        grid=(DEPTH, n_out // tn),
        in_specs=[
            pl.BlockSpec((COND_ROWS, D_MODEL), lambda i, n: (0, 0)),
            pl.BlockSpec((None, D_MODEL, tn), lambda i, n: (i, 0, n)),
            pl.BlockSpec((None, 1, tn), lambda i, n: (i, 0, n)),
        ],
        out_specs=pl.BlockSpec((None, COND_ROWS, tn), lambda i, n: (i, 0, n)),
        out_shape=jax.ShapeDtypeStruct((DEPTH, COND_ROWS, n_out), F32),
        compiler_params=_params(32),
        name="ada_mod",
    )(cond, ada_w, ada_b.reshape(DEPTH, 1, n_out))


def _gmlp_head_kernel(x_ref, mod_ref, ng_ref, win_ref, lng_ref, ws_ref, bs_ref, z_ref,
                      h_s, u_s, v_s, vb_s, sum_s, sq_s):
    tm = x_ref.shape[0]
    width = 2 * MXU_COLS
    n_half = A_WIDTH // width
    lng = lng_ref[...]
    half = tm // 2
    lanes = sum_s.shape[1]

    def proj(unit):
        c, a = unit
        return _dot(h_s[a * half:(a + 1) * half, :], win_ref[:, c * width:(c + 1) * width])

    def gelu_store(d, unit):
        c, a = unit
        is_v = c >= n_half
        dst, c = (v_s, c - n_half) if is_v else (u_s, c)
        for r in range(0, half, ROW_BLOCK):
            rows = slice(a * half + r, a * half + r + ROW_BLOCK)
            g = _gelu_tanh(d[r:r + ROW_BLOCK, :])
            dst[rows, c * width:(c + 1) * width] = g
            if is_v:
                parts = [g[:, i:i + lanes] for i in range(0, width, lanes)]
                p1 = sum(parts[1:], parts[0])
                p2 = sum([p * p for p in parts[1:]], parts[0] * parts[0])
                if c == 0:
                    sum_s[rows, :] = p1
                    sq_s[rows, :] = p2
                else:
                    sum_s[rows, :] += p1
                    sq_s[rows, :] += p2

    def layer_norm_rows(part):
        for r in range(part * tm // n_half, (part + 1) * tm // n_half, ROW_BLOCK // 2):
            rows = slice(r, r + ROW_BLOCK // 2)
            mu = jnp.sum(sum_s[rows, :], axis=-1, keepdims=True) * (1.0 / A_WIDTH)
            ex2 = jnp.sum(sq_s[rows, :], axis=-1, keepdims=True) * (1.0 / A_WIDTH)
            rstd = lax.rsqrt(ex2 - mu * mu + LN_EPS)
            vb_s[rows, :] = ((v_s[rows, :] - mu) * rstd * lng).astype(BF16)

    chunks = list(range(n_half, 2 * n_half)) + list(range(n_half))
    order = [(c, a) for c in chunks for a in range(2)]
    _modulated_norm_to(h_s, x_ref, mod_ref, ng_ref, 0, half)
    d = proj(order[0])
    _modulated_norm_to(h_s, x_ref, mod_ref, ng_ref, half, tm)
    for i in range(1, len(order)):
        d_next = proj(order[i])
        gelu_store(d, order[i - 1])
        if i > 2 * n_half and i % 2 == 0:
            layer_norm_rows((i - 2 * n_half) // 2 - 1)
        d = d_next
    gelu_store(d, order[-1])
    layer_norm_rows(n_half - 1)
    for g in range(A_GROUPS):
        w = ws_ref[g]
        b = bs_ref[g]
        cols = slice(g * A_GROUP_W, (g + 1) * A_GROUP_W)
        for n in range(tm // CHUNK):
            rows = slice(n * CHUNK, (n + 1) * CHUNK)
            sv = _dot(w, vb_s[rows, cols]) + b
            z_ref[rows, cols] = (u_s[rows, cols] * sv).astype(BF16)


def _gmlp_head(x, mods, ng, w_in, ln_g, w_s, b_s, is_ctx):
    bsz, length, _ = x.shape
    tm = min(1024, length)
    return pl.pallas_call(
        _gmlp_head_kernel,
        grid=(bsz, length // tm),
        in_specs=[
            pl.BlockSpec((None, tm, D_MODEL), lambda b, t: (b, t, 0)),
            _mod_spec(is_ctx, bsz),
            _const_spec((4, D_MODEL)),
            _const_spec((D_MODEL, 2 * A_WIDTH)),
            _const_spec((1, A_WIDTH)),
            _const_spec((A_GROUPS, CHUNK, CHUNK)),
            _const_spec((A_GROUPS, CHUNK, 1)),
        ],
        out_specs=pl.BlockSpec((None, tm, A_WIDTH), lambda b, t: (b, t, 0)),
        out_shape=jax.ShapeDtypeStruct((bsz, length, A_WIDTH), BF16),
        scratch_shapes=[
            pltpu.VMEM((tm, D_MODEL), BF16),
            pltpu.VMEM((tm, A_WIDTH), F32),
            pltpu.VMEM((tm, A_WIDTH), F32),
            pltpu.VMEM((tm, A_WIDTH), BF16),
            pltpu.VMEM((tm, LANES), F32),
            pltpu.VMEM((tm, LANES), F32),
        ],
        compiler_params=_params(56),
        name="gmlp_head",
    )(x, mods, ng, w_in, ln_g, w_s, b_s)


def _interleave(dots, fillers):
    per = -(-len(fillers) // len(dots))
    for i, dot_fn in enumerate(dots):
        dot_fn()
        for filler in fillers[i * per:(i + 1) * per]:
            filler()


def _tail_kernel(x_ref, z_ref, mod_ref, ng_ref, wp_ref, w1_ref, w2_ref, o_ref, *scratch):
    half = x_ref.shape[0] // 2
    y_s, hm_s, hid_s = scratch[0:2], scratch[2:4], scratch[4:6]
    g1 = mod_ref[2:3, :]
    sh2 = mod_ref[3:4, :]
    sc2p = 1.0 + mod_ref[4:5, :]
    g2 = mod_ref[5:6, :]
    ff_cols = 2 * MXU_COLS
    out_chunks = [slice(n * MXU_COLS, (n + 1) * MXU_COLS) for n in range(D_MODEL // MXU_COLS)]
    ff_chunks = [slice(c * ff_cols, (c + 1) * ff_cols) for c in range(D_FF // ff_cols)]
    blocks = list(range(0, half, ROW_BLOCK))

    def post_dot(a, cols):
        y_s[a][:, cols] = _dot(z_ref[a * half:(a + 1) * half, :], wp_ref[:, cols])

    def norm1(a, r):
        rows = slice(a * half + r, a * half + r + ROW_BLOCK)
        x1 = x_ref[rows, :] + g1 * _rms(y_s[a][r:r + ROW_BLOCK, :], ng_ref[1:2, :])
        o_ref[rows, :] = x1
        hm_s[a][r:r + ROW_BLOCK, :] = (_rms(x1, ng_ref[2:3, :]) * sc2p + sh2).astype(BF16)

    def up_dot(a, cols):
        t = jnp.maximum(_dot(hm_s[a][...], w1_ref[:, cols]), 0.0)
        hid_s[a][:, cols] = (t * t).astype(BF16)

    def down_dot(a, cols):
        y_s[a][:, cols] = _dot(hid_s[a][...], w2_ref[:, cols])

    def norm3(a, r):
        rows = slice(a * half + r, a * half + r + ROW_BLOCK)
        o_ref[rows, :] = o_ref[rows, :] + g2 * _rms(y_s[a][r:r + ROW_BLOCK, :], ng_ref[3:4, :])

    def each(fn, a, items):
        return [functools.partial(fn, a, item) for item in items]

    _interleave(each(post_dot, 0, out_chunks), [])
    _interleave(each(post_dot, 1, out_chunks), each(norm1, 0, blocks))
    _interleave(each(up_dot, 0, ff_chunks), each(norm1, 1, blocks))
    _interleave(each(up_dot, 1, ff_chunks), [])
    _interleave(each(down_dot, 0, out_chunks), [])
    _interleave(each(down_dot, 1, out_chunks), each(norm3, 0, blocks))
    _interleave(each(norm3, 1, blocks), [])


def _tail(x, z, mods, ng, w_post, w1_all, w2_all, layer, is_ctx, z_mirrored=False):
    bsz, length, _ = x.shape
    kz = z.shape[-1]
    tm = min(512, length)
    if z_mirrored:
        nt = length // 2 // tm
        z_spec = pl.BlockSpec(
            (None, None, tm, kz),
            lambda b, t: (b, t // nt, jnp.where(t < nt, t, 2 * nt - 1 - t), 0))
    else:
        z_spec = pl.BlockSpec((None, tm, kz), lambda b, t: (b, t, 0))
    return pl.pallas_call(
        _tail_kernel,
        grid=(bsz, length // tm),
        in_specs=[
            pl.BlockSpec((None, tm, D_MODEL), lambda b, t: (b, t, 0)),
            z_spec,
            _mod_spec(is_ctx, bsz),
            _const_spec((4, D_MODEL)),
            _const_spec((kz, D_MODEL)),
            pl.BlockSpec((None, D_MODEL, D_FF), lambda b, t: (layer, 0, 0), pipeline_mode=pl.Buffered(1)),
            pl.BlockSpec((None, D_FF, D_MODEL), lambda b, t: (layer, 0, 0), pipeline_mode=pl.Buffered(1)),
        ],
        out_specs=pl.BlockSpec((None, tm, D_MODEL), lambda b, t: (b, t, 0)),
        out_shape=jax.ShapeDtypeStruct((bsz, length, D_MODEL), F32),
        scratch_shapes=(
            [pltpu.VMEM((tm // 2, D_MODEL), F32)] * 2
            + [pltpu.VMEM((tm // 2, D_MODEL), BF16)] * 2
            + [pltpu.VMEM((tm // 2, D_FF), BF16)] * 2),
        compiler_params=_params(54),
        name="tail",
    )(x, z, mods, ng, w_post, w1_all, w2_all)


def _fourier_chan_kernel(x_ref, mod_ref, ng_ref, cs_ref, ab_ref, h_s):
    _modulated_norm_to(h_s, x_ref, mod_ref, ng_ref)
    for g in range(B_GROUPS):
        cols = slice(g * B_GROUP_W, (g + 1) * B_GROUP_W)
        t = _dot(h_s[:, cols], cs_ref[...])
        ab_ref[0, :, cols] = t[:, :B_GROUP_W].astype(BF16)
        ab_ref[1, :, cols] = t[:, B_GROUP_W:].astype(BF16)


def _fourier_seq_kernel(cs_ref, ab_ref, y_ref):
    for n in range(D_MODEL // MXU_COLS):
        cols = slice(n * MXU_COLS, (n + 1) * MXU_COLS)
        y_ref[:, cols] = _dot(cs_ref[...], ab_ref[:, cols]).astype(BF16)


def _dft_tables(length):
    m = jnp.arange(B_GROUP_W, dtype=jnp.int32)
    ang_c = (2.0 * jnp.pi / B_GROUP_W) * ((m[:, None] * m[None, :]) % B_GROUP_W).astype(F32)
    s_c = B_GROUP_W ** -0.5
    cs_chan = jnp.concatenate([jnp.cos(ang_c), jnp.sin(ang_c)], axis=1) * s_c
    k = jnp.arange(length, dtype=jnp.int32)
    ang_l = (2.0 * jnp.pi / length) * ((k[:, None] * k[None, :]) % length).astype(F32)
    s_l = length ** -0.5
    cs_seq = jnp.concatenate([jnp.cos(ang_l), -jnp.sin(ang_l)], axis=1) * s_l
    return cs_chan.astype(BF16), cs_seq.astype(BF16)


def _fourier_fold_kernel(xp_ref, xm_ref, x0_ref, mod_ref, ng_ref, cc_ref, sc_ref, rev_ref,
                         ab_ref, a0_ref, hp_s, hm_s):
    tm = xm_ref.shape[0]
    sh1 = mod_ref[0:1, :]
    sc1p = 1.0 + mod_ref[1:2, :]
    g0 = ng_ref[0:1, :]
    for r in range(0, tm, ROW_BLOCK):
        rows = slice(r, r + ROW_BLOCK)
        hp_s[rows, :] = _rms(xp_ref[0, r + 1:r + 1 + ROW_BLOCK, :], g0) * sc1p + sh1
        hm_s[rows, :] = (_rms(xm_ref[rows, :], g0) * sc1p + sh1).astype(BF16)
    h0 = (_rms(x0_ref[...], g0) * sc1p + sh1).astype(BF16)
    for g in range(B_GROUPS):
        cols = slice(g * B_GROUP_W, (g + 1) * B_GROUP_W)
        hm_rev = _dot(rev_ref[...], hm_s[:, cols])
        hp = hp_s[:, cols]
        ab_ref[0, :, cols] = _dot((hp + hm_rev).astype(BF16), cc_ref[...]).astype(BF16)
        ab_ref[1, :, cols] = _dot((hp - hm_rev).astype(BF16), sc_ref[...]).astype(BF16)
        a0_ref[:, cols] = _dot(h0[:, cols], cc_ref[...])


def _fourier_seq_fold_kernel(ct_ref, st_ref, ab_ref, a0_ref, perm_ref, y_ref, w_s, *, scale):
    tk = y_ref.shape[1]
    half = ct_ref.shape[1]
    for n in range(D_MODEL // MXU_COLS):
        cols = slice(n * MXU_COLS, (n + 1) * MXU_COLS)
        yc = _dot(ct_ref[...], ab_ref[0:half, cols]) + a0_ref[0:1, cols] * scale
        ys = _dot(st_ref[...], ab_ref[half:2 * half, cols])
        y_ref[0, :, cols] = (yc[:tk] - ys[:tk]).astype(BF16)
        w_s[:, cols] = (yc + ys).astype(BF16)
    for n in range(D_MODEL // MXU_COLS):
        cols = slice(n * MXU_COLS, (n + 1) * MXU_COLS)
        y_ref[1, :, cols] = _dot(perm_ref[...], w_s[:, cols]).astype(BF16)


def _seq_fold_tables(length, tk):
    half = length // 2
    nt = half // tk
    rows = tk + BF16_ROWS
    fine = 64
    n_coarse = -(-(nt * tk + BF16_ROWS) // fine)
    t = jnp.arange(1, half + 1, dtype=jnp.int32)

    def cos_sin(k):
        ang = (2.0 * jnp.pi / length) * ((k[:, None] * t[None, :]) % length).astype(F32)
        return jnp.cos(ang), jnp.sin(ang)

    ch, sh = cos_sin(jnp.arange(n_coarse, dtype=jnp.int32) * fine)
    cl, sl = cos_sin(jnp.arange(fine, dtype=jnp.int32))
    weight = jnp.where(t == half, 0.5, 1.0) * length ** -0.5
    c = (ch[:, None, :] * cl[None] - sh[:, None, :] * sl[None]).reshape(n_coarse * fine, half)
    s = (sh[:, None, :] * cl[None] + ch[:, None, :] * sl[None]).reshape(n_coarse * fine, half)
    ct = jnp.stack([c[i * tk:i * tk + rows] for i in range(nt)]) * weight
    st = jnp.stack([s[i * tk:i * tk + rows] for i in range(nt)]) * weight
    return ct.astype(BF16), st.astype(BF16)


def _fourier_head_folded(x, mods, ng):
    bsz, length, _ = x.shape
    tm = 512
    half = length // 2
    nt = half // tm
    rows = tm + BF16_ROWS
    m = jnp.arange(B_GROUP_W, dtype=jnp.int32)
    ang_c = (2.0 * jnp.pi / B_GROUP_W) * ((m[:, None] * m[None, :]) % B_GROUP_W).astype(F32)
    cc = (jnp.cos(ang_c) * B_GROUP_W ** -0.5).astype(BF16)
    sc = (jnp.sin(ang_c) * B_GROUP_W ** -0.5).astype(BF16)
    rev = (jnp.arange(tm)[:, None] + jnp.arange(tm)[None, :] == tm - 1).astype(BF16)
    ab, a0 = pl.pallas_call(
        _fourier_fold_kernel,
        grid=(bsz, nt),
        in_specs=[
            pl.BlockSpec((pl.Element(1), pl.Element(tm + 8), pl.Element(D_MODEL)),
                         lambda b, t: (b, t * tm, 0)),
            pl.BlockSpec((None, tm, D_MODEL), lambda b, t: (b, 2 * nt - 1 - t, 0)),
            pl.BlockSpec((None, 8, D_MODEL), lambda b, t: (b, 0, 0)),
            _mod_spec(False, bsz),
            _const_spec((4, D_MODEL)),
            _const_spec((B_GROUP_W, B_GROUP_W)),
            _const_spec((B_GROUP_W, B_GROUP_W)),
            _const_spec((tm, tm)),
        ],
        out_specs=[
            pl.BlockSpec((None, 2, tm, D_MODEL), lambda b, t: (b, 0, t, 0)),
            pl.BlockSpec((None, 8, D_MODEL), lambda b, t: (b, 0, 0)),
        ],
        out_shape=[
            jax.ShapeDtypeStruct((bsz, 2, half, D_MODEL), BF16),
            jax.ShapeDtypeStruct((bsz, 8, D_MODEL), F32),
        ],
        scratch_shapes=[pltpu.VMEM((tm, D_MODEL), F32), pltpu.VMEM((tm, D_MODEL), BF16)],
        compiler_params=_params(32),
        name="fourier_fold",
    )(x, x, x, mods, ng, cc, sc, rev)
    ct, st = _seq_fold_tables(length, tm)
    perm = (jnp.arange(tm)[:, None] + jnp.arange(rows)[None, :] == tm).astype(BF16)
    return pl.pallas_call(
        functools.partial(_fourier_seq_fold_kernel, scale=length ** -0.5),
        grid=(bsz, nt),
        in_specs=[
            pl.BlockSpec((None, rows, half), lambda b, t: (t, 0, 0)),
            pl.BlockSpec((None, rows, half), lambda b, t: (t, 0, 0)),
            pl.BlockSpec((None, 2 * half, D_MODEL), lambda b, t: (b, 0, 0)),
            pl.BlockSpec((None, 8, D_MODEL), lambda b, t: (b, 0, 0)),
            _const_spec((tm, rows)),
        ],
        out_specs=pl.BlockSpec((None, 2, tm, D_MODEL), lambda b, t: (b, 0, t, 0)),
        out_shape=jax.ShapeDtypeStruct((bsz, 2, half, D_MODEL), BF16),
        scratch_shapes=[pltpu.VMEM((rows, D_MODEL), BF16)],
        compiler_params=_params(40),
        name="fourier_seq_fold",
    )(ct, st, ab.reshape(bsz, 2 * half, D_MODEL), a0, perm)


def _fourier_head(x, mods, ng, is_ctx):
    bsz, length, _ = x.shape
    tm = min(512, length)
    cs_chan, cs_seq = _dft_tables(length)
    ab = pl.pallas_call(
        _fourier_chan_kernel,
        grid=(bsz, length // tm),
        in_specs=[
            pl.BlockSpec((None, tm, D_MODEL), lambda b, t: (b, t, 0)),
            _mod_spec(is_ctx, bsz),
            _const_spec((4, D_MODEL)),
            _const_spec((B_GROUP_W, 2 * B_GROUP_W)),
        ],
        out_specs=pl.BlockSpec((None, 2, tm, D_MODEL), lambda b, t: (b, 0, t, 0)),
        out_shape=jax.ShapeDtypeStruct((bsz, 2, length, D_MODEL), BF16),
        scratch_shapes=[pltpu.VMEM((tm, D_MODEL), BF16)],
        compiler_params=_params(32),
        name="fourier_chan",
    )(x, mods, ng, cs_chan)
    ab = ab.reshape(bsz, 2 * length, D_MODEL)
    return pl.pallas_call(
        _fourier_seq_kernel,
        grid=(bsz, length // tm),
        in_specs=[
            pl.BlockSpec((tm, 2 * length), lambda b, t: (t, 0)),
            pl.BlockSpec((None, 2 * length, D_MODEL), lambda b, t: (b, 0, 0),
                         pipeline_mode=pl.Buffered(1)),
        ],
        out_specs=pl.BlockSpec((None, tm, D_MODEL), lambda b, t: (b, t, 0)),
        out_shape=jax.ShapeDtypeStruct((bsz, length, D_MODEL), BF16),
        compiler_params=_params(48),
        name="fourier_seq",
    )(cs_seq, ab)


def _head_rms(r, g):
    return r * lax.rsqrt(jnp.mean(r * r, axis=-1, keepdims=True) + NORM_EPS) * g


def _qkv_kernel(x_ref, mod_ref, ng_ref, w_ref, qg_ref, kg_ref, mean_ref, cos_ref, sin_ref,
                q_ref, k_ref, v_ref, h_s):
    _modulated_norm_to(h_s, x_ref, mod_ref, ng_ref)
    cosf = cos_ref[...]
    sinf = sin_ref[...]

    qg = qg_ref[...] * (HEAD_DIM ** -0.5 * LOG2_E)
    gains = [jnp.concatenate([qg, qg], axis=1)] * (N_HEADS // 2)
    gains.append(jnp.concatenate([kg_ref[...], kg_ref[...]], axis=1))
    outs = [(q_ref, 2 * p) for p in range(N_HEADS // 2)] + [(k_ref, 0)]
    n_pairs = len(outs)

    def project(p):
        return _dot(h_s[...], w_ref[:, p * MXU_COLS:(p + 1) * MXU_COLS])

    def normalise(t, p):
        sq = t * t
        hi = sq.astype(BF16)
        lo = (sq - hi.astype(F32)).astype(BF16)
        ms = _dot(hi, mean_ref[...]) + _dot(lo, mean_ref[...])
        return t * lax.rsqrt(ms + NORM_EPS) * gains[p]

    def rope_store(rn, p):
        ref, first = outs[p]
        for e in range(2):
            r = rn[:, e * HEAD_DIM:(e + 1) * HEAD_DIM]
            ref[first + e] = (r * cosf + pltpu.roll(r, HEAD_DIM // 2, 1) * sinf).astype(BF16)

    t_next = project(0)
    rn_prev = None
    for p in range(n_pairs + 1):
        t = t_next
        if p < n_pairs:
            t_next = project(p + 1)
        rn = normalise(t, p) if p < n_pairs else None
        if rn_prev is not None:
            rope_store(rn_prev, p - 1)
        rn_prev = rn
    for e in range(N_KV_HEADS):
        v_ref[e] = t[:, e * HEAD_DIM:(e + 1) * HEAD_DIM].astype(BF16)


def _kv_ctx_kernel(x_ref, mod_ref, ng_ref, w_ref, kg_ref, k_ref, v_ref, h_s):
    _modulated_norm_to(h_s, x_ref, mod_ref, ng_ref)
    t = _dot(h_s[...], w_ref[:, 0:MXU_COLS])
    for e in range(N_KV_HEADS):
        k_ref[e] = _head_rms(t[:, e * HEAD_DIM:(e + 1) * HEAD_DIM], kg_ref[...]).astype(BF16)
    t = _dot(h_s[...], w_ref[:, MXU_COLS:2 * MXU_COLS])
    for e in range(N_KV_HEADS):
        v_ref[e] = t[:, e * HEAD_DIM:(e + 1) * HEAD_DIM].astype(BF16)


def _rope_tables(length):
    t = jnp.arange(length)
    row = (t // GRID_W).astype(F32)
    col = (t % GRID_W).astype(F32)
    n_freq = HEAD_DIM // 4
    inv = ROPE_THETA ** (-jnp.arange(n_freq, dtype=F32) / n_freq)
    ang = jnp.concatenate([row[:, None] * inv, col[:, None] * inv], axis=-1)
    cos, sin = jnp.cos(ang), jnp.sin(ang)
    return jnp.concatenate([cos, cos], axis=-1), jnp.concatenate([-sin, sin], axis=-1)


def _qkv_latent(x, mods, ng, w_qkv, q_g, k_g):
    bsz, length, _ = x.shape
    tm = 512
    cosf, sinf = _rope_tables(length)
    n_qkv = w_qkv.shape[-1]
    lane_head = jnp.arange(MXU_COLS) // HEAD_DIM
    head_mean = ((lane_head[:, None] == lane_head[None, :]) * (1.0 / HEAD_DIM)).astype(BF16)
    kv_shape = jax.ShapeDtypeStruct((bsz, N_KV_HEADS, length, HEAD_DIM), BF16)
    kv_spec = pl.BlockSpec((None, N_KV_HEADS, tm, HEAD_DIM), lambda b, t: (b, 0, t, 0))
    return pl.pallas_call(
        _qkv_kernel,
        grid=(bsz, length // tm),
        in_specs=[
            pl.BlockSpec((None, tm, D_MODEL), lambda b, t: (b, t, 0)),
            _mod_spec(False, bsz),
            _const_spec((4, D_MODEL)),
            _const_spec((D_MODEL, n_qkv)),
            _const_spec((1, HEAD_DIM)),
            _const_spec((1, HEAD_DIM)),
            _const_spec((MXU_COLS, MXU_COLS)),
            pl.BlockSpec((tm, HEAD_DIM), lambda b, t: (t, 0)),
            pl.BlockSpec((tm, HEAD_DIM), lambda b, t: (t, 0)),
        ],
        out_specs=[
            pl.BlockSpec((None, N_HEADS, tm, HEAD_DIM), lambda b, t: (b, 0, t, 0)),
            kv_spec, kv_spec,
        ],
        out_shape=[
            jax.ShapeDtypeStruct((bsz, N_HEADS, length, HEAD_DIM), BF16),
            kv_shape, kv_shape,
        ],
        scratch_shapes=[pltpu.VMEM((tm, D_MODEL), BF16)],
        compiler_params=_params(32),
        name="qkv_latent",
    )(x, mods, ng, w_qkv, q_g, k_g, head_mean, cosf, sinf)


def _kv_ctx(ctx, mods, ng, w_kv, k_g):
    bsz, length, _ = ctx.shape
    tm = length
    kv_shape = jax.ShapeDtypeStruct((bsz, N_KV_HEADS, length, HEAD_DIM), BF16)
    kv_spec = pl.BlockSpec((None, N_KV_HEADS, tm, HEAD_DIM), lambda b, t: (b, 0, t, 0))
    return pl.pallas_call(
        _kv_ctx_kernel,
        grid=(bsz, length // tm),
        in_specs=[
            pl.BlockSpec((None, tm, D_MODEL), lambda b, t: (b, t, 0)),
            _mod_spec(True, bsz),
            _const_spec((4, D_MODEL)),
            _const_spec((D_MODEL, 2 * MXU_COLS)),
            _const_spec((1, HEAD_DIM)),
        ],
        out_specs=[kv_spec, kv_spec],
        out_shape=[kv_shape, kv_shape],
        scratch_shapes=[pltpu.VMEM((tm, D_MODEL), BF16)],
        compiler_params=_params(32),
        name="kv_ctx",
    )(ctx, mods, ng, w_kv, k_g)


def _region(index, fn):
    pl.when(pl.program_id(0) >= -index)(fn)


def _attn_kernel(q_ref, k_ref, vt_ref, o_ref, *scratch):
    tq = q_ref.shape[1]
    n_keys = k_ref.shape[1]
    nq = HEADS_PER_KV * tq
    q4_s, s_s, p_s, m_s = (scratch[i * N_KV_HEADS:(i + 1) * N_KV_HEADS] for i in range(4))
    key_chunks = [slice(r, r + MXU_COLS) for r in range(0, n_keys, MXU_COLS)]

    @pl.when(pl.program_id(0) == 0)
    def _():
        s_s[1][...] = jnp.zeros(s_s[1].shape, F32)
        m_s[1][...] = jnp.zeros(m_s[1].shape, F32)
        p_s[0][...] = jnp.ones(p_s[0].shape, BF16)

    def stage(j_scores, j_probs, j_pv):
        for h in range(HEADS_PER_KV):
            q4_s[j_scores][h * tq:(h + 1) * tq, :] = q_ref[HEADS_PER_KV * j_scores + h]
        m_probs = m_s[j_probs][...]
        m = None
        acc = None
        for rows in key_chunks:
            s = lax.dot_general(k_ref[j_scores, rows, :], q4_s[j_scores][...],
                                (((1,), (1,)), ((), ())), preferred_element_type=F32)
            s_s[j_scores][rows, :] = s
            cm = jnp.max(s, axis=0, keepdims=True)
            m = cm if m is None else jnp.maximum(m, cm)
            d = _dot(vt_ref[j_pv, :, rows], p_s[j_pv][rows, :])
            acc = d if acc is None else acc + d
            for r in range(rows.start, rows.stop, BF16_ROWS):
                tile = slice(r, r + BF16_ROWS)
                p_s[j_probs][tile, :] = jnp.exp2(s_s[j_probs][tile, :] - m_probs).astype(BF16)
        m_s[j_scores][...] = jnp.broadcast_to(m, (BF16_ROWS, nq))
        o = acc[:HEAD_DIM, :] / acc[HEAD_DIM:HEAD_DIM + 1, :]
        for h in range(HEADS_PER_KV):
            head = HEADS_PER_KV * j_pv + h
            o_ref[:, head * HEAD_DIM:(head + 1) * HEAD_DIM] = o[:, h * tq:(h + 1) * tq].T.astype(BF16)

    _region(0, lambda: stage(0, 1, 0))
    _region(1, lambda: stage(1, 0, 1))


def _attention(q, k, vt_ext):
    bsz, _, length, _ = q.shape
    n_keys = k.shape[2]
    vt_rows = vt_ext.shape[2]
    tq = 128
    nq = HEADS_PER_KV * tq
    per_batch = length // tq
    n_tiles = bsz * per_batch

    def cur(i):
        return jnp.minimum(i, n_tiles - 1)

    def prev(i):
        return jnp.maximum(i - 1, 0)

    return pl.pallas_call(
        _attn_kernel,
        grid=(n_tiles + 1,),
        in_specs=[
            pl.BlockSpec((None, N_HEADS, tq, HEAD_DIM),
                         lambda i: (cur(i) // per_batch, 0, cur(i) % per_batch, 0)),
            pl.BlockSpec((None, N_KV_HEADS, n_keys, HEAD_DIM), lambda i: (cur(i) // per_batch, 0, 0, 0)),
            pl.BlockSpec((None, N_KV_HEADS, vt_rows, n_keys), lambda i: (prev(i) // per_batch, 0, 0, 0)),
        ],
        out_specs=pl.BlockSpec((None, tq, D_MODEL),
                               lambda i: (prev(i) // per_batch, prev(i) % per_batch, 0)),
        out_shape=jax.ShapeDtypeStruct((bsz, length, D_MODEL), BF16),
        scratch_shapes=(
            [pltpu.VMEM((nq, HEAD_DIM), BF16)] * N_KV_HEADS
            + [pltpu.VMEM((n_keys, nq), F32)] * N_KV_HEADS
            + [pltpu.VMEM((n_keys, nq), BF16)] * N_KV_HEADS
            + [pltpu.VMEM((BF16_ROWS, nq), F32)] * N_KV_HEADS),
        compiler_params=pltpu.CompilerParams(
            dimension_semantics=("arbitrary",), vmem_limit_bytes=54 * MIB),
        name="attention",
    )(q, k, vt_ext)


def kernel(x, c, ctx, c_ctx, ada_w, ada_b, norm_g, mlp_w1, mlp_w2, a_w_in, a_ln_g, a_w_s, a_b_s, a_w_out,
           b_w_out, c_w_qkv, c_q_g, c_k_g, c_w_o):
    bsz = x.shape[0]
    attn_layers = [i for i in range(DEPTH) if i % N_MIXERS == 2]
    last_ctx_read = attn_layers[-1] if attn_layers else -1

    cond = jnp.zeros((COND_ROWS, D_MODEL), F32).at[:bsz].set(c).at[bsz].set(c_ctx)
    mods_all = _ada_all(cond, ada_w, ada_b).reshape(DEPTH, COND_ROWS, 6, D_MODEL)
    w1_all = mlp_w1.astype(BF16)
    w2_all = mlp_w2.astype(BF16)


    for i in range(DEPTH):
        kind, j = i % N_MIXERS, i // N_MIXERS
        ctx_in = i <= last_ctx_read
        ctx_out = i < last_ctx_read
        mods = mods_all[i]
        ng = norm_g[i]

        if kind == 0:
            w_in = a_w_in[j].astype(BF16)
            ln_g = a_ln_g[j].reshape(1, A_WIDTH)
            w_s = a_w_s[j].astype(BF16)
            b_s = a_b_s[j].reshape(A_GROUPS, CHUNK, 1)
            w_post = a_w_out[j].astype(BF16)
            z = _gmlp_head(x, mods, ng, w_in, ln_g, w_s, b_s, False)
            zc = _gmlp_head(ctx, mods, ng, w_in, ln_g, w_s, b_s, True) if ctx_out else None
        elif kind == 1:
            w_post = b_w_out[j].astype(BF16)
            z = _fourier_head_folded(x, mods, ng)
            zc = _fourier_head(ctx, mods, ng, True) if ctx_out else None
        else:
            w_qkv = c_w_qkv[j].astype(BF16)
            w_post = c_w_o[j].astype(BF16)
            q_g = c_q_g[j].reshape(1, HEAD_DIM)
            k_g = c_k_g[j].reshape(1, HEAD_DIM)
            q, k, v = _qkv_latent(x, mods, ng, w_qkv, q_g, k_g)
            if ctx_in:
                kc, vc = _kv_ctx(ctx, mods, ng, w_qkv[:, N_HEADS * HEAD_DIM:], k_g)
                k = jnp.concatenate([kc, k], axis=2)
                v = jnp.concatenate([vc, v], axis=2)
            ones_row = (jnp.arange(BF16_ROWS) == 0).astype(BF16)[:, None]
            pad = jnp.broadcast_to(ones_row, v.shape[:2] + (BF16_ROWS, v.shape[2]))
            vt_ext = jnp.concatenate([jnp.swapaxes(v, 2, 3), pad], axis=2)
            z = _attention(q, k, vt_ext)
            zc = None
            assert not ctx_out

        x = _tail(x, z, mods, ng, w_post, w1_all, w2_all, i, False, z_mirrored=(kind == 1))
        if ctx_out:
            ctx = _tail(ctx, zc, mods, ng, w_post, w1_all, w2_all, i, True)

    return x
```

```python
import functools

import jax
import jax.numpy as jnp
from jax import lax
from jax.experimental import pallas as pl
from jax.experimental.pallas import tpu as pltpu

D_MODEL = 1024
DEPTH = 4
GRID_W = 64
N_MIXERS = 3
CHUNK = 128
A_WIDTH = 2 * D_MODEL
A_GROUPS = 8
A_GROUP_W = A_WIDTH // A_GROUPS
B_GROUPS = 4
B_GROUP_W = D_MODEL // B_GROUPS
HEAD_DIM = 128
N_HEADS = D_MODEL // HEAD_DIM
N_KV_HEADS = 2
HEADS_PER_KV = N_HEADS // N_KV_HEADS
ROPE_THETA = 10000.0
D_FF = 4 * D_MODEL
NORM_EPS = 1e-6
LN_EPS = 1e-5
LOG2_E = 1.4426950408889634

F32 = jnp.float32
BF16 = jnp.bfloat16

MXU_COLS = 256
ROW_BLOCK = 64
BF16_ROWS = 16
LANES = 128
COND_ROWS = 16
MIB = 1024 * 1024


def _dot(a, b):
    return jnp.dot(a, b, preferred_element_type=F32)


def _rms(x, g):
    ms = jnp.mean(x * x, axis=-1, keepdims=True)
    return x * lax.rsqrt(ms + NORM_EPS) * g


def _gelu_tanh(x):
    c = 2.0 * 0.7978845608028654 * LOG2_E
    z = x * (-c - (c * 0.044715) * (x * x))
    return x / (1.0 + jnp.exp2(z))


def _const_spec(shape):
    zeros = (0,) * len(shape)
    return pl.BlockSpec(shape, lambda b, t: zeros, pipeline_mode=pl.Buffered(1))


def _mod_spec(is_ctx, batch):
    if is_ctx:
        return pl.BlockSpec((None, 6, D_MODEL), lambda b, t: (batch, 0, 0))
    return pl.BlockSpec((None, 6, D_MODEL), lambda b, t: (b, 0, 0))


def _params(vmem_mib):
    return pltpu.CompilerParams(
        dimension_semantics=("arbitrary", "arbitrary"),
        vmem_limit_bytes=vmem_mib * MIB)


def _modulated_norm_to(h_s, x_ref, mod_ref, ng_ref, start=0, stop=None):
    sh1 = mod_ref[0:1, :]
    sc1p = 1.0 + mod_ref[1:2, :]
    g0 = ng_ref[0:1, :]
    stop = x_ref.shape[0] if stop is None else stop
    for r in range(start, stop, ROW_BLOCK):
        rows = slice(r, r + ROW_BLOCK)
        h_s[rows, :] = (_rms(x_ref[rows, :], g0) * sc1p + sh1).astype(BF16)


def _ada_kernel(cond_ref, w_ref, b_ref, o_ref):
    a = cond_ref[...]
    a = a * jax.nn.sigmoid(a)
    o_ref[...] = _dot(a.astype(BF16), w_ref[...].astype(BF16)) + b_ref[...]


def _ada_all(cond, ada_w, ada_b):
    tn = 1536
    n_out = 6 * D_MODEL
    return pl.pallas_call(
        _ada_kernel,
        grid=(DEPTH, n_out // tn),
        in_specs=[
            pl.BlockSpec((COND_ROWS, D_MODEL), lambda i, n: (0, 0)),
            pl.BlockSpec((None, D_MODEL, tn), lambda i, n: (i, 0, n)),
            pl.BlockSpec((None, 1, tn), lambda i, n: (i, 0, n)),
        ],
        out_specs=pl.BlockSpec((None, COND_ROWS, tn), lambda i, n: (i, 0, n)),
        out_shape=jax.ShapeDtypeStruct((DEPTH, COND_ROWS, n_out), F32),
        compiler_params=_params(32),
        name="ada_mod",
    )(cond, ada_w, ada_b.reshape(DEPTH, 1, n_out))


def _gmlp_head_kernel(x_ref, mod_ref, ng_ref, win_ref, lng_ref, ws_ref, bs_ref, z_ref,
                      h_s, u_s, v_s, vb_s, sum_s, sq_s):
    tm = x_ref.shape[0]
    width = 2 * MXU_COLS
    n_half = A_WIDTH // width
    lng = lng_ref[...]
    half = tm // 2
    lanes = sum_s.shape[1]

    def proj(unit):
        c, a = unit
        return _dot(h_s[a * half:(a + 1) * half, :], win_ref[:, c * width:(c + 1) * width])

    def gelu_store(d, unit):
        c, a = unit
        is_v = c >= n_half
        dst, c = (v_s, c - n_half) if is_v else (u_s, c)
        for r in range(0, half, ROW_BLOCK):
            rows = slice(a * half + r, a * half + r + ROW_BLOCK)
            g = _gelu_tanh(d[r:r + ROW_BLOCK, :])
            dst[rows, c * width:(c + 1) * width] = g
            if is_v:
                parts = [g[:, i:i + lanes] for i in range(0, width, lanes)]
                p1 = sum(parts[1:], parts[0])
                p2 = sum([p * p for p in parts[1:]], parts[0] * parts[0])
                if c == 0:
                    sum_s[rows, :] = p1
                    sq_s[rows, :] = p2
                else:
                    sum_s[rows, :] += p1
                    sq_s[rows, :] += p2

    def layer_norm_rows(part):
        for r in range(part * tm // n_half, (part + 1) * tm // n_half, ROW_BLOCK // 2):
            rows = slice(r, r + ROW_BLOCK // 2)
            mu = jnp.sum(sum_s[rows, :], axis=-1, keepdims=True) * (1.0 / A_WIDTH)
            ex2 = jnp.sum(sq_s[rows, :], axis=-1, keepdims=True) * (1.0 / A_WIDTH)
            rstd = lax.rsqrt(ex2 - mu * mu + LN_EPS)
            vb_s[rows, :] = ((v_s[rows, :] - mu) * rstd * lng).astype(BF16)

    chunks = list(range(n_half, 2 * n_half)) + list(range(n_half))
    order = [(c, a) for c in chunks for a in range(2)]
    _modulated_norm_to(h_s, x_ref, mod_ref, ng_ref, 0, half)
    d = proj(order[0])
    _modulated_norm_to(h_s, x_ref, mod_ref, ng_ref, half, tm)
    for i in range(1, len(order)):
        d_next = proj(order[i])
        gelu_store(d, order[i - 1])
        if i > 2 * n_half and i % 2 == 0:
            layer_norm_rows((i - 2 * n_half) // 2 - 1)
        d = d_next
    gelu_store(d, order[-1])
    layer_norm_rows(n_half - 1)
    for g in range(A_GROUPS):
        w = ws_ref[g]
        b = bs_ref[g]
        cols = slice(g * A_GROUP_W, (g + 1) * A_GROUP_W)
        for n in range(tm // CHUNK):
            rows = slice(n * CHUNK, (n + 1) * CHUNK)
            sv = _dot(w, vb_s[rows, cols]) + b
            z_ref[rows, cols] = (u_s[rows, cols] * sv).astype(BF16)


def _gmlp_head(x, mods, ng, w_in, ln_g, w_s, b_s, is_ctx):
    bsz, length, _ = x.shape
    tm = min(1024, length)
    return pl.pallas_call(
        _gmlp_head_kernel,
        grid=(bsz, length // tm),
        in_specs=[
            pl.BlockSpec((None, tm, D_MODEL), lambda b, t: (b, t, 0)),
            _mod_spec(is_ctx, bsz),
            _const_spec((4, D_MODEL)),
            _const_spec((D_MODEL, 2 * A_WIDTH)),
            _const_spec((1, A_WIDTH)),
            _const_spec((A_GROUPS, CHUNK, CHUNK)),
            _const_spec((A_GROUPS, CHUNK, 1)),
        ],
        out_specs=pl.BlockSpec((None, tm, A_WIDTH), lambda b, t: (b, t, 0)),
        out_shape=jax.ShapeDtypeStruct((bsz, length, A_WIDTH), BF16),
        scratch_shapes=[
            pltpu.VMEM((tm, D_MODEL), BF16),
            pltpu.VMEM((tm, A_WIDTH), F32),
            pltpu.VMEM((tm, A_WIDTH), F32),
            pltpu.VMEM((tm, A_WIDTH), BF16),
            pltpu.VMEM((tm, LANES), F32),
            pltpu.VMEM((tm, LANES), F32),
        ],
        compiler_params=_params(56),
        name="gmlp_head",
    )(x, mods, ng, w_in, ln_g, w_s, b_s)


def _interleave(dots, fillers):
    per = -(-len(fillers) // len(dots))
    for i, dot_fn in enumerate(dots):
        dot_fn()
        for filler in fillers[i * per:(i + 1) * per]:
            filler()


def _tail_kernel(x_ref, z_ref, mod_ref, ng_ref, wp_ref, w1_ref, w2_ref, o_ref, *scratch,
                 sub, z_mirrored):
    n_sub = x_ref.shape[0] // sub
    weights = (mod_ref, ng_ref, wp_ref, w1_ref, w2_ref)
    if n_sub == 1:
        _tail_subtile(x_ref, z_ref, *weights, o_ref, *scratch)
        return

    def body(i, carry):
        zi = i
        if z_mirrored:
            second_half = pl.program_id(1) >= pl.num_programs(1) // 2
            zi = jnp.where(second_half, n_sub - 1 - i, i)
        rows = pl.ds(pl.multiple_of(i * sub, sub), sub)
        z_rows = pl.ds(pl.multiple_of(zi * sub, sub), sub)
        _tail_subtile(x_ref.at[rows, :], z_ref.at[z_rows, :], *weights, o_ref.at[rows, :], *scratch)
        return carry

    lax.fori_loop(0, n_sub, body, 0)


def _tail_subtile(x_ref, z_ref, mod_ref, ng_ref, wp_ref, w1_ref, w2_ref, o_ref, *scratch):
    half = x_ref.shape[0] // 2
    y_s, hm_s, hid_s = scratch[0:2], scratch[2:4], scratch[4:6]
    g1 = mod_ref[2:3, :]
    sh2 = mod_ref[3:4, :]
    sc2p = 1.0 + mod_ref[4:5, :]
    g2 = mod_ref[5:6, :]
    ff_cols = 2 * MXU_COLS
    out_chunks = [slice(n * MXU_COLS, (n + 1) * MXU_COLS) for n in range(D_MODEL // MXU_COLS)]
    ff_chunks = [slice(c * ff_cols, (c + 1) * ff_cols) for c in range(D_FF // ff_cols)]
    blocks = list(range(0, half, ROW_BLOCK))

    def post_dot(a, cols):
        y_s[a][:, cols] = _dot(z_ref[a * half:(a + 1) * half, :], wp_ref[:, cols])

    def norm1(a, r):
        rows = slice(a * half + r, a * half + r + ROW_BLOCK)
        x1 = x_ref[rows, :] + g1 * _rms(y_s[a][r:r + ROW_BLOCK, :], ng_ref[1:2, :])
        o_ref[rows, :] = x1
        hm_s[a][r:r + ROW_BLOCK, :] = (_rms(x1, ng_ref[2:3, :]) * sc2p + sh2).astype(BF16)

    def up_dot(a, cols):
        t = jnp.maximum(_dot(hm_s[a][...], w1_ref[:, cols]), 0.0)
        hid_s[a][:, cols] = (t * t).astype(BF16)

    def down_dot(a, cols):
        y_s[a][:, cols] = _dot(hid_s[a][...], w2_ref[:, cols])

    def norm3(a, r):
        rows = slice(a * half + r, a * half + r + ROW_BLOCK)
        o_ref[rows, :] = o_ref[rows, :] + g2 * _rms(y_s[a][r:r + ROW_BLOCK, :], ng_ref[3:4, :])

    def each(fn, a, items):
        return [functools.partial(fn, a, item) for item in items]

    _interleave(each(post_dot, 0, out_chunks), [])
    _interleave(each(post_dot, 1, out_chunks), each(norm1, 0, blocks))
    _interleave(each(up_dot, 0, ff_chunks), each(norm1, 1, blocks))
    _interleave(each(up_dot, 1, ff_chunks), [])
    _interleave(each(down_dot, 0, out_chunks), [])
    _interleave(each(down_dot, 1, out_chunks), each(norm3, 0, blocks))
    _interleave(each(norm3, 1, blocks), [])


def _tail(x, z, mods, ng, w_post, w1_all, w2_all, layer, is_ctx, z_mirrored=False):
    bsz, length, _ = x.shape
    kz = z.shape[-1]
    sub = min(512, length)
    tm = min(1024, length)
    if z_mirrored:
        nt = length // 2 // tm
        z_spec = pl.BlockSpec(
            (None, None, tm, kz),
            lambda b, t: (b, t // nt, jnp.where(t < nt, t, 2 * nt - 1 - t), 0))
    else:
        z_spec = pl.BlockSpec((None, tm, kz), lambda b, t: (b, t, 0))
    return pl.pallas_call(
        functools.partial(_tail_kernel, sub=sub, z_mirrored=z_mirrored),
        grid=(bsz, length // tm),
        in_specs=[
            pl.BlockSpec((None, tm, D_MODEL), lambda b, t: (b, t, 0)),
            z_spec,
            _mod_spec(is_ctx, bsz),
            _const_spec((4, D_MODEL)),
            _const_spec((kz, D_MODEL)),
            pl.BlockSpec((None, D_MODEL, D_FF), lambda b, t: (layer, 0, 0), pipeline_mode=pl.Buffered(1)),
            pl.BlockSpec((None, D_FF, D_MODEL), lambda b, t: (layer, 0, 0), pipeline_mode=pl.Buffered(1)),
        ],
        out_specs=pl.BlockSpec((None, tm, D_MODEL), lambda b, t: (b, t, 0)),
        out_shape=jax.ShapeDtypeStruct((bsz, length, D_MODEL), F32),
        scratch_shapes=(
            [pltpu.VMEM((sub // 2, D_MODEL), F32)] * 2
            + [pltpu.VMEM((sub // 2, D_MODEL), BF16)] * 2
            + [pltpu.VMEM((sub // 2, D_FF), BF16)] * 2),
        compiler_params=_params(58),
        name="tail",
    )(x, z, mods, ng, w_post, w1_all, w2_all)


def _fourier_chan_kernel(x_ref, mod_ref, ng_ref, cs_ref, ab_ref, h_s):
    _modulated_norm_to(h_s, x_ref, mod_ref, ng_ref)
    for g in range(B_GROUPS):
        cols = slice(g * B_GROUP_W, (g + 1) * B_GROUP_W)
        t = _dot(h_s[:, cols], cs_ref[...])
        ab_ref[0, :, cols] = t[:, :B_GROUP_W].astype(BF16)
        ab_ref[1, :, cols] = t[:, B_GROUP_W:].astype(BF16)


def _fourier_seq_kernel(cs_ref, ab_ref, y_ref):
    for n in range(D_MODEL // MXU_COLS):
        cols = slice(n * MXU_COLS, (n + 1) * MXU_COLS)
        y_ref[:, cols] = _dot(cs_ref[...], ab_ref[:, cols]).astype(BF16)


def _dft_tables(length):
    m = jnp.arange(B_GROUP_W, dtype=jnp.int32)
    ang_c = (2.0 * jnp.pi / B_GROUP_W) * ((m[:, None] * m[None, :]) % B_GROUP_W).astype(F32)
    s_c = B_GROUP_W ** -0.5
    cs_chan = jnp.concatenate([jnp.cos(ang_c), jnp.sin(ang_c)], axis=1) * s_c
    k = jnp.arange(length, dtype=jnp.int32)
    ang_l = (2.0 * jnp.pi / length) * ((k[:, None] * k[None, :]) % length).astype(F32)
    s_l = length ** -0.5
    cs_seq = jnp.concatenate([jnp.cos(ang_l), -jnp.sin(ang_l)], axis=1) * s_l
    return cs_chan.astype(BF16), cs_seq.astype(BF16)


def _fourier_fold_kernel(xp_ref, xm_ref, x0_ref, mod_ref, ng_ref, cc_ref, sc_ref, rev_ref,
                         ab_ref, a0_ref, hp_s, hm_s):
    tm = xm_ref.shape[0]
    sh1 = mod_ref[0:1, :]
    sc1p = 1.0 + mod_ref[1:2, :]
    g0 = ng_ref[0:1, :]
    for r in range(0, tm, ROW_BLOCK):
        rows = slice(r, r + ROW_BLOCK)
        hp_s[rows, :] = _rms(xp_ref[0, r + 1:r + 1 + ROW_BLOCK, :], g0) * sc1p + sh1
        hm_s[rows, :] = (_rms(xm_ref[rows, :], g0) * sc1p + sh1).astype(BF16)
    h0 = (_rms(x0_ref[...], g0) * sc1p + sh1).astype(BF16)
    for g in range(B_GROUPS):
        cols = slice(g * B_GROUP_W, (g + 1) * B_GROUP_W)
        hm_rev = _dot(rev_ref[...], hm_s[:, cols])
        hp = hp_s[:, cols]
        ab_ref[0, :, cols] = _dot((hp + hm_rev).astype(BF16), cc_ref[...]).astype(BF16)
        ab_ref[1, :, cols] = _dot((hp - hm_rev).astype(BF16), sc_ref[...]).astype(BF16)
        a0_ref[:, cols] = _dot(h0[:, cols], cc_ref[...])


def _fourier_seq_fold_kernel(ct_ref, st_ref, ab_ref, a0_ref, perm_ref, y_ref, w_s, *, scale):
    tk = y_ref.shape[1]
    half = ct_ref.shape[1]
    for n in range(D_MODEL // MXU_COLS):
        cols = slice(n * MXU_COLS, (n + 1) * MXU_COLS)
        yc = _dot(ct_ref[...], ab_ref[0:half, cols]) + a0_ref[0:1, cols] * scale
        ys = _dot(st_ref[...], ab_ref[half:2 * half, cols])
        y_ref[0, :, cols] = (yc[:tk] - ys[:tk]).astype(BF16)
        w_s[:, cols] = (yc + ys).astype(BF16)
    for n in range(D_MODEL // MXU_COLS):
        cols = slice(n * MXU_COLS, (n + 1) * MXU_COLS)
        y_ref[1, :, cols] = _dot(perm_ref[...], w_s[:, cols]).astype(BF16)


def _seq_fold_tables(length, tk):
    half = length // 2
    nt = half // tk
    rows = tk + BF16_ROWS
    fine = 64
    n_coarse = -(-(nt * tk + BF16_ROWS) // fine)
    t = jnp.arange(1, half + 1, dtype=jnp.int32)

    def cos_sin(k):
        ang = (2.0 * jnp.pi / length) * ((k[:, None] * t[None, :]) % length).astype(F32)
        return jnp.cos(ang), jnp.sin(ang)

    ch, sh = cos_sin(jnp.arange(n_coarse, dtype=jnp.int32) * fine)
    cl, sl = cos_sin(jnp.arange(fine, dtype=jnp.int32))
    weight = jnp.where(t == half, 0.5, 1.0) * length ** -0.5
    c = (ch[:, None, :] * cl[None] - sh[:, None, :] * sl[None]).reshape(n_coarse * fine, half)
    s = (sh[:, None, :] * cl[None] + ch[:, None, :] * sl[None]).reshape(n_coarse * fine, half)
    ct = jnp.stack([c[i * tk:i * tk + rows] for i in range(nt)]) * weight
    st = jnp.stack([s[i * tk:i * tk + rows] for i in range(nt)]) * weight
    return ct.astype(BF16), st.astype(BF16)


def _fourier_head_folded(x, mods, ng):
    bsz, length, _ = x.shape
    tm = 512
    half = length // 2
    nt = half // tm
    rows = tm + BF16_ROWS
    m = jnp.arange(B_GROUP_W, dtype=jnp.int32)
    ang_c = (2.0 * jnp.pi / B_GROUP_W) * ((m[:, None] * m[None, :]) % B_GROUP_W).astype(F32)
    cc = (jnp.cos(ang_c) * B_GROUP_W ** -0.5).astype(BF16)
    sc = (jnp.sin(ang_c) * B_GROUP_W ** -0.5).astype(BF16)
    rev = (jnp.arange(tm)[:, None] + jnp.arange(tm)[None, :] == tm - 1).astype(BF16)
    ab, a0 = pl.pallas_call(
        _fourier_fold_kernel,
        grid=(bsz, nt),
        in_specs=[
            pl.BlockSpec((pl.Element(1), pl.Element(tm + 8), pl.Element(D_MODEL)),
                         lambda b, t: (b, t * tm, 0)),
            pl.BlockSpec((None, tm, D_MODEL), lambda b, t: (b, 2 * nt - 1 - t, 0)),
            pl.BlockSpec((None, 8, D_MODEL), lambda b, t: (b, 0, 0)),
            _mod_spec(False, bsz),
            _const_spec((4, D_MODEL)),
            _const_spec((B_GROUP_W, B_GROUP_W)),
            _const_spec((B_GROUP_W, B_GROUP_W)),
            _const_spec((tm, tm)),
        ],
        out_specs=[
            pl.BlockSpec((None, 2, tm, D_MODEL), lambda b, t: (b, 0, t, 0)),
            pl.BlockSpec((None, 8, D_MODEL), lambda b, t: (b, 0, 0)),
        ],
        out_shape=[
            jax.ShapeDtypeStruct((bsz, 2, half, D_MODEL), BF16),
            jax.ShapeDtypeStruct((bsz, 8, D_MODEL), F32),
        ],
        scratch_shapes=[pltpu.VMEM((tm, D_MODEL), F32), pltpu.VMEM((tm, D_MODEL), BF16)],
        compiler_params=_params(32),
        name="fourier_fold",
    )(x, x, x, mods, ng, cc, sc, rev)
    ct, st = _seq_fold_tables(length, tm)
    perm = (jnp.arange(tm)[:, None] + jnp.arange(rows)[None, :] == tm).astype(BF16)
    return pl.pallas_call(
        functools.partial(_fourier_seq_fold_kernel, scale=length ** -0.5),
        grid=(bsz, nt),
        in_specs=[
            pl.BlockSpec((None, rows, half), lambda b, t: (t, 0, 0)),
            pl.BlockSpec((None, rows, half), lambda b, t: (t, 0, 0)),
            pl.BlockSpec((None, 2 * half, D_MODEL), lambda b, t: (b, 0, 0)),
            pl.BlockSpec((None, 8, D_MODEL), lambda b, t: (b, 0, 0)),
            _const_spec((tm, rows)),
        ],
        out_specs=pl.BlockSpec((None, 2, tm, D_MODEL), lambda b, t: (b, 0, t, 0)),
        out_shape=jax.ShapeDtypeStruct((bsz, 2, half, D_MODEL), BF16),
        scratch_shapes=[pltpu.VMEM((rows, D_MODEL), BF16)],
        compiler_params=_params(40),
        name="fourier_seq_fold",
    )(ct, st, ab.reshape(bsz, 2 * half, D_MODEL), a0, perm)


def _fourier_head(x, mods, ng, is_ctx):
    bsz, length, _ = x.shape
    tm = min(512, length)
    cs_chan, cs_seq = _dft_tables(length)
    ab = pl.pallas_call(
        _fourier_chan_kernel,
        grid=(bsz, length // tm),
        in_specs=[
            pl.BlockSpec((None, tm, D_MODEL), lambda b, t: (b, t, 0)),
            _mod_spec(is_ctx, bsz),
            _const_spec((4, D_MODEL)),
            _const_spec((B_GROUP_W, 2 * B_GROUP_W)),
        ],
        out_specs=pl.BlockSpec((None, 2, tm, D_MODEL), lambda b, t: (b, 0, t, 0)),
        out_shape=jax.ShapeDtypeStruct((bsz, 2, length, D_MODEL), BF16),
        scratch_shapes=[pltpu.VMEM((tm, D_MODEL), BF16)],
        compiler_params=_params(32),
        name="fourier_chan",
    )(x, mods, ng, cs_chan)
    ab = ab.reshape(bsz, 2 * length, D_MODEL)
    return pl.pallas_call(
        _fourier_seq_kernel,
        grid=(bsz, length // tm),
        in_specs=[
            pl.BlockSpec((tm, 2 * length), lambda b, t: (t, 0)),
            pl.BlockSpec((None, 2 * length, D_MODEL), lambda b, t: (b, 0, 0),
                         pipeline_mode=pl.Buffered(1)),
        ],
        out_specs=pl.BlockSpec((None, tm, D_MODEL), lambda b, t: (b, t, 0)),
        out_shape=jax.ShapeDtypeStruct((bsz, length, D_MODEL), BF16),
        compiler_params=_params(48),
        name="fourier_seq",
    )(cs_seq, ab)


def _head_rms(r, g):
    return r * lax.rsqrt(jnp.mean(r * r, axis=-1, keepdims=True) + NORM_EPS) * g


def _qkv_kernel(x_ref, mod_ref, ng_ref, w_ref, qg_ref, kg_ref, mean_ref, cos_ref, sin_ref,
                q_ref, k_ref, v_ref, h_s):
    _modulated_norm_to(h_s, x_ref, mod_ref, ng_ref)
    cosf = cos_ref[...]
    sinf = sin_ref[...]

    qg = qg_ref[...] * (HEAD_DIM ** -0.5 * LOG2_E)
    gains = [jnp.concatenate([qg, qg], axis=1)] * (N_HEADS // 2)
    gains.append(jnp.concatenate([kg_ref[...], kg_ref[...]], axis=1))
    outs = [(q_ref, 2 * p) for p in range(N_HEADS // 2)] + [(k_ref, 0)]
    n_pairs = len(outs)

    def project(p):
        return _dot(h_s[...], w_ref[:, p * MXU_COLS:(p + 1) * MXU_COLS])

    def normalise(t, p):
        sq = t * t
        hi = sq.astype(BF16)
        lo = (sq - hi.astype(F32)).astype(BF16)
        ms = _dot(hi, mean_ref[...]) + _dot(lo, mean_ref[...])
        return t * lax.rsqrt(ms + NORM_EPS) * gains[p]

    def rope_store(rn, p):
        ref, first = outs[p]
        for e in range(2):
            r = rn[:, e * HEAD_DIM:(e + 1) * HEAD_DIM]
            ref[first + e] = (r * cosf + pltpu.roll(r, HEAD_DIM // 2, 1) * sinf).astype(BF16)

    t_next = project(0)
    rn_prev = None
    for p in range(n_pairs + 1):
        t = t_next
        if p < n_pairs:
            t_next = project(p + 1)
        rn = normalise(t, p) if p < n_pairs else None
        if rn_prev is not None:
            rope_store(rn_prev, p - 1)
        rn_prev = rn
    for e in range(N_KV_HEADS):
        v_ref[e] = t[:, e * HEAD_DIM:(e + 1) * HEAD_DIM].astype(BF16)


def _kv_ctx_kernel(x_ref, mod_ref, ng_ref, w_ref, kg_ref, k_ref, v_ref, h_s):
    _modulated_norm_to(h_s, x_ref, mod_ref, ng_ref)
    t = _dot(h_s[...], w_ref[:, 0:MXU_COLS])
    for e in range(N_KV_HEADS):
        k_ref[e] = _head_rms(t[:, e * HEAD_DIM:(e + 1) * HEAD_DIM], kg_ref[...]).astype(BF16)
    t = _dot(h_s[...], w_ref[:, MXU_COLS:2 * MXU_COLS])
    for e in range(N_KV_HEADS):
        v_ref[e] = t[:, e * HEAD_DIM:(e + 1) * HEAD_DIM].astype(BF16)


def _rope_tables(length):
    t = jnp.arange(length)
    row = (t // GRID_W).astype(F32)
    col = (t % GRID_W).astype(F32)
    n_freq = HEAD_DIM // 4
    inv = ROPE_THETA ** (-jnp.arange(n_freq, dtype=F32) / n_freq)
    ang = jnp.concatenate([row[:, None] * inv, col[:, None] * inv], axis=-1)
    cos, sin = jnp.cos(ang), jnp.sin(ang)
    return jnp.concatenate([cos, cos], axis=-1), jnp.concatenate([-sin, sin], axis=-1)


def _qkv_latent(x, mods, ng, w_qkv, q_g, k_g):
    bsz, length, _ = x.shape
    tm = 512
    cosf, sinf = _rope_tables(length)
    n_qkv = w_qkv.shape[-1]
    lane_head = jnp.arange(MXU_COLS) // HEAD_DIM
    head_mean = ((lane_head[:, None] == lane_head[None, :]) * (1.0 / HEAD_DIM)).astype(BF16)
    kv_shape = jax.ShapeDtypeStruct((bsz, N_KV_HEADS, length, HEAD_DIM), BF16)
    kv_spec = pl.BlockSpec((None, N_KV_HEADS, tm, HEAD_DIM), lambda b, t: (b, 0, t, 0))
    return pl.pallas_call(
        _qkv_kernel,
        grid=(bsz, length // tm),
        in_specs=[
            pl.BlockSpec((None, tm, D_MODEL), lambda b, t: (b, t, 0)),
            _mod_spec(False, bsz),
            _const_spec((4, D_MODEL)),
            _const_spec((D_MODEL, n_qkv)),
            _const_spec((1, HEAD_DIM)),
            _const_spec((1, HEAD_DIM)),
            _const_spec((MXU_COLS, MXU_COLS)),
            pl.BlockSpec((tm, HEAD_DIM), lambda b, t: (t, 0)),
            pl.BlockSpec((tm, HEAD_DIM), lambda b, t: (t, 0)),
        ],
        out_specs=[
            pl.BlockSpec((None, N_HEADS, tm, HEAD_DIM), lambda b, t: (b, 0, t, 0)),
            kv_spec, kv_spec,
        ],
        out_shape=[
            jax.ShapeDtypeStruct((bsz, N_HEADS, length, HEAD_DIM), BF16),
            kv_shape, kv_shape,
        ],
        scratch_shapes=[pltpu.VMEM((tm, D_MODEL), BF16)],
        compiler_params=_params(32),
        name="qkv_latent",
    )(x, mods, ng, w_qkv, q_g, k_g, head_mean, cosf, sinf)


def _kv_ctx(ctx, mods, ng, w_kv, k_g):
    bsz, length, _ = ctx.shape
    tm = length
    kv_shape = jax.ShapeDtypeStruct((bsz, N_KV_HEADS, length, HEAD_DIM), BF16)
    kv_spec = pl.BlockSpec((None, N_KV_HEADS, tm, HEAD_DIM), lambda b, t: (b, 0, t, 0))
    return pl.pallas_call(
        _kv_ctx_kernel,
        grid=(bsz, length // tm),
        in_specs=[
            pl.BlockSpec((None, tm, D_MODEL), lambda b, t: (b, t, 0)),
            _mod_spec(True, bsz),
            _const_spec((4, D_MODEL)),
            _const_spec((D_MODEL, 2 * MXU_COLS)),
            _const_spec((1, HEAD_DIM)),
        ],
        out_specs=[kv_spec, kv_spec],
        out_shape=[kv_shape, kv_shape],
        scratch_shapes=[pltpu.VMEM((tm, D_MODEL), BF16)],
        compiler_params=_params(32),
        name="kv_ctx",
    )(ctx, mods, ng, w_kv, k_g)


def _region(index, fn):
    pl.when(pl.program_id(0) >= -index)(fn)


def _attn_kernel(q_ref, k_ref, vt_ref, o_ref, *scratch):
    tq = q_ref.shape[1]
    n_keys = k_ref.shape[1]
    nq = HEADS_PER_KV * tq
    q4_s, s_s, p_s, m_s = (scratch[i * N_KV_HEADS:(i + 1) * N_KV_HEADS] for i in range(4))
    key_chunks = [slice(r, r + MXU_COLS) for r in range(0, n_keys, MXU_COLS)]

    @pl.when(pl.program_id(0) == 0)
    def _():
        s_s[1][...] = jnp.zeros(s_s[1].shape, F32)
        m_s[1][...] = jnp.zeros(m_s[1].shape, F32)
        p_s[0][...] = jnp.ones(p_s[0].shape, BF16)

    def stage(j_scores, j_probs, j_pv):
        for h in range(HEADS_PER_KV):
            q4_s[j_scores][h * tq:(h + 1) * tq, :] = q_ref[HEADS_PER_KV * j_scores + h]
        m_probs = m_s[j_probs][...]
        m = None
        acc = None
        for rows in key_chunks:
            s = lax.dot_general(k_ref[j_scores, rows, :], q4_s[j_scores][...],
                                (((1,), (1,)), ((), ())), preferred_element_type=F32)
            s_s[j_scores][rows, :] = s
            cm = jnp.max(s, axis=0, keepdims=True)
            m = cm if m is None else jnp.maximum(m, cm)
            d = _dot(vt_ref[j_pv, :, rows], p_s[j_pv][rows, :])
            acc = d if acc is None else acc + d
            for r in range(rows.start, rows.stop, BF16_ROWS):
                tile = slice(r, r + BF16_ROWS)
                p_s[j_probs][tile, :] = jnp.exp2(s_s[j_probs][tile, :] - m_probs).astype(BF16)
        m_s[j_scores][...] = jnp.broadcast_to(m, (BF16_ROWS, nq))
        o = acc[:HEAD_DIM, :] / acc[HEAD_DIM:HEAD_DIM + 1, :]
        for h in range(HEADS_PER_KV):
            head = HEADS_PER_KV * j_pv + h
            o_ref[:, head * HEAD_DIM:(head + 1) * HEAD_DIM] = o[:, h * tq:(h + 1) * tq].T.astype(BF16)

    _region(0, lambda: stage(0, 1, 0))
    _region(1, lambda: stage(1, 0, 1))


def _attention(q, k, vt_ext):
    bsz, _, length, _ = q.shape
    n_keys = k.shape[2]
    vt_rows = vt_ext.shape[2]
    tq = 128
    nq = HEADS_PER_KV * tq
    per_batch = length // tq
    n_tiles = bsz * per_batch

    def cur(i):
        return jnp.minimum(i, n_tiles - 1)

    def prev(i):
        return jnp.maximum(i - 1, 0)

    return pl.pallas_call(
        _attn_kernel,
        grid=(n_tiles + 1,),
        in_specs=[
            pl.BlockSpec((None, N_HEADS, tq, HEAD_DIM),
                         lambda i: (cur(i) // per_batch, 0, cur(i) % per_batch, 0)),
            pl.BlockSpec((None, N_KV_HEADS, n_keys, HEAD_DIM), lambda i: (cur(i) // per_batch, 0, 0, 0)),
            pl.BlockSpec((None, N_KV_HEADS, vt_rows, n_keys), lambda i: (prev(i) // per_batch, 0, 0, 0)),
        ],
        out_specs=pl.BlockSpec((None, tq, D_MODEL),
                               lambda i: (prev(i) // per_batch, prev(i) % per_batch, 0)),
        out_shape=jax.ShapeDtypeStruct((bsz, length, D_MODEL), BF16),
        scratch_shapes=(
            [pltpu.VMEM((nq, HEAD_DIM), BF16)] * N_KV_HEADS
            + [pltpu.VMEM((n_keys, nq), F32)] * N_KV_HEADS
            + [pltpu.VMEM((n_keys, nq), BF16)] * N_KV_HEADS
            + [pltpu.VMEM((BF16_ROWS, nq), F32)] * N_KV_HEADS),
        compiler_params=pltpu.CompilerParams(
            dimension_semantics=("arbitrary",), vmem_limit_bytes=54 * MIB),
        name="attention",
    )(q, k, vt_ext)


def kernel(x, c, ctx, c_ctx, ada_w, ada_b, norm_g, mlp_w1, mlp_w2, a_w_in, a_ln_g, a_w_s, a_b_s, a_w_out,
           b_w_out, c_w_qkv, c_q_g, c_k_g, c_w_o):
    bsz = x.shape[0]
    attn_layers = [i for i in range(DEPTH) if i % N_MIXERS == 2]
    last_ctx_read = attn_layers[-1] if attn_layers else -1

    cond = jnp.zeros((COND_ROWS, D_MODEL), F32).at[:bsz].set(c).at[bsz].set(c_ctx)
    mods_all = _ada_all(cond, ada_w, ada_b).reshape(DEPTH, COND_ROWS, 6, D_MODEL)
    w1_all = mlp_w1.astype(BF16)
    w2_all = mlp_w2.astype(BF16)


    for i in range(DEPTH):
        kind, j = i % N_MIXERS, i // N_MIXERS
        ctx_in = i <= last_ctx_read
        ctx_out = i < last_ctx_read
        mods = mods_all[i]
        ng = norm_g[i]

        if kind == 0:
            w_in = a_w_in[j].astype(BF16)
            ln_g = a_ln_g[j].reshape(1, A_WIDTH)
            w_s = a_w_s[j].astype(BF16)
            b_s = a_b_s[j].reshape(A_GROUPS, CHUNK, 1)
            w_post = a_w_out[j].astype(BF16)
            z = _gmlp_head(x, mods, ng, w_in, ln_g, w_s, b_s, False)
            zc = _gmlp_head(ctx, mods, ng, w_in, ln_g, w_s, b_s, True) if ctx_out else None
        elif kind == 1:
            w_post = b_w_out[j].astype(BF16)
            z = _fourier_head_folded(x, mods, ng)
            zc = _fourier_head(ctx, mods, ng, True) if ctx_out else None
        else:
            w_qkv = c_w_qkv[j].astype(BF16)
            w_post = c_w_o[j].astype(BF16)
            q_g = c_q_g[j].reshape(1, HEAD_DIM)
            k_g = c_k_g[j].reshape(1, HEAD_DIM)
            q, k, v = _qkv_latent(x, mods, ng, w_qkv, q_g, k_g)
            if ctx_in:
                kc, vc = _kv_ctx(ctx, mods, ng, w_qkv[:, N_HEADS * HEAD_DIM:], k_g)
                k = jnp.concatenate([kc, k], axis=2)
                v = jnp.concatenate([vc, v], axis=2)
            ones_row = (jnp.arange(BF16_ROWS) == 0).astype(BF16)[:, None]
            pad = jnp.broadcast_to(ones_row, v.shape[:2] + (BF16_ROWS, v.shape[2]))
            vt_ext = jnp.concatenate([jnp.swapaxes(v, 2, 3), pad], axis=2)
            z = _attention(q, k, vt_ext)
            zc = None
            assert not ctx_out

        x = _tail(x, z, mods, ng, w_post, w1_all, w2_all, i, False, z_mirrored=(kind == 1))
        if ctx_out:
            ctx = _tail(ctx, zc, mods, ng, w_post, w1_all, w2_all, i, True)

    return x
```

```python
import functools

import jax
import jax.numpy as jnp
from jax import lax
from jax.experimental import pallas as pl
from jax.experimental.pallas import tpu as pltpu

D_MODEL = 1024
DEPTH = 4
GRID_W = 64
N_MIXERS = 3
CHUNK = 128
A_WIDTH = 2 * D_MODEL
A_GROUPS = 8
A_GROUP_W = A_WIDTH // A_GROUPS
B_GROUPS = 4
B_GROUP_W = D_MODEL // B_GROUPS
HEAD_DIM = 128
N_HEADS = D_MODEL // HEAD_DIM
N_KV_HEADS = 2
HEADS_PER_KV = N_HEADS // N_KV_HEADS
ROPE_THETA = 10000.0
D_FF = 4 * D_MODEL
NORM_EPS = 1e-6
LN_EPS = 1e-5
LOG2_E = 1.4426950408889634

F32 = jnp.float32
BF16 = jnp.bfloat16

MXU_COLS = 256
ROW_BLOCK = 64
BF16_ROWS = 16
LANES = 128
COND_ROWS = 16
MIB = 1024 * 1024


def _dot(a, b):
    return jnp.dot(a, b, preferred_element_type=F32)


def _rms(x, g):
    ms = jnp.mean(x * x, axis=-1, keepdims=True)
    return x * lax.rsqrt(ms + NORM_EPS) * g


def _gelu_tanh(x):
    c = 2.0 * 0.7978845608028654 * LOG2_E
    z = x * (-c - (c * 0.044715) * (x * x))
    return x / (1.0 + jnp.exp2(z))


def _const_spec(shape):
    zeros = (0,) * len(shape)
    return pl.BlockSpec(shape, lambda b, t: zeros, pipeline_mode=pl.Buffered(1))


def _mod_spec(is_ctx, batch):
    if is_ctx:
        return pl.BlockSpec((None, 6, D_MODEL), lambda b, t: (batch, 0, 0))
    return pl.BlockSpec((None, 6, D_MODEL), lambda b, t: (b, 0, 0))


def _params(vmem_mib):
    return pltpu.CompilerParams(
        dimension_semantics=("arbitrary", "arbitrary"),
        vmem_limit_bytes=vmem_mib * MIB)


def _modulated_norm_to(h_s, x_ref, mod_ref, ng_ref, start=0, stop=None):
    sh1 = mod_ref[0:1, :]
    sc1p = 1.0 + mod_ref[1:2, :]
    g0 = ng_ref[0:1, :]
    stop = x_ref.shape[0] if stop is None else stop
    for r in range(start, stop, ROW_BLOCK):
        rows = slice(r, r + ROW_BLOCK)
        h_s[rows, :] = (_rms(x_ref[rows, :], g0) * sc1p + sh1).astype(BF16)


def _ada_kernel(cond_ref, w_ref, b_ref, o_ref):
    a = cond_ref[...]
    a = a * jax.nn.sigmoid(a)
    o_ref[...] = _dot(a.astype(BF16), w_ref[...].astype(BF16)) + b_ref[...]


def _ada_all(cond, ada_w, ada_b):
    tn = 1536
    n_out = 6 * D_MODEL
    return pl.pallas_call(
        _ada_kernel,
        grid=(DEPTH, n_out // tn),
        in_specs=[
            pl.BlockSpec((COND_ROWS, D_MODEL), lambda i, n: (0, 0)),
            pl.BlockSpec((None, D_MODEL, tn), lambda i, n: (i, 0, n)),
            pl.BlockSpec((None, 1, tn), lambda i, n: (i, 0, n)),
        ],
        out_specs=pl.BlockSpec((None, COND_ROWS, tn), lambda i, n: (i, 0, n)),
        out_shape=jax.ShapeDtypeStruct((DEPTH, COND_ROWS, n_out), F32),
        compiler_params=_params(32),
        name="ada_mod",
    )(cond, ada_w, ada_b.reshape(DEPTH, 1, n_out))


def _gmlp_head_kernel(x_ref, mod_ref, ng_ref, win_ref, lng_ref, ws_ref, bs_ref, z_ref,
                      h_s, u_s, v_s, vb_s, sum_s, sq_s):
    tm = x_ref.shape[0]
    width = 2 * MXU_COLS
    n_half = A_WIDTH // width
    lng = lng_ref[...]
    half = tm // 2
    lanes = sum_s.shape[1]

    def proj(unit):
        c, a = unit
        return _dot(h_s[a * half:(a + 1) * half, :], win_ref[:, c * width:(c + 1) * width])

    def gelu_store(d, unit):
        c, a = unit
        is_v = c >= n_half
        dst, c = (v_s, c - n_half) if is_v else (u_s, c)
        for r in range(0, half, ROW_BLOCK):
            rows = slice(a * half + r, a * half + r + ROW_BLOCK)
            g = _gelu_tanh(d[r:r + ROW_BLOCK, :])
            dst[rows, c * width:(c + 1) * width] = g
            if is_v:
                parts = [g[:, i:i + lanes] for i in range(0, width, lanes)]
                p1 = sum(parts[1:], parts[0])
                p2 = sum([p * p for p in parts[1:]], parts[0] * parts[0])
                if c == 0:
                    sum_s[rows, :] = p1
                    sq_s[rows, :] = p2
                else:
                    sum_s[rows, :] += p1
                    sq_s[rows, :] += p2

    def layer_norm_rows(part):
        for r in range(part * tm // n_half, (part + 1) * tm // n_half, ROW_BLOCK // 2):
            rows = slice(r, r + ROW_BLOCK // 2)
            mu = jnp.sum(sum_s[rows, :], axis=-1, keepdims=True) * (1.0 / A_WIDTH)
            ex2 = jnp.sum(sq_s[rows, :], axis=-1, keepdims=True) * (1.0 / A_WIDTH)
            rstd = lax.rsqrt(ex2 - mu * mu + LN_EPS)
            vb_s[rows, :] = ((v_s[rows, :] - mu) * rstd * lng).astype(BF16)

    chunks = list(range(n_half, 2 * n_half)) + list(range(n_half))
    order = [(c, a) for c in chunks for a in range(2)]
    _modulated_norm_to(h_s, x_ref, mod_ref, ng_ref, 0, half)
    d = proj(order[0])
    _modulated_norm_to(h_s, x_ref, mod_ref, ng_ref, half, tm)
    for i in range(1, len(order)):
        d_next = proj(order[i])
        gelu_store(d, order[i - 1])
        if i > 2 * n_half and i % 2 == 0:
            layer_norm_rows((i - 2 * n_half) // 2 - 1)
        d = d_next
    gelu_store(d, order[-1])
    layer_norm_rows(n_half - 1)
    for g in range(A_GROUPS):
        w = ws_ref[g]
        b = bs_ref[g]
        cols = slice(g * A_GROUP_W, (g + 1) * A_GROUP_W)
        for n in range(tm // CHUNK):
            rows = slice(n * CHUNK, (n + 1) * CHUNK)
            sv = _dot(w, vb_s[rows, cols]) + b
            z_ref[rows, cols] = (u_s[rows, cols] * sv).astype(BF16)


def _gmlp_head(x, mods, ng, w_in, ln_g, w_s, b_s, is_ctx):
    bsz, length, _ = x.shape
    tm = min(1024, length)
    return pl.pallas_call(
        _gmlp_head_kernel,
        grid=(bsz, length // tm),
        in_specs=[
            pl.BlockSpec((None, tm, D_MODEL), lambda b, t: (b, t, 0)),
            _mod_spec(is_ctx, bsz),
            _const_spec((4, D_MODEL)),
            _const_spec((D_MODEL, 2 * A_WIDTH)),
            _const_spec((1, A_WIDTH)),
            _const_spec((A_GROUPS, CHUNK, CHUNK)),
            _const_spec((A_GROUPS, CHUNK, 1)),
        ],
        out_specs=pl.BlockSpec((None, tm, A_WIDTH), lambda b, t: (b, t, 0)),
        out_shape=jax.ShapeDtypeStruct((bsz, length, A_WIDTH), BF16),
        scratch_shapes=[
            pltpu.VMEM((tm, D_MODEL), BF16),
            pltpu.VMEM((tm, A_WIDTH), F32),
            pltpu.VMEM((tm, A_WIDTH), F32),
            pltpu.VMEM((tm, A_WIDTH), BF16),
            pltpu.VMEM((tm, LANES), F32),
            pltpu.VMEM((tm, LANES), F32),
        ],
        compiler_params=_params(56),
        name="gmlp_head",
    )(x, mods, ng, w_in, ln_g, w_s, b_s)


def _interleave(dots, fillers):
    per = -(-len(fillers) // len(dots))
    for i, dot_fn in enumerate(dots):
        dot_fn()
        for filler in fillers[i * per:(i + 1) * per]:
            filler()


def _tail_kernel(x_ref, z_ref, mod_ref, ng_ref, wp_ref, w1_ref, w2_ref, o_ref, *scratch,
                 sub, z_mirrored):
    n_sub = x_ref.shape[0] // sub
    weights = (mod_ref, ng_ref, wp_ref, w1_ref, w2_ref)
    if n_sub == 1:
        _tail_subtile(x_ref, z_ref, *weights, o_ref, *scratch)
        return

    def body(i, carry):
        zi = i
        if z_mirrored:
            second_half = pl.program_id(1) >= pl.num_programs(1) // 2
            zi = jnp.where(second_half, n_sub - 1 - i, i)
        rows = pl.ds(pl.multiple_of(i * sub, sub), sub)
        z_rows = pl.ds(pl.multiple_of(zi * sub, sub), sub)
        _tail_subtile(x_ref.at[rows, :], z_ref.at[z_rows, :], *weights, o_ref.at[rows, :], *scratch)
        return carry

    lax.fori_loop(0, n_sub, body, 0)


def _tail_subtile(x_ref, z_ref, mod_ref, ng_ref, wp_ref, w1_ref, w2_ref, o_ref, *scratch):
    half = x_ref.shape[0] // 2
    y_s, hm_s, hid_s = scratch[0:2], scratch[2:4], scratch[4:6]
    g1 = mod_ref[2:3, :]
    sh2 = mod_ref[3:4, :]
    sc2p = 1.0 + mod_ref[4:5, :]
    g2 = mod_ref[5:6, :]
    ff_cols = 2 * MXU_COLS
    out_chunks = [slice(n * MXU_COLS, (n + 1) * MXU_COLS) for n in range(D_MODEL // MXU_COLS)]
    ff_chunks = [slice(c * ff_cols, (c + 1) * ff_cols) for c in range(D_FF // ff_cols)]
    blocks = list(range(0, half, ROW_BLOCK))

    def post_dot(a, cols):
        y_s[a][:, cols] = _dot(z_ref[a * half:(a + 1) * half, :], wp_ref[:, cols])

    def norm1(a, r):
        rows = slice(a * half + r, a * half + r + ROW_BLOCK)
        x1 = x_ref[rows, :] + g1 * _rms(y_s[a][r:r + ROW_BLOCK, :], ng_ref[1:2, :])
        o_ref[rows, :] = x1
        hm_s[a][r:r + ROW_BLOCK, :] = (_rms(x1, ng_ref[2:3, :]) * sc2p + sh2).astype(BF16)

    def up_dot(a, cols):
        t = jnp.maximum(_dot(hm_s[a][...], w1_ref[:, cols]), 0.0)
        hid_s[a][:, cols] = (t * t).astype(BF16)

    def down_dot(a, cols):
        y_s[a][:, cols] = _dot(hid_s[a][...], w2_ref[:, cols])

    def norm3(a, r):
        rows = slice(a * half + r, a * half + r + ROW_BLOCK)
        o_ref[rows, :] = o_ref[rows, :] + g2 * _rms(y_s[a][r:r + ROW_BLOCK, :], ng_ref[3:4, :])

    def each(fn, a, items):
        return [functools.partial(fn, a, item) for item in items]

    _interleave(each(post_dot, 0, out_chunks), [])
    _interleave(each(post_dot, 1, out_chunks), each(norm1, 0, blocks))
    _interleave(each(up_dot, 0, ff_chunks), each(norm1, 1, blocks))
    _interleave(each(up_dot, 1, ff_chunks), [])
    _interleave(each(down_dot, 0, out_chunks), [])
    _interleave(each(down_dot, 1, out_chunks), each(norm3, 0, blocks))
    _interleave(each(norm3, 1, blocks), [])


def _tail(x, z, mods, ng, w_post, w1_all, w2_all, layer, is_ctx, z_mirrored=False):
    bsz, length, _ = x.shape
    kz = z.shape[-1]
    sub = min(512, length)
    tm = min(1024, length)
    if z_mirrored:
        nt = length // 2 // tm
        z_spec = pl.BlockSpec(
            (None, None, tm, kz),
            lambda b, t: (b, t // nt, jnp.where(t < nt, t, 2 * nt - 1 - t), 0))
    else:
        z_spec = pl.BlockSpec((None, tm, kz), lambda b, t: (b, t, 0))
    return pl.pallas_call(
        functools.partial(_tail_kernel, sub=sub, z_mirrored=z_mirrored),
        grid=(bsz, length // tm),
        in_specs=[
            pl.BlockSpec((None, tm, D_MODEL), lambda b, t: (b, t, 0)),
            z_spec,
            _mod_spec(is_ctx, bsz),
            _const_spec((4, D_MODEL)),
            _const_spec((kz, D_MODEL)),
            pl.BlockSpec((None, D_MODEL, D_FF), lambda b, t: (layer, 0, 0), pipeline_mode=pl.Buffered(1)),
            pl.BlockSpec((None, D_FF, D_MODEL), lambda b, t: (layer, 0, 0), pipeline_mode=pl.Buffered(1)),
        ],
        out_specs=pl.BlockSpec((None, tm, D_MODEL), lambda b, t: (b, t, 0)),
        out_shape=jax.ShapeDtypeStruct((bsz, length, D_MODEL), F32),
        scratch_shapes=(
            [pltpu.VMEM((sub // 2, D_MODEL), F32)] * 2
            + [pltpu.VMEM((sub // 2, D_MODEL), BF16)] * 2
            + [pltpu.VMEM((sub // 2, D_FF), BF16)] * 2),
        compiler_params=_params(58),
        name="tail",
    )(x, z, mods, ng, w_post, w1_all, w2_all)


def _fourier_chan_kernel(x_ref, mod_ref, ng_ref, cs_ref, ab_ref, h_s):
    _modulated_norm_to(h_s, x_ref, mod_ref, ng_ref)
    for g in range(B_GROUPS):
        cols = slice(g * B_GROUP_W, (g + 1) * B_GROUP_W)
        t = _dot(h_s[:, cols], cs_ref[...])
        ab_ref[0, :, cols] = t[:, :B_GROUP_W].astype(BF16)
        ab_ref[1, :, cols] = t[:, B_GROUP_W:].astype(BF16)


def _fourier_seq_kernel(cs_ref, ab_ref, y_ref):
    for n in range(D_MODEL // MXU_COLS):
        cols = slice(n * MXU_COLS, (n + 1) * MXU_COLS)
        y_ref[:, cols] = _dot(cs_ref[...], ab_ref[:, cols]).astype(BF16)


def _dft_tables(length):
    m = jnp.arange(B_GROUP_W, dtype=jnp.int32)
    ang_c = (2.0 * jnp.pi / B_GROUP_W) * ((m[:, None] * m[None, :]) % B_GROUP_W).astype(F32)
    s_c = B_GROUP_W ** -0.5
    cs_chan = jnp.concatenate([jnp.cos(ang_c), jnp.sin(ang_c)], axis=1) * s_c
    k = jnp.arange(length, dtype=jnp.int32)
    ang_l = (2.0 * jnp.pi / length) * ((k[:, None] * k[None, :]) % length).astype(F32)
    s_l = length ** -0.5
    cs_seq = jnp.concatenate([jnp.cos(ang_l), -jnp.sin(ang_l)], axis=1) * s_l
    return cs_chan.astype(BF16), cs_seq.astype(BF16)


def _fourier_fold_kernel(xp_ref, xm_ref, x0_ref, mod_ref, ng_ref, cc_ref, sc_ref, rev_ref,
                         ab_ref, a0_ref, hp_s, hm_s):
    tm = xm_ref.shape[0]
    sh1 = mod_ref[0:1, :]
    sc1p = 1.0 + mod_ref[1:2, :]
    g0 = ng_ref[0:1, :]
    for r in range(0, tm, ROW_BLOCK):
        rows = slice(r, r + ROW_BLOCK)
        hp_s[rows, :] = _rms(xp_ref[0, r + 1:r + 1 + ROW_BLOCK, :], g0) * sc1p + sh1
        hm_s[rows, :] = (_rms(xm_ref[rows, :], g0) * sc1p + sh1).astype(BF16)
    h0 = (_rms(x0_ref[...], g0) * sc1p + sh1).astype(BF16)
    for g in range(B_GROUPS):
        cols = slice(g * B_GROUP_W, (g + 1) * B_GROUP_W)
        hm_rev = _dot(rev_ref[...], hm_s[:, cols])
        hp = hp_s[:, cols]
        ab_ref[0, :, cols] = _dot((hp + hm_rev).astype(BF16), cc_ref[...]).astype(BF16)
        ab_ref[1, :, cols] = _dot((hp - hm_rev).astype(BF16), sc_ref[...]).astype(BF16)
        a0_ref[:, cols] = _dot(h0[:, cols], cc_ref[...])


def _fourier_seq_fold_kernel(ct_ref, st_ref, ab_ref, a0_ref, perm_ref, y_ref, w_s, *, scale):
    tk = y_ref.shape[1]
    half = ct_ref.shape[1]
    for n in range(D_MODEL // MXU_COLS):
        cols = slice(n * MXU_COLS, (n + 1) * MXU_COLS)
        yc = _dot(ct_ref[...], ab_ref[0:half, cols]) + a0_ref[0:1, cols] * scale
        ys = _dot(st_ref[...], ab_ref[half:2 * half, cols])
        y_ref[0, :, cols] = (yc[:tk] - ys[:tk]).astype(BF16)
        w_s[:, cols] = (yc + ys).astype(BF16)
    for n in range(D_MODEL // MXU_COLS):
        cols = slice(n * MXU_COLS, (n + 1) * MXU_COLS)
        y_ref[1, :, cols] = _dot(perm_ref[...], w_s[:, cols]).astype(BF16)


def _seq_fold_tables(length, tk):
    half = length // 2
    nt = half // tk
    rows = tk + BF16_ROWS
    fine = 64
    n_coarse = -(-(nt * tk + BF16_ROWS) // fine)
    t = jnp.arange(1, half + 1, dtype=jnp.int32)

    def cos_sin(k):
        ang = (2.0 * jnp.pi / length) * ((k[:, None] * t[None, :]) % length).astype(F32)
        return jnp.cos(ang), jnp.sin(ang)

    ch, sh = cos_sin(jnp.arange(n_coarse, dtype=jnp.int32) * fine)
    cl, sl = cos_sin(jnp.arange(fine, dtype=jnp.int32))
    weight = jnp.where(t == half, 0.5, 1.0) * length ** -0.5
    c = (ch[:, None, :] * cl[None] - sh[:, None, :] * sl[None]).reshape(n_coarse * fine, half)
    s = (sh[:, None, :] * cl[None] + ch[:, None, :] * sl[None]).reshape(n_coarse * fine, half)
    ct = jnp.stack([c[i * tk:i * tk + rows] for i in range(nt)]) * weight
    st = jnp.stack([s[i * tk:i * tk + rows] for i in range(nt)]) * weight
    return ct.astype(BF16), st.astype(BF16)


def _fourier_head_folded(x, mods, ng):
    bsz, length, _ = x.shape
    tm = 512
    half = length // 2
    nt = half // tm
    rows = tm + BF16_ROWS
    m = jnp.arange(B_GROUP_W, dtype=jnp.int32)
    ang_c = (2.0 * jnp.pi / B_GROUP_W) * ((m[:, None] * m[None, :]) % B_GROUP_W).astype(F32)
    cc = (jnp.cos(ang_c) * B_GROUP_W ** -0.5).astype(BF16)
    sc = (jnp.sin(ang_c) * B_GROUP_W ** -0.5).astype(BF16)
    rev = (jnp.arange(tm)[:, None] + jnp.arange(tm)[None, :] == tm - 1).astype(BF16)
    ab, a0 = pl.pallas_call(
        _fourier_fold_kernel,
        grid=(bsz, nt),
        in_specs=[
            pl.BlockSpec((pl.Element(1), pl.Element(tm + 8), pl.Element(D_MODEL)),
                         lambda b, t: (b, t * tm, 0)),
            pl.BlockSpec((None, tm, D_MODEL), lambda b, t: (b, 2 * nt - 1 - t, 0)),
            pl.BlockSpec((None, 8, D_MODEL), lambda b, t: (b, 0, 0)),
            _mod_spec(False, bsz),
            _const_spec((4, D_MODEL)),
            _const_spec((B_GROUP_W, B_GROUP_W)),
            _const_spec((B_GROUP_W, B_GROUP_W)),
            _const_spec((tm, tm)),
        ],
        out_specs=[
            pl.BlockSpec((None, 2, tm, D_MODEL), lambda b, t: (b, 0, t, 0)),
            pl.BlockSpec((None, 8, D_MODEL), lambda b, t: (b, 0, 0)),
        ],
        out_shape=[
            jax.ShapeDtypeStruct((bsz, 2, half, D_MODEL), BF16),
            jax.ShapeDtypeStruct((bsz, 8, D_MODEL), F32),
        ],
        scratch_shapes=[pltpu.VMEM((tm, D_MODEL), F32), pltpu.VMEM((tm, D_MODEL), BF16)],
        compiler_params=_params(32),
        name="fourier_fold",
    )(x, x, x, mods, ng, cc, sc, rev)
    ct, st = _seq_fold_tables(length, tm)
    perm = (jnp.arange(tm)[:, None] + jnp.arange(rows)[None, :] == tm).astype(BF16)
    return pl.pallas_call(
        functools.partial(_fourier_seq_fold_kernel, scale=length ** -0.5),
        grid=(bsz, nt),
        in_specs=[
            pl.BlockSpec((None, rows, half), lambda b, t: (t, 0, 0)),
            pl.BlockSpec((None, rows, half), lambda b, t: (t, 0, 0)),
            pl.BlockSpec((None, 2 * half, D_MODEL), lambda b, t: (b, 0, 0)),
            pl.BlockSpec((None, 8, D_MODEL), lambda b, t: (b, 0, 0)),
            _const_spec((tm, rows)),
        ],
        out_specs=pl.BlockSpec((None, 2, tm, D_MODEL), lambda b, t: (b, 0, t, 0)),
        out_shape=jax.ShapeDtypeStruct((bsz, 2, half, D_MODEL), BF16),
        scratch_shapes=[pltpu.VMEM((rows, D_MODEL), BF16)],
        compiler_params=_params(40),
        name="fourier_seq_fold",
    )(ct, st, ab.reshape(bsz, 2 * half, D_MODEL), a0, perm)


def _fourier_head(x, mods, ng, is_ctx):
    bsz, length, _ = x.shape
    tm = min(512, length)
    cs_chan, cs_seq = _dft_tables(length)
    ab = pl.pallas_call(
        _fourier_chan_kernel,
        grid=(bsz, length // tm),
        in_specs=[
            pl.BlockSpec((None, tm, D_MODEL), lambda b, t: (b, t, 0)),
            _mod_spec(is_ctx, bsz),
            _const_spec((4, D_MODEL)),
            _const_spec((B_GROUP_W, 2 * B_GROUP_W)),
        ],
        out_specs=pl.BlockSpec((None, 2, tm, D_MODEL), lambda b, t: (b, 0, t, 0)),
        out_shape=jax.ShapeDtypeStruct((bsz, 2, length, D_MODEL), BF16),
        scratch_shapes=[pltpu.VMEM((tm, D_MODEL), BF16)],
        compiler_params=_params(32),
        name="fourier_chan",
    )(x, mods, ng, cs_chan)
    ab = ab.reshape(bsz, 2 * length, D_MODEL)
    return pl.pallas_call(
        _fourier_seq_kernel,
        grid=(bsz, length // tm),
        in_specs=[
            pl.BlockSpec((tm, 2 * length), lambda b, t: (t, 0)),
            pl.BlockSpec((None, 2 * length, D_MODEL), lambda b, t: (b, 0, 0),
                         pipeline_mode=pl.Buffered(1)),
        ],
        out_specs=pl.BlockSpec((None, tm, D_MODEL), lambda b, t: (b, t, 0)),
        out_shape=jax.ShapeDtypeStruct((bsz, length, D_MODEL), BF16),
        compiler_params=_params(48),
        name="fourier_seq",
    )(cs_seq, ab)


def _head_rms(r, g):
    return r * lax.rsqrt(jnp.mean(r * r, axis=-1, keepdims=True) + NORM_EPS) * g


def _qkv_kernel(x_ref, mod_ref, ng_ref, w_ref, qg_ref, kg_ref, mean_ref, cos_ref, sin_ref,
                q_ref, k_ref, v_ref, h_s):
    _modulated_norm_to(h_s, x_ref, mod_ref, ng_ref)
    cosf = cos_ref[...]
    sinf = sin_ref[...]

    qg = qg_ref[...] * (HEAD_DIM ** -0.5 * LOG2_E)
    gains = [jnp.concatenate([qg, qg], axis=1)] * (N_HEADS // 2)
    gains.append(jnp.concatenate([kg_ref[...], kg_ref[...]], axis=1))
    outs = [(q_ref, 2 * p) for p in range(N_HEADS // 2)] + [(k_ref, 0)]
    n_pairs = len(outs)

    def project(p):
        return _dot(h_s[...], w_ref[:, p * MXU_COLS:(p + 1) * MXU_COLS])

    def normalise(t, p):
        sq = t * t
        hi = sq.astype(BF16)
        lo = (sq - hi.astype(F32)).astype(BF16)
        ms = _dot(hi, mean_ref[...]) + _dot(lo, mean_ref[...])
        return t * lax.rsqrt(ms + NORM_EPS) * gains[p]

    def rope_store(rn, p):
        ref, first = outs[p]
        for e in range(2):
            r = rn[:, e * HEAD_DIM:(e + 1) * HEAD_DIM]
            ref[first + e] = (r * cosf + pltpu.roll(r, HEAD_DIM // 2, 1) * sinf).astype(BF16)

    t_next = project(0)
    rn_prev = None
    for p in range(n_pairs + 1):
        t = t_next
        if p < n_pairs:
            t_next = project(p + 1)
        rn = normalise(t, p) if p < n_pairs else None
        if rn_prev is not None:
            rope_store(rn_prev, p - 1)
        rn_prev = rn
    for e in range(N_KV_HEADS):
        v_ref[e] = t[:, e * HEAD_DIM:(e + 1) * HEAD_DIM].astype(BF16)


def _kv_ctx_kernel(x_ref, mod_ref, ng_ref, w_ref, kg_ref, k_ref, v_ref, h_s):
    _modulated_norm_to(h_s, x_ref, mod_ref, ng_ref)
    t = _dot(h_s[...], w_ref[:, 0:MXU_COLS])
    for e in range(N_KV_HEADS):
        k_ref[e] = _head_rms(t[:, e * HEAD_DIM:(e + 1) * HEAD_DIM], kg_ref[...]).astype(BF16)
    t = _dot(h_s[...], w_ref[:, MXU_COLS:2 * MXU_COLS])
    for e in range(N_KV_HEADS):
        v_ref[e] = t[:, e * HEAD_DIM:(e + 1) * HEAD_DIM].astype(BF16)


def _rope_tables(length):
    t = jnp.arange(length)
    row = (t // GRID_W).astype(F32)
    col = (t % GRID_W).astype(F32)
    n_freq = HEAD_DIM // 4
    inv = ROPE_THETA ** (-jnp.arange(n_freq, dtype=F32) / n_freq)
    ang = jnp.concatenate([row[:, None] * inv, col[:, None] * inv], axis=-1)
    cos, sin = jnp.cos(ang), jnp.sin(ang)
    return jnp.concatenate([cos, cos], axis=-1), jnp.concatenate([-sin, sin], axis=-1)


def _qkv_latent(x, mods, ng, w_qkv, q_g, k_g):
    bsz, length, _ = x.shape
    tm = 512
    cosf, sinf = _rope_tables(length)
    n_qkv = w_qkv.shape[-1]
    lane_head = jnp.arange(MXU_COLS) // HEAD_DIM
    head_mean = ((lane_head[:, None] == lane_head[None, :]) * (1.0 / HEAD_DIM)).astype(BF16)
    kv_shape = jax.ShapeDtypeStruct((bsz, N_KV_HEADS, length, HEAD_DIM), BF16)
    kv_spec = pl.BlockSpec((None, N_KV_HEADS, tm, HEAD_DIM), lambda b, t: (b, 0, t, 0))
    return pl.pallas_call(
        _qkv_kernel,
        grid=(bsz, length // tm),
        in_specs=[
            pl.BlockSpec((None, tm, D_MODEL), lambda b, t: (b, t, 0)),
            _mod_spec(False, bsz),
            _const_spec((4, D_MODEL)),
            _const_spec((D_MODEL, n_qkv)),
            _const_spec((1, HEAD_DIM)),
            _const_spec((1, HEAD_DIM)),
            _const_spec((MXU_COLS, MXU_COLS)),
            pl.BlockSpec((tm, HEAD_DIM), lambda b, t: (t, 0)),
            pl.BlockSpec((tm, HEAD_DIM), lambda b, t: (t, 0)),
        ],
        out_specs=[
            pl.BlockSpec((None, N_HEADS, tm, HEAD_DIM), lambda b, t: (b, 0, t, 0)),
            kv_spec, kv_spec,
        ],
        out_shape=[
            jax.ShapeDtypeStruct((bsz, N_HEADS, length, HEAD_DIM), BF16),
            kv_shape, kv_shape,
        ],
        scratch_shapes=[pltpu.VMEM((tm, D_MODEL), BF16)],
        compiler_params=_params(32),
        name="qkv_latent",
    )(x, mods, ng, w_qkv, q_g, k_g, head_mean, cosf, sinf)


def _kv_ctx(ctx, mods, ng, w_kv, k_g):
    bsz, length, _ = ctx.shape
    tm = length
    kv_shape = jax.ShapeDtypeStruct((bsz, N_KV_HEADS, length, HEAD_DIM), BF16)
    kv_spec = pl.BlockSpec((None, N_KV_HEADS, tm, HEAD_DIM), lambda b, t: (b, 0, t, 0))
    return pl.pallas_call(
        _kv_ctx_kernel,
        grid=(bsz, length // tm),
        in_specs=[
            pl.BlockSpec((None, tm, D_MODEL), lambda b, t: (b, t, 0)),
            _mod_spec(True, bsz),
            _const_spec((4, D_MODEL)),
            _const_spec((D_MODEL, 2 * MXU_COLS)),
            _const_spec((1, HEAD_DIM)),
        ],
        out_specs=[kv_spec, kv_spec],
        out_shape=[kv_shape, kv_shape],
        scratch_shapes=[pltpu.VMEM((tm, D_MODEL), BF16)],
        compiler_params=_params(32),
        name="kv_ctx",
    )(ctx, mods, ng, w_kv, k_g)


def _region(index, fn):
    pl.when(pl.program_id(0) >= -index)(fn)


def _attn_kernel(q_ref, k_ref, vt_ref, o_ref, *scratch):
    tq = q_ref.shape[1]
    n_keys = k_ref.shape[1]
    nq = HEADS_PER_KV * tq
    q4_s, s_s, p_s, m_s = (scratch[i * N_KV_HEADS:(i + 1) * N_KV_HEADS] for i in range(4))
    key_chunks = [slice(r, r + MXU_COLS) for r in range(0, n_keys, MXU_COLS)]

    @pl.when(pl.program_id(0) == 0)
    def _():
        s_s[1][...] = jnp.zeros(s_s[1].shape, F32)
        m_s[1][...] = jnp.zeros(m_s[1].shape, F32)
        p_s[0][...] = jnp.ones(p_s[0].shape, BF16)

    def stage(j_scores, j_probs, j_pv):
        for h in range(HEADS_PER_KV):
            q4_s[j_scores][h * tq:(h + 1) * tq, :] = q_ref[HEADS_PER_KV * j_scores + h]
        m_probs = m_s[j_probs][...]
        m = None
        acc = None

        def store_head(acc, h):
            cols = slice(h * tq, (h + 1) * tq)
            o = acc[:HEAD_DIM, cols] / acc[HEAD_DIM:HEAD_DIM + 1, cols]
            head = HEADS_PER_KV * j_pv + h
            o_ref[:, head * HEAD_DIM:(head + 1) * HEAD_DIM] = o.T.astype(BF16)

        pv_chunks = list(key_chunks)
        heads_left = list(range(HEADS_PER_KV))
        tiles = [slice(r, r + BF16_ROWS) for r in range(0, n_keys, BF16_ROWS)]
        pv_rows_per_score_row = vt_ref.shape[1] / MXU_COLS
        work_total = len(key_chunks) * (1.0 + pv_rows_per_score_row)
        work_done = 0.0
        tiles_done = 0
        for rows in key_chunks:
            s = lax.dot_general(k_ref[j_scores, rows, :], q4_s[j_scores][...],
                                (((1,), (1,)), ((), ())), preferred_element_type=F32)
            s_s[j_scores][rows, :] = s
            cm = jnp.max(s, axis=0, keepdims=True)
            m = cm if m is None else jnp.maximum(m, cm)
            work_done += 1.0
            if pv_chunks:
                for pv_rows in (pv_chunks.pop(0) for _ in range(min(2, len(pv_chunks)))):
                    d = _dot(vt_ref[j_pv, :, pv_rows], p_s[j_pv][pv_rows, :])
                    acc = d if acc is None else acc + d
                    work_done += pv_rows_per_score_row
            elif heads_left:
                store_head(acc, heads_left.pop(0))
            tiles_until = round(len(tiles) * work_done / work_total)
            for tile in tiles[tiles_done:tiles_until]:
                p_s[j_probs][tile, :] = jnp.exp2(s_s[j_probs][tile, :] - m_probs).astype(BF16)
            tiles_done = tiles_until
        assert not pv_chunks and tiles_done == len(tiles)
        for h in heads_left:
            store_head(acc, h)
        m_s[j_scores][...] = jnp.broadcast_to(m, (BF16_ROWS, nq))

    _region(0, lambda: stage(0, 1, 0))
    _region(1, lambda: stage(1, 0, 1))


def _attention(q, k, vt_ext):
    bsz, _, length, _ = q.shape
    n_keys = k.shape[2]
    vt_rows = vt_ext.shape[2]
    tq = 128
    nq = HEADS_PER_KV * tq
    per_batch = length // tq
    n_tiles = bsz * per_batch

    def cur(i):
        return jnp.minimum(i, n_tiles - 1)

    def prev(i):
        return jnp.maximum(i - 1, 0)

    return pl.pallas_call(
        _attn_kernel,
        grid=(n_tiles + 1,),
        in_specs=[
            pl.BlockSpec((None, N_HEADS, tq, HEAD_DIM),
                         lambda i: (cur(i) // per_batch, 0, cur(i) % per_batch, 0)),
            pl.BlockSpec((None, N_KV_HEADS, n_keys, HEAD_DIM), lambda i: (cur(i) // per_batch, 0, 0, 0)),
            pl.BlockSpec((None, N_KV_HEADS, vt_rows, n_keys), lambda i: (prev(i) // per_batch, 0, 0, 0)),
        ],
        out_specs=pl.BlockSpec((None, tq, D_MODEL),
                               lambda i: (prev(i) // per_batch, prev(i) % per_batch, 0)),
        out_shape=jax.ShapeDtypeStruct((bsz, length, D_MODEL), BF16),
        scratch_shapes=(
            [pltpu.VMEM((nq, HEAD_DIM), BF16)] * N_KV_HEADS
            + [pltpu.VMEM((n_keys, nq), F32)] * N_KV_HEADS
            + [pltpu.VMEM((n_keys, nq), BF16)] * N_KV_HEADS
            + [pltpu.VMEM((BF16_ROWS, nq), F32)] * N_KV_HEADS),
        compiler_params=pltpu.CompilerParams(
            dimension_semantics=("arbitrary",), vmem_limit_bytes=54 * MIB),
        name="attention",
    )(q, k, vt_ext)


def kernel(x, c, ctx, c_ctx, ada_w, ada_b, norm_g, mlp_w1, mlp_w2, a_w_in, a_ln_g, a_w_s, a_b_s, a_w_out,
           b_w_out, c_w_qkv, c_q_g, c_k_g, c_w_o):
    bsz = x.shape[0]
    attn_layers = [i for i in range(DEPTH) if i % N_MIXERS == 2]
    last_ctx_read = attn_layers[-1] if attn_layers else -1

    cond = jnp.zeros((COND_ROWS, D_MODEL), F32).at[:bsz].set(c).at[bsz].set(c_ctx)
    mods_all = _ada_all(cond, ada_w, ada_b).reshape(DEPTH, COND_ROWS, 6, D_MODEL)
    w1_all = mlp_w1.astype(BF16)
    w2_all = mlp_w2.astype(BF16)


    for i in range(DEPTH):
        kind, j = i % N_MIXERS, i // N_MIXERS
        ctx_in = i <= last_ctx_read
        ctx_out = i < last_ctx_read
        mods = mods_all[i]
        ng = norm_g[i]

        if kind == 0:
            w_in = a_w_in[j].astype(BF16)
            ln_g = a_ln_g[j].reshape(1, A_WIDTH)
            w_s = a_w_s[j].astype(BF16)
            b_s = a_b_s[j].reshape(A_GROUPS, CHUNK, 1)
            w_post = a_w_out[j].astype(BF16)
            z = _gmlp_head(x, mods, ng, w_in, ln_g, w_s, b_s, False)
            zc = _gmlp_head(ctx, mods, ng, w_in, ln_g, w_s, b_s, True) if ctx_out else None
        elif kind == 1:
            w_post = b_w_out[j].astype(BF16)
            z = _fourier_head_folded(x, mods, ng)
            zc = _fourier_head(ctx, mods, ng, True) if ctx_out else None
        else:
            w_qkv = c_w_qkv[j].astype(BF16)
            w_post = c_w_o[j].astype(BF16)
            q_g = c_q_g[j].reshape(1, HEAD_DIM)
            k_g = c_k_g[j].reshape(1, HEAD_DIM)
            q, k, v = _qkv_latent(x, mods, ng, w_qkv, q_g, k_g)
            if ctx_in:
                kc, vc = _kv_ctx(ctx, mods, ng, w_qkv[:, N_HEADS * HEAD_DIM:], k_g)
                k = jnp.concatenate([kc, k], axis=2)
                v = jnp.concatenate([vc, v], axis=2)
            ones_row = (jnp.arange(BF16_ROWS) == 0).astype(BF16)[:, None]
            pad = jnp.broadcast_to(ones_row, v.shape[:2] + (BF16_ROWS, v.shape[2]))
            vt_ext = jnp.concatenate([jnp.swapaxes(v, 2, 3), pad], axis=2)
            z = _attention(q, k, vt_ext)
            zc = None
            assert not ctx_out

        x = _tail(x, z, mods, ng, w_post, w1_all, w2_all, i, False, z_mirrored=(kind == 1))
        if ctx_out:
            ctx = _tail(ctx, zc, mods, ng, w_post, w1_all, w2_all, i, True)

    return x
```

```python
import functools

import jax
import jax.numpy as jnp
from jax import lax
from jax.experimental import pallas as pl
from jax.experimental.pallas import tpu as pltpu

D_MODEL = 1024
DEPTH = 4
GRID_W = 64
N_MIXERS = 3
CHUNK = 128
A_WIDTH = 2 * D_MODEL
A_GROUPS = 8
A_GROUP_W = A_WIDTH // A_GROUPS
B_GROUPS = 4
B_GROUP_W = D_MODEL // B_GROUPS
HEAD_DIM = 128
N_HEADS = D_MODEL // HEAD_DIM
N_KV_HEADS = 2
HEADS_PER_KV = N_HEADS // N_KV_HEADS
ROPE_THETA = 10000.0
D_FF = 4 * D_MODEL
NORM_EPS = 1e-6
LN_EPS = 1e-5
LOG2_E = 1.4426950408889634

F32 = jnp.float32
BF16 = jnp.bfloat16

MXU_COLS = 256
ROW_BLOCK = 64
BF16_ROWS = 16
LANES = 128
COND_ROWS = 16
MIB = 1024 * 1024


def _dot(a, b):
    return jnp.dot(a, b, preferred_element_type=F32)


def _rms(x, g):
    ms = jnp.mean(x * x, axis=-1, keepdims=True)
    return x * lax.rsqrt(ms + NORM_EPS) * g


def _gelu_tanh(x):
    c = 2.0 * 0.7978845608028654 * LOG2_E
    z = x * (-c - (c * 0.044715) * (x * x))
    return x / (1.0 + jnp.exp2(z))


def _column_slabs(w, width):
    k, n = w.shape[-2:]
    return jnp.swapaxes(w.reshape(w.shape[:-1] + (n // width, width)), -3, -2)


def _const_spec(shape):
    zeros = (0,) * len(shape)
    return pl.BlockSpec(shape, lambda b, t: zeros, pipeline_mode=pl.Buffered(1))


def _mod_spec(is_ctx, batch):
    if is_ctx:
        return pl.BlockSpec((None, 6, D_MODEL), lambda b, t: (batch, 0, 0))
    return pl.BlockSpec((None, 6, D_MODEL), lambda b, t: (b, 0, 0))


def _params(vmem_mib):
    return pltpu.CompilerParams(
        dimension_semantics=("arbitrary", "arbitrary"),
        vmem_limit_bytes=vmem_mib * MIB)


def _modulated_norm_to(h_s, x_ref, mod_ref, ng_ref, start=0, stop=None):
    sh1 = mod_ref[0:1, :]
    sc1p = 1.0 + mod_ref[1:2, :]
    g0 = ng_ref[0:1, :]
    stop = x_ref.shape[0] if stop is None else stop
    for r in range(start, stop, ROW_BLOCK):
        rows = slice(r, r + ROW_BLOCK)
        h_s[rows, :] = (_rms(x_ref[rows, :], g0) * sc1p + sh1).astype(BF16)


def _ada_kernel(cond_ref, w_ref, b_ref, o_ref):
    a = cond_ref[...]
    a = a * jax.nn.sigmoid(a)
    o_ref[...] = _dot(a.astype(BF16), w_ref[...].astype(BF16)) + b_ref[...]


def _ada_all(cond, ada_w, ada_b):
    tn = 1536
    n_out = 6 * D_MODEL
    return pl.pallas_call(
        _ada_kernel,
        grid=(DEPTH, n_out // tn),
        in_specs=[
            pl.BlockSpec((COND_ROWS, D_MODEL), lambda i, n: (0, 0)),
            pl.BlockSpec((None, D_MODEL, tn), lambda i, n: (i, 0, n)),
            pl.BlockSpec((None, 1, tn), lambda i, n: (i, 0, n)),
        ],
        out_specs=pl.BlockSpec((None, COND_ROWS, tn), lambda i, n: (i, 0, n)),
        out_shape=jax.ShapeDtypeStruct((DEPTH, COND_ROWS, n_out), F32),
        compiler_params=_params(32),
        name="ada_mod",
    )(cond, ada_w, ada_b.reshape(DEPTH, 1, n_out))


def _gmlp_head_kernel(x_ref, mod_ref, ng_ref, win_ref, lng_ref, ws_ref, bs_ref, z_ref,
                      h_s, u_s, v_s, vb_s, sum_s, sq_s):
    tm = x_ref.shape[0]
    width = 2 * MXU_COLS
    n_half = A_WIDTH // width
    lng = lng_ref[...]
    half = tm // 2
    lanes = sum_s.shape[1]

    def proj(unit):
        c, a = unit
        return _dot(h_s[a * half:(a + 1) * half, :], win_ref[c])

    def gelu_store(d, unit):
        c, a = unit
        is_v = c >= n_half
        dst, c = (v_s, c - n_half) if is_v else (u_s, c)
        for r in range(0, half, ROW_BLOCK):
            rows = slice(a * half + r, a * half + r + ROW_BLOCK)
            g = _gelu_tanh(d[r:r + ROW_BLOCK, :])
            dst[rows, c * width:(c + 1) * width] = g
            if is_v:
                parts = [g[:, i:i + lanes] for i in range(0, width, lanes)]
                p1 = sum(parts[1:], parts[0])
                p2 = sum([p * p for p in parts[1:]], parts[0] * parts[0])
                if c == 0:
                    sum_s[rows, :] = p1
                    sq_s[rows, :] = p2
                else:
                    sum_s[rows, :] += p1
                    sq_s[rows, :] += p2

    def layer_norm_rows(a, part):
        start = a * half + part * half // n_half
        for r in range(start, start + half // n_half, ROW_BLOCK // 2):
            rows = slice(r, r + ROW_BLOCK // 2)
            mu = jnp.sum(sum_s[rows, :], axis=-1, keepdims=True) * (1.0 / A_WIDTH)
            ex2 = jnp.sum(sq_s[rows, :], axis=-1, keepdims=True) * (1.0 / A_WIDTH)
            rstd = lax.rsqrt(ex2 - mu * mu + LN_EPS)
            vb_s[rows, :] = ((v_s[rows, :] - mu) * rstd * lng).astype(BF16)

    def gate_group(g):
        w = ws_ref[g]
        b = jnp.broadcast_to(bs_ref[g], (CHUNK, A_GROUP_W))
        cols = slice(g * A_GROUP_W, (g + 1) * A_GROUP_W)
        for n in range(tm // CHUNK):
            rows = slice(n * CHUNK, (n + 1) * CHUNK)
            sv = _dot(w, vb_s[rows, cols]) + b
            z_ref[rows, cols] = (u_s[rows, cols] * sv).astype(BF16)

    chunks = list(range(n_half, 2 * n_half)) + list(range(n_half))
    order = [(c, a) for c in chunks for a in range(2)]
    norms =[functools.partial(layer_norm_rows, a, part) for a in range(2) for part in range(n_half)]
    fillers = {2 * n_half + 1 + k: norms[k:k + 2] for k in range(0, len(norms), 2)}

    _modulated_norm_to(h_s, x_ref, mod_ref, ng_ref, 0, half)
    d = proj(order[0])
    _modulated_norm_to(h_s, x_ref, mod_ref, ng_ref, half, tm)
    for i in range(1, len(order) + 1):
        d_next = proj(order[i]) if i < len(order) else None
        gelu_store(d, order[i - 1])
        for filler in fillers.get(i - 1, []):
            filler()
        d = d_next
    for g in range(A_GROUPS):
        gate_group(g)


def _gmlp_head(x, mods, ng, w_in, ln_g, w_s, b_s, is_ctx):
    bsz, length, _ = x.shape
    tm = min(1024, length)
    return pl.pallas_call(
        _gmlp_head_kernel,
        grid=(bsz, length // tm),
        in_specs=[
            pl.BlockSpec((None, tm, D_MODEL), lambda b, t: (b, t, 0)),
            _mod_spec(is_ctx, bsz),
            _const_spec((4, D_MODEL)),
            _const_spec((A_WIDTH // MXU_COLS, D_MODEL, 2 * MXU_COLS)),
            _const_spec((1, A_WIDTH)),
            _const_spec((A_GROUPS, CHUNK, CHUNK)),
            _const_spec((A_GROUPS, CHUNK, 1)),
        ],
        out_specs=pl.BlockSpec((None, tm, A_WIDTH), lambda b, t: (b, t, 0)),
        out_shape=jax.ShapeDtypeStruct((bsz, length, A_WIDTH), BF16),
        scratch_shapes=[
            pltpu.VMEM((tm, D_MODEL), BF16),
            pltpu.VMEM((tm, A_WIDTH), F32),
            pltpu.VMEM((tm, A_WIDTH), F32),
            pltpu.VMEM((tm, A_WIDTH), BF16),
            pltpu.VMEM((tm, LANES), F32),
            pltpu.VMEM((tm, LANES), F32),
        ],
        compiler_params=_params(56),
        name="gmlp_head",
    )(x, mods, ng, w_in, ln_g, w_s, b_s)


def _interleave(dots, fillers):
    per = -(-len(fillers) // len(dots))
    for i, dot_fn in enumerate(dots):
        dot_fn()
        for filler in fillers[i * per:(i + 1) * per]:
            filler()


def _tail_kernel(x_ref, z_ref, mod_ref, ng_ref, wp_ref, w1_ref, w2_ref, o_ref, *scratch,
                 sub, z_mirrored):
    n_sub = x_ref.shape[0] // sub
    weights = (mod_ref, ng_ref, wp_ref, w1_ref, w2_ref)
    if n_sub == 1:
        _tail_subtile(x_ref, z_ref, *weights, o_ref, *scratch)
        return

    def body(i, carry):
        zi = i
        if z_mirrored:
            second_half = pl.program_id(1) >= pl.num_programs(1) // 2
            zi = jnp.where(second_half, n_sub - 1 - i, i)
        rows = pl.ds(pl.multiple_of(i * sub, sub), sub)
        z_rows = pl.ds(pl.multiple_of(zi * sub, sub), sub)
        _tail_subtile(x_ref.at[rows, :], z_ref.at[z_rows, :], *weights, o_ref.at[rows, :], *scratch)
        return carry

    lax.fori_loop(0, n_sub, body, 0)


def _tail_subtile(x_ref, z_ref, mod_ref, ng_ref, wp_ref, w1_ref, w2_ref, o_ref, *scratch):
    half = x_ref.shape[0] // 2
    y_s, hm_s, hid_s = scratch[0:2], scratch[2:4], scratch[4:6]
    g1 = mod_ref[2:3, :]
    sh2 = mod_ref[3:4, :]
    sc2p = 1.0 + mod_ref[4:5, :]
    g2 = mod_ref[5:6, :]
    ff_cols = 2 * MXU_COLS
    out_chunks = list(range(D_MODEL // MXU_COLS))
    ff_chunks = list(range(D_FF // ff_cols))
    blocks = list(range(0, half, ROW_BLOCK))

    def post_dot(a, n):
        y_s[a][:, n * MXU_COLS:(n + 1) * MXU_COLS] = _dot(z_ref[a * half:(a + 1) * half, :], wp_ref[n])

    def norm1(a, r):
        rows = slice(a * half + r, a * half + r + ROW_BLOCK)
        x1 = x_ref[rows, :] + g1 * _rms(y_s[a][r:r + ROW_BLOCK, :], ng_ref[1:2, :])
        o_ref[rows, :] = x1
        hm_s[a][r:r + ROW_BLOCK, :] = (_rms(x1, ng_ref[2:3, :]) * sc2p + sh2).astype(BF16)

    def up_dot(a, c):
        t = jnp.maximum(_dot(hm_s[a][...], w1_ref[c]), 0.0)
        hid_s[a][:, c * ff_cols:(c + 1) * ff_cols] = (t * t).astype(BF16)

    def down_dot(a, n):
        y_s[a][:, n * MXU_COLS:(n + 1) * MXU_COLS] = _dot(hid_s[a][...], w2_ref[n])

    def norm3(a, r):
        rows = slice(a * half + r, a * half + r + ROW_BLOCK)
        o_ref[rows, :] = o_ref[rows, :] + g2 * _rms(y_s[a][r:r + ROW_BLOCK, :], ng_ref[3:4, :])

    def each(fn, a, items):
        return [functools.partial(fn, a, item) for item in items]

    _interleave(each(post_dot, 0, out_chunks), [])
    _interleave(each(post_dot, 1, out_chunks), each(norm1, 0, blocks))
    _interleave(each(up_dot, 0, ff_chunks), each(norm1, 1, blocks))
    _interleave(each(up_dot, 1, ff_chunks), [])
    _interleave(each(down_dot, 0, out_chunks), [])
    _interleave(each(down_dot, 1, out_chunks), each(norm3, 0, blocks))
    _interleave(each(norm3, 1, blocks), [])


def _tail(x, z, mods, ng, w_post, w1_all, w2_all, layer, is_ctx, z_mirrored=False):
    bsz, length, _ = x.shape
    kz = z.shape[-1]
    sub = min(512, length)
    tm = min(1024, length)
    if z_mirrored:
        nt = length // 2 // tm
        z_spec = pl.BlockSpec(
            (None, None, tm, kz),
            lambda b, t: (b, t // nt, jnp.where(t < nt, t, 2 * nt - 1 - t), 0))
    else:
        z_spec = pl.BlockSpec((None, tm, kz), lambda b, t: (b, t, 0))
    return pl.pallas_call(
        functools.partial(_tail_kernel, sub=sub, z_mirrored=z_mirrored),
        grid=(bsz, length // tm),
        in_specs=[
            pl.BlockSpec((None, tm, D_MODEL), lambda b, t: (b, t, 0)),
            z_spec,
            _mod_spec(is_ctx, bsz),
            _const_spec((4, D_MODEL)),
            _const_spec((D_MODEL // MXU_COLS, kz, MXU_COLS)),
            pl.BlockSpec((None,) + w1_all.shape[1:], lambda b, t: (layer, 0, 0, 0),
                         pipeline_mode=pl.Buffered(1)),
            pl.BlockSpec((None,) + w2_all.shape[1:], lambda b, t: (layer, 0, 0, 0),
                         pipeline_mode=pl.Buffered(1)),
        ],
        out_specs=pl.BlockSpec((None, tm, D_MODEL), lambda b, t: (b, t, 0)),
        out_shape=jax.ShapeDtypeStruct((bsz, length, D_MODEL), F32),
        scratch_shapes=(
            [pltpu.VMEM((sub // 2, D_MODEL), F32)] * 2
            + [pltpu.VMEM((sub // 2, D_MODEL), BF16)] * 2
            + [pltpu.VMEM((sub // 2, D_FF), BF16)] * 2),
        compiler_params=_params(58),
        name="tail",
    )(x, z, mods, ng, w_post, w1_all, w2_all)


def _fourier_chan_kernel(x_ref, mod_ref, ng_ref, cs_ref, ab_ref, h_s):
    _modulated_norm_to(h_s, x_ref, mod_ref, ng_ref)
    for g in range(B_GROUPS):
        cols = slice(g * B_GROUP_W, (g + 1) * B_GROUP_W)
        t = _dot(h_s[:, cols], cs_ref[...])
        ab_ref[0, :, cols] = t[:, :B_GROUP_W].astype(BF16)
        ab_ref[1, :, cols] = t[:, B_GROUP_W:].astype(BF16)


def _fourier_seq_kernel(cs_ref, ab_ref, y_ref):
    for n in range(D_MODEL // MXU_COLS):
        cols = slice(n * MXU_COLS, (n + 1) * MXU_COLS)
        y_ref[:, cols] = _dot(cs_ref[...], ab_ref[:, cols]).astype(BF16)


def _dft_tables(length):
    m = jnp.arange(B_GROUP_W, dtype=jnp.int32)
    ang_c = (2.0 * jnp.pi / B_GROUP_W) * ((m[:, None] * m[None, :]) % B_GROUP_W).astype(F32)
    s_c = B_GROUP_W ** -0.5
    cs_chan = jnp.concatenate([jnp.cos(ang_c), jnp.sin(ang_c)], axis=1) * s_c
    k = jnp.arange(length, dtype=jnp.int32)
    ang_l = (2.0 * jnp.pi / length) * ((k[:, None] * k[None, :]) % length).astype(F32)
    s_l = length ** -0.5
    cs_seq = jnp.concatenate([jnp.cos(ang_l), -jnp.sin(ang_l)], axis=1) * s_l
    return cs_chan.astype(BF16), cs_seq.astype(BF16)


def _fourier_fold_kernel(xp_ref, xm_ref, x0_ref, mod_ref, ng_ref, cc_ref, sc_ref, rev_ref,
                         ab_ref, a0_ref, hp_s, hm_s):
    tm = xm_ref.shape[0]
    sh1 = mod_ref[0:1, :]
    sc1p = 1.0 + mod_ref[1:2, :]
    g0 = ng_ref[0:1, :]
    for r in range(0, tm, ROW_BLOCK):
        rows = slice(r, r + ROW_BLOCK)
        hp_s[rows, :] = _rms(xp_ref[0, r + 1:r + 1 + ROW_BLOCK, :], g0) * sc1p + sh1
        hm_s[rows, :] = (_rms(xm_ref[rows, :], g0) * sc1p + sh1).astype(BF16)
    h0 = (_rms(x0_ref[...], g0) * sc1p + sh1).astype(BF16)
    for g in range(B_GROUPS):
        cols = slice(g * B_GROUP_W, (g + 1) * B_GROUP_W)
        hm_rev = _dot(rev_ref[...], hm_s[:, cols])
        hp = hp_s[:, cols]
        ab_ref[0, :, cols] = _dot((hp + hm_rev).astype(BF16), cc_ref[...]).astype(BF16)
        ab_ref[1, :, cols] = _dot((hp - hm_rev).astype(BF16), sc_ref[...]).astype(BF16)
        a0_ref[:, cols] = _dot(h0[:, cols], cc_ref[...])


def _fourier_seq_fold_kernel(ct_ref, st_ref, ab_ref, a0_ref, perm_ref, y_ref, w_s, *, scale):
    tk = y_ref.shape[1]
    half = ct_ref.shape[1]
    for n in range(D_MODEL // MXU_COLS):
        cols = slice(n * MXU_COLS, (n + 1) * MXU_COLS)
        yc = _dot(ct_ref[...], ab_ref[0:half, cols]) + a0_ref[0:1, cols] * scale
        ys = _dot(st_ref[...], ab_ref[half:2 * half, cols])
        y_ref[0, :, cols] = (yc[:tk] - ys[:tk]).astype(BF16)
        w_s[:, cols] = (yc + ys).astype(BF16)
    for n in range(D_MODEL // MXU_COLS):
        cols = slice(n * MXU_COLS, (n + 1) * MXU_COLS)
        y_ref[1, :, cols] = _dot(perm_ref[...], w_s[:, cols]).astype(BF16)


def _seq_fold_tables(length, tk):
    half = length // 2
    nt = half // tk
    rows = tk + BF16_ROWS
    fine = 64
    n_coarse = -(-(nt * tk + BF16_ROWS) // fine)
    t = jnp.arange(1, half + 1, dtype=jnp.int32)

    def cos_sin(k):
        ang = (2.0 * jnp.pi / length) * ((k[:, None] * t[None, :]) % length).astype(F32)
        return jnp.cos(ang), jnp.sin(ang)

    ch, sh = cos_sin(jnp.arange(n_coarse, dtype=jnp.int32) * fine)
    cl, sl = cos_sin(jnp.arange(fine, dtype=jnp.int32))
    weight = jnp.where(t == half, 0.5, 1.0) * length ** -0.5
    c = (ch[:, None, :] * cl[None] - sh[:, None, :] * sl[None]).reshape(n_coarse * fine, half)
    s = (sh[:, None, :] * cl[None] + ch[:, None, :] * sl[None]).reshape(n_coarse * fine, half)
    ct = jnp.stack([c[i * tk:i * tk + rows] for i in range(nt)]) * weight
    st = jnp.stack([s[i * tk:i * tk + rows] for i in range(nt)]) * weight
    return ct.astype(BF16), st.astype(BF16)


def _fourier_head_folded(x, mods, ng):
    bsz, length, _ = x.shape
    tm = 512
    half = length // 2
    nt = half // tm
    rows = tm + BF16_ROWS
    m = jnp.arange(B_GROUP_W, dtype=jnp.int32)
    ang_c = (2.0 * jnp.pi / B_GROUP_W) * ((m[:, None] * m[None, :]) % B_GROUP_W).astype(F32)
    cc = (jnp.cos(ang_c) * B_GROUP_W ** -0.5).astype(BF16)
    sc = (jnp.sin(ang_c) * B_GROUP_W ** -0.5).astype(BF16)
    rev = (jnp.arange(tm)[:, None] + jnp.arange(tm)[None, :] == tm - 1).astype(BF16)
    ab, a0 = pl.pallas_call(
        _fourier_fold_kernel,
        grid=(bsz, nt),
        in_specs=[
            pl.BlockSpec((pl.Element(1), pl.Element(tm + 8), pl.Element(D_MODEL)),
                         lambda b, t: (b, t * tm, 0)),
            pl.BlockSpec((None, tm, D_MODEL), lambda b, t: (b, 2 * nt - 1 - t, 0)),
            pl.BlockSpec((None, 8, D_MODEL), lambda b, t: (b, 0, 0)),
            _mod_spec(False, bsz),
            _const_spec((4, D_MODEL)),
            _const_spec((B_GROUP_W, B_GROUP_W)),
            _const_spec((B_GROUP_W, B_GROUP_W)),
            _const_spec((tm, tm)),
        ],
        out_specs=[
            pl.BlockSpec((None, 2, tm, D_MODEL), lambda b, t: (b, 0, t, 0)),
            pl.BlockSpec((None, 8, D_MODEL), lambda b, t: (b, 0, 0)),
        ],
        out_shape=[
            jax.ShapeDtypeStruct((bsz, 2, half, D_MODEL), BF16),
            jax.ShapeDtypeStruct((bsz, 8, D_MODEL), F32),
        ],
        scratch_shapes=[pltpu.VMEM((tm, D_MODEL), F32), pltpu.VMEM((tm, D_MODEL), BF16)],
        compiler_params=_params(32),
        name="fourier_fold",
    )(x, x, x, mods, ng, cc, sc, rev)
    ct, st = _seq_fold_tables(length, tm)
    perm = (jnp.arange(tm)[:, None] + jnp.arange(rows)[None, :] == tm).astype(BF16)
    return pl.pallas_call(
        functools.partial(_fourier_seq_fold_kernel, scale=length ** -0.5),
        grid=(bsz, nt),
        in_specs=[
            pl.BlockSpec((None, rows, half), lambda b, t: (t, 0, 0)),
            pl.BlockSpec((None, rows, half), lambda b, t: (t, 0, 0)),
            pl.BlockSpec((None, 2 * half, D_MODEL), lambda b, t: (b, 0, 0)),
            pl.BlockSpec((None, 8, D_MODEL), lambda b, t: (b, 0, 0)),
            _const_spec((tm, rows)),
        ],
        out_specs=pl.BlockSpec((None, 2, tm, D_MODEL), lambda b, t: (b, 0, t, 0)),
        out_shape=jax.ShapeDtypeStruct((bsz, 2, half, D_MODEL), BF16),
        scratch_shapes=[pltpu.VMEM((rows, D_MODEL), BF16)],
        compiler_params=_params(40),
        name="fourier_seq_fold",
    )(ct, st, ab.reshape(bsz, 2 * half, D_MODEL), a0, perm)


def _fourier_head(x, mods, ng, is_ctx):
    bsz, length, _ = x.shape
    tm = min(512, length)
    cs_chan, cs_seq = _dft_tables(length)
    ab = pl.pallas_call(
        _fourier_chan_kernel,
        grid=(bsz, length // tm),
        in_specs=[
            pl.BlockSpec((None, tm, D_MODEL), lambda b, t: (b, t, 0)),
            _mod_spec(is_ctx, bsz),
            _const_spec((4, D_MODEL)),
            _const_spec((B_GROUP_W, 2 * B_GROUP_W)),
        ],
        out_specs=pl.BlockSpec((None, 2, tm, D_MODEL), lambda b, t: (b, 0, t, 0)),
        out_shape=jax.ShapeDtypeStruct((bsz, 2, length, D_MODEL), BF16),
        scratch_shapes=[pltpu.VMEM((tm, D_MODEL), BF16)],
        compiler_params=_params(32),
        name="fourier_chan",
    )(x, mods, ng, cs_chan)
    ab = ab.reshape(bsz, 2 * length, D_MODEL)
    return pl.pallas_call(
        _fourier_seq_kernel,
        grid=(bsz, length // tm),
        in_specs=[
            pl.BlockSpec((tm, 2 * length), lambda b, t: (t, 0)),
            pl.BlockSpec((None, 2 * length, D_MODEL), lambda b, t: (b, 0, 0),
                         pipeline_mode=pl.Buffered(1)),
        ],
        out_specs=pl.BlockSpec((None, tm, D_MODEL), lambda b, t: (b, t, 0)),
        out_shape=jax.ShapeDtypeStruct((bsz, length, D_MODEL), BF16),
        compiler_params=_params(48),
        name="fourier_seq",
    )(cs_seq, ab)


def _head_rms(r, g):
    return r * lax.rsqrt(jnp.mean(r * r, axis=-1, keepdims=True) + NORM_EPS) * g


def _store_vt_ext(vt_ref, t):
    tokens = t.shape[0]
    first_row = lax.broadcasted_iota(jnp.int32, (BF16_ROWS, tokens), 0) == 0
    for e in range(N_KV_HEADS):
        vt_ref[e, 0:HEAD_DIM, :] = t[:, e * HEAD_DIM:(e + 1) * HEAD_DIM].T.astype(BF16)
        vt_ref[e, HEAD_DIM:HEAD_DIM + BF16_ROWS, :] = first_row.astype(BF16)


def _qkv_kernel(x_ref, mod_ref, ng_ref, w_ref, qg_ref, kg_ref, mean_ref, cos_ref, sin_ref,
                q_ref, k_ref, vt_ref, h_s):
    _modulated_norm_to(h_s, x_ref, mod_ref, ng_ref)
    cosf = cos_ref[...]
    sinf = sin_ref[...]

    qg = qg_ref[...] * (HEAD_DIM ** -0.5 * LOG2_E)
    gains = [jnp.concatenate([qg, qg], axis=1)] * (N_HEADS // 2)
    gains.append(jnp.concatenate([kg_ref[...], kg_ref[...]], axis=1))
    outs = [(q_ref, 2 * p) for p in range(N_HEADS // 2)] + [(k_ref, 0)]
    n_pairs = len(outs)

    def project(p):
        return _dot(h_s[...], w_ref[:, p * MXU_COLS:(p + 1) * MXU_COLS])

    def normalise(t, p):
        sq = t * t
        hi = sq.astype(BF16)
        lo = (sq - hi.astype(F32)).astype(BF16)
        ms = _dot(hi, mean_ref[...]) + _dot(lo, mean_ref[...])
        return t * lax.rsqrt(ms + NORM_EPS) * gains[p]

    def rope_store(rn, p):
        ref, first = outs[p]
        for e in range(2):
            r = rn[:, e * HEAD_DIM:(e + 1) * HEAD_DIM]
            ref[first + e] = (r * cosf + pltpu.roll(r, HEAD_DIM // 2, 1) * sinf).astype(BF16)

    t_next = project(0)
    rn_prev = None
    for p in range(n_pairs + 1):
        t = t_next
        if p < n_pairs:
            t_next = project(p + 1)
        rn = normalise(t, p) if p < n_pairs else None
        if rn_prev is not None:
            rope_store(rn_prev, p - 1)
        rn_prev = rn
    _store_vt_ext(vt_ref, t)


def _kv_ctx_kernel(x_ref, mod_ref, ng_ref, w_ref, kg_ref, k_ref, vt_ref, h_s):
    _modulated_norm_to(h_s, x_ref, mod_ref, ng_ref)
    t = _dot(h_s[...], w_ref[:, 0:MXU_COLS])
    for e in range(N_KV_HEADS):
        k_ref[e] = _head_rms(t[:, e * HEAD_DIM:(e + 1) * HEAD_DIM], kg_ref[...]).astype(BF16)
    _store_vt_ext(vt_ref, _dot(h_s[...], w_ref[:, MXU_COLS:2 * MXU_COLS]))


def _rope_tables(length):
    t = jnp.arange(length)
    row = (t // GRID_W).astype(F32)
    col = (t % GRID_W).astype(F32)
    n_freq = HEAD_DIM // 4
    inv = ROPE_THETA ** (-jnp.arange(n_freq, dtype=F32) / n_freq)
    ang = jnp.concatenate([row[:, None] * inv, col[:, None] * inv], axis=-1)
    cos, sin = jnp.cos(ang), jnp.sin(ang)
    return jnp.concatenate([cos, cos], axis=-1), jnp.concatenate([-sin, sin], axis=-1)


def _qkv_latent(x, mods, ng, w_qkv, q_g, k_g):
    bsz, length, _ = x.shape
    tm = 512
    cosf, sinf = _rope_tables(length)
    n_qkv = w_qkv.shape[-1]
    lane_head = jnp.arange(MXU_COLS) // HEAD_DIM
    head_mean = ((lane_head[:, None] == lane_head[None, :]) * (1.0 / HEAD_DIM)).astype(BF16)
    kv_shape = jax.ShapeDtypeStruct((bsz, N_KV_HEADS, length, HEAD_DIM), BF16)
    kv_spec = pl.BlockSpec((None, N_KV_HEADS, tm, HEAD_DIM), lambda b, t: (b, 0, t, 0))
    vt_shape = jax.ShapeDtypeStruct((bsz, N_KV_HEADS, HEAD_DIM + BF16_ROWS, length), BF16)
    vt_spec = pl.BlockSpec((None, N_KV_HEADS, HEAD_DIM + BF16_ROWS, tm), lambda b, t: (b, 0, 0, t))
    return pl.pallas_call(
        _qkv_kernel,
        grid=(bsz, length // tm),
        in_specs=[
            pl.BlockSpec((None, tm, D_MODEL), lambda b, t: (b, t, 0)),
            _mod_spec(False, bsz),
            _const_spec((4, D_MODEL)),
            _const_spec((D_MODEL, n_qkv)),
            _const_spec((1, HEAD_DIM)),
            _const_spec((1, HEAD_DIM)),
            _const_spec((MXU_COLS, MXU_COLS)),
            pl.BlockSpec((tm, HEAD_DIM), lambda b, t: (t, 0)),
            pl.BlockSpec((tm, HEAD_DIM), lambda b, t: (t, 0)),
        ],
        out_specs=[
            pl.BlockSpec((None, N_HEADS, tm, HEAD_DIM), lambda b, t: (b, 0, t, 0)),
            kv_spec, vt_spec,
        ],
        out_shape=[
            jax.ShapeDtypeStruct((bsz, N_HEADS, length, HEAD_DIM), BF16),
            kv_shape, vt_shape,
        ],
        scratch_shapes=[pltpu.VMEM((tm, D_MODEL), BF16)],
        compiler_params=_params(32),
        name="qkv_latent",
    )(x, mods, ng, w_qkv, q_g, k_g, head_mean, cosf, sinf)


def _kv_ctx(ctx, mods, ng, w_kv, k_g):
    bsz, length, _ = ctx.shape
    tm = length
    kv_shape = jax.ShapeDtypeStruct((bsz, N_KV_HEADS, length, HEAD_DIM), BF16)
    kv_spec = pl.BlockSpec((None, N_KV_HEADS, tm, HEAD_DIM), lambda b, t: (b, 0, t, 0))
    vt_shape = jax.ShapeDtypeStruct((bsz, N_KV_HEADS, HEAD_DIM + BF16_ROWS, length), BF16)
    vt_spec = pl.BlockSpec((None, N_KV_HEADS, HEAD_DIM + BF16_ROWS, tm), lambda b, t: (b, 0, 0, t))
    return pl.pallas_call(
        _kv_ctx_kernel,
        grid=(bsz, length // tm),
        in_specs=[
            pl.BlockSpec((None, tm, D_MODEL), lambda b, t: (b, t, 0)),
            _mod_spec(True, bsz),
            _const_spec((4, D_MODEL)),
            _const_spec((D_MODEL, 2 * MXU_COLS)),
            _const_spec((1, HEAD_DIM)),
        ],
        out_specs=[kv_spec, vt_spec],
        out_shape=[kv_shape, vt_shape],
        scratch_shapes=[pltpu.VMEM((tm, D_MODEL), BF16)],
        compiler_params=_params(32),
        name="kv_ctx",
    )(ctx, mods, ng, w_kv, k_g)


def _region(index, fn):
    pl.when(pl.program_id(0) >= -index)(fn)


def _attn_kernel(q_ref, k_ref, kc_ref, vt_ref, vtc_ref, o_ref, *scratch):
    tq = q_ref.shape[1]
    n_latent = k_ref.shape[1]
    n_keys = n_latent + kc_ref.shape[1]
    nq = HEADS_PER_KV * tq
    q4_s, s_s, p_s, m_s = (scratch[i * N_KV_HEADS:(i + 1) * N_KV_HEADS] for i in range(4))
    key_chunks = [(k_ref, vt_ref, slice(r, r + MXU_COLS), slice(r, r + MXU_COLS))
                  for r in range(0, n_latent, MXU_COLS)]
    key_chunks += [(kc_ref, vtc_ref, slice(r, r + MXU_COLS), slice(n_latent + r, n_latent + r + MXU_COLS))
                   for r in range(0, n_keys - n_latent, MXU_COLS)]

    @pl.when(pl.program_id(0) == 0)
    def _():
        s_s[1][...] = jnp.zeros(s_s[1].shape, F32)
        m_s[1][...] = jnp.zeros(m_s[1].shape, F32)
        p_s[0][...] = jnp.ones(p_s[0].shape, BF16)

    def stage(j_scores, j_probs, j_pv):
        for h in range(HEADS_PER_KV):
            q4_s[j_scores][h * tq:(h + 1) * tq, :] = q_ref[HEADS_PER_KV * j_scores + h]
        m_probs = m_s[j_probs][...]
        m = None
        acc = None

        def store_head(acc, h):
            cols = slice(h * tq, (h + 1) * tq)
            o = acc[:HEAD_DIM, cols] / acc[HEAD_DIM:HEAD_DIM + 1, cols]
            head = HEADS_PER_KV * j_pv + h
            o_ref[:, head * HEAD_DIM:(head + 1) * HEAD_DIM] = o.T.astype(BF16)

        pv_chunks = list(key_chunks)
        heads_left = list(range(HEADS_PER_KV))
        tiles = [slice(r, r + BF16_ROWS) for r in range(0, n_keys, BF16_ROWS)]
        pv_rows_per_score_row = vt_ref.shape[1] / MXU_COLS
        work_total = len(key_chunks) * (1.0 + pv_rows_per_score_row)
        work_done = 0.0
        tiles_done = 0
        for k_src, _, local, rows in key_chunks:
            s = lax.dot_general(k_src[j_scores, local, :], q4_s[j_scores][...],
                                (((1,), (1,)), ((), ())), preferred_element_type=F32)
            s_s[j_scores][rows, :] = s
            cm = jnp.max(s, axis=0, keepdims=True)
            m = cm if m is None else jnp.maximum(m, cm)
            work_done += 1.0
            if pv_chunks:
                for _, vt_src, pv_local, pv_rows in (
                        pv_chunks.pop(0) for _ in range(min(2, len(pv_chunks)))):
                    d = _dot(vt_src[j_pv, :, pv_local], p_s[j_pv][pv_rows, :])
                    acc = d if acc is None else acc + d
                    work_done += pv_rows_per_score_row
            elif heads_left:
                store_head(acc, heads_left.pop(0))
            tiles_until = round(len(tiles) * work_done / work_total)
            for tile in tiles[tiles_done:tiles_until]:
                p_s[j_probs][tile, :] = jnp.exp2(s_s[j_probs][tile, :] - m_probs).astype(BF16)
            tiles_done = tiles_until
        assert not pv_chunks and tiles_done == len(tiles)
        for h in heads_left:
            store_head(acc, h)
        m_s[j_scores][...] = jnp.broadcast_to(m, (BF16_ROWS, nq))

    _region(0, lambda: stage(0, 1, 0))
    _region(1, lambda: stage(1, 0, 1))


def _attention(q, k, kc, vt, vtc):
    bsz, _, length, _ = q.shape
    n_ctx = kc.shape[2]
    n_keys = length + n_ctx
    vt_rows = vt.shape[2]
    tq = 128
    nq = HEADS_PER_KV * tq
    per_batch = length // tq
    n_tiles = bsz * per_batch

    def cur(i):
        return jnp.minimum(i, n_tiles - 1)

    def prev(i):
        return jnp.maximum(i - 1, 0)

    return pl.pallas_call(
        _attn_kernel,
        grid=(n_tiles + 1,),
        in_specs=[
            pl.BlockSpec((None, N_HEADS, tq, HEAD_DIM),
                         lambda i: (cur(i) // per_batch, 0, cur(i) % per_batch, 0)),
            pl.BlockSpec((None, N_KV_HEADS, length, HEAD_DIM), lambda i: (cur(i) // per_batch, 0, 0, 0)),
            pl.BlockSpec((None, N_KV_HEADS, n_ctx, HEAD_DIM), lambda i: (cur(i) // per_batch, 0, 0, 0)),
            pl.BlockSpec((None, N_KV_HEADS, vt_rows, length), lambda i: (prev(i) // per_batch, 0, 0, 0)),
            pl.BlockSpec((None, N_KV_HEADS, vt_rows, n_ctx), lambda i: (prev(i) // per_batch, 0, 0, 0)),
        ],
        out_specs=pl.BlockSpec((None, tq, D_MODEL),
                               lambda i: (prev(i) // per_batch, prev(i) % per_batch, 0)),
        out_shape=jax.ShapeDtypeStruct((bsz, length, D_MODEL), BF16),
        scratch_shapes=(
            [pltpu.VMEM((nq, HEAD_DIM), BF16)] * N_KV_HEADS
            + [pltpu.VMEM((n_keys, nq), F32)] * N_KV_HEADS
            + [pltpu.VMEM((n_keys, nq), BF16)] * N_KV_HEADS
            + [pltpu.VMEM((BF16_ROWS, nq), F32)] * N_KV_HEADS),
        compiler_params=pltpu.CompilerParams(
            dimension_semantics=("arbitrary",), vmem_limit_bytes=54 * MIB),
        name="attention",
    )(q, k, kc, vt, vtc)


def kernel(x, c, ctx, c_ctx, ada_w, ada_b, norm_g, mlp_w1, mlp_w2, a_w_in, a_ln_g, a_w_s, a_b_s, a_w_out,
           b_w_out, c_w_qkv, c_q_g, c_k_g, c_w_o):
    bsz = x.shape[0]
    attn_layers = [i for i in range(DEPTH) if i % N_MIXERS == 2]
    last_ctx_read = attn_layers[-1] if attn_layers else -1

    cond = jnp.zeros((COND_ROWS, D_MODEL), F32).at[:bsz].set(c).at[bsz].set(c_ctx)
    mods_all = _ada_all(cond, ada_w, ada_b).reshape(DEPTH, COND_ROWS, 6, D_MODEL)
    w1_all = _column_slabs(mlp_w1.astype(BF16), 2 * MXU_COLS)
    w2_all = _column_slabs(mlp_w2.astype(BF16), MXU_COLS)


    for i in range(DEPTH):
        kind, j = i % N_MIXERS, i // N_MIXERS
        ctx_in = i <= last_ctx_read
        ctx_out = i < last_ctx_read
        mods = mods_all[i]
        ng = norm_g[i]

        if kind == 0:
            w_in = _column_slabs(a_w_in[j].astype(BF16), 2 * MXU_COLS)
            ln_g = a_ln_g[j].reshape(1, A_WIDTH)
            w_s = a_w_s[j].astype(BF16)
            b_s = a_b_s[j].reshape(A_GROUPS, CHUNK, 1)
            w_post = _column_slabs(a_w_out[j].astype(BF16), MXU_COLS)
            z = _gmlp_head(x, mods, ng, w_in, ln_g, w_s, b_s, False)
            zc = _gmlp_head(ctx, mods, ng, w_in, ln_g, w_s, b_s, True) if ctx_out else None
        elif kind == 1:
            w_post = _column_slabs(b_w_out[j].astype(BF16), MXU_COLS)
            z = _fourier_head_folded(x, mods, ng)
            zc = _fourier_head(ctx, mods, ng, True) if ctx_out else None
        else:
            w_qkv = c_w_qkv[j].astype(BF16)
            w_post = _column_slabs(c_w_o[j].astype(BF16), MXU_COLS)
            q_g = c_q_g[j].reshape(1, HEAD_DIM)
            k_g = c_k_g[j].reshape(1, HEAD_DIM)
            assert ctx_in and not ctx_out
            q, k, vt = _qkv_latent(x, mods, ng, w_qkv, q_g, k_g)
            kc, vtc = _kv_ctx(ctx, mods, ng, w_qkv[:, N_HEADS * HEAD_DIM:], k_g)
            z = _attention(q, k, kc, vt, vtc)
            zc = None

        x = _tail(x, z, mods, ng, w_post, w1_all, w2_all, i, False, z_mirrored=(kind == 1))
        if ctx_out:
            ctx = _tail(ctx, zc, mods, ng, w_post, w1_all, w2_all, i, True)

    return x
```

```python
import functools

import jax
import jax.numpy as jnp
from jax import lax
from jax.experimental import pallas as pl
from jax.experimental.pallas import tpu as pltpu

D_MODEL = 1024
DEPTH = 4
GRID_W = 64
N_MIXERS = 3
CHUNK = 128
A_WIDTH = 2 * D_MODEL
A_GROUPS = 8
A_GROUP_W = A_WIDTH // A_GROUPS
B_GROUPS = 4
B_GROUP_W = D_MODEL // B_GROUPS
HEAD_DIM = 128
N_HEADS = D_MODEL // HEAD_DIM
N_KV_HEADS = 2
HEADS_PER_KV = N_HEADS // N_KV_HEADS
ROPE_THETA = 10000.0
D_FF = 4 * D_MODEL
NORM_EPS = 1e-6
LN_EPS = 1e-5
LOG2_E = 1.4426950408889634

F32 = jnp.float32
BF16 = jnp.bfloat16

MXU_COLS = 256
ROW_BLOCK = 64
BF16_ROWS = 16
LANES = 128
COND_ROWS = 16
MIB = 1024 * 1024


def _dot(a, b):
    return jnp.dot(a, b, preferred_element_type=F32)


def _rms(x, g):
    ms = jnp.mean(x * x, axis=-1, keepdims=True)
    return x * lax.rsqrt(ms + NORM_EPS) * g


def _gelu_tanh(x):
    c = 2.0 * 0.7978845608028654 * LOG2_E
    z = x * (-c - (c * 0.044715) * (x * x))
    return x / (1.0 + jnp.exp2(z))


def _slab_specs(w, width, layer=None):
    k, n = w.shape[-2:]
    if layer is None:
        return [pl.BlockSpec((k, width), lambda b, t, c=c: (0, c), pipeline_mode=pl.Buffered(1))
                for c in range(n // width)]
    return [pl.BlockSpec((None, k, width), lambda b, t, c=c: (layer, 0, c), pipeline_mode=pl.Buffered(1))
            for c in range(n // width)]


def _const_spec(shape):
    zeros = (0,) * len(shape)
    return pl.BlockSpec(shape, lambda b, t: zeros, pipeline_mode=pl.Buffered(1))


def _mod_spec(is_ctx, batch):
    if is_ctx:
        return pl.BlockSpec((None, 6, D_MODEL), lambda b, t: (batch, 0, 0))
    return pl.BlockSpec((None, 6, D_MODEL), lambda b, t: (b, 0, 0))


def _params(vmem_mib):
    return pltpu.CompilerParams(
        dimension_semantics=("arbitrary", "arbitrary"),
        vmem_limit_bytes=vmem_mib * MIB)


def _modulated_norm_to(h_s, x_ref, mod_ref, ng_ref, start=0, stop=None):
    sh1 = mod_ref[0:1, :]
    sc1p = 1.0 + mod_ref[1:2, :]
    g0 = ng_ref[0:1, :]
    stop = x_ref.shape[0] if stop is None else stop
    for r in range(start, stop, ROW_BLOCK):
        rows = slice(r, r + ROW_BLOCK)
        h_s[rows, :] = (_rms(x_ref[rows, :], g0) * sc1p + sh1).astype(BF16)


def _ada_kernel(cond_ref, w_ref, b_ref, o_ref):
    a = cond_ref[...]
    a = a * jax.nn.sigmoid(a)
    o_ref[...] = _dot(a.astype(BF16), w_ref[...].astype(BF16)) + b_ref[...]


def _ada_all(cond, ada_w, ada_b):
    tn = 1536
    n_out = 6 * D_MODEL
    return pl.pallas_call(
        _ada_kernel,
        grid=(DEPTH, n_out // tn),
        in_specs=[
            pl.BlockSpec((COND_ROWS, D_MODEL), lambda i, n: (0, 0)),
            pl.BlockSpec((None, D_MODEL, tn), lambda i, n: (i, 0, n)),
            pl.BlockSpec((None, 1, tn), lambda i, n: (i, 0, n)),
        ],
        out_specs=pl.BlockSpec((None, COND_ROWS, tn), lambda i, n: (i, 0, n)),
        out_shape=jax.ShapeDtypeStruct((DEPTH, COND_ROWS, n_out), F32),
        compiler_params=_params(32),
        name="ada_mod",
    )(cond, ada_w, ada_b.reshape(DEPTH, 1, n_out))


def _gmlp_head_kernel(x_ref, mod_ref, ng_ref, lng_ref, ws_ref, bs_ref, *refs):
    width = 2 * MXU_COLS
    n_half = A_WIDTH // width
    win = refs[:2 * n_half]
    z_ref, h_s, u_s, v_s, vb_s, sum_s, sq_s = refs[2 * n_half:]
    tm = x_ref.shape[0]
    lng = lng_ref[...]
    half = tm // 2
    lanes = sum_s.shape[1]

    def proj(unit):
        c, a = unit
        return _dot(h_s[a * half:(a + 1) * half, :], win[c][...])

    def gelu_store(d, unit):
        c, a = unit
        is_v = c >= n_half
        dst, c = (v_s, c - n_half) if is_v else (u_s, c)
        for r in range(0, half, ROW_BLOCK):
            rows = slice(a * half + r, a * half + r + ROW_BLOCK)
            g = _gelu_tanh(d[r:r + ROW_BLOCK, :])
            dst[rows, c * width:(c + 1) * width] = g
            if is_v:
                parts = [g[:, i:i + lanes] for i in range(0, width, lanes)]
                p1 = sum(parts[1:], parts[0])
                p2 = sum([p * p for p in parts[1:]], parts[0] * parts[0])
                if c == 0:
                    sum_s[rows, :] = p1
                    sq_s[rows, :] = p2
                else:
                    sum_s[rows, :] += p1
                    sq_s[rows, :] += p2

    def layer_norm_rows(a, part):
        start = a * half + part * half // n_half
        for r in range(start, start + half // n_half, ROW_BLOCK // 2):
            rows = slice(r, r + ROW_BLOCK // 2)
            mu = jnp.sum(sum_s[rows, :], axis=-1, keepdims=True) * (1.0 / A_WIDTH)
            ex2 = jnp.sum(sq_s[rows, :], axis=-1, keepdims=True) * (1.0 / A_WIDTH)
            rstd = lax.rsqrt(ex2 - mu * mu + LN_EPS)
            vb_s[rows, :] = ((v_s[rows, :] - mu) * rstd * lng).astype(BF16)

    def gate_group(g):
        w = ws_ref[g]
        b = jnp.broadcast_to(bs_ref[g], (CHUNK, A_GROUP_W))
        cols = slice(g * A_GROUP_W, (g + 1) * A_GROUP_W)
        for n in range(tm // CHUNK):
            rows = slice(n * CHUNK, (n + 1) * CHUNK)
            sv = _dot(w, vb_s[rows, cols]) + b
            z_ref[rows, cols] = (u_s[rows, cols] * sv).astype(BF16)

    chunks = list(range(n_half, 2 * n_half)) + list(range(n_half))
    order = [(c, a) for c in chunks for a in range(2)]
    norms =[functools.partial(layer_norm_rows, a, part) for a in range(2) for part in range(n_half)]
    fillers = {2 * n_half + 1 + k: norms[k:k + 2] for k in range(0, len(norms), 2)}

    _modulated_norm_to(h_s, x_ref, mod_ref, ng_ref, 0, half)
    d = proj(order[0])
    _modulated_norm_to(h_s, x_ref, mod_ref, ng_ref, half, tm)
    for i in range(1, len(order) + 1):
        d_next = proj(order[i]) if i < len(order) else None
        gelu_store(d, order[i - 1])
        for filler in fillers.get(i - 1, []):
            filler()
        d = d_next
    for g in range(A_GROUPS):
        gate_group(g)


def _gmlp_head(x, mods, ng, w_in, ln_g, w_s, b_s, is_ctx):
    bsz, length, _ = x.shape
    tm = min(1024, length)
    w_in_specs = _slab_specs(w_in, 2 * MXU_COLS)
    return pl.pallas_call(
        _gmlp_head_kernel,
        grid=(bsz, length // tm),
        in_specs=[
            pl.BlockSpec((None, tm, D_MODEL), lambda b, t: (b, t, 0)),
            _mod_spec(is_ctx, bsz),
            _const_spec((4, D_MODEL)),
            _const_spec((1, A_WIDTH)),
            _const_spec((A_GROUPS, CHUNK, CHUNK)),
            _const_spec((A_GROUPS, CHUNK, 1)),
        ] + w_in_specs,
        out_specs=pl.BlockSpec((None, tm, A_WIDTH), lambda b, t: (b, t, 0)),
        out_shape=jax.ShapeDtypeStruct((bsz, length, A_WIDTH), BF16),
        scratch_shapes=[
            pltpu.VMEM((tm, D_MODEL), BF16),
            pltpu.VMEM((tm, A_WIDTH), F32),
            pltpu.VMEM((tm, A_WIDTH), F32),
            pltpu.VMEM((tm, A_WIDTH), BF16),
            pltpu.VMEM((tm, LANES), F32),
            pltpu.VMEM((tm, LANES), F32),
        ],
        compiler_params=_params(56),
        name="gmlp_head",
    )(x, mods, ng, ln_g, w_s, b_s, *([w_in] * len(w_in_specs)))


def _interleave(dots, fillers):
    per = -(-len(fillers) // len(dots))
    for i, dot_fn in enumerate(dots):
        dot_fn()
        for filler in fillers[i * per:(i + 1) * per]:
            filler()


def _tail_kernel(x_ref, z_ref, mod_ref, ng_ref, *refs, sub, z_mirrored):
    n_out = D_MODEL // MXU_COLS
    n_ff = D_FF // (2 * MXU_COLS)
    wp, w1, w2 = refs[:n_out], refs[n_out:n_out + n_ff], refs[n_out + n_ff:2 * n_out + n_ff]
    o_ref = refs[2 * n_out + n_ff]
    scratch = refs[2 * n_out + n_ff + 1:]
    n_sub = x_ref.shape[0] // sub
    weights = (mod_ref, ng_ref, wp, w1, w2)
    if n_sub == 1:
        _tail_subtile(x_ref, z_ref, *weights, o_ref, *scratch)
        return

    def body(i, carry):
        zi = i
        if z_mirrored:
            second_half = pl.program_id(1) >= pl.num_programs(1) // 2
            zi = jnp.where(second_half, n_sub - 1 - i, i)
        rows = pl.ds(pl.multiple_of(i * sub, sub), sub)
        z_rows = pl.ds(pl.multiple_of(zi * sub, sub), sub)
        _tail_subtile(x_ref.at[rows, :], z_ref.at[z_rows, :], *weights, o_ref.at[rows, :], *scratch)
        return carry

    lax.fori_loop(0, n_sub, body, 0)


def _tail_subtile(x_ref, z_ref, mod_ref, ng_ref, wp, w1, w2, o_ref, *scratch):
    half = x_ref.shape[0] // 2
    y_s, hm_s, hid_s = scratch[0:2], scratch[2:4], scratch[4:6]
    g1 = mod_ref[2:3, :]
    sh2 = mod_ref[3:4, :]
    sc2p = 1.0 + mod_ref[4:5, :]
    g2 = mod_ref[5:6, :]
    ff_cols = 2 * MXU_COLS
    out_chunks = list(range(D_MODEL // MXU_COLS))
    ff_chunks = list(range(D_FF // ff_cols))
    blocks = list(range(0, half, ROW_BLOCK))

    def post_dot(a, n):
        y_s[a][:, n * MXU_COLS:(n + 1) * MXU_COLS] = _dot(z_ref[a * half:(a + 1) * half, :], wp[n][...])

    def norm1(a, r):
        rows = slice(a * half + r, a * half + r + ROW_BLOCK)
        x1 = x_ref[rows, :] + g1 * _rms(y_s[a][r:r + ROW_BLOCK, :], ng_ref[1:2, :])
        o_ref[rows, :] = x1
        hm_s[a][r:r + ROW_BLOCK, :] = (_rms(x1, ng_ref[2:3, :]) * sc2p + sh2).astype(BF16)

    def up_dot(a, c):
        t = jnp.maximum(_dot(hm_s[a][...], w1[c][...]), 0.0)
        hid_s[a][:, c * ff_cols:(c + 1) * ff_cols] = (t * t).astype(BF16)

    def down_dot(a, n):
        y_s[a][:, n * MXU_COLS:(n + 1) * MXU_COLS] = _dot(hid_s[a][...], w2[n][...])

    def norm3(a, r):
        rows = slice(a * half + r, a * half + r + ROW_BLOCK)
        o_ref[rows, :] = o_ref[rows, :] + g2 * _rms(y_s[a][r:r + ROW_BLOCK, :], ng_ref[3:4, :])

    def each(fn, a, items):
        return [functools.partial(fn, a, item) for item in items]

    _interleave(each(post_dot, 0, out_chunks), [])
    _interleave(each(post_dot, 1, out_chunks), each(norm1, 0, blocks))
    _interleave(each(up_dot, 0, ff_chunks), each(norm1, 1, blocks))
    _interleave(each(up_dot, 1, ff_chunks), [])
    _interleave(each(down_dot, 0, out_chunks), [])
    _interleave(each(down_dot, 1, out_chunks), each(norm3, 0, blocks))
    _interleave(each(norm3, 1, blocks), [])


def _tail(x, z, mods, ng, w_post, w1_all, w2_all, layer, is_ctx, z_mirrored=False):
    bsz, length, _ = x.shape
    kz = z.shape[-1]
    sub = min(512, length)
    tm = min(1024, length)
    wp_specs = _slab_specs(w_post, MXU_COLS)
    w1_specs = _slab_specs(w1_all, 2 * MXU_COLS, layer)
    w2_specs = _slab_specs(w2_all, MXU_COLS, layer)
    if z_mirrored:
        nt = length // 2 // tm
        z_spec = pl.BlockSpec(
            (None, None, tm, kz),
            lambda b, t: (b, t // nt, jnp.where(t < nt, t, 2 * nt - 1 - t), 0))
    else:
        z_spec = pl.BlockSpec((None, tm, kz), lambda b, t: (b, t, 0))
    return pl.pallas_call(
        functools.partial(_tail_kernel, sub=sub, z_mirrored=z_mirrored),
        grid=(bsz, length // tm),
        in_specs=[
            pl.BlockSpec((None, tm, D_MODEL), lambda b, t: (b, t, 0)),
            z_spec,
            _mod_spec(is_ctx, bsz),
            _const_spec((4, D_MODEL)),
        ] + wp_specs + w1_specs + w2_specs,
        out_specs=pl.BlockSpec((None, tm, D_MODEL), lambda b, t: (b, t, 0)),
        out_shape=jax.ShapeDtypeStruct((bsz, length, D_MODEL), F32),
        scratch_shapes=(
            [pltpu.VMEM((sub // 2, D_MODEL), F32)] * 2
            + [pltpu.VMEM((sub // 2, D_MODEL), BF16)] * 2
            + [pltpu.VMEM((sub // 2, D_FF), BF16)] * 2),
        compiler_params=_params(58),
        name="tail",
    )(x, z, mods, ng, *([w_post] * len(wp_specs)), *([w1_all] * len(w1_specs)),
      *([w2_all] * len(w2_specs)))


def _fourier_chan_kernel(x_ref, mod_ref, ng_ref, cs_ref, ab_ref, h_s):
    _modulated_norm_to(h_s, x_ref, mod_ref, ng_ref)
    for g in range(B_GROUPS):
        cols = slice(g * B_GROUP_W, (g + 1) * B_GROUP_W)
        t = _dot(h_s[:, cols], cs_ref[...])
        ab_ref[0, :, cols] = t[:, :B_GROUP_W].astype(BF16)
        ab_ref[1, :, cols] = t[:, B_GROUP_W:].astype(BF16)


def _fourier_seq_kernel(cs_ref, ab_ref, y_ref):
    for n in range(D_MODEL // MXU_COLS):
        cols = slice(n * MXU_COLS, (n + 1) * MXU_COLS)
        y_ref[:, cols] = _dot(cs_ref[...], ab_ref[:, cols]).astype(BF16)


def _dft_tables(length):
    m = jnp.arange(B_GROUP_W, dtype=jnp.int32)
    ang_c = (2.0 * jnp.pi / B_GROUP_W) * ((m[:, None] * m[None, :]) % B_GROUP_W).astype(F32)
    s_c = B_GROUP_W ** -0.5
    cs_chan = jnp.concatenate([jnp.cos(ang_c), jnp.sin(ang_c)], axis=1) * s_c
    k = jnp.arange(length, dtype=jnp.int32)
    ang_l = (2.0 * jnp.pi / length) * ((k[:, None] * k[None, :]) % length).astype(F32)
    s_l = length ** -0.5
    cs_seq = jnp.concatenate([jnp.cos(ang_l), -jnp.sin(ang_l)], axis=1) * s_l
    return cs_chan.astype(BF16), cs_seq.astype(BF16)


def _fourier_fold_kernel(xp_ref, xm_ref, x0_ref, mod_ref, ng_ref, cc_ref, sc_ref, rev_ref,
                         ab_ref, a0_ref, hp_s, hm_s):
    tm = xm_ref.shape[0]
    sh1 = mod_ref[0:1, :]
    sc1p = 1.0 + mod_ref[1:2, :]
    g0 = ng_ref[0:1, :]
    for r in range(0, tm, ROW_BLOCK):
        rows = slice(r, r + ROW_BLOCK)
        hp_s[rows, :] = _rms(xp_ref[0, r + 1:r + 1 + ROW_BLOCK, :], g0) * sc1p + sh1
        hm_s[rows, :] = (_rms(xm_ref[rows, :], g0) * sc1p + sh1).astype(BF16)
    h0 = (_rms(x0_ref[...], g0) * sc1p + sh1).astype(BF16)
    for g in range(B_GROUPS):
        cols = slice(g * B_GROUP_W, (g + 1) * B_GROUP_W)
        hm_rev = _dot(rev_ref[...], hm_s[:, cols])
        hp = hp_s[:, cols]
        ab_ref[0, :, cols] = _dot((hp + hm_rev).astype(BF16), cc_ref[...]).astype(BF16)
        ab_ref[1, :, cols] = _dot((hp - hm_rev).astype(BF16), sc_ref[...]).astype(BF16)
        a0_ref[:, cols] = _dot(h0[:, cols], cc_ref[...])


def _fourier_seq_fold_kernel(ct_ref, st_ref, ab_ref, a0_ref, perm_ref, y_ref, w_s, *, scale):
    tk = y_ref.shape[1]
    half = ct_ref.shape[1]
    for n in range(D_MODEL // MXU_COLS):
        cols = slice(n * MXU_COLS, (n + 1) * MXU_COLS)
        yc = _dot(ct_ref[...], ab_ref[0:half, cols]) + a0_ref[0:1, cols] * scale
        ys = _dot(st_ref[...], ab_ref[half:2 * half, cols])
        y_ref[0, :, cols] = (yc[:tk] - ys[:tk]).astype(BF16)
        w_s[:, cols] = (yc + ys).astype(BF16)
    for n in range(D_MODEL // MXU_COLS):
        cols = slice(n * MXU_COLS, (n + 1) * MXU_COLS)
        y_ref[1, :, cols] = _dot(perm_ref[...], w_s[:, cols]).astype(BF16)


def _seq_fold_tables(length, tk):
    half = length // 2
    nt = half // tk
    rows = tk + BF16_ROWS
    fine = 64
    n_coarse = -(-(nt * tk + BF16_ROWS) // fine)
    t = jnp.arange(1, half + 1, dtype=jnp.int32)

    def cos_sin(k):
        ang = (2.0 * jnp.pi / length) * ((k[:, None] * t[None, :]) % length).astype(F32)
        return jnp.cos(ang), jnp.sin(ang)

    ch, sh = cos_sin(jnp.arange(n_coarse, dtype=jnp.int32) * fine)
    cl, sl = cos_sin(jnp.arange(fine, dtype=jnp.int32))
    weight = jnp.where(t == half, 0.5, 1.0) * length ** -0.5
    c = (ch[:, None, :] * cl[None] - sh[:, None, :] * sl[None]).reshape(n_coarse * fine, half)
    s = (sh[:, None, :] * cl[None] + ch[:, None, :] * sl[None]).reshape(n_coarse * fine, half)
    ct = jnp.stack([c[i * tk:i * tk + rows] for i in range(nt)]) * weight
    st = jnp.stack([s[i * tk:i * tk + rows] for i in range(nt)]) * weight
    return ct.astype(BF16), st.astype(BF16)


def _fourier_head_folded(x, mods, ng):
    bsz, length, _ = x.shape
    tm = 512
    half = length // 2
    nt = half // tm
    rows = tm + BF16_ROWS
    m = jnp.arange(B_GROUP_W, dtype=jnp.int32)
    ang_c = (2.0 * jnp.pi / B_GROUP_W) * ((m[:, None] * m[None, :]) % B_GROUP_W).astype(F32)
    cc = (jnp.cos(ang_c) * B_GROUP_W ** -0.5).astype(BF16)
    sc = (jnp.sin(ang_c) * B_GROUP_W ** -0.5).astype(BF16)
    rev = (jnp.arange(tm)[:, None] + jnp.arange(tm)[None, :] == tm - 1).astype(BF16)
    ab, a0 = pl.pallas_call(
        _fourier_fold_kernel,
        grid=(bsz, nt),
        in_specs=[
            pl.BlockSpec((pl.Element(1), pl.Element(tm + 8), pl.Element(D_MODEL)),
                         lambda b, t: (b, t * tm, 0)),
            pl.BlockSpec((None, tm, D_MODEL), lambda b, t: (b, 2 * nt - 1 - t, 0)),
            pl.BlockSpec((None, 8, D_MODEL), lambda b, t: (b, 0, 0)),
            _mod_spec(False, bsz),
            _const_spec((4, D_MODEL)),
            _const_spec((B_GROUP_W, B_GROUP_W)),
            _const_spec((B_GROUP_W, B_GROUP_W)),
            _const_spec((tm, tm)),
        ],
        out_specs=[
            pl.BlockSpec((None, 2, tm, D_MODEL), lambda b, t: (b, 0, t, 0)),
            pl.BlockSpec((None, 8, D_MODEL), lambda b, t: (b, 0, 0)),
        ],
        out_shape=[
            jax.ShapeDtypeStruct((bsz, 2, half, D_MODEL), BF16),
            jax.ShapeDtypeStruct((bsz, 8, D_MODEL), F32),
        ],
        scratch_shapes=[pltpu.VMEM((tm, D_MODEL), F32), pltpu.VMEM((tm, D_MODEL), BF16)],
        compiler_params=_params(32),
        name="fourier_fold",
    )(x, x, x, mods, ng, cc, sc, rev)
    ct, st = _seq_fold_tables(length, tm)
    perm = (jnp.arange(tm)[:, None] + jnp.arange(rows)[None, :] == tm).astype(BF16)
    return pl.pallas_call(
        functools.partial(_fourier_seq_fold_kernel, scale=length ** -0.5),
        grid=(bsz, nt),
        in_specs=[
            pl.BlockSpec((None, rows, half), lambda b, t: (t, 0, 0)),
            pl.BlockSpec((None, rows, half), lambda b, t: (t, 0, 0)),
            pl.BlockSpec((None, 2 * half, D_MODEL), lambda b, t: (b, 0, 0)),
            pl.BlockSpec((None, 8, D_MODEL), lambda b, t: (b, 0, 0)),
            _const_spec((tm, rows)),
        ],
        out_specs=pl.BlockSpec((None, 2, tm, D_MODEL), lambda b, t: (b, 0, t, 0)),
        out_shape=jax.ShapeDtypeStruct((bsz, 2, half, D_MODEL), BF16),
        scratch_shapes=[pltpu.VMEM((rows, D_MODEL), BF16)],
        compiler_params=_params(40),
        name="fourier_seq_fold",
    )(ct, st, ab.reshape(bsz, 2 * half, D_MODEL), a0, perm)


def _fourier_head(x, mods, ng, is_ctx):
    bsz, length, _ = x.shape
    tm = min(512, length)
    cs_chan, cs_seq = _dft_tables(length)
    ab = pl.pallas_call(
        _fourier_chan_kernel,
        grid=(bsz, length // tm),
        in_specs=[
            pl.BlockSpec((None, tm, D_MODEL), lambda b, t: (b, t, 0)),
            _mod_spec(is_ctx, bsz),
            _const_spec((4, D_MODEL)),
            _const_spec((B_GROUP_W, 2 * B_GROUP_W)),
        ],
        out_specs=pl.BlockSpec((None, 2, tm, D_MODEL), lambda b, t: (b, 0, t, 0)),
        out_shape=jax.ShapeDtypeStruct((bsz, 2, length, D_MODEL), BF16),
        scratch_shapes=[pltpu.VMEM((tm, D_MODEL), BF16)],
        compiler_params=_params(32),
        name="fourier_chan",
    )(x, mods, ng, cs_chan)
    ab = ab.reshape(bsz, 2 * length, D_MODEL)
    return pl.pallas_call(
        _fourier_seq_kernel,
        grid=(bsz, length // tm),
        in_specs=[
            pl.BlockSpec((tm, 2 * length), lambda b, t: (t, 0)),
            pl.BlockSpec((None, 2 * length, D_MODEL), lambda b, t: (b, 0, 0),
                         pipeline_mode=pl.Buffered(1)),
        ],
        out_specs=pl.BlockSpec((None, tm, D_MODEL), lambda b, t: (b, t, 0)),
        out_shape=jax.ShapeDtypeStruct((bsz, length, D_MODEL), BF16),
        compiler_params=_params(48),
        name="fourier_seq",
    )(cs_seq, ab)


def _head_rms(r, g):
    return r * lax.rsqrt(jnp.mean(r * r, axis=-1, keepdims=True) + NORM_EPS) * g


def _store_vt_ext(vt_ref, t):
    tokens = t.shape[0]
    first_row = lax.broadcasted_iota(jnp.int32, (BF16_ROWS, tokens), 0) == 0
    for e in range(N_KV_HEADS):
        vt_ref[e, 0:HEAD_DIM, :] = t[:, e * HEAD_DIM:(e + 1) * HEAD_DIM].T.astype(BF16)
        vt_ref[e, HEAD_DIM:HEAD_DIM + BF16_ROWS, :] = first_row.astype(BF16)


def _qkv_kernel(x_ref, mod_ref, ng_ref, w_ref, qg_ref, kg_ref, mean_ref, cos_ref, sin_ref,
                q_ref, k_ref, vt_ref, h_s):
    _modulated_norm_to(h_s, x_ref, mod_ref, ng_ref)
    cosf = cos_ref[...]
    sinf = sin_ref[...]

    qg = qg_ref[...] * (HEAD_DIM ** -0.5 * LOG2_E)
    gains = [jnp.concatenate([qg, qg], axis=1)] * (N_HEADS // 2)
    gains.append(jnp.concatenate([kg_ref[...], kg_ref[...]], axis=1))
    outs = [(q_ref, 2 * p) for p in range(N_HEADS // 2)] + [(k_ref, 0)]
    n_pairs = len(outs)

    def project(p):
        return _dot(h_s[...], w_ref[:, p * MXU_COLS:(p + 1) * MXU_COLS])

    def normalise(t, p):
        sq = t * t
        hi = sq.astype(BF16)
        lo = (sq - hi.astype(F32)).astype(BF16)
        ms = _dot(hi, mean_ref[...]) + _dot(lo, mean_ref[...])
        return t * lax.rsqrt(ms + NORM_EPS) * gains[p]

    def rope_store(rn, p):
        ref, first = outs[p]
        for e in range(2):
            r = rn[:, e * HEAD_DIM:(e + 1) * HEAD_DIM]
            ref[first + e] = (r * cosf + pltpu.roll(r, HEAD_DIM // 2, 1) * sinf).astype(BF16)

    t_next = project(0)
    rn_prev = None
    for p in range(n_pairs + 1):
        t = t_next
        if p < n_pairs:
            t_next = project(p + 1)
        rn = normalise(t, p) if p < n_pairs else None
        if rn_prev is not None:
            rope_store(rn_prev, p - 1)
        rn_prev = rn
    _store_vt_ext(vt_ref, t)


def _kv_ctx_kernel(x_ref, mod_ref, ng_ref, w_ref, kg_ref, k_ref, vt_ref, h_s):
    _modulated_norm_to(h_s, x_ref, mod_ref, ng_ref)
    t = _dot(h_s[...], w_ref[:, 0:MXU_COLS])
    for e in range(N_KV_HEADS):
        k_ref[e] = _head_rms(t[:, e * HEAD_DIM:(e + 1) * HEAD_DIM], kg_ref[...]).astype(BF16)
    _store_vt_ext(vt_ref, _dot(h_s[...], w_ref[:, MXU_COLS:2 * MXU_COLS]))


def _rope_tables(length):
    t = jnp.arange(length)
    row = (t // GRID_W).astype(F32)
    col = (t % GRID_W).astype(F32)
    n_freq = HEAD_DIM // 4
    inv = ROPE_THETA ** (-jnp.arange(n_freq, dtype=F32) / n_freq)
    ang = jnp.concatenate([row[:, None] * inv, col[:, None] * inv], axis=-1)
    cos, sin = jnp.cos(ang), jnp.sin(ang)
    return jnp.concatenate([cos, cos], axis=-1), jnp.concatenate([-sin, sin], axis=-1)


def _qkv_latent(x, mods, ng, w_qkv, q_g, k_g):
    bsz, length, _ = x.shape
    tm = 512
    cosf, sinf = _rope_tables(length)
    n_qkv = w_qkv.shape[-1]
    lane_head = jnp.arange(MXU_COLS) // HEAD_DIM
    head_mean = ((lane_head[:, None] == lane_head[None, :]) * (1.0 / HEAD_DIM)).astype(BF16)
    kv_shape = jax.ShapeDtypeStruct((bsz, N_KV_HEADS, length, HEAD_DIM), BF16)
    kv_spec = pl.BlockSpec((None, N_KV_HEADS, tm, HEAD_DIM), lambda b, t: (b, 0, t, 0))
    vt_shape = jax.ShapeDtypeStruct((bsz, N_KV_HEADS, HEAD_DIM + BF16_ROWS, length), BF16)
    vt_spec = pl.BlockSpec((None, N_KV_HEADS, HEAD_DIM + BF16_ROWS, tm), lambda b, t: (b, 0, 0, t))
    return pl.pallas_call(
        _qkv_kernel,
        grid=(bsz, length // tm),
        in_specs=[
            pl.BlockSpec((None, tm, D_MODEL), lambda b, t: (b, t, 0)),
            _mod_spec(False, bsz),
            _const_spec((4, D_MODEL)),
            _const_spec((D_MODEL, n_qkv)),
            _const_spec((1, HEAD_DIM)),
            _const_spec((1, HEAD_DIM)),
            _const_spec((MXU_COLS, MXU_COLS)),
            pl.BlockSpec((tm, HEAD_DIM), lambda b, t: (t, 0)),
            pl.BlockSpec((tm, HEAD_DIM), lambda b, t: (t, 0)),
        ],
        out_specs=[
            pl.BlockSpec((None, N_HEADS, tm, HEAD_DIM), lambda b, t: (b, 0, t, 0)),
            kv_spec, vt_spec,
        ],
        out_shape=[
            jax.ShapeDtypeStruct((bsz, N_HEADS, length, HEAD_DIM), BF16),
            kv_shape, vt_shape,
        ],
        scratch_shapes=[pltpu.VMEM((tm, D_MODEL), BF16)],
        compiler_params=_params(32),
        name="qkv_latent",
    )(x, mods, ng, w_qkv, q_g, k_g, head_mean, cosf, sinf)


def _kv_ctx(ctx, mods, ng, w_kv, k_g):
    bsz, length, _ = ctx.shape
    tm = length
    kv_shape = jax.ShapeDtypeStruct((bsz, N_KV_HEADS, length, HEAD_DIM), BF16)
    kv_spec = pl.BlockSpec((None, N_KV_HEADS, tm, HEAD_DIM), lambda b, t: (b, 0, t, 0))
    vt_shape = jax.ShapeDtypeStruct((bsz, N_KV_HEADS, HEAD_DIM + BF16_ROWS, length), BF16)
    vt_spec = pl.BlockSpec((None, N_KV_HEADS, HEAD_DIM + BF16_ROWS, tm), lambda b, t: (b, 0, 0, t))
    return pl.pallas_call(
        _kv_ctx_kernel,
        grid=(bsz, length // tm),
        in_specs=[
            pl.BlockSpec((None, tm, D_MODEL), lambda b, t: (b, t, 0)),
            _mod_spec(True, bsz),
            _const_spec((4, D_MODEL)),
            _const_spec((D_MODEL, 2 * MXU_COLS)),
            _const_spec((1, HEAD_DIM)),
        ],
        out_specs=[kv_spec, vt_spec],
        out_shape=[kv_shape, vt_shape],
        scratch_shapes=[pltpu.VMEM((tm, D_MODEL), BF16)],
        compiler_params=_params(32),
        name="kv_ctx",
    )(ctx, mods, ng, w_kv, k_g)


def _region(index, fn):
    pl.when(pl.program_id(0) >= -index)(fn)


def _attn_kernel(q_ref, k_ref, kc_ref, vt_ref, vtc_ref, o_ref, *scratch):
    tq = q_ref.shape[1]
    n_latent = k_ref.shape[1]
    n_keys = n_latent + kc_ref.shape[1]
    nq = HEADS_PER_KV * tq
    q4_s, s_s, p_s, m_s = (scratch[i * N_KV_HEADS:(i + 1) * N_KV_HEADS] for i in range(4))
    key_chunks = [(k_ref, vt_ref, slice(r, r + MXU_COLS), slice(r, r + MXU_COLS))
                  for r in range(0, n_latent, MXU_COLS)]
    key_chunks += [(kc_ref, vtc_ref, slice(r, r + MXU_COLS), slice(n_latent + r, n_latent + r + MXU_COLS))
                   for r in range(0, n_keys - n_latent, MXU_COLS)]

    @pl.when(pl.program_id(0) == 0)
    def _():
        s_s[1][...] = jnp.zeros(s_s[1].shape, F32)
        m_s[1][...] = jnp.zeros(m_s[1].shape, F32)
        p_s[0][...] = jnp.ones(p_s[0].shape, BF16)

    def stage(j_scores, j_probs, j_pv):
        for h in range(HEADS_PER_KV):
            q4_s[j_scores][h * tq:(h + 1) * tq, :] = q_ref[HEADS_PER_KV * j_scores + h]
        m_probs = m_s[j_probs][...]
        m = None
        acc = None

        def store_head(acc, h):
            cols = slice(h * tq, (h + 1) * tq)
            o = acc[:HEAD_DIM, cols] / acc[HEAD_DIM:HEAD_DIM + 1, cols]
            head = HEADS_PER_KV * j_pv + h
            o_ref[:, head * HEAD_DIM:(head + 1) * HEAD_DIM] = o.T.astype(BF16)

        pv_chunks = list(key_chunks)
        heads_left = list(range(HEADS_PER_KV))
        tiles = [slice(r, r + BF16_ROWS) for r in range(0, n_keys, BF16_ROWS)]
        pv_rows_per_score_row = vt_ref.shape[1] / MXU_COLS
        work_total = len(key_chunks) * (1.0 + pv_rows_per_score_row)
        work_done = 0.0
        tiles_done = 0
        for k_src, _, local, rows in key_chunks:
            s = lax.dot_general(k_src[j_scores, local, :], q4_s[j_scores][...],
                                (((1,), (1,)), ((), ())), preferred_element_type=F32)
            s_s[j_scores][rows, :] = s
            cm = jnp.max(s, axis=0, keepdims=True)
            m = cm if m is None else jnp.maximum(m, cm)
            work_done += 1.0
            if pv_chunks:
                for _, vt_src, pv_local, pv_rows in (
                        pv_chunks.pop(0) for _ in range(min(2, len(pv_chunks)))):
                    d = _dot(vt_src[j_pv, :, pv_local], p_s[j_pv][pv_rows, :])
                    acc = d if acc is None else acc + d
                    work_done += pv_rows_per_score_row
            elif heads_left:
                store_head(acc, heads_left.pop(0))
            tiles_until = round(len(tiles) * work_done / work_total)
            for tile in tiles[tiles_done:tiles_until]:
                p_s[j_probs][tile, :] = jnp.exp2(s_s[j_probs][tile, :] - m_probs).astype(BF16)
            tiles_done = tiles_until
        assert not pv_chunks and tiles_done == len(tiles)
        for h in heads_left:
            store_head(acc, h)
        m_s[j_scores][...] = jnp.broadcast_to(m, (BF16_ROWS, nq))

    _region(0, lambda: stage(0, 1, 0))
    _region(1, lambda: stage(1, 0, 1))


def _attention(q, k, kc, vt, vtc):
    bsz, _, length, _ = q.shape
    n_ctx = kc.shape[2]
    n_keys = length + n_ctx
    vt_rows = vt.shape[2]
    tq = 128
    nq = HEADS_PER_KV * tq
    per_batch = length // tq
    n_tiles = bsz * per_batch

    def cur(i):
        return jnp.minimum(i, n_tiles - 1)

    def prev(i):
        return jnp.maximum(i - 1, 0)

    return pl.pallas_call(
        _attn_kernel,
        grid=(n_tiles + 1,),
        in_specs=[
            pl.BlockSpec((None, N_HEADS, tq, HEAD_DIM),
                         lambda i: (cur(i) // per_batch, 0, cur(i) % per_batch, 0)),
            pl.BlockSpec((None, N_KV_HEADS, length, HEAD_DIM), lambda i: (cur(i) // per_batch, 0, 0, 0)),
            pl.BlockSpec((None, N_KV_HEADS, n_ctx, HEAD_DIM), lambda i: (cur(i) // per_batch, 0, 0, 0)),
            pl.BlockSpec((None, N_KV_HEADS, vt_rows, length), lambda i: (prev(i) // per_batch, 0, 0, 0)),
            pl.BlockSpec((None, N_KV_HEADS, vt_rows, n_ctx), lambda i: (prev(i) // per_batch, 0, 0, 0)),
        ],
        out_specs=pl.BlockSpec((None, tq, D_MODEL),
                               lambda i: (prev(i) // per_batch, prev(i) % per_batch, 0)),
        out_shape=jax.ShapeDtypeStruct((bsz, length, D_MODEL), BF16),
        scratch_shapes=(
            [pltpu.VMEM((nq, HEAD_DIM), BF16)] * N_KV_HEADS
            + [pltpu.VMEM((n_keys, nq), F32)] * N_KV_HEADS
            + [pltpu.VMEM((n_keys, nq), BF16)] * N_KV_HEADS
            + [pltpu.VMEM((BF16_ROWS, nq), F32)] * N_KV_HEADS),
        compiler_params=pltpu.CompilerParams(
            dimension_semantics=("arbitrary",), vmem_limit_bytes=54 * MIB),
        name="attention",
    )(q, k, kc, vt, vtc)


def kernel(x, c, ctx, c_ctx, ada_w, ada_b, norm_g, mlp_w1, mlp_w2, a_w_in, a_ln_g, a_w_s, a_b_s, a_w_out,
           b_w_out, c_w_qkv, c_q_g, c_k_g, c_w_o):
    bsz = x.shape[0]
    attn_layers = [i for i in range(DEPTH) if i % N_MIXERS == 2]
    last_ctx_read = attn_layers[-1] if attn_layers else -1

    cond = jnp.zeros((COND_ROWS, D_MODEL), F32).at[:bsz].set(c).at[bsz].set(c_ctx)
    mods_all = _ada_all(cond, ada_w, ada_b).reshape(DEPTH, COND_ROWS, 6, D_MODEL)
    w1_all = mlp_w1.astype(BF16)
    w2_all = mlp_w2.astype(BF16)


    for i in range(DEPTH):
        kind, j = i % N_MIXERS, i // N_MIXERS
        ctx_in = i <= last_ctx_read
        ctx_out = i < last_ctx_read
        mods = mods_all[i]
        ng = norm_g[i]

        if kind == 0:
            w_in = a_w_in[j].astype(BF16)
            ln_g = a_ln_g[j].reshape(1, A_WIDTH)
            w_s = a_w_s[j].astype(BF16)
            b_s = a_b_s[j].reshape(A_GROUPS, CHUNK, 1)
            w_post = a_w_out[j].astype(BF16)
            z = _gmlp_head(x, mods, ng, w_in, ln_g, w_s, b_s, False)
            zc = _gmlp_head(ctx, mods, ng, w_in, ln_g, w_s, b_s, True) if ctx_out else None
        elif kind == 1:
            w_post = b_w_out[j].astype(BF16)
            z = _fourier_head_folded(x, mods, ng)
            zc = _fourier_head(ctx, mods, ng, True) if ctx_out else None
        else:
            w_qkv = c_w_qkv[j].astype(BF16)
            w_post = c_w_o[j].astype(BF16)
            q_g = c_q_g[j].reshape(1, HEAD_DIM)
            k_g = c_k_g[j].reshape(1, HEAD_DIM)
            assert ctx_in and not ctx_out
            q, k, vt = _qkv_latent(x, mods, ng, w_qkv, q_g, k_g)
            kc, vtc = _kv_ctx(ctx, mods, ng, w_qkv[:, N_HEADS * HEAD_DIM:], k_g)
            z = _attention(q, k, kc, vt, vtc)
            zc = None

        x = _tail(x, z, mods, ng, w_post, w1_all, w2_all, i, False, z_mirrored=(kind == 1))
        if ctx_out:
            ctx = _tail(ctx, zc, mods, ng, w_post, w1_all, w2_all, i, True)

    return x
```

```python
import functools

import jax
import jax.numpy as jnp
from jax import lax
from jax.experimental import pallas as pl
from jax.experimental.pallas import tpu as pltpu

D_MODEL = 1024
DEPTH = 4
GRID_W = 64
N_MIXERS = 3
CHUNK = 128
A_WIDTH = 2 * D_MODEL
A_GROUPS = 8
A_GROUP_W = A_WIDTH // A_GROUPS
B_GROUPS = 4
B_GROUP_W = D_MODEL // B_GROUPS
HEAD_DIM = 128
N_HEADS = D_MODEL // HEAD_DIM
N_KV_HEADS = 2
HEADS_PER_KV = N_HEADS // N_KV_HEADS
ROPE_THETA = 10000.0
D_FF = 4 * D_MODEL
NORM_EPS = 1e-6
LN_EPS = 1e-5
LOG2_E = 1.4426950408889634

F32 = jnp.float32
BF16 = jnp.bfloat16

MXU_COLS = 256
ROW_BLOCK = 64
BF16_ROWS = 16
LANES = 128
COND_ROWS = 16
MIB = 1024 * 1024


def _dot(a, b):
    return jnp.dot(a, b, preferred_element_type=F32)


def _rms(x, g):
    ms = jnp.mean(x * x, axis=-1, keepdims=True)
    return x * lax.rsqrt(ms + NORM_EPS) * g


def _gelu_tanh(x):
    c = 2.0 * 0.7978845608028654 * LOG2_E
    z = x * (-c - (c * 0.044715) * (x * x))
    return x / (1.0 + jnp.exp2(z))


def _slab_specs(w, width, layer=None):
    k, n = w.shape[-2:]
    if layer is None:
        return [pl.BlockSpec((k, width), lambda b, t, c=c: (0, c), pipeline_mode=pl.Buffered(1))
                for c in range(n // width)]
    return [pl.BlockSpec((None, k, width), lambda b, t, c=c: (layer, 0, c), pipeline_mode=pl.Buffered(1))
            for c in range(n // width)]


def _const_spec(shape):
    zeros = (0,) * len(shape)
    return pl.BlockSpec(shape, lambda b, t: zeros, pipeline_mode=pl.Buffered(1))


def _mod_spec(is_ctx, batch):
    if is_ctx:
        return pl.BlockSpec((None, 6, D_MODEL), lambda b, t: (batch, 0, 0))
    return pl.BlockSpec((None, 6, D_MODEL), lambda b, t: (b, 0, 0))


def _params(vmem_mib):
    return pltpu.CompilerParams(
        dimension_semantics=("arbitrary", "arbitrary"),
        vmem_limit_bytes=vmem_mib * MIB)


def _modulated_norm_to(h_s, x_ref, mod_ref, ng_ref, start=0, stop=None):
    sh1 = mod_ref[0:1, :]
    sc1p = 1.0 + mod_ref[1:2, :]
    g0 = ng_ref[0:1, :]
    stop = x_ref.shape[0] if stop is None else stop
    for r in range(start, stop, ROW_BLOCK):
        rows = slice(r, r + ROW_BLOCK)
        h_s[rows, :] = (_rms(x_ref[rows, :], g0) * sc1p + sh1).astype(BF16)


def _ada_kernel(cond_ref, w_ref, b_ref, o_ref):
    a = cond_ref[...]
    a = a * jax.nn.sigmoid(a)
    o_ref[...] = _dot(a.astype(BF16), w_ref[...].astype(BF16)) + b_ref[...]


def _ada_all(cond, ada_w, ada_b):
    tn = 1536
    n_out = 6 * D_MODEL
    return pl.pallas_call(
        _ada_kernel,
        grid=(DEPTH, n_out // tn),
        in_specs=[
            pl.BlockSpec((COND_ROWS, D_MODEL), lambda i, n: (0, 0)),
            pl.BlockSpec((None, D_MODEL, tn), lambda i, n: (i, 0, n)),
            pl.BlockSpec((None, 1, tn), lambda i, n: (i, 0, n)),
        ],
        out_specs=pl.BlockSpec((None, COND_ROWS, tn), lambda i, n: (i, 0, n)),
        out_shape=jax.ShapeDtypeStruct((DEPTH, COND_ROWS, n_out), F32),
        compiler_params=_params(32),
        name="ada_mod",
    )(cond, ada_w, ada_b.reshape(DEPTH, 1, n_out))


def _gmlp_head_kernel(x_ref, mod_ref, ng_ref, lng_ref, ws_ref, bs_ref, *refs):
    width = 2 * MXU_COLS
    n_half = A_WIDTH // width
    win = refs[:2 * n_half]
    z_ref, h_s, u_s, v_s, vb_s, sum_s, sq_s = refs[2 * n_half:]
    tm = x_ref.shape[0]
    lng = lng_ref[...]
    half = tm // 2
    lanes = sum_s.shape[1]

    def proj(unit):
        c, a = unit
        return _dot(h_s[a * half:(a + 1) * half, :], win[c][...])

    def gelu_store(d, unit):
        c, a = unit
        is_v = c >= n_half
        dst, c = (v_s, c - n_half) if is_v else (u_s, c)
        for r in range(0, half, ROW_BLOCK):
            rows = slice(a * half + r, a * half + r + ROW_BLOCK)
            g = _gelu_tanh(d[r:r + ROW_BLOCK, :])
            dst[rows, c * width:(c + 1) * width] = g
            if is_v:
                parts = [g[:, i:i + lanes] for i in range(0, width, lanes)]
                p1 = sum(parts[1:], parts[0])
                p2 = sum([p * p for p in parts[1:]], parts[0] * parts[0])
                if c == 0:
                    sum_s[rows, :] = p1
                    sq_s[rows, :] = p2
                else:
                    sum_s[rows, :] += p1
                    sq_s[rows, :] += p2

    def layer_norm_rows(a, part):
        start = a * half + part * half // n_half
        for r in range(start, start + half // n_half, ROW_BLOCK // 2):
            rows = slice(r, r + ROW_BLOCK // 2)
            mu = jnp.sum(sum_s[rows, :], axis=-1, keepdims=True) * (1.0 / A_WIDTH)
            ex2 = jnp.sum(sq_s[rows, :], axis=-1, keepdims=True) * (1.0 / A_WIDTH)
            rstd = lax.rsqrt(ex2 - mu * mu + LN_EPS)
            vb_s[rows, :] = ((v_s[rows, :] - mu) * rstd * lng).astype(BF16)

    def gate_group(g):
        w = ws_ref[g]
        b = jnp.broadcast_to(bs_ref[g], (CHUNK, A_GROUP_W))
        cols = slice(g * A_GROUP_W, (g + 1) * A_GROUP_W)
        for n in range(tm // CHUNK):
            rows = slice(n * CHUNK, (n + 1) * CHUNK)
            sv = _dot(w, vb_s[rows, cols]) + b
            z_ref[rows, cols] = (u_s[rows, cols] * sv).astype(BF16)

    chunks = list(range(n_half, 2 * n_half)) + list(range(n_half))
    order = [(c, a) for c in chunks for a in range(2)]
    norms =[functools.partial(layer_norm_rows, a, part) for a in range(2) for part in range(n_half)]
    fillers = {2 * n_half + 1 + k: norms[k:k + 2] for k in range(0, len(norms), 2)}

    _modulated_norm_to(h_s, x_ref, mod_ref, ng_ref, 0, half)
    d = proj(order[0])
    _modulated_norm_to(h_s, x_ref, mod_ref, ng_ref, half, tm)
    for i in range(1, len(order) + 1):
        d_next = proj(order[i]) if i < len(order) else None
        gelu_store(d, order[i - 1])
        for filler in fillers.get(i - 1, []):
            filler()
        d = d_next
    for g in range(A_GROUPS):
        gate_group(g)


def _gmlp_head(x, mods, ng, w_in, ln_g, w_s, b_s, is_ctx):
    bsz, length, _ = x.shape
    tm = min(1024, length)
    w_in_specs = _slab_specs(w_in, 2 * MXU_COLS)
    return pl.pallas_call(
        _gmlp_head_kernel,
        grid=(bsz, length // tm),
        in_specs=[
            pl.BlockSpec((None, tm, D_MODEL), lambda b, t: (b, t, 0)),
            _mod_spec(is_ctx, bsz),
            _const_spec((4, D_MODEL)),
            _const_spec((1, A_WIDTH)),
            _const_spec((A_GROUPS, CHUNK, CHUNK)),
            _const_spec((A_GROUPS, CHUNK, 1)),
        ] + w_in_specs,
        out_specs=pl.BlockSpec((None, tm, A_WIDTH), lambda b, t: (b, t, 0)),
        out_shape=jax.ShapeDtypeStruct((bsz, length, A_WIDTH), BF16),
        scratch_shapes=[
            pltpu.VMEM((tm, D_MODEL), BF16),
            pltpu.VMEM((tm, A_WIDTH), F32),
            pltpu.VMEM((tm, A_WIDTH), F32),
            pltpu.VMEM((tm, A_WIDTH), BF16),
            pltpu.VMEM((tm, LANES), F32),
            pltpu.VMEM((tm, LANES), F32),
        ],
        compiler_params=_params(56),
        name="gmlp_head",
    )(x, mods, ng, ln_g, w_s, b_s, *([w_in] * len(w_in_specs)))


def _interleave(dots, fillers):
    per = -(-len(fillers) // len(dots))
    for i, dot_fn in enumerate(dots):
        dot_fn()
        for filler in fillers[i * per:(i + 1) * per]:
            filler()


def _tail_kernel(x_ref, z_ref, mod_ref, ng_ref, *refs, sub, z_mirrored, n_cast):
    n_out = D_MODEL // MXU_COLS
    n_ff = D_FF // (2 * MXU_COLS)
    n_w = 2 * n_out + n_ff
    wp, w1, w2 = refs[:n_out], refs[n_out:n_out + n_ff], refs[n_out + n_ff:n_w]
    cast_src = refs[n_w:n_w + n_cast]
    o_ref = refs[n_w + n_cast]
    cast_dst = refs[n_w + n_cast + 1:n_w + 2 * n_cast + 1]
    scratch = refs[n_w + 2 * n_cast + 1:]
    for src, dst in zip(cast_src, cast_dst):
        dst[...] = src[...].astype(BF16)
    n_sub = x_ref.shape[0] // sub
    weights = (mod_ref, ng_ref, wp, w1, w2)
    if n_sub == 1:
        _tail_subtile(x_ref, z_ref, *weights, o_ref, *scratch)
        return

    def body(i, carry):
        zi = i
        if z_mirrored:
            second_half = pl.program_id(1) >= pl.num_programs(1) // 2
            zi = jnp.where(second_half, n_sub - 1 - i, i)
        rows = pl.ds(pl.multiple_of(i * sub, sub), sub)
        z_rows = pl.ds(pl.multiple_of(zi * sub, sub), sub)
        _tail_subtile(x_ref.at[rows, :], z_ref.at[z_rows, :], *weights, o_ref.at[rows, :], *scratch)
        return carry

    lax.fori_loop(0, n_sub, body, 0)


def _tail_subtile(x_ref, z_ref, mod_ref, ng_ref, wp, w1, w2, o_ref, *scratch):
    half = x_ref.shape[0] // 2
    y_s, hm_s, hid_s = scratch[0:2], scratch[2:4], scratch[4:6]
    g1 = mod_ref[2:3, :]
    sh2 = mod_ref[3:4, :]
    sc2p = 1.0 + mod_ref[4:5, :]
    g2 = mod_ref[5:6, :]
    ff_cols = 2 * MXU_COLS
    out_chunks = list(range(D_MODEL // MXU_COLS))
    ff_chunks = list(range(D_FF // ff_cols))
    blocks = list(range(0, half, ROW_BLOCK))

    def post_dot(a, n):
        y_s[a][:, n * MXU_COLS:(n + 1) * MXU_COLS] = _dot(z_ref[a * half:(a + 1) * half, :], wp[n][...])

    def norm1(a, r):
        rows = slice(a * half + r, a * half + r + ROW_BLOCK)
        x1 = x_ref[rows, :] + g1 * _rms(y_s[a][r:r + ROW_BLOCK, :], ng_ref[1:2, :])
        o_ref[rows, :] = x1
        hm_s[a][r:r + ROW_BLOCK, :] = (_rms(x1, ng_ref[2:3, :]) * sc2p + sh2).astype(BF16)

    def up_dot(a, c):
        t = jnp.maximum(_dot(hm_s[a][...], w1[c][...]), 0.0)
        hid_s[a][:, c * ff_cols:(c + 1) * ff_cols] = (t * t).astype(BF16)

    def down_dot(a, n):
        y_s[a][:, n * MXU_COLS:(n + 1) * MXU_COLS] = _dot(hid_s[a][...], w2[n][...])

    def norm3(a, r):
        rows = slice(a * half + r, a * half + r + ROW_BLOCK)
        o_ref[rows, :] = o_ref[rows, :] + g2 * _rms(y_s[a][r:r + ROW_BLOCK, :], ng_ref[3:4, :])

    def each(fn, a, items):
        return [functools.partial(fn, a, item) for item in items]

    _interleave(each(post_dot, 0, out_chunks), [])
    _interleave(each(post_dot, 1, out_chunks), each(norm1, 0, blocks))
    _interleave(each(up_dot, 0, ff_chunks), each(norm1, 1, blocks))
    _interleave(each(up_dot, 1, ff_chunks), [])
    _interleave(each(down_dot, 0, out_chunks), [])
    _interleave(each(down_dot, 1, out_chunks), each(norm3, 0, blocks))
    _interleave(each(norm3, 1, blocks), [])


def _tail(x, z, mods, ng, w_post, w1, w2, is_ctx, z_mirrored=False, cast_next=()):
    bsz, length, _ = x.shape
    kz = z.shape[-1]
    sub = min(512, length)
    tm = min(1024, length)
    w_post, post_index = w_post if isinstance(w_post, tuple) else (w_post, None)
    wp_specs = _slab_specs(w_post, MXU_COLS, post_index)
    w1_specs = _slab_specs(w1, 2 * MXU_COLS)
    w2_specs = _slab_specs(w2, MXU_COLS)
    n_steps = bsz * (length // tm)
    per_batch = length // tm
    cast_in_specs, cast_out_specs, cast_out_shapes = [], [], []
    for w_all, index in cast_next:
        k, n = w_all.shape[-2:]
        rows = k // n_steps
        cast_in_specs.append(pl.BlockSpec(
            (None, rows, n), lambda b, t, index=index: (index, b * per_batch + t, 0)))
        cast_out_specs.append(pl.BlockSpec((rows, n), lambda b, t: (b * per_batch + t, 0)))
        cast_out_shapes.append(jax.ShapeDtypeStruct((k, n), BF16))
    if z_mirrored:
        nt = length // 2 // tm
        z_spec = pl.BlockSpec(
            (None, None, tm, kz),
            lambda b, t: (b, t // nt, jnp.where(t < nt, t, 2 * nt - 1 - t), 0))
    else:
        z_spec = pl.BlockSpec((None, tm, kz), lambda b, t: (b, t, 0))
    out, *casts = pl.pallas_call(
        functools.partial(_tail_kernel, sub=sub, z_mirrored=z_mirrored, n_cast=len(cast_next)),
        grid=(bsz, length // tm),
        in_specs=[
            pl.BlockSpec((None, tm, D_MODEL), lambda b, t: (b, t, 0)),
            z_spec,
            _mod_spec(is_ctx, bsz),
            _const_spec((4, D_MODEL)),
        ] + wp_specs + w1_specs + w2_specs + cast_in_specs,
        out_specs=[pl.BlockSpec((None, tm, D_MODEL), lambda b, t: (b, t, 0))] + cast_out_specs,
        out_shape=[jax.ShapeDtypeStruct((bsz, length, D_MODEL), F32)] + cast_out_shapes,
        scratch_shapes=(
            [pltpu.VMEM((sub // 2, D_MODEL), F32)] * 2
            + [pltpu.VMEM((sub // 2, D_MODEL), BF16)] * 2
            + [pltpu.VMEM((sub // 2, D_FF), BF16)] * 2),
        compiler_params=_params(58),
        name="tail",
    )(x, z, mods, ng, *([w_post] * len(wp_specs)), *([w1] * len(w1_specs)),
      *([w2] * len(w2_specs)), *[w_all for w_all, _ in cast_next])
    return out, casts


def _fourier_chan_kernel(x_ref, mod_ref, ng_ref, cs_ref, ab_ref, h_s):
    _modulated_norm_to(h_s, x_ref, mod_ref, ng_ref)
    for g in range(B_GROUPS):
        cols = slice(g * B_GROUP_W, (g + 1) * B_GROUP_W)
        t = _dot(h_s[:, cols], cs_ref[...])
        ab_ref[0, :, cols] = t[:, :B_GROUP_W].astype(BF16)
        ab_ref[1, :, cols] = t[:, B_GROUP_W:].astype(BF16)


def _fourier_seq_kernel(cs_ref, ab_ref, y_ref):
    for n in range(D_MODEL // MXU_COLS):
        cols = slice(n * MXU_COLS, (n + 1) * MXU_COLS)
        y_ref[:, cols] = _dot(cs_ref[...], ab_ref[:, cols]).astype(BF16)


def _dft_tables(length):
    m = jnp.arange(B_GROUP_W, dtype=jnp.int32)
    ang_c = (2.0 * jnp.pi / B_GROUP_W) * ((m[:, None] * m[None, :]) % B_GROUP_W).astype(F32)
    s_c = B_GROUP_W ** -0.5
    cs_chan = jnp.concatenate([jnp.cos(ang_c), jnp.sin(ang_c)], axis=1) * s_c
    k = jnp.arange(length, dtype=jnp.int32)
    ang_l = (2.0 * jnp.pi / length) * ((k[:, None] * k[None, :]) % length).astype(F32)
    s_l = length ** -0.5
    cs_seq = jnp.concatenate([jnp.cos(ang_l), -jnp.sin(ang_l)], axis=1) * s_l
    return cs_chan.astype(BF16), cs_seq.astype(BF16)


def _fourier_fold_kernel(xp_ref, xm_ref, x0_ref, mod_ref, ng_ref, cc_ref, sc_ref, rev_ref,
                         ab_ref, a0_ref, hp_s, hm_s):
    tm = xm_ref.shape[0]
    sh1 = mod_ref[0:1, :]
    sc1p = 1.0 + mod_ref[1:2, :]
    g0 = ng_ref[0:1, :]
    for r in range(0, tm, ROW_BLOCK):
        rows = slice(r, r + ROW_BLOCK)
        hp_s[rows, :] = _rms(xp_ref[0, r + 1:r + 1 + ROW_BLOCK, :], g0) * sc1p + sh1
        hm_s[rows, :] = (_rms(xm_ref[rows, :], g0) * sc1p + sh1).astype(BF16)
    h0 = (_rms(x0_ref[...], g0) * sc1p + sh1).astype(BF16)
    for g in range(B_GROUPS):
        cols = slice(g * B_GROUP_W, (g + 1) * B_GROUP_W)
        hm_rev = _dot(rev_ref[...], hm_s[:, cols])
        hp = hp_s[:, cols]
        ab_ref[0, :, cols] = _dot((hp + hm_rev).astype(BF16), cc_ref[...]).astype(BF16)
        ab_ref[1, :, cols] = _dot((hp - hm_rev).astype(BF16), sc_ref[...]).astype(BF16)
        a0_ref[:, cols] = _dot(h0[:, cols], cc_ref[...])


def _fourier_seq_fold_kernel(ct_ref, st_ref, ab_ref, a0_ref, perm_ref, y_ref, w_s, *, scale):
    tk = y_ref.shape[1]
    half = ct_ref.shape[1]
    for n in range(D_MODEL // MXU_COLS):
        cols = slice(n * MXU_COLS, (n + 1) * MXU_COLS)
        yc = _dot(ct_ref[...], ab_ref[0:half, cols]) + a0_ref[0:1, cols] * scale
        ys = _dot(st_ref[...], ab_ref[half:2 * half, cols])
        y_ref[0, :, cols] = (yc[:tk] - ys[:tk]).astype(BF16)
        w_s[:, cols] = (yc + ys).astype(BF16)
    for n in range(D_MODEL // MXU_COLS):
        cols = slice(n * MXU_COLS, (n + 1) * MXU_COLS)
        y_ref[1, :, cols] = _dot(perm_ref[...], w_s[:, cols]).astype(BF16)


def _seq_fold_tables(length, tk):
    half = length // 2
    nt = half // tk
    rows = tk + BF16_ROWS
    fine = 64
    n_coarse = -(-(nt * tk + BF16_ROWS) // fine)
    t = jnp.arange(1, half + 1, dtype=jnp.int32)

    def cos_sin(k):
        ang = (2.0 * jnp.pi / length) * ((k[:, None] * t[None, :]) % length).astype(F32)
        return jnp.cos(ang), jnp.sin(ang)

    ch, sh = cos_sin(jnp.arange(n_coarse, dtype=jnp.int32) * fine)
    cl, sl = cos_sin(jnp.arange(fine, dtype=jnp.int32))
    weight = jnp.where(t == half, 0.5, 1.0) * length ** -0.5
    c = (ch[:, None, :] * cl[None] - sh[:, None, :] * sl[None]).reshape(n_coarse * fine, half)
    s = (sh[:, None, :] * cl[None] + ch[:, None, :] * sl[None]).reshape(n_coarse * fine, half)
    ct = jnp.stack([c[i * tk:i * tk + rows] for i in range(nt)]) * weight
    st = jnp.stack([s[i * tk:i * tk + rows] for i in range(nt)]) * weight
    return ct.astype(BF16), st.astype(BF16)


def _fourier_head_folded(x, mods, ng):
    bsz, length, _ = x.shape
    tm = 512
    half = length // 2
    nt = half // tm
    rows = tm + BF16_ROWS
    m = jnp.arange(B_GROUP_W, dtype=jnp.int32)
    ang_c = (2.0 * jnp.pi / B_GROUP_W) * ((m[:, None] * m[None, :]) % B_GROUP_W).astype(F32)
    cc = (jnp.cos(ang_c) * B_GROUP_W ** -0.5).astype(BF16)
    sc = (jnp.sin(ang_c) * B_GROUP_W ** -0.5).astype(BF16)
    rev = (jnp.arange(tm)[:, None] + jnp.arange(tm)[None, :] == tm - 1).astype(BF16)
    ab, a0 = pl.pallas_call(
        _fourier_fold_kernel,
        grid=(bsz, nt),
        in_specs=[
            pl.BlockSpec((pl.Element(1), pl.Element(tm + 8), pl.Element(D_MODEL)),
                         lambda b, t: (b, t * tm, 0)),
            pl.BlockSpec((None, tm, D_MODEL), lambda b, t: (b, 2 * nt - 1 - t, 0)),
            pl.BlockSpec((None, 8, D_MODEL), lambda b, t: (b, 0, 0)),
            _mod_spec(False, bsz),
            _const_spec((4, D_MODEL)),
            _const_spec((B_GROUP_W, B_GROUP_W)),
            _const_spec((B_GROUP_W, B_GROUP_W)),
            _const_spec((tm, tm)),
        ],
        out_specs=[
            pl.BlockSpec((None, 2, tm, D_MODEL), lambda b, t: (b, 0, t, 0)),
            pl.BlockSpec((None, 8, D_MODEL), lambda b, t: (b, 0, 0)),
        ],
        out_shape=[
            jax.ShapeDtypeStruct((bsz, 2, half, D_MODEL), BF16),
            jax.ShapeDtypeStruct((bsz, 8, D_MODEL), F32),
        ],
        scratch_shapes=[pltpu.VMEM((tm, D_MODEL), F32), pltpu.VMEM((tm, D_MODEL), BF16)],
        compiler_params=_params(32),
        name="fourier_fold",
    )(x, x, x, mods, ng, cc, sc, rev)
    ct, st = _seq_fold_tables(length, tm)
    perm = (jnp.arange(tm)[:, None] + jnp.arange(rows)[None, :] == tm).astype(BF16)
    return pl.pallas_call(
        functools.partial(_fourier_seq_fold_kernel, scale=length ** -0.5),
        grid=(bsz, nt),
        in_specs=[
            pl.BlockSpec((None, rows, half), lambda b, t: (t, 0, 0)),
            pl.BlockSpec((None, rows, half), lambda b, t: (t, 0, 0)),
            pl.BlockSpec((None, 2 * half, D_MODEL), lambda b, t: (b, 0, 0)),
            pl.BlockSpec((None, 8, D_MODEL), lambda b, t: (b, 0, 0)),
            _const_spec((tm, rows)),
        ],
        out_specs=pl.BlockSpec((None, 2, tm, D_MODEL), lambda b, t: (b, 0, t, 0)),
        out_shape=jax.ShapeDtypeStruct((bsz, 2, half, D_MODEL), BF16),
        scratch_shapes=[pltpu.VMEM((rows, D_MODEL), BF16)],
        compiler_params=_params(40),
        name="fourier_seq_fold",
    )(ct, st, ab.reshape(bsz, 2 * half, D_MODEL), a0, perm)


def _fourier_head(x, mods, ng, is_ctx):
    bsz, length, _ = x.shape
    tm = min(512, length)
    cs_chan, cs_seq = _dft_tables(length)
    ab = pl.pallas_call(
        _fourier_chan_kernel,
        grid=(bsz, length // tm),
        in_specs=[
            pl.BlockSpec((None, tm, D_MODEL), lambda b, t: (b, t, 0)),
            _mod_spec(is_ctx, bsz),
            _const_spec((4, D_MODEL)),
            _const_spec((B_GROUP_W, 2 * B_GROUP_W)),
        ],
        out_specs=pl.BlockSpec((None, 2, tm, D_MODEL), lambda b, t: (b, 0, t, 0)),
        out_shape=jax.ShapeDtypeStruct((bsz, 2, length, D_MODEL), BF16),
        scratch_shapes=[pltpu.VMEM((tm, D_MODEL), BF16)],
        compiler_params=_params(32),
        name="fourier_chan",
    )(x, mods, ng, cs_chan)
    ab = ab.reshape(bsz, 2 * length, D_MODEL)
    return pl.pallas_call(
        _fourier_seq_kernel,
        grid=(bsz, length // tm),
        in_specs=[
            pl.BlockSpec((tm, 2 * length), lambda b, t: (t, 0)),
            pl.BlockSpec((None, 2 * length, D_MODEL), lambda b, t: (b, 0, 0),
                         pipeline_mode=pl.Buffered(1)),
        ],
        out_specs=pl.BlockSpec((None, tm, D_MODEL), lambda b, t: (b, t, 0)),
        out_shape=jax.ShapeDtypeStruct((bsz, length, D_MODEL), BF16),
        compiler_params=_params(48),
        name="fourier_seq",
    )(cs_seq, ab)


def _head_rms(r, g):
    return r * lax.rsqrt(jnp.mean(r * r, axis=-1, keepdims=True) + NORM_EPS) * g


def _store_vt_ext(vt_ref, t):
    tokens = t.shape[0]
    first_row = lax.broadcasted_iota(jnp.int32, (BF16_ROWS, tokens), 0) == 0
    for e in range(N_KV_HEADS):
        vt_ref[e, 0:HEAD_DIM, :] = t[:, e * HEAD_DIM:(e + 1) * HEAD_DIM].T.astype(BF16)
        vt_ref[e, HEAD_DIM:HEAD_DIM + BF16_ROWS, :] = first_row.astype(BF16)


def _qkv_kernel(x_ref, mod_ref, ng_ref, w_ref, qg_ref, kg_ref, mean_ref, cos_ref, sin_ref,
                q_ref, k_ref, vt_ref, h_s):
    _modulated_norm_to(h_s, x_ref, mod_ref, ng_ref)
    cosf = cos_ref[...]
    sinf = sin_ref[...]

    qg = qg_ref[...] * (HEAD_DIM ** -0.5 * LOG2_E)
    gains = [jnp.concatenate([qg, qg], axis=1)] * (N_HEADS // 2)
    gains.append(jnp.concatenate([kg_ref[...], kg_ref[...]], axis=1))
    outs = [(q_ref, 2 * p) for p in range(N_HEADS // 2)] + [(k_ref, 0)]
    n_pairs = len(outs)

    def project(p):
        return _dot(h_s[...], w_ref[:, p * MXU_COLS:(p + 1) * MXU_COLS])

    def normalise(t, p):
        sq = t * t
        hi = sq.astype(BF16)
        lo = (sq - hi.astype(F32)).astype(BF16)
        ms = _dot(hi, mean_ref[...]) + _dot(lo, mean_ref[...])
        return t * lax.rsqrt(ms + NORM_EPS) * gains[p]

    def rope_store(rn, p):
        ref, first = outs[p]
        for e in range(2):
            r = rn[:, e * HEAD_DIM:(e + 1) * HEAD_DIM]
            ref[first + e] = (r * cosf + pltpu.roll(r, HEAD_DIM // 2, 1) * sinf).astype(BF16)

    t_next = project(0)
    rn_prev = None
    for p in range(n_pairs + 1):
        t = t_next
        if p < n_pairs:
            t_next = project(p + 1)
        rn = normalise(t, p) if p < n_pairs else None
        if rn_prev is not None:
            rope_store(rn_prev, p - 1)
        rn_prev = rn
    _store_vt_ext(vt_ref, t)


def _kv_ctx_kernel(x_ref, mod_ref, ng_ref, w_ref, kg_ref, k_ref, vt_ref, h_s):
    _modulated_norm_to(h_s, x_ref, mod_ref, ng_ref)
    t = _dot(h_s[...], w_ref[:, 0:MXU_COLS])
    for e in range(N_KV_HEADS):
        k_ref[e] = _head_rms(t[:, e * HEAD_DIM:(e + 1) * HEAD_DIM], kg_ref[...]).astype(BF16)
    _store_vt_ext(vt_ref, _dot(h_s[...], w_ref[:, MXU_COLS:2 * MXU_COLS]))


def _rope_tables(length):
    t = jnp.arange(length)
    row = (t // GRID_W).astype(F32)
    col = (t % GRID_W).astype(F32)
    n_freq = HEAD_DIM // 4
    inv = ROPE_THETA ** (-jnp.arange(n_freq, dtype=F32) / n_freq)
    ang = jnp.concatenate([row[:, None] * inv, col[:, None] * inv], axis=-1)
    cos, sin = jnp.cos(ang), jnp.sin(ang)
    return jnp.concatenate([cos, cos], axis=-1), jnp.concatenate([-sin, sin], axis=-1)


def _qkv_latent(x, mods, ng, w_qkv, q_g, k_g):
    bsz, length, _ = x.shape
    tm = 512
    cosf, sinf = _rope_tables(length)
    n_qkv = w_qkv.shape[-1]
    lane_head = jnp.arange(MXU_COLS) // HEAD_DIM
    head_mean = ((lane_head[:, None] == lane_head[None, :]) * (1.0 / HEAD_DIM)).astype(BF16)
    kv_shape = jax.ShapeDtypeStruct((bsz, N_KV_HEADS, length, HEAD_DIM), BF16)
    kv_spec = pl.BlockSpec((None, N_KV_HEADS, tm, HEAD_DIM), lambda b, t: (b, 0, t, 0))
    vt_shape = jax.ShapeDtypeStruct((bsz, N_KV_HEADS, HEAD_DIM + BF16_ROWS, length), BF16)
    vt_spec = pl.BlockSpec((None, N_KV_HEADS, HEAD_DIM + BF16_ROWS, tm), lambda b, t: (b, 0, 0, t))
    return pl.pallas_call(
        _qkv_kernel,
        grid=(bsz, length // tm),
        in_specs=[
            pl.BlockSpec((None, tm, D_MODEL), lambda b, t: (b, t, 0)),
            _mod_spec(False, bsz),
            _const_spec((4, D_MODEL)),
            _const_spec((D_MODEL, n_qkv)),
            _const_spec((1, HEAD_DIM)),
            _const_spec((1, HEAD_DIM)),
            _const_spec((MXU_COLS, MXU_COLS)),
            pl.BlockSpec((tm, HEAD_DIM), lambda b, t: (t, 0)),
            pl.BlockSpec((tm, HEAD_DIM), lambda b, t: (t, 0)),
        ],
        out_specs=[
            pl.BlockSpec((None, N_HEADS, tm, HEAD_DIM), lambda b, t: (b, 0, t, 0)),
            kv_spec, vt_spec,
        ],
        out_shape=[
            jax.ShapeDtypeStruct((bsz, N_HEADS, length, HEAD_DIM), BF16),
            kv_shape, vt_shape,
        ],
        scratch_shapes=[pltpu.VMEM((tm, D_MODEL), BF16)],
        compiler_params=_params(32),
        name="qkv_latent",
    )(x, mods, ng, w_qkv, q_g, k_g, head_mean, cosf, sinf)


def _kv_ctx(ctx, mods, ng, w_kv, k_g):
    bsz, length, _ = ctx.shape
    tm = length
    kv_shape = jax.ShapeDtypeStruct((bsz, N_KV_HEADS, length, HEAD_DIM), BF16)
    kv_spec = pl.BlockSpec((None, N_KV_HEADS, tm, HEAD_DIM), lambda b, t: (b, 0, t, 0))
    vt_shape = jax.ShapeDtypeStruct((bsz, N_KV_HEADS, HEAD_DIM + BF16_ROWS, length), BF16)
    vt_spec = pl.BlockSpec((None, N_KV_HEADS, HEAD_DIM + BF16_ROWS, tm), lambda b, t: (b, 0, 0, t))
    return pl.pallas_call(
        _kv_ctx_kernel,
        grid=(bsz, length // tm),
        in_specs=[
            pl.BlockSpec((None, tm, D_MODEL), lambda b, t: (b, t, 0)),
            _mod_spec(True, bsz),
            _const_spec((4, D_MODEL)),
            _const_spec((D_MODEL, 2 * MXU_COLS)),
            _const_spec((1, HEAD_DIM)),
        ],
        out_specs=[kv_spec, vt_spec],
        out_shape=[kv_shape, vt_shape],
        scratch_shapes=[pltpu.VMEM((tm, D_MODEL), BF16)],
        compiler_params=_params(32),
        name="kv_ctx",
    )(ctx, mods, ng, w_kv, k_g)


def _region(index, fn):
    pl.when(pl.program_id(0) >= -index)(fn)


def _attn_kernel(q_ref, k_ref, kc_ref, vt_ref, vtc_ref, o_ref, *scratch):
    tq = q_ref.shape[1]
    n_latent = k_ref.shape[1]
    n_keys = n_latent + kc_ref.shape[1]
    nq = HEADS_PER_KV * tq
    q4_s, s_s, p_s, m_s = (scratch[i * N_KV_HEADS:(i + 1) * N_KV_HEADS] for i in range(4))
    key_chunks = [(k_ref, vt_ref, slice(r, r + MXU_COLS), slice(r, r + MXU_COLS))
                  for r in range(0, n_latent, MXU_COLS)]
    key_chunks += [(kc_ref, vtc_ref, slice(r, r + MXU_COLS), slice(n_latent + r, n_latent + r + MXU_COLS))
                   for r in range(0, n_keys - n_latent, MXU_COLS)]

    @pl.when(pl.program_id(0) == 0)
    def _():
        s_s[1][...] = jnp.zeros(s_s[1].shape, F32)
        m_s[1][...] = jnp.zeros(m_s[1].shape, F32)
        p_s[0][...] = jnp.ones(p_s[0].shape, BF16)

    def stage(j_scores, j_probs, j_pv):
        for h in range(HEADS_PER_KV):
            q4_s[j_scores][h * tq:(h + 1) * tq, :] = q_ref[HEADS_PER_KV * j_scores + h]
        m_probs = m_s[j_probs][...]
        m = None
        acc = None

        def store_head(acc, h):
            cols = slice(h * tq, (h + 1) * tq)
            o = acc[:HEAD_DIM, cols] / acc[HEAD_DIM:HEAD_DIM + 1, cols]
            head = HEADS_PER_KV * j_pv + h
            o_ref[:, head * HEAD_DIM:(head + 1) * HEAD_DIM] = o.T.astype(BF16)

        pv_chunks = list(key_chunks)
        heads_left = list(range(HEADS_PER_KV))
        tiles = [slice(r, r + BF16_ROWS) for r in range(0, n_keys, BF16_ROWS)]
        pv_rows_per_score_row = vt_ref.shape[1] / MXU_COLS
        work_total = len(key_chunks) * (1.0 + pv_rows_per_score_row)
        work_done = 0.0
        tiles_done = 0
        for k_src, _, local, rows in key_chunks:
            s = lax.dot_general(k_src[j_scores, local, :], q4_s[j_scores][...],
                                (((1,), (1,)), ((), ())), preferred_element_type=F32)
            s_s[j_scores][rows, :] = s
            cm = jnp.max(s, axis=0, keepdims=True)
            m = cm if m is None else jnp.maximum(m, cm)
            work_done += 1.0
            if pv_chunks:
                for _, vt_src, pv_local, pv_rows in (
                        pv_chunks.pop(0) for _ in range(min(2, len(pv_chunks)))):
                    d = _dot(vt_src[j_pv, :, pv_local], p_s[j_pv][pv_rows, :])
                    acc = d if acc is None else acc + d
                    work_done += pv_rows_per_score_row
            elif heads_left:
                store_head(acc, heads_left.pop(0))
            tiles_until = round(len(tiles) * work_done / work_total)
            for tile in tiles[tiles_done:tiles_until]:
                p_s[j_probs][tile, :] = jnp.exp2(s_s[j_probs][tile, :] - m_probs).astype(BF16)
            tiles_done = tiles_until
        assert not pv_chunks and tiles_done == len(tiles)
        for h in heads_left:
            store_head(acc, h)
        m_s[j_scores][...] = jnp.broadcast_to(m, (BF16_ROWS, nq))

    _region(0, lambda: stage(0, 1, 0))
    _region(1, lambda: stage(1, 0, 1))


def _attention(q, k, kc, vt, vtc):
    bsz, _, length, _ = q.shape
    n_ctx = kc.shape[2]
    n_keys = length + n_ctx
    vt_rows = vt.shape[2]
    tq = 128
    nq = HEADS_PER_KV * tq
    per_batch = length // tq
    n_tiles = bsz * per_batch

    def cur(i):
        return jnp.minimum(i, n_tiles - 1)

    def prev(i):
        return jnp.maximum(i - 1, 0)

    return pl.pallas_call(
        _attn_kernel,
        grid=(n_tiles + 1,),
        in_specs=[
            pl.BlockSpec((None, N_HEADS, tq, HEAD_DIM),
                         lambda i: (cur(i) // per_batch, 0, cur(i) % per_batch, 0)),
            pl.BlockSpec((None, N_KV_HEADS, length, HEAD_DIM), lambda i: (cur(i) // per_batch, 0, 0, 0)),
            pl.BlockSpec((None, N_KV_HEADS, n_ctx, HEAD_DIM), lambda i: (cur(i) // per_batch, 0, 0, 0)),
            pl.BlockSpec((None, N_KV_HEADS, vt_rows, length), lambda i: (prev(i) // per_batch, 0, 0, 0)),
            pl.BlockSpec((None, N_KV_HEADS, vt_rows, n_ctx), lambda i: (prev(i) // per_batch, 0, 0, 0)),
        ],
        out_specs=pl.BlockSpec((None, tq, D_MODEL),
                               lambda i: (prev(i) // per_batch, prev(i) % per_batch, 0)),
        out_shape=jax.ShapeDtypeStruct((bsz, length, D_MODEL), BF16),
        scratch_shapes=(
            [pltpu.VMEM((nq, HEAD_DIM), BF16)] * N_KV_HEADS
            + [pltpu.VMEM((n_keys, nq), F32)] * N_KV_HEADS
            + [pltpu.VMEM((n_keys, nq), BF16)] * N_KV_HEADS
            + [pltpu.VMEM((BF16_ROWS, nq), F32)] * N_KV_HEADS),
        compiler_params=pltpu.CompilerParams(
            dimension_semantics=("arbitrary",), vmem_limit_bytes=54 * MIB),
        name="attention",
    )(q, k, kc, vt, vtc)


def kernel(x, c, ctx, c_ctx, ada_w, ada_b, norm_g, mlp_w1, mlp_w2, a_w_in, a_ln_g, a_w_s, a_b_s, a_w_out,
           b_w_out, c_w_qkv, c_q_g, c_k_g, c_w_o):
    bsz = x.shape[0]
    attn_layers = [i for i in range(DEPTH) if i % N_MIXERS == 2]
    last_ctx_read = attn_layers[-1] if attn_layers else -1

    cond = jnp.zeros((COND_ROWS, D_MODEL), F32).at[:bsz].set(c).at[bsz].set(c_ctx)
    mods_all = _ada_all(cond, ada_w, ada_b).reshape(DEPTH, COND_ROWS, 6, D_MODEL)

    def big_weights(i):
        pairs = [(mlp_w1, i), (mlp_w2, i)]
        if i % N_MIXERS == 0:
            pairs += [(a_w_in, i // N_MIXERS), (a_w_out, i // N_MIXERS)]
        return pairs

    weights = [w_all[index].astype(BF16) for w_all, index in big_weights(0)]

    for i in range(DEPTH):
        kind, j = i % N_MIXERS, i // N_MIXERS
        ctx_in = i <= last_ctx_read
        ctx_out = i < last_ctx_read
        mods = mods_all[i]
        ng = norm_g[i]
        w1, w2 = weights[:2]

        if kind == 0:
            w_in, w_post = weights[2:]
            ln_g = a_ln_g[j].reshape(1, A_WIDTH)
            w_s = a_w_s[j].astype(BF16)
            b_s = a_b_s[j].reshape(A_GROUPS, CHUNK, 1)
            z = _gmlp_head(x, mods, ng, w_in, ln_g, w_s, b_s, False)
            zc = _gmlp_head(ctx, mods, ng, w_in, ln_g, w_s, b_s, True) if ctx_out else None
        elif kind == 1:
            w_post = b_w_out[j].astype(BF16)
            z = _fourier_head_folded(x, mods, ng)
            zc = _fourier_head(ctx, mods, ng, True) if ctx_out else None
        else:
            w_qkv = c_w_qkv[j].astype(BF16)
            w_post = c_w_o[j].astype(BF16)
            q_g = c_q_g[j].reshape(1, HEAD_DIM)
            k_g = c_k_g[j].reshape(1, HEAD_DIM)
            assert ctx_in and not ctx_out
            q, k, vt = _qkv_latent(x, mods, ng, w_qkv, q_g, k_g)
            kc, vtc = _kv_ctx(ctx, mods, ng, w_qkv[:, N_HEADS * HEAD_DIM:], k_g)
            z = _attention(q, k, kc, vt, vtc)
            zc = None

        cast_next = big_weights(i + 1) if i + 1 < DEPTH else ()
        x, next_weights = _tail(x, z, mods, ng, w_post, w1, w2, False, z_mirrored=(kind == 1),
                                cast_next=cast_next)
        if ctx_out:
            ctx, _ = _tail(ctx, zc, mods, ng, w_post, w1, w2, True)
        weights = next_weights

    return x
```

```python
import functools

import jax
import jax.numpy as jnp
from jax import lax
from jax.experimental import pallas as pl
from jax.experimental.pallas import tpu as pltpu

D_MODEL = 1024
DEPTH = 4
GRID_W = 64
N_MIXERS = 3
CHUNK = 128
A_WIDTH = 2 * D_MODEL
A_GROUPS = 8
A_GROUP_W = A_WIDTH // A_GROUPS
B_GROUPS = 4
B_GROUP_W = D_MODEL // B_GROUPS
HEAD_DIM = 128
N_HEADS = D_MODEL // HEAD_DIM
N_KV_HEADS = 2
HEADS_PER_KV = N_HEADS // N_KV_HEADS
ROPE_THETA = 10000.0
D_FF = 4 * D_MODEL
NORM_EPS = 1e-6
LN_EPS = 1e-5
LOG2_E = 1.4426950408889634

F32 = jnp.float32
BF16 = jnp.bfloat16

MXU_COLS = 256
ROW_BLOCK = 64
BF16_ROWS = 16
LANES = 128
COND_ROWS = 16
MIB = 1024 * 1024


def _dot(a, b):
    return jnp.dot(a, b, preferred_element_type=F32)


def _rms(x, g):
    ms = jnp.mean(x * x, axis=-1, keepdims=True)
    return x * lax.rsqrt(ms + NORM_EPS) * g


def _gelu_tanh(x):
    c = 2.0 * 0.7978845608028654 * LOG2_E
    z = x * (-c - (c * 0.044715) * (x * x))
    return x / (1.0 + jnp.exp2(z))


def _slab_specs(w, width, layer=None):
    k, n = w.shape[-2:]
    if layer is None:
        return [pl.BlockSpec((k, width), lambda b, t, c=c: (0, c), pipeline_mode=pl.Buffered(1))
                for c in range(n // width)]
    return [pl.BlockSpec((None, k, width), lambda b, t, c=c: (layer, 0, c), pipeline_mode=pl.Buffered(1))
            for c in range(n // width)]


def _const_spec(shape):
    zeros = (0,) * len(shape)
    return pl.BlockSpec(shape, lambda b, t: zeros, pipeline_mode=pl.Buffered(1))


def _mod_spec(is_ctx, batch):
    if is_ctx:
        return pl.BlockSpec((None, 6, D_MODEL), lambda b, t: (batch, 0, 0))
    return pl.BlockSpec((None, 6, D_MODEL), lambda b, t: (b, 0, 0))


def _params(vmem_mib):
    return pltpu.CompilerParams(
        dimension_semantics=("arbitrary", "arbitrary"),
        vmem_limit_bytes=vmem_mib * MIB)


def _modulated_norm_to(h_s, x_ref, mod_ref, ng_ref, start=0, stop=None):
    sh1 = mod_ref[0:1, :]
    sc1p = 1.0 + mod_ref[1:2, :]
    g0 = ng_ref[0:1, :]
    stop = x_ref.shape[0] if stop is None else stop
    for r in range(start, stop, ROW_BLOCK):
        rows = slice(r, r + ROW_BLOCK)
        h_s[rows, :] = (_rms(x_ref[rows, :], g0) * sc1p + sh1).astype(BF16)


def _ada_kernel(cond_ref, w_ref, b_ref, o_ref):
    a = cond_ref[...]
    a = a * jax.nn.sigmoid(a)
    o_ref[...] = _dot(a.astype(BF16), w_ref[...].astype(BF16)) + b_ref[...]


def _ada_all(cond, ada_w, ada_b):
    tn = 1536
    n_out = 6 * D_MODEL
    return pl.pallas_call(
        _ada_kernel,
        grid=(DEPTH, n_out // tn),
        in_specs=[
            pl.BlockSpec((COND_ROWS, D_MODEL), lambda i, n: (0, 0)),
            pl.BlockSpec((None, D_MODEL, tn), lambda i, n: (i, 0, n)),
            pl.BlockSpec((None, 1, tn), lambda i, n: (i, 0, n)),
        ],
        out_specs=pl.BlockSpec((None, COND_ROWS, tn), lambda i, n: (i, 0, n)),
        out_shape=jax.ShapeDtypeStruct((DEPTH, COND_ROWS, n_out), F32),
        compiler_params=_params(32),
        name="ada_mod",
    )(cond, ada_w, ada_b.reshape(DEPTH, 1, n_out))


def _gmlp_head_kernel(x_ref, mod_ref, ng_ref, lng_ref, ws_ref, bs_ref, *refs):
    width = 2 * MXU_COLS
    n_half = A_WIDTH // width
    win = refs[:2 * n_half]
    z_ref, h_s, u_s, v_s, vb_s, sum_s, sq_s = refs[2 * n_half:]
    tm = x_ref.shape[0]
    lng = lng_ref[...]
    half = tm // 2
    lanes = sum_s.shape[1]

    def proj(unit):
        c, a = unit
        return _dot(h_s[a * half:(a + 1) * half, :], win[c][...])

    def gelu_store(d, unit):
        c, a = unit
        is_v = c >= n_half
        dst, c = (v_s, c - n_half) if is_v else (u_s, c)
        for r in range(0, half, ROW_BLOCK):
            rows = slice(a * half + r, a * half + r + ROW_BLOCK)
            g = _gelu_tanh(d[r:r + ROW_BLOCK, :])
            dst[rows, c * width:(c + 1) * width] = g
            if is_v:
                parts = [g[:, i:i + lanes] for i in range(0, width, lanes)]
                p1 = sum(parts[1:], parts[0])
                p2 = sum([p * p for p in parts[1:]], parts[0] * parts[0])
                if c == 0:
                    sum_s[rows, :] = p1
                    sq_s[rows, :] = p2
                else:
                    sum_s[rows, :] += p1
                    sq_s[rows, :] += p2

    def layer_norm_rows(a, part):
        start = a * half + part * half // n_half
        for r in range(start, start + half // n_half, ROW_BLOCK // 2):
            rows = slice(r, r + ROW_BLOCK // 2)
            mu = jnp.sum(sum_s[rows, :], axis=-1, keepdims=True) * (1.0 / A_WIDTH)
            ex2 = jnp.sum(sq_s[rows, :], axis=-1, keepdims=True) * (1.0 / A_WIDTH)
            rstd = lax.rsqrt(ex2 - mu * mu + LN_EPS)
            vb_s[rows, :] = ((v_s[rows, :] - mu) * rstd * lng).astype(BF16)

    def gate_group(g):
        w = ws_ref[g]
        b = jnp.broadcast_to(bs_ref[g], (CHUNK, A_GROUP_W))
        cols = slice(g * A_GROUP_W, (g + 1) * A_GROUP_W)
        for n in range(tm // CHUNK):
            rows = slice(n * CHUNK, (n + 1) * CHUNK)
            sv = _dot(w, vb_s[rows, cols]) + b
            z_ref[rows, cols] = (u_s[rows, cols] * sv).astype(BF16)

    chunks = list(range(n_half, 2 * n_half)) + list(range(n_half))
    order = [(c, a) for c in chunks for a in range(2)]
    norms =[functools.partial(layer_norm_rows, a, part) for a in range(2) for part in range(n_half)]
    fillers = {2 * n_half + 1 + k: norms[k:k + 2] for k in range(0, len(norms), 2)}

    _modulated_norm_to(h_s, x_ref, mod_ref, ng_ref, 0, half)
    d = proj(order[0])
    _modulated_norm_to(h_s, x_ref, mod_ref, ng_ref, half, tm)
    for i in range(1, len(order) + 1):
        d_next = proj(order[i]) if i < len(order) else None
        gelu_store(d, order[i - 1])
        for filler in fillers.get(i - 1, []):
            filler()
        d = d_next
    for g in range(A_GROUPS):
        gate_group(g)


def _gmlp_head(x, mods, ng, w_in, ln_g, w_s, b_s, is_ctx):
    bsz, length, _ = x.shape
    tm = min(1024, length)
    w_in_specs = _slab_specs(w_in, 2 * MXU_COLS)
    return pl.pallas_call(
        _gmlp_head_kernel,
        grid=(bsz, length // tm),
        in_specs=[
            pl.BlockSpec((None, tm, D_MODEL), lambda b, t: (b, t, 0)),
            _mod_spec(is_ctx, bsz),
            _const_spec((4, D_MODEL)),
            _const_spec((1, A_WIDTH)),
            _const_spec((A_GROUPS, CHUNK, CHUNK)),
            _const_spec((A_GROUPS, CHUNK, 1)),
        ] + w_in_specs,
        out_specs=pl.BlockSpec((None, tm, A_WIDTH), lambda b, t: (b, t, 0)),
        out_shape=jax.ShapeDtypeStruct((bsz, length, A_WIDTH), BF16),
        scratch_shapes=[
            pltpu.VMEM((tm, D_MODEL), BF16),
            pltpu.VMEM((tm, A_WIDTH), F32),
            pltpu.VMEM((tm, A_WIDTH), F32),
            pltpu.VMEM((tm, A_WIDTH), BF16),
            pltpu.VMEM((tm, LANES), F32),
            pltpu.VMEM((tm, LANES), F32),
        ],
        compiler_params=_params(56),
        name="gmlp_head",
    )(x, mods, ng, ln_g, w_s, b_s, *([w_in] * len(w_in_specs)))


def _interleave(dots, fillers):
    per = -(-len(fillers) // len(dots))
    for i, dot_fn in enumerate(dots):
        dot_fn()
        for filler in fillers[i * per:(i + 1) * per]:
            filler()


def _tail_kernel(x_ref, z_ref, mod_ref, ng_ref, *refs, sub, z_mirrored, n_cast):
    n_out = D_MODEL // MXU_COLS
    n_ff = D_FF // (2 * MXU_COLS)
    n_w = 2 * n_out + n_ff
    wp, w1, w2 = refs[:n_out], refs[n_out:n_out + n_ff], refs[n_out + n_ff:n_w]
    cast_src = refs[n_w:n_w + n_cast]
    o_ref = refs[n_w + n_cast]
    cast_dst = refs[n_w + n_cast + 1:n_w + 2 * n_cast + 1]
    scratch = refs[n_w + 2 * n_cast + 1:]
    for src, dst in zip(cast_src, cast_dst):
        dst[...] = src[...].astype(BF16)
    n_sub = x_ref.shape[0] // sub
    weights = (mod_ref, ng_ref, wp, w1, w2)
    if n_sub == 1:
        _tail_subtile(x_ref, z_ref, *weights, o_ref, *scratch)
        return

    def body(i, carry):
        zi = i
        if z_mirrored:
            second_half = pl.program_id(1) >= pl.num_programs(1) // 2
            zi = jnp.where(second_half, n_sub - 1 - i, i)
        rows = pl.ds(pl.multiple_of(i * sub, sub), sub)
        z_rows = pl.ds(pl.multiple_of(zi * sub, sub), sub)
        _tail_subtile(x_ref.at[rows, :], z_ref.at[z_rows, :], *weights, o_ref.at[rows, :], *scratch)
        return carry

    lax.fori_loop(0, n_sub, body, 0)


def _tail_subtile(x_ref, z_ref, mod_ref, ng_ref, wp, w1, w2, o_ref, *scratch):
    half = x_ref.shape[0] // 2
    y_s, hm_s, hid_s = scratch[0:2], scratch[2:4], scratch[4:6]
    g1 = mod_ref[2:3, :]
    sh2 = mod_ref[3:4, :]
    sc2p = 1.0 + mod_ref[4:5, :]
    g2 = mod_ref[5:6, :]
    ff_cols = 2 * MXU_COLS
    out_chunks = list(range(D_MODEL // MXU_COLS))
    ff_chunks = list(range(D_FF // ff_cols))
    blocks = list(range(0, half, ROW_BLOCK))

    def post_dot(a, n):
        y_s[a][:, n * MXU_COLS:(n + 1) * MXU_COLS] = _dot(z_ref[a * half:(a + 1) * half, :], wp[n][...])

    def norm1(a, r):
        rows = slice(a * half + r, a * half + r + ROW_BLOCK)
        x1 = x_ref[rows, :] + g1 * _rms(y_s[a][r:r + ROW_BLOCK, :], ng_ref[1:2, :])
        o_ref[rows, :] = x1
        hm_s[a][r:r + ROW_BLOCK, :] = (_rms(x1, ng_ref[2:3, :]) * sc2p + sh2).astype(BF16)

    def up_dot(a, c):
        t = jnp.maximum(_dot(hm_s[a][...], w1[c][...]), 0.0)
        hid_s[a][:, c * ff_cols:(c + 1) * ff_cols] = (t * t).astype(BF16)

    def down_dot(a, n):
        y_s[a][:, n * MXU_COLS:(n + 1) * MXU_COLS] = _dot(hid_s[a][...], w2[n][...])

    def norm3(a, r):
        rows = slice(a * half + r, a * half + r + ROW_BLOCK)
        o_ref[rows, :] = o_ref[rows, :] + g2 * _rms(y_s[a][r:r + ROW_BLOCK, :], ng_ref[3:4, :])

    def each(fn, a, items):
        return [functools.partial(fn, a, item) for item in items]

    _interleave(each(post_dot, 0, out_chunks), [])
    _interleave(each(post_dot, 1, out_chunks), each(norm1, 0, blocks))
    _interleave(each(up_dot, 0, ff_chunks), each(norm1, 1, blocks))
    _interleave(each(up_dot, 1, ff_chunks), [])
    _interleave(each(down_dot, 0, out_chunks), [])
    _interleave(each(down_dot, 1, out_chunks), each(norm3, 0, blocks))
    _interleave(each(norm3, 1, blocks), [])


def _tail(x, z, mods, ng, w_post, w1, w2, is_ctx, z_mirrored=False, cast_next=()):
    bsz, length, _ = x.shape
    kz = z.shape[-1]
    sub = min(512, length)
    tm = min(1024, length)
    w_post, post_index = w_post if isinstance(w_post, tuple) else (w_post, None)
    wp_specs = _slab_specs(w_post, MXU_COLS, post_index)
    w1_specs = _slab_specs(w1, 2 * MXU_COLS)
    w2_specs = _slab_specs(w2, MXU_COLS)
    n_steps = bsz * (length // tm)
    per_batch = length // tm
    cast_in_specs, cast_out_specs, cast_out_shapes = [], [], []
    for w_all, index in cast_next:
        k, n = w_all.shape[-2:]
        rows = k // n_steps
        cast_in_specs.append(pl.BlockSpec(
            (None, rows, n), lambda b, t, index=index: (index, b * per_batch + t, 0)))
        cast_out_specs.append(pl.BlockSpec((rows, n), lambda b, t: (b * per_batch + t, 0)))
        cast_out_shapes.append(jax.ShapeDtypeStruct((k, n), BF16))
    if z_mirrored:
        nt = length // 2 // tm
        z_spec = pl.BlockSpec(
            (None, None, tm, kz),
            lambda b, t: (b, t // nt, jnp.where(t < nt, t, 2 * nt - 1 - t), 0))
    else:
        z_spec = pl.BlockSpec((None, tm, kz), lambda b, t: (b, t, 0))
    out, *casts = pl.pallas_call(
        functools.partial(_tail_kernel, sub=sub, z_mirrored=z_mirrored, n_cast=len(cast_next)),
        grid=(bsz, length // tm),
        in_specs=[
            pl.BlockSpec((None, tm, D_MODEL), lambda b, t: (b, t, 0)),
            z_spec,
            _mod_spec(is_ctx, bsz),
            _const_spec((4, D_MODEL)),
        ] + wp_specs + w1_specs + w2_specs + cast_in_specs,
        out_specs=[pl.BlockSpec((None, tm, D_MODEL), lambda b, t: (b, t, 0))] + cast_out_specs,
        out_shape=[jax.ShapeDtypeStruct((bsz, length, D_MODEL), F32)] + cast_out_shapes,
        scratch_shapes=(
            [pltpu.VMEM((sub // 2, D_MODEL), F32)] * 2
            + [pltpu.VMEM((sub // 2, D_MODEL), BF16)] * 2
            + [pltpu.VMEM((sub // 2, D_FF), BF16)] * 2),
        compiler_params=_params(58),
        name="tail",
    )(x, z, mods, ng, *([w_post] * len(wp_specs)), *([w1] * len(w1_specs)),
      *([w2] * len(w2_specs)), *[w_all for w_all, _ in cast_next])
    return out, casts


def _fourier_chan_kernel(x_ref, mod_ref, ng_ref, cs_ref, ab_ref, h_s):
    _modulated_norm_to(h_s, x_ref, mod_ref, ng_ref)
    for g in range(B_GROUPS):
        cols = slice(g * B_GROUP_W, (g + 1) * B_GROUP_W)
        t = _dot(h_s[:, cols], cs_ref[...])
        ab_ref[0, :, cols] = t[:, :B_GROUP_W].astype(BF16)
        ab_ref[1, :, cols] = t[:, B_GROUP_W:].astype(BF16)


def _fourier_seq_kernel(cs_ref, ab_ref, y_ref):
    for n in range(D_MODEL // MXU_COLS):
        cols = slice(n * MXU_COLS, (n + 1) * MXU_COLS)
        y_ref[:, cols] = _dot(cs_ref[...], ab_ref[:, cols]).astype(BF16)


def _dft_tables(length):
    m = jnp.arange(B_GROUP_W, dtype=jnp.int32)
    ang_c = (2.0 * jnp.pi / B_GROUP_W) * ((m[:, None] * m[None, :]) % B_GROUP_W).astype(F32)
    s_c = B_GROUP_W ** -0.5
    cs_chan = jnp.concatenate([jnp.cos(ang_c), jnp.sin(ang_c)], axis=1) * s_c
    k = jnp.arange(length, dtype=jnp.int32)
    ang_l = (2.0 * jnp.pi / length) * ((k[:, None] * k[None, :]) % length).astype(F32)
    s_l = length ** -0.5
    cs_seq = jnp.concatenate([jnp.cos(ang_l), -jnp.sin(ang_l)], axis=1) * s_l
    return cs_chan.astype(BF16), cs_seq.astype(BF16)


def _fourier_fold_kernel(xp_ref, xm_ref, x0_ref, mod_ref, ng_ref, cc_ref, sc_ref, rev_ref,
                         ab_ref, a0_ref, hp_s, hm_s):
    tm = xm_ref.shape[0]
    sh1 = mod_ref[0:1, :]
    sc1p = 1.0 + mod_ref[1:2, :]
    g0 = ng_ref[0:1, :]
    for r in range(0, tm, ROW_BLOCK):
        rows = slice(r, r + ROW_BLOCK)
        hp_s[rows, :] = _rms(xp_ref[0, r + 1:r + 1 + ROW_BLOCK, :], g0) * sc1p + sh1
        hm_s[rows, :] = (_rms(xm_ref[rows, :], g0) * sc1p + sh1).astype(BF16)
    h0 = (_rms(x0_ref[...], g0) * sc1p + sh1).astype(BF16)
    for g in range(B_GROUPS):
        cols = slice(g * B_GROUP_W, (g + 1) * B_GROUP_W)
        hm_rev = _dot(rev_ref[...], hm_s[:, cols])
        hp = hp_s[:, cols]
        ab_ref[0, :, cols] = _dot((hp + hm_rev).astype(BF16), cc_ref[...]).astype(BF16)
        ab_ref[1, :, cols] = _dot((hp - hm_rev).astype(BF16), sc_ref[...]).astype(BF16)
        a0_ref[:, cols] = _dot(h0[:, cols], cc_ref[...])


def _fourier_seq_fold_kernel(ct_ref, st_ref, ab_ref, a0_ref, perm_ref, y_ref, w_s, *, scale):
    tk = y_ref.shape[1]
    half = ct_ref.shape[1]
    for n in range(D_MODEL // MXU_COLS):
        cols = slice(n * MXU_COLS, (n + 1) * MXU_COLS)
        yc = _dot(ct_ref[...], ab_ref[0:half, cols]) + a0_ref[0:1, cols] * scale
        ys = _dot(st_ref[...], ab_ref[half:2 * half, cols])
        y_ref[0, :, cols] = (yc[:tk] - ys[:tk]).astype(BF16)
        w_s[:, cols] = (yc + ys).astype(BF16)
    for n in range(D_MODEL // MXU_COLS):
        cols = slice(n * MXU_COLS, (n + 1) * MXU_COLS)
        y_ref[1, :, cols] = _dot(perm_ref[...], w_s[:, cols]).astype(BF16)


def _seq_fold_tables(length, tk):
    half = length // 2
    nt = half // tk
    rows = tk + BF16_ROWS
    fine = 64
    per_tile = -(-rows // fine)
    assert tk % fine == 0
    t = jnp.arange(1, half + 1, dtype=jnp.int32)

    def cos_sin(k):
        ang = (2.0 * jnp.pi / length) * ((k[..., None] * t) % length).astype(F32)
        return jnp.cos(ang), jnp.sin(ang)

    coarse = (jnp.arange(nt, dtype=jnp.int32)[:, None] * tk
              + jnp.arange(per_tile, dtype=jnp.int32)[None, :] * fine)
    ch, sh = cos_sin(coarse)
    cl, sl = cos_sin(jnp.arange(fine, dtype=jnp.int32))
    weight = jnp.where(t == half, 0.5, 1.0) * length ** -0.5
    c = ch[:, :, None, :] * cl - sh[:, :, None, :] * sl
    s = sh[:, :, None, :] * cl + ch[:, :, None, :] * sl
    ct = (c.reshape(nt, per_tile * fine, half)[:, :rows] * weight).astype(BF16)
    st = (s.reshape(nt, per_tile * fine, half)[:, :rows] * weight).astype(BF16)
    return ct, st


def _fourier_head_folded(x, mods, ng):
    bsz, length, _ = x.shape
    tm = 512
    half = length // 2
    nt = half // tm
    rows = tm + BF16_ROWS
    m = jnp.arange(B_GROUP_W, dtype=jnp.int32)
    ang_c = (2.0 * jnp.pi / B_GROUP_W) * ((m[:, None] * m[None, :]) % B_GROUP_W).astype(F32)
    cc = (jnp.cos(ang_c) * B_GROUP_W ** -0.5).astype(BF16)
    sc = (jnp.sin(ang_c) * B_GROUP_W ** -0.5).astype(BF16)
    rev = (jnp.arange(tm)[:, None] + jnp.arange(tm)[None, :] == tm - 1).astype(BF16)
    ab, a0 = pl.pallas_call(
        _fourier_fold_kernel,
        grid=(bsz, nt),
        in_specs=[
            pl.BlockSpec((pl.Element(1), pl.Element(tm + 8), pl.Element(D_MODEL)),
                         lambda b, t: (b, t * tm, 0)),
            pl.BlockSpec((None, tm, D_MODEL), lambda b, t: (b, 2 * nt - 1 - t, 0)),
            pl.BlockSpec((None, 8, D_MODEL), lambda b, t: (b, 0, 0)),
            _mod_spec(False, bsz),
            _const_spec((4, D_MODEL)),
            _const_spec((B_GROUP_W, B_GROUP_W)),
            _const_spec((B_GROUP_W, B_GROUP_W)),
            _const_spec((tm, tm)),
        ],
        out_specs=[
            pl.BlockSpec((None, 2, tm, D_MODEL), lambda b, t: (b, 0, t, 0)),
            pl.BlockSpec((None, 8, D_MODEL), lambda b, t: (b, 0, 0)),
        ],
        out_shape=[
            jax.ShapeDtypeStruct((bsz, 2, half, D_MODEL), BF16),
            jax.ShapeDtypeStruct((bsz, 8, D_MODEL), F32),
        ],
        scratch_shapes=[pltpu.VMEM((tm, D_MODEL), F32), pltpu.VMEM((tm, D_MODEL), BF16)],
        compiler_params=_params(32),
        name="fourier_fold",
    )(x, x, x, mods, ng, cc, sc, rev)
    ct, st = _seq_fold_tables(length, tm)
    perm = (jnp.arange(tm)[:, None] + jnp.arange(rows)[None, :] == tm).astype(BF16)
    return pl.pallas_call(
        functools.partial(_fourier_seq_fold_kernel, scale=length ** -0.5),
        grid=(bsz, nt),
        in_specs=[
            pl.BlockSpec((None, rows, half), lambda b, t: (t, 0, 0)),
            pl.BlockSpec((None, rows, half), lambda b, t: (t, 0, 0)),
            pl.BlockSpec((None, 2 * half, D_MODEL), lambda b, t: (b, 0, 0)),
            pl.BlockSpec((None, 8, D_MODEL), lambda b, t: (b, 0, 0)),
            _const_spec((tm, rows)),
        ],
        out_specs=pl.BlockSpec((None, 2, tm, D_MODEL), lambda b, t: (b, 0, t, 0)),
        out_shape=jax.ShapeDtypeStruct((bsz, 2, half, D_MODEL), BF16),
        scratch_shapes=[pltpu.VMEM((rows, D_MODEL), BF16)],
        compiler_params=_params(40),
        name="fourier_seq_fold",
    )(ct, st, ab.reshape(bsz, 2 * half, D_MODEL), a0, perm)


def _fourier_head(x, mods, ng, is_ctx):
    bsz, length, _ = x.shape
    tm = min(512, length)
    cs_chan, cs_seq = _dft_tables(length)
    ab = pl.pallas_call(
        _fourier_chan_kernel,
        grid=(bsz, length // tm),
        in_specs=[
            pl.BlockSpec((None, tm, D_MODEL), lambda b, t: (b, t, 0)),
            _mod_spec(is_ctx, bsz),
            _const_spec((4, D_MODEL)),
            _const_spec((B_GROUP_W, 2 * B_GROUP_W)),
        ],
        out_specs=pl.BlockSpec((None, 2, tm, D_MODEL), lambda b, t: (b, 0, t, 0)),
        out_shape=jax.ShapeDtypeStruct((bsz, 2, length, D_MODEL), BF16),
        scratch_shapes=[pltpu.VMEM((tm, D_MODEL), BF16)],
        compiler_params=_params(32),
        name="fourier_chan",
    )(x, mods, ng, cs_chan)
    ab = ab.reshape(bsz, 2 * length, D_MODEL)
    return pl.pallas_call(
        _fourier_seq_kernel,
        grid=(bsz, length // tm),
        in_specs=[
            pl.BlockSpec((tm, 2 * length), lambda b, t: (t, 0)),
            pl.BlockSpec((None, 2 * length, D_MODEL), lambda b, t: (b, 0, 0),
                         pipeline_mode=pl.Buffered(1)),
        ],
        out_specs=pl.BlockSpec((None, tm, D_MODEL), lambda b, t: (b, t, 0)),
        out_shape=jax.ShapeDtypeStruct((bsz, length, D_MODEL), BF16),
        compiler_params=_params(48),
        name="fourier_seq",
    )(cs_seq, ab)


def _head_rms(r, g):
    return r * lax.rsqrt(jnp.mean(r * r, axis=-1, keepdims=True) + NORM_EPS) * g


def _store_vt_ext(vt_ref, t):
    tokens = t.shape[0]
    first_row = lax.broadcasted_iota(jnp.int32, (BF16_ROWS, tokens), 0) == 0
    for e in range(N_KV_HEADS):
        vt_ref[e, 0:HEAD_DIM, :] = t[:, e * HEAD_DIM:(e + 1) * HEAD_DIM].T.astype(BF16)
        vt_ref[e, HEAD_DIM:HEAD_DIM + BF16_ROWS, :] = first_row.astype(BF16)


def _qkv_kernel(x_ref, mod_ref, ng_ref, w_ref, qg_ref, kg_ref, mean_ref, cos_ref, sin_ref,
                q_ref, k_ref, vt_ref, h_s):
    _modulated_norm_to(h_s, x_ref, mod_ref, ng_ref)
    cosf = cos_ref[...]
    sinf = sin_ref[...]

    qg = qg_ref[...] * (HEAD_DIM ** -0.5 * LOG2_E)
    gains = [jnp.concatenate([qg, qg], axis=1)] * (N_HEADS // 2)
    gains.append(jnp.concatenate([kg_ref[...], kg_ref[...]], axis=1))
    outs = [(q_ref, 2 * p) for p in range(N_HEADS // 2)] + [(k_ref, 0)]
    n_pairs = len(outs)

    def project(p):
        return _dot(h_s[...], w_ref[:, p * MXU_COLS:(p + 1) * MXU_COLS])

    def normalise(t, p):
        sq = t * t
        hi = sq.astype(BF16)
        lo = (sq - hi.astype(F32)).astype(BF16)
        ms = _dot(hi, mean_ref[...]) + _dot(lo, mean_ref[...])
        return t * lax.rsqrt(ms + NORM_EPS) * gains[p]

    def rope_store(rn, p):
        ref, first = outs[p]
        for e in range(2):
            r = rn[:, e * HEAD_DIM:(e + 1) * HEAD_DIM]
            ref[first + e] = (r * cosf + pltpu.roll(r, HEAD_DIM // 2, 1) * sinf).astype(BF16)

    t_next = project(0)
    rn_prev = None
    for p in range(n_pairs + 1):
        t = t_next
        if p < n_pairs:
            t_next = project(p + 1)
        rn = normalise(t, p) if p < n_pairs else None
        if rn_prev is not None:
            rope_store(rn_prev, p - 1)
        rn_prev = rn
    _store_vt_ext(vt_ref, t)


def _kv_ctx_kernel(x_ref, mod_ref, ng_ref, w_ref, kg_ref, k_ref, vt_ref, h_s):
    _modulated_norm_to(h_s, x_ref, mod_ref, ng_ref)
    t = _dot(h_s[...], w_ref[:, 0:MXU_COLS])
    for e in range(N_KV_HEADS):
        k_ref[e] = _head_rms(t[:, e * HEAD_DIM:(e + 1) * HEAD_DIM], kg_ref[...]).astype(BF16)
    _store_vt_ext(vt_ref, _dot(h_s[...], w_ref[:, MXU_COLS:2 * MXU_COLS]))


def _rope_tables(length):
    t = jnp.arange(length)
    row = (t // GRID_W).astype(F32)
    col = (t % GRID_W).astype(F32)
    n_freq = HEAD_DIM // 4
    inv = ROPE_THETA ** (-jnp.arange(n_freq, dtype=F32) / n_freq)
    ang = jnp.concatenate([row[:, None] * inv, col[:, None] * inv], axis=-1)
    cos, sin = jnp.cos(ang), jnp.sin(ang)
    return jnp.concatenate([cos, cos], axis=-1), jnp.concatenate([-sin, sin], axis=-1)


def _qkv_latent(x, mods, ng, w_qkv, q_g, k_g):
    bsz, length, _ = x.shape
    tm = 512
    cosf, sinf = _rope_tables(length)
    n_qkv = w_qkv.shape[-1]
    lane_head = jnp.arange(MXU_COLS) // HEAD_DIM
    head_mean = ((lane_head[:, None] == lane_head[None, :]) * (1.0 / HEAD_DIM)).astype(BF16)
    kv_shape = jax.ShapeDtypeStruct((bsz, N_KV_HEADS, length, HEAD_DIM), BF16)
    kv_spec = pl.BlockSpec((None, N_KV_HEADS, tm, HEAD_DIM), lambda b, t: (b, 0, t, 0))
    vt_shape = jax.ShapeDtypeStruct((bsz, N_KV_HEADS, HEAD_DIM + BF16_ROWS, length), BF16)
    vt_spec = pl.BlockSpec((None, N_KV_HEADS, HEAD_DIM + BF16_ROWS, tm), lambda b, t: (b, 0, 0, t))
    return pl.pallas_call(
        _qkv_kernel,
        grid=(bsz, length // tm),
        in_specs=[
            pl.BlockSpec((None, tm, D_MODEL), lambda b, t: (b, t, 0)),
            _mod_spec(False, bsz),
            _const_spec((4, D_MODEL)),
            _const_spec((D_MODEL, n_qkv)),
            _const_spec((1, HEAD_DIM)),
            _const_spec((1, HEAD_DIM)),
            _const_spec((MXU_COLS, MXU_COLS)),
            pl.BlockSpec((tm, HEAD_DIM), lambda b, t: (t, 0)),
            pl.BlockSpec((tm, HEAD_DIM), lambda b, t: (t, 0)),
        ],
        out_specs=[
            pl.BlockSpec((None, N_HEADS, tm, HEAD_DIM), lambda b, t: (b, 0, t, 0)),
            kv_spec, vt_spec,
        ],
        out_shape=[
            jax.ShapeDtypeStruct((bsz, N_HEADS, length, HEAD_DIM), BF16),
            kv_shape, vt_shape,
        ],
        scratch_shapes=[pltpu.VMEM((tm, D_MODEL), BF16)],
        compiler_params=_params(32),
        name="qkv_latent",
    )(x, mods, ng, w_qkv, q_g, k_g, head_mean, cosf, sinf)


def _kv_ctx(ctx, mods, ng, w_kv, k_g):
    bsz, length, _ = ctx.shape
    tm = length
    kv_shape = jax.ShapeDtypeStruct((bsz, N_KV_HEADS, length, HEAD_DIM), BF16)
    kv_spec = pl.BlockSpec((None, N_KV_HEADS, tm, HEAD_DIM), lambda b, t: (b, 0, t, 0))
    vt_shape = jax.ShapeDtypeStruct((bsz, N_KV_HEADS, HEAD_DIM + BF16_ROWS, length), BF16)
    vt_spec = pl.BlockSpec((None, N_KV_HEADS, HEAD_DIM + BF16_ROWS, tm), lambda b, t: (b, 0, 0, t))
    return pl.pallas_call(
        _kv_ctx_kernel,
        grid=(bsz, length // tm),
        in_specs=[
            pl.BlockSpec((None, tm, D_MODEL), lambda b, t: (b, t, 0)),
            _mod_spec(True, bsz),
            _const_spec((4, D_MODEL)),
            _const_spec((D_MODEL, 2 * MXU_COLS)),
            _const_spec((1, HEAD_DIM)),
        ],
        out_specs=[kv_spec, vt_spec],
        out_shape=[kv_shape, vt_shape],
        scratch_shapes=[pltpu.VMEM((tm, D_MODEL), BF16)],
        compiler_params=_params(32),
        name="kv_ctx",
    )(ctx, mods, ng, w_kv, k_g)


def _region(index, fn):
    pl.when(pl.program_id(0) >= -index)(fn)


def _attn_kernel(q_ref, k_ref, kc_ref, vt_ref, vtc_ref, o_ref, *scratch):
    tq = q_ref.shape[1]
    n_latent = k_ref.shape[1]
    n_keys = n_latent + kc_ref.shape[1]
    nq = HEADS_PER_KV * tq
    q4_s, s_s, p_s, m_s = (scratch[i * N_KV_HEADS:(i + 1) * N_KV_HEADS] for i in range(4))
    key_chunks = [(k_ref, vt_ref, slice(r, r + MXU_COLS), slice(r, r + MXU_COLS))
                  for r in range(0, n_latent, MXU_COLS)]
    key_chunks += [(kc_ref, vtc_ref, slice(r, r + MXU_COLS), slice(n_latent + r, n_latent + r + MXU_COLS))
                   for r in range(0, n_keys - n_latent, MXU_COLS)]

    @pl.when(pl.program_id(0) == 0)
    def _():
        s_s[1][...] = jnp.zeros(s_s[1].shape, F32)
        m_s[1][...] = jnp.zeros(m_s[1].shape, F32)
        p_s[0][...] = jnp.ones(p_s[0].shape, BF16)

    def stage(j_scores, j_probs, j_pv):
        for h in range(HEADS_PER_KV):
            q4_s[j_scores][h * tq:(h + 1) * tq, :] = q_ref[HEADS_PER_KV * j_scores + h]
        m_probs = m_s[j_probs][...]
        m = None
        acc = None

        def store_head(acc, h):
            cols = slice(h * tq, (h + 1) * tq)
            o = acc[:HEAD_DIM, cols] / acc[HEAD_DIM:HEAD_DIM + 1, cols]
            head = HEADS_PER_KV * j_pv + h
            o_ref[:, head * HEAD_DIM:(head + 1) * HEAD_DIM] = o.T.astype(BF16)

        pv_chunks = list(key_chunks)
        heads_left = list(range(HEADS_PER_KV))
        tiles = [slice(r, r + BF16_ROWS) for r in range(0, n_keys, BF16_ROWS)]
        pv_rows_per_score_row = vt_ref.shape[1] / MXU_COLS
        work_total = len(key_chunks) * (1.0 + pv_rows_per_score_row)
        work_done = 0.0
        tiles_done = 0
        for k_src, _, local, rows in key_chunks:
            s = lax.dot_general(k_src[j_scores, local, :], q4_s[j_scores][...],
                                (((1,), (1,)), ((), ())), preferred_element_type=F32)
            s_s[j_scores][rows, :] = s
            cm = jnp.max(s, axis=0, keepdims=True)
            m = cm if m is None else jnp.maximum(m, cm)
            work_done += 1.0
            if pv_chunks:
                for _, vt_src, pv_local, pv_rows in (
                        pv_chunks.pop(0) for _ in range(min(2, len(pv_chunks)))):
                    d = _dot(vt_src[j_pv, :, pv_local], p_s[j_pv][pv_rows, :])
                    acc = d if acc is None else acc + d
                    work_done += pv_rows_per_score_row
            elif heads_left:
                store_head(acc, heads_left.pop(0))
            tiles_until = round(len(tiles) * work_done / work_total)
            for tile in tiles[tiles_done:tiles_until]:
                p_s[j_probs][tile, :] = jnp.exp2(s_s[j_probs][tile, :] - m_probs).astype(BF16)
            tiles_done = tiles_until
        assert not pv_chunks and tiles_done == len(tiles)
        for h in heads_left:
            store_head(acc, h)
        m_s[j_scores][...] = jnp.broadcast_to(m, (BF16_ROWS, nq))

    _region(0, lambda: stage(0, 1, 0))
    _region(1, lambda: stage(1, 0, 1))


def _attention(q, k, kc, vt, vtc):
    bsz, _, length, _ = q.shape
    n_ctx = kc.shape[2]
    n_keys = length + n_ctx
    vt_rows = vt.shape[2]
    tq = 128
    nq = HEADS_PER_KV * tq
    per_batch = length // tq
    n_tiles = bsz * per_batch

    def cur(i):
        return jnp.minimum(i, n_tiles - 1)

    def prev(i):
        return jnp.maximum(i - 1, 0)

    return pl.pallas_call(
        _attn_kernel,
        grid=(n_tiles + 1,),
        in_specs=[
            pl.BlockSpec((None, N_HEADS, tq, HEAD_DIM),
                         lambda i: (cur(i) // per_batch, 0, cur(i) % per_batch, 0)),
            pl.BlockSpec((None, N_KV_HEADS, length, HEAD_DIM), lambda i: (cur(i) // per_batch, 0, 0, 0)),
            pl.BlockSpec((None, N_KV_HEADS, n_ctx, HEAD_DIM), lambda i: (cur(i) // per_batch, 0, 0, 0)),
            pl.BlockSpec((None, N_KV_HEADS, vt_rows, length), lambda i: (prev(i) // per_batch, 0, 0, 0)),
            pl.BlockSpec((None, N_KV_HEADS, vt_rows, n_ctx), lambda i: (prev(i) // per_batch, 0, 0, 0)),
        ],
        out_specs=pl.BlockSpec((None, tq, D_MODEL),
                               lambda i: (prev(i) // per_batch, prev(i) % per_batch, 0)),
        out_shape=jax.ShapeDtypeStruct((bsz, length, D_MODEL), BF16),
        scratch_shapes=(
            [pltpu.VMEM((nq, HEAD_DIM), BF16)] * N_KV_HEADS
            + [pltpu.VMEM((n_keys, nq), F32)] * N_KV_HEADS
            + [pltpu.VMEM((n_keys, nq), BF16)] * N_KV_HEADS
            + [pltpu.VMEM((BF16_ROWS, nq), F32)] * N_KV_HEADS),
        compiler_params=pltpu.CompilerParams(
            dimension_semantics=("arbitrary",), vmem_limit_bytes=54 * MIB),
        name="attention",
    )(q, k, kc, vt, vtc)


def kernel(x, c, ctx, c_ctx, ada_w, ada_b, norm_g, mlp_w1, mlp_w2, a_w_in, a_ln_g, a_w_s, a_b_s, a_w_out,
           b_w_out, c_w_qkv, c_q_g, c_k_g, c_w_o):
    bsz = x.shape[0]
    attn_layers = [i for i in range(DEPTH) if i % N_MIXERS == 2]
    last_ctx_read = attn_layers[-1] if attn_layers else -1

    cond = jnp.zeros((COND_ROWS, D_MODEL), F32).at[:bsz].set(c).at[bsz].set(c_ctx)
    mods_all = _ada_all(cond, ada_w, ada_b).reshape(DEPTH, COND_ROWS, 6, D_MODEL)

    def big_weights(i):
        pairs = [(mlp_w1, i), (mlp_w2, i)]
        if i % N_MIXERS == 0:
            pairs += [(a_w_in, i // N_MIXERS), (a_w_out, i // N_MIXERS)]
        return pairs

    weights = [w_all[index].astype(BF16) for w_all, index in big_weights(0)]

    for i in range(DEPTH):
        kind, j = i % N_MIXERS, i // N_MIXERS
        ctx_in = i <= last_ctx_read
        ctx_out = i < last_ctx_read
        mods = mods_all[i]
        ng = norm_g[i]
        w1, w2 = weights[:2]

        if kind == 0:
            w_in, w_post = weights[2:]
            ln_g = a_ln_g[j].reshape(1, A_WIDTH)
            w_s = a_w_s[j].astype(BF16)
            b_s = a_b_s[j].reshape(A_GROUPS, CHUNK, 1)
            z = _gmlp_head(x, mods, ng, w_in, ln_g, w_s, b_s, False)
            zc = _gmlp_head(ctx, mods, ng, w_in, ln_g, w_s, b_s, True) if ctx_out else None
        elif kind == 1:
            w_post = b_w_out[j].astype(BF16)
            z = _fourier_head_folded(x, mods, ng)
            zc = _fourier_head(ctx, mods, ng, True) if ctx_out else None
        else:
            w_qkv = c_w_qkv[j].astype(BF16)
            w_post = c_w_o[j].astype(BF16)
            q_g = c_q_g[j].reshape(1, HEAD_DIM)
            k_g = c_k_g[j].reshape(1, HEAD_DIM)
            assert ctx_in and not ctx_out
            q, k, vt = _qkv_latent(x, mods, ng, w_qkv, q_g, k_g)
            kc, vtc = _kv_ctx(ctx, mods, ng, w_qkv[:, N_HEADS * HEAD_DIM:], k_g)
            z = _attention(q, k, kc, vt, vtc)
            zc = None

        cast_next = big_weights(i + 1) if i + 1 < DEPTH else ()
        x, next_weights = _tail(x, z, mods, ng, w_post, w1, w2, False, z_mirrored=(kind == 1),
                                cast_next=cast_next)
        if ctx_out:
            ctx, _ = _tail(ctx, zc, mods, ng, w_post, w1, w2, True)
        weights = next_weights

    return x
```

```python
import functools

import numpy as np
import jax
import jax.numpy as jnp
from jax import lax
from jax.experimental import pallas as pl
from jax.experimental.pallas import tpu as pltpu

D_MODEL = 1024
DEPTH = 4
GRID_W = 64
N_MIXERS = 3
CHUNK = 128
A_WIDTH = 2 * D_MODEL
A_GROUPS = 8
A_GROUP_W = A_WIDTH // A_GROUPS
B_GROUPS = 4
B_GROUP_W = D_MODEL // B_GROUPS
HEAD_DIM = 128
N_HEADS = D_MODEL // HEAD_DIM
N_KV_HEADS = 2
HEADS_PER_KV = N_HEADS // N_KV_HEADS
ROPE_THETA = 10000.0
D_FF = 4 * D_MODEL
NORM_EPS = 1e-6
LN_EPS = 1e-5
LOG2_E = 1.4426950408889634

F32 = jnp.float32
BF16 = jnp.bfloat16

MXU_COLS = 256
ROW_BLOCK = 64
BF16_ROWS = 16
LANES = 128
COND_ROWS = 16
MIB = 1024 * 1024


def _dot(a, b):
    return jnp.dot(a, b, preferred_element_type=F32)


def _rms(x, g):
    ms = jnp.mean(x * x, axis=-1, keepdims=True)
    return x * lax.rsqrt(ms + NORM_EPS) * g


def _gelu_tanh(x):
    c = 2.0 * 0.7978845608028654 * LOG2_E
    z = x * (-c - (c * 0.044715) * (x * x))
    return x / (1.0 + jnp.exp2(z))


def _slab_specs(w, width, layer=None):
    k, n = w.shape[-2:]
    if layer is None:
        return [pl.BlockSpec((k, width), lambda b, t, c=c: (0, c), pipeline_mode=pl.Buffered(1))
                for c in range(n // width)]
    return [pl.BlockSpec((None, k, width), lambda b, t, c=c: (layer, 0, c), pipeline_mode=pl.Buffered(1))
            for c in range(n // width)]


def _const_spec(shape):
    zeros = (0,) * len(shape)
    return pl.BlockSpec(shape, lambda b, t: zeros, pipeline_mode=pl.Buffered(1))


def _mod_spec(is_ctx, batch):
    if is_ctx:
        return pl.BlockSpec((None, 6, D_MODEL), lambda b, t: (batch, 0, 0))
    return pl.BlockSpec((None, 6, D_MODEL), lambda b, t: (b, 0, 0))


def _params(vmem_mib):
    return pltpu.CompilerParams(
        dimension_semantics=("arbitrary", "arbitrary"),
        vmem_limit_bytes=vmem_mib * MIB)


def _modulated_norm_to(h_s, x_ref, mod_ref, ng_ref, start=0, stop=None):
    sh1 = mod_ref[0:1, :]
    sc1p = 1.0 + mod_ref[1:2, :]
    g0 = ng_ref[0:1, :]
    stop = x_ref.shape[0] if stop is None else stop
    for r in range(start, stop, ROW_BLOCK):
        rows = slice(r, r + ROW_BLOCK)
        h_s[rows, :] = (_rms(x_ref[rows, :], g0) * sc1p + sh1).astype(BF16)


def _ada_kernel(cond_ref, w_ref, b_ref, o_ref):
    a = cond_ref[...]
    a = a * jax.nn.sigmoid(a)
    o_ref[...] = _dot(a.astype(BF16), w_ref[...].astype(BF16)) + b_ref[...]


def _ada_all(cond, ada_w, ada_b):
    tn = 1536
    n_out = 6 * D_MODEL
    return pl.pallas_call(
        _ada_kernel,
        grid=(DEPTH, n_out // tn),
        in_specs=[
            pl.BlockSpec((COND_ROWS, D_MODEL), lambda i, n: (0, 0)),
            pl.BlockSpec((None, D_MODEL, tn), lambda i, n: (i, 0, n)),
            pl.BlockSpec((None, 1, tn), lambda i, n: (i, 0, n)),
        ],
        out_specs=pl.BlockSpec((None, COND_ROWS, tn), lambda i, n: (i, 0, n)),
        out_shape=jax.ShapeDtypeStruct((DEPTH, COND_ROWS, n_out), F32),
        compiler_params=_params(32),
        name="ada_mod",
    )(cond, ada_w, ada_b.reshape(DEPTH, 1, n_out))


def _gmlp_head_kernel(x_ref, mod_ref, ng_ref, lng_ref, ws_ref, bs_ref, *refs):
    width = 2 * MXU_COLS
    n_half = A_WIDTH // width
    win = refs[:2 * n_half]
    z_ref, h_s, u_s, v_s, vb_s, sum_s, sq_s = refs[2 * n_half:]
    tm = x_ref.shape[0]
    lng = lng_ref[...]
    half = tm // 2
    lanes = sum_s.shape[1]

    def proj(unit):
        c, a = unit
        return _dot(h_s[a * half:(a + 1) * half, :], win[c][...])

    def gelu_store(d, unit):
        c, a = unit
        is_v = c >= n_half
        dst, c = (v_s, c - n_half) if is_v else (u_s, c)
        for r in range(0, half, ROW_BLOCK):
            rows = slice(a * half + r, a * half + r + ROW_BLOCK)
            g = _gelu_tanh(d[r:r + ROW_BLOCK, :])
            dst[rows, c * width:(c + 1) * width] = g
            if is_v:
                parts = [g[:, i:i + lanes] for i in range(0, width, lanes)]
                p1 = sum(parts[1:], parts[0])
                p2 = sum([p * p for p in parts[1:]], parts[0] * parts[0])
                if c == 0:
                    sum_s[rows, :] = p1
                    sq_s[rows, :] = p2
                else:
                    sum_s[rows, :] += p1
                    sq_s[rows, :] += p2

    def layer_norm_rows(a, part):
        start = a * half + part * half // n_half
        for r in range(start, start + half // n_half, ROW_BLOCK // 2):
            rows = slice(r, r + ROW_BLOCK // 2)
            mu = jnp.sum(sum_s[rows, :], axis=-1, keepdims=True) * (1.0 / A_WIDTH)
            ex2 = jnp.sum(sq_s[rows, :], axis=-1, keepdims=True) * (1.0 / A_WIDTH)
            rstd = lax.rsqrt(ex2 - mu * mu + LN_EPS)
            vb_s[rows, :] = ((v_s[rows, :] - mu) * rstd * lng).astype(BF16)

    def gate_group(g):
        w = ws_ref[g]
        b = jnp.broadcast_to(bs_ref[g], (CHUNK, A_GROUP_W))
        cols = slice(g * A_GROUP_W, (g + 1) * A_GROUP_W)
        for n in range(tm // CHUNK):
            rows = slice(n * CHUNK, (n + 1) * CHUNK)
            sv = _dot(w, vb_s[rows, cols]) + b
            z_ref[rows, cols] = (u_s[rows, cols] * sv).astype(BF16)

    chunks = list(range(n_half, 2 * n_half)) + list(range(n_half))
    order = [(c, a) for c in chunks for a in range(2)]
    norms =[functools.partial(layer_norm_rows, a, part) for a in range(2) for part in range(n_half)]
    fillers = {2 * n_half + 1 + k: norms[k:k + 2] for k in range(0, len(norms), 2)}

    _modulated_norm_to(h_s, x_ref, mod_ref, ng_ref, 0, half)
    d = proj(order[0])
    _modulated_norm_to(h_s, x_ref, mod_ref, ng_ref, half, tm)
    for i in range(1, len(order) + 1):
        d_next = proj(order[i]) if i < len(order) else None
        gelu_store(d, order[i - 1])
        for filler in fillers.get(i - 1, []):
            filler()
        d = d_next
    for g in range(A_GROUPS):
        gate_group(g)


def _gmlp_head(x, mods, ng, w_in, ln_g, w_s, b_s, is_ctx):
    bsz, length, _ = x.shape
    tm = min(1024, length)
    w_in_specs = _slab_specs(w_in, 2 * MXU_COLS)
    return pl.pallas_call(
        _gmlp_head_kernel,
        grid=(bsz, length // tm),
        in_specs=[
            pl.BlockSpec((None, tm, D_MODEL), lambda b, t: (b, t, 0)),
            _mod_spec(is_ctx, bsz),
            _const_spec((4, D_MODEL)),
            _const_spec((1, A_WIDTH)),
            _const_spec((A_GROUPS, CHUNK, CHUNK)),
            _const_spec((A_GROUPS, CHUNK, 1)),
        ] + w_in_specs,
        out_specs=pl.BlockSpec((None, tm, A_WIDTH), lambda b, t: (b, t, 0)),
        out_shape=jax.ShapeDtypeStruct((bsz, length, A_WIDTH), BF16),
        scratch_shapes=[
            pltpu.VMEM((tm, D_MODEL), BF16),
            pltpu.VMEM((tm, A_WIDTH), F32),
            pltpu.VMEM((tm, A_WIDTH), F32),
            pltpu.VMEM((tm, A_WIDTH), BF16),
            pltpu.VMEM((tm, LANES), F32),
            pltpu.VMEM((tm, LANES), F32),
        ],
        compiler_params=_params(56),
        name="gmlp_head",
    )(x, mods, ng, ln_g, w_s, b_s, *([w_in] * len(w_in_specs)))


def _interleave(dots, fillers):
    per = -(-len(fillers) // len(dots))
    for i, dot_fn in enumerate(dots):
        dot_fn()
        for filler in fillers[i * per:(i + 1) * per]:
            filler()


def _tail_kernel(x_ref, z_ref, mod_ref, ng_ref, *refs, sub, z_mirrored, n_cast):
    n_out = D_MODEL // MXU_COLS
    n_ff = D_FF // (2 * MXU_COLS)
    n_w = 2 * n_out + n_ff
    wp, w1, w2 = refs[:n_out], refs[n_out:n_out + n_ff], refs[n_out + n_ff:n_w]
    cast_src = refs[n_w:n_w + n_cast]
    o_ref = refs[n_w + n_cast]
    cast_dst = refs[n_w + n_cast + 1:n_w + 2 * n_cast + 1]
    scratch = refs[n_w + 2 * n_cast + 1:]
    for src, dst in zip(cast_src, cast_dst):
        dst[...] = src[...].astype(BF16)
    n_sub = x_ref.shape[0] // sub
    weights = (mod_ref, ng_ref, wp, w1, w2)
    pending = []
    for i in range(n_sub):
        zi = i
        if z_mirrored and n_sub > 1:
            second_half = pl.program_id(1) >= pl.num_programs(1) // 2
            zi = jnp.where(second_half, n_sub - 1 - i, i)
        rows = pl.ds(i * sub, sub)
        z_rows = rows if isinstance(zi, int) else pl.ds(pl.multiple_of(zi * sub, sub), sub)
        pending = _tail_subtile(x_ref.at[rows, :], z_ref.at[z_rows, :], *weights, o_ref.at[rows, :],
                                *scratch, pending=pending, defer_last=i + 1 < n_sub)


def _tail_subtile(x_ref, z_ref, mod_ref, ng_ref, wp, w1, w2, o_ref, *scratch, pending, defer_last):
    half = x_ref.shape[0] // 2
    y_s, hm_s, hid_s = scratch[0:2], scratch[2:4], scratch[4:6]
    g1 = mod_ref[2:3, :]
    sh2 = mod_ref[3:4, :]
    sc2p = 1.0 + mod_ref[4:5, :]
    g2 = mod_ref[5:6, :]
    ff_cols = 2 * MXU_COLS
    out_chunks = list(range(D_MODEL // MXU_COLS))
    ff_chunks = list(range(D_FF // ff_cols))
    blocks = list(range(0, half, ROW_BLOCK))

    def post_dot(a, n):
        y_s[a][:, n * MXU_COLS:(n + 1) * MXU_COLS] = _dot(z_ref[a * half:(a + 1) * half, :], wp[n][...])

    def norm1(a, r):
        rows = slice(a * half + r, a * half + r + ROW_BLOCK)
        x1 = x_ref[rows, :] + g1 * _rms(y_s[a][r:r + ROW_BLOCK, :], ng_ref[1:2, :])
        o_ref[rows, :] = x1
        hm_s[a][r:r + ROW_BLOCK, :] = (_rms(x1, ng_ref[2:3, :]) * sc2p + sh2).astype(BF16)

    def up_dot(a, c):
        t = jnp.maximum(_dot(hm_s[a][...], w1[c][...]), 0.0)
        hid_s[a][:, c * ff_cols:(c + 1) * ff_cols] = (t * t).astype(BF16)

    def down_dot(a, n):
        y_s[a][:, n * MXU_COLS:(n + 1) * MXU_COLS] = _dot(hid_s[a][...], w2[n][...])

    def norm3(a, r):
        rows = slice(a * half + r, a * half + r + ROW_BLOCK)
        o_ref[rows, :] = o_ref[rows, :] + g2 * _rms(y_s[a][r:r + ROW_BLOCK, :], ng_ref[3:4, :])

    def each(fn, a, items):
        return [functools.partial(fn, a, item) for item in items]

    _interleave(each(post_dot, 0, out_chunks), list(pending))
    _interleave(each(post_dot, 1, out_chunks), each(norm1, 0, blocks))
    _interleave(each(up_dot, 0, ff_chunks), each(norm1, 1, blocks))
    _interleave(each(up_dot, 1, ff_chunks), [])
    _interleave(each(down_dot, 0, out_chunks), [])
    _interleave(each(down_dot, 1, out_chunks), each(norm3, 0, blocks))
    last = each(norm3, 1, blocks)
    if defer_last:
        return last
    _interleave(last, [])
    return []


def _tail(x, z, mods, ng, w_post, w1, w2, is_ctx, z_mirrored=False, cast_next=()):
    bsz, length, _ = x.shape
    kz = z.shape[-1]
    sub = min(512, length)
    tm = min(1024, length)
    w_post, post_index = w_post if isinstance(w_post, tuple) else (w_post, None)
    wp_specs = _slab_specs(w_post, MXU_COLS, post_index)
    w1_specs = _slab_specs(w1, 2 * MXU_COLS)
    w2_specs = _slab_specs(w2, MXU_COLS)
    n_steps = bsz * (length // tm)
    per_batch = length // tm
    cast_in_specs, cast_out_specs, cast_out_shapes = [], [], []
    for w_all, index in cast_next:
        k, n = w_all.shape[-2:]
        rows = k // n_steps
        cast_in_specs.append(pl.BlockSpec(
            (None, rows, n), lambda b, t, index=index: (index, b * per_batch + t, 0)))
        cast_out_specs.append(pl.BlockSpec((rows, n), lambda b, t: (b * per_batch + t, 0)))
        cast_out_shapes.append(jax.ShapeDtypeStruct((k, n), BF16))
    if z_mirrored:
        nt = length // 2 // tm
        z_spec = pl.BlockSpec(
            (None, None, tm, kz),
            lambda b, t: (b, t // nt, jnp.where(t < nt, t, 2 * nt - 1 - t), 0))
    else:
        z_spec = pl.BlockSpec((None, tm, kz), lambda b, t: (b, t, 0))
    out, *casts = pl.pallas_call(
        functools.partial(_tail_kernel, sub=sub, z_mirrored=z_mirrored, n_cast=len(cast_next)),
        grid=(bsz, length // tm),
        in_specs=[
            pl.BlockSpec((None, tm, D_MODEL), lambda b, t: (b, t, 0)),
            z_spec,
            _mod_spec(is_ctx, bsz),
            _const_spec((4, D_MODEL)),
        ] + wp_specs + w1_specs + w2_specs + cast_in_specs,
        out_specs=[pl.BlockSpec((None, tm, D_MODEL), lambda b, t: (b, t, 0))] + cast_out_specs,
        out_shape=[jax.ShapeDtypeStruct((bsz, length, D_MODEL), F32)] + cast_out_shapes,
        scratch_shapes=(
            [pltpu.VMEM((sub // 2, D_MODEL), F32)] * 2
            + [pltpu.VMEM((sub // 2, D_MODEL), BF16)] * 2
            + [pltpu.VMEM((sub // 2, D_FF), BF16)] * 2),
        compiler_params=_params(58),
        name="tail",
    )(x, z, mods, ng, *([w_post] * len(wp_specs)), *([w1] * len(w1_specs)),
      *([w2] * len(w2_specs)), *[w_all for w_all, _ in cast_next])
    return out, casts


def _fourier_chan_kernel(x_ref, mod_ref, ng_ref, cs_ref, ab_ref, h_s):
    _modulated_norm_to(h_s, x_ref, mod_ref, ng_ref)
    for g in range(B_GROUPS):
        cols = slice(g * B_GROUP_W, (g + 1) * B_GROUP_W)
        t = _dot(h_s[:, cols], cs_ref[...])
        ab_ref[0, :, cols] = t[:, :B_GROUP_W].astype(BF16)
        ab_ref[1, :, cols] = t[:, B_GROUP_W:].astype(BF16)


def _fourier_seq_kernel(cs_ref, ab_ref, y_ref):
    for n in range(D_MODEL // MXU_COLS):
        cols = slice(n * MXU_COLS, (n + 1) * MXU_COLS)
        y_ref[:, cols] = _dot(cs_ref[...], ab_ref[:, cols]).astype(BF16)


def _bf16_const(a):
    return jnp.asarray(np.asarray(a, np.float32)).astype(BF16)


def _dft_angles(n):
    idx = np.arange(n, dtype=np.int64)
    return (2.0 * np.pi / n) * ((idx[:, None] * idx[None, :]) % n)


def _dft_tables(length):
    ang_c = _dft_angles(B_GROUP_W)
    cs_chan = np.concatenate([np.cos(ang_c), np.sin(ang_c)], axis=1) * B_GROUP_W ** -0.5
    ang_l = _dft_angles(length)
    cs_seq = np.concatenate([np.cos(ang_l), -np.sin(ang_l)], axis=1) * length ** -0.5
    return _bf16_const(cs_chan), _bf16_const(cs_seq)


def _fourier_fold_kernel(xp_ref, xm_ref, x0_ref, mod_ref, ng_ref, cc_ref, sc_ref, rev_ref,
                         ab_ref, a0_ref, hp_s, hm_s):
    tm = xm_ref.shape[0]
    sh1 = mod_ref[0:1, :]
    sc1p = 1.0 + mod_ref[1:2, :]
    g0 = ng_ref[0:1, :]
    for r in range(0, tm, ROW_BLOCK):
        rows = slice(r, r + ROW_BLOCK)
        hp_s[rows, :] = _rms(xp_ref[0, r + 1:r + 1 + ROW_BLOCK, :], g0) * sc1p + sh1
        hm_s[rows, :] = (_rms(xm_ref[rows, :], g0) * sc1p + sh1).astype(BF16)
    h0 = (_rms(x0_ref[...], g0) * sc1p + sh1).astype(BF16)
    for g in range(B_GROUPS):
        cols = slice(g * B_GROUP_W, (g + 1) * B_GROUP_W)
        hm_rev = _dot(rev_ref[...], hm_s[:, cols])
        hp = hp_s[:, cols]
        ab_ref[0, :, cols] = _dot((hp + hm_rev).astype(BF16), cc_ref[...]).astype(BF16)
        ab_ref[1, :, cols] = _dot((hp - hm_rev).astype(BF16), sc_ref[...]).astype(BF16)
        a0_ref[:, cols] = _dot(h0[:, cols], cc_ref[...])


def _fourier_seq_fold_kernel(ct_ref, st_ref, ab_ref, a0_ref, perm_ref, y_ref, w_s, *, scale):
    tk = y_ref.shape[1]
    half = ct_ref.shape[1]
    for n in range(D_MODEL // MXU_COLS):
        cols = slice(n * MXU_COLS, (n + 1) * MXU_COLS)
        yc = _dot(ct_ref[...], ab_ref[0:half, cols]) + a0_ref[0:1, cols] * scale
        ys = _dot(st_ref[...], ab_ref[half:2 * half, cols])
        y_ref[0, :, cols] = (yc[:tk] - ys[:tk]).astype(BF16)
        w_s[:, cols] = (yc + ys).astype(BF16)
    for n in range(D_MODEL // MXU_COLS):
        cols = slice(n * MXU_COLS, (n + 1) * MXU_COLS)
        y_ref[1, :, cols] = _dot(perm_ref[...], w_s[:, cols]).astype(BF16)


def _seq_fold_tables(length, tk):
    half = length // 2
    nt = half // tk
    rows = tk + BF16_ROWS
    t = np.arange(1, half + 1, dtype=np.int64)
    k = np.arange(nt, dtype=np.int64)[:, None] * tk + np.arange(rows, dtype=np.int64)[None, :]
    ang = (2.0 * np.pi / length) * ((k[..., None] * t) % length)
    weight = np.where(t == half, 0.5, 1.0) * length ** -0.5
    return _bf16_const(np.cos(ang) * weight), _bf16_const(np.sin(ang) * weight)


def _fourier_head_folded(x, mods, ng):
    bsz, length, _ = x.shape
    tm = 512
    half = length // 2
    nt = half // tm
    rows = tm + BF16_ROWS
    ang_c = _dft_angles(B_GROUP_W)
    cc = _bf16_const(np.cos(ang_c) * B_GROUP_W ** -0.5)
    sc = _bf16_const(np.sin(ang_c) * B_GROUP_W ** -0.5)
    rev = _bf16_const(np.arange(tm)[:, None] + np.arange(tm)[None, :] == tm - 1)
    ab, a0 = pl.pallas_call(
        _fourier_fold_kernel,
        grid=(bsz, nt),
        in_specs=[
            pl.BlockSpec((pl.Element(1), pl.Element(tm + 8), pl.Element(D_MODEL)),
                         lambda b, t: (b, t * tm, 0)),
            pl.BlockSpec((None, tm, D_MODEL), lambda b, t: (b, 2 * nt - 1 - t, 0)),
            pl.BlockSpec((None, 8, D_MODEL), lambda b, t: (b, 0, 0)),
            _mod_spec(False, bsz),
            _const_spec((4, D_MODEL)),
            _const_spec((B_GROUP_W, B_GROUP_W)),
            _const_spec((B_GROUP_W, B_GROUP_W)),
            _const_spec((tm, tm)),
        ],
        out_specs=[
            pl.BlockSpec((None, 2, tm, D_MODEL), lambda b, t: (b, 0, t, 0)),
            pl.BlockSpec((None, 8, D_MODEL), lambda b, t: (b, 0, 0)),
        ],
        out_shape=[
            jax.ShapeDtypeStruct((bsz, 2, half, D_MODEL), BF16),
            jax.ShapeDtypeStruct((bsz, 8, D_MODEL), F32),
        ],
        scratch_shapes=[pltpu.VMEM((tm, D_MODEL), F32), pltpu.VMEM((tm, D_MODEL), BF16)],
        compiler_params=_params(32),
        name="fourier_fold",
    )(x, x, x, mods, ng, cc, sc, rev)
    ct, st = _seq_fold_tables(length, tm)
    perm = _bf16_const(np.arange(tm)[:, None] + np.arange(rows)[None, :] == tm)
    return pl.pallas_call(
        functools.partial(_fourier_seq_fold_kernel, scale=length ** -0.5),
        grid=(bsz, nt),
        in_specs=[
            pl.BlockSpec((None, rows, half), lambda b, t: (t, 0, 0)),
            pl.BlockSpec((None, rows, half), lambda b, t: (t, 0, 0)),
            pl.BlockSpec((None, 2 * half, D_MODEL), lambda b, t: (b, 0, 0)),
            pl.BlockSpec((None, 8, D_MODEL), lambda b, t: (b, 0, 0)),
            _const_spec((tm, rows)),
        ],
        out_specs=pl.BlockSpec((None, 2, tm, D_MODEL), lambda b, t: (b, 0, t, 0)),
        out_shape=jax.ShapeDtypeStruct((bsz, 2, half, D_MODEL), BF16),
        scratch_shapes=[pltpu.VMEM((rows, D_MODEL), BF16)],
        compiler_params=_params(40),
        name="fourier_seq_fold",
    )(ct, st, ab.reshape(bsz, 2 * half, D_MODEL), a0, perm)


def _fourier_head(x, mods, ng, is_ctx):
    bsz, length, _ = x.shape
    tm = min(512, length)
    cs_chan, cs_seq = _dft_tables(length)
    ab = pl.pallas_call(
        _fourier_chan_kernel,
        grid=(bsz, length // tm),
        in_specs=[
            pl.BlockSpec((None, tm, D_MODEL), lambda b, t: (b, t, 0)),
            _mod_spec(is_ctx, bsz),
            _const_spec((4, D_MODEL)),
            _const_spec((B_GROUP_W, 2 * B_GROUP_W)),
        ],
        out_specs=pl.BlockSpec((None, 2, tm, D_MODEL), lambda b, t: (b, 0, t, 0)),
        out_shape=jax.ShapeDtypeStruct((bsz, 2, length, D_MODEL), BF16),
        scratch_shapes=[pltpu.VMEM((tm, D_MODEL), BF16)],
        compiler_params=_params(32),
        name="fourier_chan",
    )(x, mods, ng, cs_chan)
    ab = ab.reshape(bsz, 2 * length, D_MODEL)
    return pl.pallas_call(
        _fourier_seq_kernel,
        grid=(bsz, length // tm),
        in_specs=[
            pl.BlockSpec((tm, 2 * length), lambda b, t: (t, 0)),
            pl.BlockSpec((None, 2 * length, D_MODEL), lambda b, t: (b, 0, 0),
                         pipeline_mode=pl.Buffered(1)),
        ],
        out_specs=pl.BlockSpec((None, tm, D_MODEL), lambda b, t: (b, t, 0)),
        out_shape=jax.ShapeDtypeStruct((bsz, length, D_MODEL), BF16),
        compiler_params=_params(48),
        name="fourier_seq",
    )(cs_seq, ab)


def _head_rms(r, g):
    return r * lax.rsqrt(jnp.mean(r * r, axis=-1, keepdims=True) + NORM_EPS) * g


def _store_vt_ext(vt_ref, t):
    tokens = t.shape[0]
    first_row = lax.broadcasted_iota(jnp.int32, (BF16_ROWS, tokens), 0) == 0
    for e in range(N_KV_HEADS):
        vt_ref[e, 0:HEAD_DIM, :] = t[:, e * HEAD_DIM:(e + 1) * HEAD_DIM].T.astype(BF16)
        vt_ref[e, HEAD_DIM:HEAD_DIM + BF16_ROWS, :] = first_row.astype(BF16)


def _qkv_kernel(x_ref, mod_ref, ng_ref, w_ref, qg_ref, kg_ref, mean_ref, cos_ref, sin_ref,
                q_ref, k_ref, vt_ref, h_s):
    _modulated_norm_to(h_s, x_ref, mod_ref, ng_ref)
    cosf = cos_ref[...]
    sinf = sin_ref[...]

    qg = qg_ref[...] * (HEAD_DIM ** -0.5 * LOG2_E)
    gains = [jnp.concatenate([qg, qg], axis=1)] * (N_HEADS // 2)
    gains.append(jnp.concatenate([kg_ref[...], kg_ref[...]], axis=1))
    outs = [(q_ref, 2 * p) for p in range(N_HEADS // 2)] + [(k_ref, 0)]
    n_pairs = len(outs)

    def project(p):
        return _dot(h_s[...], w_ref[:, p * MXU_COLS:(p + 1) * MXU_COLS])

    def normalise(t, p):
        sq = t * t
        hi = sq.astype(BF16)
        lo = (sq - hi.astype(F32)).astype(BF16)
        ms = _dot(hi, mean_ref[...]) + _dot(lo, mean_ref[...])
        return t * lax.rsqrt(ms + NORM_EPS) * gains[p]

    def rope_store(rn, p):
        ref, first = outs[p]
        for e in range(2):
            r = rn[:, e * HEAD_DIM:(e + 1) * HEAD_DIM]
            ref[first + e] = (r * cosf + pltpu.roll(r, HEAD_DIM // 2, 1) * sinf).astype(BF16)

    t_next = project(0)
    rn_prev = None
    for p in range(n_pairs + 1):
        t = t_next
        if p < n_pairs:
            t_next = project(p + 1)
        rn = normalise(t, p) if p < n_pairs else None
        if rn_prev is not None:
            rope_store(rn_prev, p - 1)
        rn_prev = rn
    _store_vt_ext(vt_ref, t)


def _kv_ctx_kernel(x_ref, mod_ref, ng_ref, w_ref, kg_ref, k_ref, vt_ref, h_s):
    _modulated_norm_to(h_s, x_ref, mod_ref, ng_ref)
    t = _dot(h_s[...], w_ref[:, 0:MXU_COLS])
    for e in range(N_KV_HEADS):
        k_ref[e] = _head_rms(t[:, e * HEAD_DIM:(e + 1) * HEAD_DIM], kg_ref[...]).astype(BF16)
    _store_vt_ext(vt_ref, _dot(h_s[...], w_ref[:, MXU_COLS:2 * MXU_COLS]))


def _rope_tables(length):
    t = np.arange(length)
    row = (t // GRID_W).astype(np.float64)
    col = (t % GRID_W).astype(np.float64)
    n_freq = HEAD_DIM // 4
    inv = ROPE_THETA ** (-np.arange(n_freq, dtype=np.float64) / n_freq)
    ang = np.concatenate([row[:, None] * inv, col[:, None] * inv], axis=-1)
    cos, sin = np.cos(ang), np.sin(ang)
    return (jnp.asarray(np.concatenate([cos, cos], axis=-1), F32),
            jnp.asarray(np.concatenate([-sin, sin], axis=-1), F32))


def _qkv_latent(x, mods, ng, w_qkv, q_g, k_g):
    bsz, length, _ = x.shape
    tm = 512
    cosf, sinf = _rope_tables(length)
    n_qkv = w_qkv.shape[-1]
    lane_head = np.arange(MXU_COLS) // HEAD_DIM
    head_mean = _bf16_const((lane_head[:, None] == lane_head[None, :]) * (1.0 / HEAD_DIM))
    kv_shape = jax.ShapeDtypeStruct((bsz, N_KV_HEADS, length, HEAD_DIM), BF16)
    kv_spec = pl.BlockSpec((None, N_KV_HEADS, tm, HEAD_DIM), lambda b, t: (b, 0, t, 0))
    vt_shape = jax.ShapeDtypeStruct((bsz, N_KV_HEADS, HEAD_DIM + BF16_ROWS, length), BF16)
    vt_spec = pl.BlockSpec((None, N_KV_HEADS, HEAD_DIM + BF16_ROWS, tm), lambda b, t: (b, 0, 0, t))
    return pl.pallas_call(
        _qkv_kernel,
        grid=(bsz, length // tm),
        in_specs=[
            pl.BlockSpec((None, tm, D_MODEL), lambda b, t: (b, t, 0)),
            _mod_spec(False, bsz),
            _const_spec((4, D_MODEL)),
            _const_spec((D_MODEL, n_qkv)),
            _const_spec((1, HEAD_DIM)),
            _const_spec((1, HEAD_DIM)),
            _const_spec((MXU_COLS, MXU_COLS)),
            pl.BlockSpec((tm, HEAD_DIM), lambda b, t: (t, 0)),
            pl.BlockSpec((tm, HEAD_DIM), lambda b, t: (t, 0)),
        ],
        out_specs=[
            pl.BlockSpec((None, N_HEADS, tm, HEAD_DIM), lambda b, t: (b, 0, t, 0)),
            kv_spec, vt_spec,
        ],
        out_shape=[
            jax.ShapeDtypeStruct((bsz, N_HEADS, length, HEAD_DIM), BF16),
            kv_shape, vt_shape,
        ],
        scratch_shapes=[pltpu.VMEM((tm, D_MODEL), BF16)],
        compiler_params=_params(32),
        name="qkv_latent",
    )(x, mods, ng, w_qkv, q_g, k_g, head_mean, cosf, sinf)


def _kv_ctx(ctx, mods, ng, w_kv, k_g):
    bsz, length, _ = ctx.shape
    tm = length
    kv_shape = jax.ShapeDtypeStruct((bsz, N_KV_HEADS, length, HEAD_DIM), BF16)
    kv_spec = pl.BlockSpec((None, N_KV_HEADS, tm, HEAD_DIM), lambda b, t: (b, 0, t, 0))
    vt_shape = jax.ShapeDtypeStruct((bsz, N_KV_HEADS, HEAD_DIM + BF16_ROWS, length), BF16)
    vt_spec = pl.BlockSpec((None, N_KV_HEADS, HEAD_DIM + BF16_ROWS, tm), lambda b, t: (b, 0, 0, t))
    return pl.pallas_call(
        _kv_ctx_kernel,
        grid=(bsz, length // tm),
        in_specs=[
            pl.BlockSpec((None, tm, D_MODEL), lambda b, t: (b, t, 0)),
            _mod_spec(True, bsz),
            _const_spec((4, D_MODEL)),
            _const_spec((D_MODEL, 2 * MXU_COLS)),
            _const_spec((1, HEAD_DIM)),
        ],
        out_specs=[kv_spec, vt_spec],
        out_shape=[kv_shape, vt_shape],
        scratch_shapes=[pltpu.VMEM((tm, D_MODEL), BF16)],
        compiler_params=_params(32),
        name="kv_ctx",
    )(ctx, mods, ng, w_kv, k_g)


def _region(index, fn):
    pl.when(pl.program_id(0) >= -index)(fn)


def _attn_kernel(q_ref, k_ref, kc_ref, vt_ref, vtc_ref, o_ref, *scratch):
    tq = q_ref.shape[1]
    n_latent = k_ref.shape[1]
    n_keys = n_latent + kc_ref.shape[1]
    nq = HEADS_PER_KV * tq
    q4_s, s_s, p_s, m_s = (scratch[i * N_KV_HEADS:(i + 1) * N_KV_HEADS] for i in range(4))
    key_chunks = [(k_ref, vt_ref, slice(r, r + MXU_COLS), slice(r, r + MXU_COLS))
                  for r in range(0, n_latent, MXU_COLS)]
    key_chunks += [(kc_ref, vtc_ref, slice(r, r + MXU_COLS), slice(n_latent + r, n_latent + r + MXU_COLS))
                   for r in range(0, n_keys - n_latent, MXU_COLS)]

    @pl.when(pl.program_id(0) == 0)
    def _():
        s_s[1][...] = jnp.zeros(s_s[1].shape, F32)
        m_s[1][...] = jnp.zeros(m_s[1].shape, F32)
        p_s[0][...] = jnp.ones(p_s[0].shape, BF16)

    def stage(j_scores, j_probs, j_pv):
        for h in range(HEADS_PER_KV):
            q4_s[j_scores][h * tq:(h + 1) * tq, :] = q_ref[HEADS_PER_KV * j_scores + h]
        m_probs = m_s[j_probs][...]
        m = None
        acc = None

        def store_head(acc, h):
            cols = slice(h * tq, (h + 1) * tq)
            o = acc[:HEAD_DIM, cols] / acc[HEAD_DIM:HEAD_DIM + 1, cols]
            head = HEADS_PER_KV * j_pv + h
            o_ref[:, head * HEAD_DIM:(head + 1) * HEAD_DIM] = o.T.astype(BF16)

        pv_chunks = list(key_chunks)
        heads_left = list(range(HEADS_PER_KV))
        tiles = [slice(r, r + BF16_ROWS) for r in range(0, n_keys, BF16_ROWS)]
        pv_rows_per_score_row = vt_ref.shape[1] / MXU_COLS
        work_total = len(key_chunks) * (1.0 + pv_rows_per_score_row)
        work_done = 0.0
        tiles_done = 0
        for k_src, _, local, rows in key_chunks:
            s = lax.dot_general(k_src[j_scores, local, :], q4_s[j_scores][...],
                                (((1,), (1,)), ((), ())), preferred_element_type=F32)
            s_s[j_scores][rows, :] = s
            cm = jnp.max(s, axis=0, keepdims=True)
            m = cm if m is None else jnp.maximum(m, cm)
            work_done += 1.0
            if pv_chunks:
                for _, vt_src, pv_local, pv_rows in (
                        pv_chunks.pop(0) for _ in range(min(2, len(pv_chunks)))):
                    d = _dot(vt_src[j_pv, :, pv_local], p_s[j_pv][pv_rows, :])
                    acc = d if acc is None else acc + d
                    work_done += pv_rows_per_score_row
            elif heads_left:
                store_head(acc, heads_left.pop(0))
            tiles_until = round(len(tiles) * work_done / work_total)
            for tile in tiles[tiles_done:tiles_until]:
                p_s[j_probs][tile, :] = jnp.exp2(s_s[j_probs][tile, :] - m_probs).astype(BF16)
            tiles_done = tiles_until
        assert not pv_chunks and tiles_done == len(tiles)
        for h in heads_left:
            store_head(acc, h)
        m_s[j_scores][...] = jnp.broadcast_to(m, (BF16_ROWS, nq))

    _region(0, lambda: stage(0, 1, 0))
    _region(1, lambda: stage(1, 0, 1))


def _attention(q, k, kc, vt, vtc):
    bsz, _, length, _ = q.shape
    n_ctx = kc.shape[2]
    n_keys = length + n_ctx
    vt_rows = vt.shape[2]
    tq = 128
    nq = HEADS_PER_KV * tq
    per_batch = length // tq
    n_tiles = bsz * per_batch

    def cur(i):
        return jnp.minimum(i, n_tiles - 1)

    def prev(i):
        return jnp.maximum(i - 1, 0)

    return pl.pallas_call(
        _attn_kernel,
        grid=(n_tiles + 1,),
        in_specs=[
            pl.BlockSpec((None, N_HEADS, tq, HEAD_DIM),
                         lambda i: (cur(i) // per_batch, 0, cur(i) % per_batch, 0)),
            pl.BlockSpec((None, N_KV_HEADS, length, HEAD_DIM), lambda i: (cur(i) // per_batch, 0, 0, 0)),
            pl.BlockSpec((None, N_KV_HEADS, n_ctx, HEAD_DIM), lambda i: (cur(i) // per_batch, 0, 0, 0)),
            pl.BlockSpec((None, N_KV_HEADS, vt_rows, length), lambda i: (prev(i) // per_batch, 0, 0, 0)),
            pl.BlockSpec((None, N_KV_HEADS, vt_rows, n_ctx), lambda i: (prev(i) // per_batch, 0, 0, 0)),
        ],
        out_specs=pl.BlockSpec((None, tq, D_MODEL),
                               lambda i: (prev(i) // per_batch, prev(i) % per_batch, 0)),
        out_shape=jax.ShapeDtypeStruct((bsz, length, D_MODEL), BF16),
        scratch_shapes=(
            [pltpu.VMEM((nq, HEAD_DIM), BF16)] * N_KV_HEADS
            + [pltpu.VMEM((n_keys, nq), F32)] * N_KV_HEADS
            + [pltpu.VMEM((n_keys, nq), BF16)] * N_KV_HEADS
            + [pltpu.VMEM((BF16_ROWS, nq), F32)] * N_KV_HEADS),
        compiler_params=pltpu.CompilerParams(
            dimension_semantics=("arbitrary",), vmem_limit_bytes=54 * MIB),
        name="attention",
    )(q, k, kc, vt, vtc)


def kernel(x, c, ctx, c_ctx, ada_w, ada_b, norm_g, mlp_w1, mlp_w2, a_w_in, a_ln_g, a_w_s, a_b_s, a_w_out,
           b_w_out, c_w_qkv, c_q_g, c_k_g, c_w_o):
    bsz = x.shape[0]
    attn_layers = [i for i in range(DEPTH) if i % N_MIXERS == 2]
    last_ctx_read = attn_layers[-1] if attn_layers else -1

    cond = jnp.zeros((COND_ROWS, D_MODEL), F32).at[:bsz].set(c).at[bsz].set(c_ctx)
    mods_all = _ada_all(cond, ada_w, ada_b).reshape(DEPTH, COND_ROWS, 6, D_MODEL)

    def big_weights(i):
        pairs = [(mlp_w1, i), (mlp_w2, i)]
        if i % N_MIXERS == 0:
            pairs += [(a_w_in, i // N_MIXERS), (a_w_out, i // N_MIXERS)]
        return pairs

    weights = [w_all[index].astype(BF16) for w_all, index in big_weights(0)]

    for i in range(DEPTH):
        kind, j = i % N_MIXERS, i // N_MIXERS
        ctx_in = i <= last_ctx_read
        ctx_out = i < last_ctx_read
        mods = mods_all[i]
        ng = norm_g[i]
        w1, w2 = weights[:2]

        if kind == 0:
            w_in, w_post = weights[2:]
            ln_g = a_ln_g[j].reshape(1, A_WIDTH)
            w_s = a_w_s[j].astype(BF16)
            b_s = a_b_s[j].reshape(A_GROUPS, CHUNK, 1)
            z = _gmlp_head(x, mods, ng, w_in, ln_g, w_s, b_s, False)
            zc = _gmlp_head(ctx, mods, ng, w_in, ln_g, w_s, b_s, True) if ctx_out else None
        elif kind == 1:
            w_post = b_w_out[j].astype(BF16)
            z = _fourier_head_folded(x, mods, ng)
            zc = _fourier_head(ctx, mods, ng, True) if ctx_out else None
        else:
            w_qkv = c_w_qkv[j].astype(BF16)
            w_post = c_w_o[j].astype(BF16)
            q_g = c_q_g[j].reshape(1, HEAD_DIM)
            k_g = c_k_g[j].reshape(1, HEAD_DIM)
            assert ctx_in and not ctx_out
            q, k, vt = _qkv_latent(x, mods, ng, w_qkv, q_g, k_g)
            kc, vtc = _kv_ctx(ctx, mods, ng, w_qkv[:, N_HEADS * HEAD_DIM:], k_g)
            z = _attention(q, k, kc, vt, vtc)
            zc = None

        cast_next = big_weights(i + 1) if i + 1 < DEPTH else ()
        x, next_weights = _tail(x, z, mods, ng, w_post, w1, w2, False, z_mirrored=(kind == 1),
                                cast_next=cast_next)
        if ctx_out:
            ctx, _ = _tail(ctx, zc, mods, ng, w_post, w1, w2, True)
        weights = next_weights

    return x
```

```python
import functools

import numpy as np
import jax
import jax.numpy as jnp
from jax import lax
from jax.experimental import pallas as pl
from jax.experimental.pallas import tpu as pltpu

D_MODEL = 1024
DEPTH = 4
GRID_W = 64
N_MIXERS = 3
CHUNK = 128
A_WIDTH = 2 * D_MODEL
A_GROUPS = 8
A_GROUP_W = A_WIDTH // A_GROUPS
B_GROUPS = 4
B_GROUP_W = D_MODEL // B_GROUPS
HEAD_DIM = 128
N_HEADS = D_MODEL // HEAD_DIM
N_KV_HEADS = 2
HEADS_PER_KV = N_HEADS // N_KV_HEADS
ROPE_THETA = 10000.0
D_FF = 4 * D_MODEL
NORM_EPS = 1e-6
LN_EPS = 1e-5
LOG2_E = 1.4426950408889634

F32 = jnp.float32
BF16 = jnp.bfloat16

MXU_COLS = 256
ROW_BLOCK = 64
BF16_ROWS = 16
LANES = 128
COND_ROWS = 16
MIB = 1024 * 1024


def _dot(a, b):
    return jnp.dot(a, b, preferred_element_type=F32)


def _rms(x, g):
    ms = jnp.mean(x * x, axis=-1, keepdims=True)
    return x * lax.rsqrt(ms + NORM_EPS) * g


def _gelu_tanh(x):
    c = 2.0 * 0.7978845608028654 * LOG2_E
    z = x * (-c - (c * 0.044715) * (x * x))
    return x / (1.0 + jnp.exp2(z))


def _slab_specs(w, width, layer=None):
    k, n = w.shape[-2:]
    if layer is None:
        return [pl.BlockSpec((k, width), lambda b, t, c=c: (0, c), pipeline_mode=pl.Buffered(1))
                for c in range(n // width)]
    return [pl.BlockSpec((None, k, width), lambda b, t, c=c: (layer, 0, c), pipeline_mode=pl.Buffered(1))
            for c in range(n // width)]


def _const_spec(shape):
    zeros = (0,) * len(shape)
    return pl.BlockSpec(shape, lambda b, t: zeros, pipeline_mode=pl.Buffered(1))


def _mod_spec(is_ctx, batch):
    if is_ctx:
        return pl.BlockSpec((None, 6, D_MODEL), lambda b, t: (batch, 0, 0))
    return pl.BlockSpec((None, 6, D_MODEL), lambda b, t: (b, 0, 0))


def _params(vmem_mib):
    return pltpu.CompilerParams(
        dimension_semantics=("arbitrary", "arbitrary"),
        vmem_limit_bytes=vmem_mib * MIB)


def _modulated_norm_to(h_s, x_ref, mod_ref, ng_ref, start=0, stop=None):
    sh1 = mod_ref[0:1, :]
    sc1p = 1.0 + mod_ref[1:2, :]
    g0 = ng_ref[0:1, :]
    stop = x_ref.shape[0] if stop is None else stop
    for r in range(start, stop, ROW_BLOCK):
        rows = slice(r, r + ROW_BLOCK)
        h_s[rows, :] = (_rms(x_ref[rows, :], g0) * sc1p + sh1).astype(BF16)


def _ada_kernel(cond_ref, w_ref, b_ref, o_ref):
    a = cond_ref[...]
    a = a * jax.nn.sigmoid(a)
    o_ref[...] = _dot(a.astype(BF16), w_ref[...].astype(BF16)) + b_ref[...]


def _ada_all(cond, ada_w, ada_b):
    tn = 1536
    n_out = 6 * D_MODEL
    return pl.pallas_call(
        _ada_kernel,
        grid=(DEPTH, n_out // tn),
        in_specs=[
            pl.BlockSpec((COND_ROWS, D_MODEL), lambda i, n: (0, 0)),
            pl.BlockSpec((None, D_MODEL, tn), lambda i, n: (i, 0, n)),
            pl.BlockSpec((None, 1, tn), lambda i, n: (i, 0, n)),
        ],
        out_specs=pl.BlockSpec((None, COND_ROWS, tn), lambda i, n: (i, 0, n)),
        out_shape=jax.ShapeDtypeStruct((DEPTH, COND_ROWS, n_out), F32),
        compiler_params=_params(32),
        name="ada_mod",
    )(cond, ada_w, ada_b.reshape(DEPTH, 1, n_out))


def _gmlp_head_kernel(x_ref, mod_ref, ng_ref, lng_ref, ws_ref, bs_ref, *refs):
    width = 2 * MXU_COLS
    n_half = A_WIDTH // width
    win = refs[:2 * n_half]
    z_ref, h_s, u_s, v_s, vb_s, sum_s, sq_s = refs[2 * n_half:]
    tm = x_ref.shape[0]
    lng = lng_ref[...]
    half = tm // 2
    lanes = sum_s.shape[1]

    def proj(unit):
        c, a = unit
        return _dot(h_s[a * half:(a + 1) * half, :], win[c][...])

    def gelu_store(d, unit):
        c, a = unit
        is_v = c >= n_half
        dst, c = (v_s, c - n_half) if is_v else (u_s, c)
        for r in range(0, half, ROW_BLOCK):
            rows = slice(a * half + r, a * half + r + ROW_BLOCK)
            g = _gelu_tanh(d[r:r + ROW_BLOCK, :])
            dst[rows, c * width:(c + 1) * width] = g
            if is_v:
                parts = [g[:, i:i + lanes] for i in range(0, width, lanes)]
                p1 = sum(parts[1:], parts[0])
                p2 = sum([p * p for p in parts[1:]], parts[0] * parts[0])
                if c == 0:
                    sum_s[rows, :] = p1
                    sq_s[rows, :] = p2
                else:
                    sum_s[rows, :] += p1
                    sq_s[rows, :] += p2

    def layer_norm_rows(a, part):
        start = a * half + part * half // n_half
        for r in range(start, start + half // n_half, ROW_BLOCK // 2):
            rows = slice(r, r + ROW_BLOCK // 2)
            mu = jnp.sum(sum_s[rows, :], axis=-1, keepdims=True) * (1.0 / A_WIDTH)
            ex2 = jnp.sum(sq_s[rows, :], axis=-1, keepdims=True) * (1.0 / A_WIDTH)
            rstd = lax.rsqrt(jnp.maximum(ex2 - mu * mu, 0.0) + LN_EPS)
            vb_s[rows, :] = ((v_s[rows, :] - mu) * rstd * lng).astype(BF16)

    def gate_group(g):
        w = ws_ref[g]
        b = jnp.broadcast_to(bs_ref[g], (CHUNK, A_GROUP_W))
        cols = slice(g * A_GROUP_W, (g + 1) * A_GROUP_W)
        for n in range(tm // CHUNK):
            rows = slice(n * CHUNK, (n + 1) * CHUNK)
            sv = _dot(w, vb_s[rows, cols]) + b
            z_ref[rows, cols] = (u_s[rows, cols] * sv).astype(BF16)

    chunks = list(range(n_half, 2 * n_half)) + list(range(n_half))
    order = [(c, a) for c in chunks for a in range(2)]
    norms =[functools.partial(layer_norm_rows, a, part) for a in range(2) for part in range(n_half)]
    fillers = {2 * n_half + 1 + k: norms[k:k + 2] for k in range(0, len(norms), 2)}

    _modulated_norm_to(h_s, x_ref, mod_ref, ng_ref, 0, half)
    d = proj(order[0])
    _modulated_norm_to(h_s, x_ref, mod_ref, ng_ref, half, tm)
    for i in range(1, len(order) + 1):
        d_next = proj(order[i]) if i < len(order) else None
        gelu_store(d, order[i - 1])
        for filler in fillers.get(i - 1, []):
            filler()
        d = d_next
    for g in range(A_GROUPS):
        gate_group(g)


def _gmlp_head(x, mods, ng, w_in, ln_g, w_s, b_s, is_ctx):
    bsz, length, _ = x.shape
    tm = min(1024, length)
    w_in_specs = _slab_specs(w_in, 2 * MXU_COLS)
    return pl.pallas_call(
        _gmlp_head_kernel,
        grid=(bsz, length // tm),
        in_specs=[
            pl.BlockSpec((None, tm, D_MODEL), lambda b, t: (b, t, 0)),
            _mod_spec(is_ctx, bsz),
            _const_spec((4, D_MODEL)),
            _const_spec((1, A_WIDTH)),
            _const_spec((A_GROUPS, CHUNK, CHUNK)),
            _const_spec((A_GROUPS, CHUNK, 1)),
        ] + w_in_specs,
        out_specs=pl.BlockSpec((None, tm, A_WIDTH), lambda b, t: (b, t, 0)),
        out_shape=jax.ShapeDtypeStruct((bsz, length, A_WIDTH), BF16),
        scratch_shapes=[
            pltpu.VMEM((tm, D_MODEL), BF16),
            pltpu.VMEM((tm, A_WIDTH), F32),
            pltpu.VMEM((tm, A_WIDTH), F32),
            pltpu.VMEM((tm, A_WIDTH), BF16),
            pltpu.VMEM((tm, LANES), F32),
            pltpu.VMEM((tm, LANES), F32),
        ],
        compiler_params=_params(56),
        name="gmlp_head",
    )(x, mods, ng, ln_g, w_s, b_s, *([w_in] * len(w_in_specs)))


def _interleave(dots, fillers):
    per = -(-len(fillers) // len(dots))
    for i, dot_fn in enumerate(dots):
        dot_fn()
        for filler in fillers[i * per:(i + 1) * per]:
            filler()


def _tail_kernel(x_ref, z_ref, mod_ref, ng_ref, *refs, sub, z_mirrored, n_cast):
    n_out = D_MODEL // MXU_COLS
    n_ff = D_FF // (2 * MXU_COLS)
    n_w = 2 * n_out + n_ff
    wp, w1, w2 = refs[:n_out], refs[n_out:n_out + n_ff], refs[n_out + n_ff:n_w]
    cast_src = refs[n_w:n_w + n_cast]
    o_ref = refs[n_w + n_cast]
    cast_dst = refs[n_w + n_cast + 1:n_w + 2 * n_cast + 1]
    scratch = refs[n_w + 2 * n_cast + 1:]
    for src, dst in zip(cast_src, cast_dst):
        dst[...] = src[...].astype(BF16)
    n_sub = x_ref.shape[0] // sub
    weights = (mod_ref, ng_ref, wp, w1, w2)
    pending = []
    for i in range(n_sub):
        zi = i
        if z_mirrored and n_sub > 1:
            second_half = pl.program_id(1) >= pl.num_programs(1) // 2
            zi = jnp.where(second_half, n_sub - 1 - i, i)
        rows = pl.ds(i * sub, sub)
        z_rows = rows if isinstance(zi, int) else pl.ds(pl.multiple_of(zi * sub, sub), sub)
        pending = _tail_subtile(x_ref.at[rows, :], z_ref.at[z_rows, :], *weights, o_ref.at[rows, :],
                                *scratch, pending=pending, defer_last=i + 1 < n_sub)


def _tail_subtile(x_ref, z_ref, mod_ref, ng_ref, wp, w1, w2, o_ref, *scratch, pending, defer_last):
    half = x_ref.shape[0] // 2
    y_s, hm_s, hid_s = scratch[0:2], scratch[2:4], scratch[4:6]
    g1 = mod_ref[2:3, :]
    sh2 = mod_ref[3:4, :]
    sc2p = 1.0 + mod_ref[4:5, :]
    g2 = mod_ref[5:6, :]
    ff_cols = 2 * MXU_COLS
    out_chunks = list(range(D_MODEL // MXU_COLS))
    ff_chunks = list(range(D_FF // ff_cols))
    blocks = list(range(0, half, ROW_BLOCK))

    def post_dot(a, n):
        y_s[a][:, n * MXU_COLS:(n + 1) * MXU_COLS] = _dot(z_ref[a * half:(a + 1) * half, :], wp[n][...])

    def norm1(a, r):
        rows = slice(a * half + r, a * half + r + ROW_BLOCK)
        x1 = x_ref[rows, :] + g1 * _rms(y_s[a][r:r + ROW_BLOCK, :], ng_ref[1:2, :])
        o_ref[rows, :] = x1
        hm_s[a][r:r + ROW_BLOCK, :] = (_rms(x1, ng_ref[2:3, :]) * sc2p + sh2).astype(BF16)

    def up_dot(a, c):
        t = jnp.maximum(_dot(hm_s[a][...], w1[c][...]), 0.0)
        hid_s[a][:, c * ff_cols:(c + 1) * ff_cols] = (t * t).astype(BF16)

    def down_dot(a, n):
        y_s[a][:, n * MXU_COLS:(n + 1) * MXU_COLS] = _dot(hid_s[a][...], w2[n][...])

    def norm3(a, r):
        rows = slice(a * half + r, a * half + r + ROW_BLOCK)
        o_ref[rows, :] = o_ref[rows, :] + g2 * _rms(y_s[a][r:r + ROW_BLOCK, :], ng_ref[3:4, :])

    def each(fn, a, items):
        return [functools.partial(fn, a, item) for item in items]

    _interleave(each(post_dot, 0, out_chunks), list(pending))
    _interleave(each(post_dot, 1, out_chunks), each(norm1, 0, blocks))
    _interleave(each(up_dot, 0, ff_chunks), each(norm1, 1, blocks))
    _interleave(each(up_dot, 1, ff_chunks), [])
    _interleave(each(down_dot, 0, out_chunks), [])
    _interleave(each(down_dot, 1, out_chunks), each(norm3, 0, blocks))
    last = each(norm3, 1, blocks)
    if defer_last:
        return last
    _interleave(last, [])
    return []


def _tail(x, z, mods, ng, w_post, w1, w2, is_ctx, z_mirrored=False, cast_next=()):
    bsz, length, _ = x.shape
    kz = z.shape[-1]
    sub = min(512, length)
    tm = min(1024, length)
    w_post, post_index = w_post if isinstance(w_post, tuple) else (w_post, None)
    wp_specs = _slab_specs(w_post, MXU_COLS, post_index)
    w1_specs = _slab_specs(w1, 2 * MXU_COLS)
    w2_specs = _slab_specs(w2, MXU_COLS)
    n_steps = bsz * (length // tm)
    per_batch = length // tm
    cast_in_specs, cast_out_specs, cast_out_shapes = [], [], []
    for w_all, index in cast_next:
        k, n = w_all.shape[-2:]
        rows = k // n_steps
        cast_in_specs.append(pl.BlockSpec(
            (None, rows, n), lambda b, t, index=index: (index, b * per_batch + t, 0)))
        cast_out_specs.append(pl.BlockSpec((rows, n), lambda b, t: (b * per_batch + t, 0)))
        cast_out_shapes.append(jax.ShapeDtypeStruct((k, n), BF16))
    if z_mirrored:
        nt = length // 2 // tm
        z_spec = pl.BlockSpec(
            (None, None, tm, kz),
            lambda b, t: (b, t // nt, jnp.where(t < nt, t, 2 * nt - 1 - t), 0))
    else:
        z_spec = pl.BlockSpec((None, tm, kz), lambda b, t: (b, t, 0))
    out, *casts = pl.pallas_call(
        functools.partial(_tail_kernel, sub=sub, z_mirrored=z_mirrored, n_cast=len(cast_next)),
        grid=(bsz, length // tm),
        in_specs=[
            pl.BlockSpec((None, tm, D_MODEL), lambda b, t: (b, t, 0)),
            z_spec,
            _mod_spec(is_ctx, bsz),
            _const_spec((4, D_MODEL)),
        ] + wp_specs + w1_specs + w2_specs + cast_in_specs,
        out_specs=[pl.BlockSpec((None, tm, D_MODEL), lambda b, t: (b, t, 0))] + cast_out_specs,
        out_shape=[jax.ShapeDtypeStruct((bsz, length, D_MODEL), F32)] + cast_out_shapes,
        scratch_shapes=(
            [pltpu.VMEM((sub // 2, D_MODEL), F32)] * 2
            + [pltpu.VMEM((sub // 2, D_MODEL), BF16)] * 2
            + [pltpu.VMEM((sub // 2, D_FF), BF16)] * 2),
        compiler_params=_params(58),
        name="tail",
    )(x, z, mods, ng, *([w_post] * len(wp_specs)), *([w1] * len(w1_specs)),
      *([w2] * len(w2_specs)), *[w_all for w_all, _ in cast_next])
    return out, casts


def _fourier_chan_kernel(x_ref, mod_ref, ng_ref, cs_ref, ab_ref, h_s):
    _modulated_norm_to(h_s, x_ref, mod_ref, ng_ref)
    for g in range(B_GROUPS):
        cols = slice(g * B_GROUP_W, (g + 1) * B_GROUP_W)
        t = _dot(h_s[:, cols], cs_ref[...])
        ab_ref[0, :, cols] = t[:, :B_GROUP_W].astype(BF16)
        ab_ref[1, :, cols] = t[:, B_GROUP_W:].astype(BF16)


def _fourier_seq_kernel(cs_ref, ab_ref, y_ref):
    for n in range(D_MODEL // MXU_COLS):
        cols = slice(n * MXU_COLS, (n + 1) * MXU_COLS)
        y_ref[:, cols] = _dot(cs_ref[...], ab_ref[:, cols]).astype(BF16)


def _bf16_const(a):
    return jnp.asarray(np.asarray(a, np.float32)).astype(BF16)


def _dft_angles(n):
    idx = np.arange(n, dtype=np.int64)
    return (2.0 * np.pi / n) * ((idx[:, None] * idx[None, :]) % n)


def _dft_tables(length):
    ang_c = _dft_angles(B_GROUP_W)
    cs_chan = np.concatenate([np.cos(ang_c), np.sin(ang_c)], axis=1) * B_GROUP_W ** -0.5
    ang_l = _dft_angles(length)
    cs_seq = np.concatenate([np.cos(ang_l), -np.sin(ang_l)], axis=1) * length ** -0.5
    return _bf16_const(cs_chan), _bf16_const(cs_seq)


def _fourier_fold_kernel(xp_ref, xm_ref, x0_ref, mod_ref, ng_ref, cc_ref, sc_ref, rev_ref,
                         ab_ref, a0_ref, hp_s, hm_s):
    tm = xm_ref.shape[0]
    sh1 = mod_ref[0:1, :]
    sc1p = 1.0 + mod_ref[1:2, :]
    g0 = ng_ref[0:1, :]
    for r in range(0, tm, ROW_BLOCK):
        rows = slice(r, r + ROW_BLOCK)
        hp_s[rows, :] = _rms(xp_ref[0, r + 1:r + 1 + ROW_BLOCK, :], g0) * sc1p + sh1
        hm_s[rows, :] = (_rms(xm_ref[rows, :], g0) * sc1p + sh1).astype(BF16)
    h0 = (_rms(x0_ref[...], g0) * sc1p + sh1).astype(BF16)
    for g in range(B_GROUPS):
        cols = slice(g * B_GROUP_W, (g + 1) * B_GROUP_W)
        hm_rev = _dot(rev_ref[...], hm_s[:, cols])
        hp = hp_s[:, cols]
        ab_ref[0, :, cols] = _dot((hp + hm_rev).astype(BF16), cc_ref[...]).astype(BF16)
        ab_ref[1, :, cols] = _dot((hp - hm_rev).astype(BF16), sc_ref[...]).astype(BF16)
        a0_ref[:, cols] = _dot(h0[:, cols], cc_ref[...])


def _fourier_seq_fold_kernel(ct_ref, st_ref, ab_ref, a0_ref, perm_ref, y_ref, w_s, *, scale):
    tk = y_ref.shape[1]
    half = ct_ref.shape[1]
    for n in range(D_MODEL // MXU_COLS):
        cols = slice(n * MXU_COLS, (n + 1) * MXU_COLS)
        yc = _dot(ct_ref[...], ab_ref[0:half, cols]) + a0_ref[0:1, cols] * scale
        ys = _dot(st_ref[...], ab_ref[half:2 * half, cols])
        y_ref[0, :, cols] = (yc[:tk] - ys[:tk]).astype(BF16)
        w_s[:, cols] = (yc + ys).astype(BF16)
    for n in range(D_MODEL // MXU_COLS):
        cols = slice(n * MXU_COLS, (n + 1) * MXU_COLS)
        y_ref[1, :, cols] = _dot(perm_ref[...], w_s[:, cols]).astype(BF16)


def _seq_fold_tables(length, tk):
    half = length // 2
    nt = half // tk
    rows = tk + BF16_ROWS
    t = np.arange(1, half + 1, dtype=np.int64)
    k = np.arange(nt, dtype=np.int64)[:, None] * tk + np.arange(rows, dtype=np.int64)[None, :]
    ang = (2.0 * np.pi / length) * ((k[..., None] * t) % length)
    weight = np.where(t == half, 0.5, 1.0) * length ** -0.5
    return _bf16_const(np.cos(ang) * weight), _bf16_const(np.sin(ang) * weight)


def _fourier_head_folded(x, mods, ng):
    bsz, length, _ = x.shape
    tm = 512
    half = length // 2
    nt = half // tm
    rows = tm + BF16_ROWS
    ang_c = _dft_angles(B_GROUP_W)
    cc = _bf16_const(np.cos(ang_c) * B_GROUP_W ** -0.5)
    sc = _bf16_const(np.sin(ang_c) * B_GROUP_W ** -0.5)
    rev = _bf16_const(np.arange(tm)[:, None] + np.arange(tm)[None, :] == tm - 1)
    ab, a0 = pl.pallas_call(
        _fourier_fold_kernel,
        grid=(bsz, nt),
        in_specs=[
            pl.BlockSpec((pl.Element(1), pl.Element(tm + 8), pl.Element(D_MODEL)),
                         lambda b, t: (b, t * tm, 0)),
            pl.BlockSpec((None, tm, D_MODEL), lambda b, t: (b, 2 * nt - 1 - t, 0)),
            pl.BlockSpec((None, 8, D_MODEL), lambda b, t: (b, 0, 0)),
            _mod_spec(False, bsz),
            _const_spec((4, D_MODEL)),
            _const_spec((B_GROUP_W, B_GROUP_W)),
            _const_spec((B_GROUP_W, B_GROUP_W)),
            _const_spec((tm, tm)),
        ],
        out_specs=[
            pl.BlockSpec((None, 2, tm, D_MODEL), lambda b, t: (b, 0, t, 0)),
            pl.BlockSpec((None, 8, D_MODEL), lambda b, t: (b, 0, 0)),
        ],
        out_shape=[
            jax.ShapeDtypeStruct((bsz, 2, half, D_MODEL), BF16),
            jax.ShapeDtypeStruct((bsz, 8, D_MODEL), F32),
        ],
        scratch_shapes=[pltpu.VMEM((tm, D_MODEL), F32), pltpu.VMEM((tm, D_MODEL), BF16)],
        compiler_params=_params(32),
        name="fourier_fold",
    )(x, x, x, mods, ng, cc, sc, rev)
    ct, st = _seq_fold_tables(length, tm)
    perm = _bf16_const(np.arange(tm)[:, None] + np.arange(rows)[None, :] == tm)
    return pl.pallas_call(
        functools.partial(_fourier_seq_fold_kernel, scale=length ** -0.5),
        grid=(bsz, nt),
        in_specs=[
            pl.BlockSpec((None, rows, half), lambda b, t: (t, 0, 0)),
            pl.BlockSpec((None, rows, half), lambda b, t: (t, 0, 0)),
            pl.BlockSpec((None, 2 * half, D_MODEL), lambda b, t: (b, 0, 0)),
            pl.BlockSpec((None, 8, D_MODEL), lambda b, t: (b, 0, 0)),
            _const_spec((tm, rows)),
        ],
        out_specs=pl.BlockSpec((None, 2, tm, D_MODEL), lambda b, t: (b, 0, t, 0)),
        out_shape=jax.ShapeDtypeStruct((bsz, 2, half, D_MODEL), BF16),
        scratch_shapes=[pltpu.VMEM((rows, D_MODEL), BF16)],
        compiler_params=_params(40),
        name="fourier_seq_fold",
    )(ct, st, ab.reshape(bsz, 2 * half, D_MODEL), a0, perm)


def _fourier_head(x, mods, ng, is_ctx):
    bsz, length, _ = x.shape
    tm = min(512, length)
    cs_chan, cs_seq = _dft_tables(length)
    ab = pl.pallas_call(
        _fourier_chan_kernel,
        grid=(bsz, length // tm),
        in_specs=[
            pl.BlockSpec((None, tm, D_MODEL), lambda b, t: (b, t, 0)),
            _mod_spec(is_ctx, bsz),
            _const_spec((4, D_MODEL)),
            _const_spec((B_GROUP_W, 2 * B_GROUP_W)),
        ],
        out_specs=pl.BlockSpec((None, 2, tm, D_MODEL), lambda b, t: (b, 0, t, 0)),
        out_shape=jax.ShapeDtypeStruct((bsz, 2, length, D_MODEL), BF16),
        scratch_shapes=[pltpu.VMEM((tm, D_MODEL), BF16)],
        compiler_params=_params(32),
        name="fourier_chan",
    )(x, mods, ng, cs_chan)
    ab = ab.reshape(bsz, 2 * length, D_MODEL)
    return pl.pallas_call(
        _fourier_seq_kernel,
        grid=(bsz, length // tm),
        in_specs=[
            pl.BlockSpec((tm, 2 * length), lambda b, t: (t, 0)),
            pl.BlockSpec((None, 2 * length, D_MODEL), lambda b, t: (b, 0, 0),
                         pipeline_mode=pl.Buffered(1)),
        ],
        out_specs=pl.BlockSpec((None, tm, D_MODEL), lambda b, t: (b, t, 0)),
        out_shape=jax.ShapeDtypeStruct((bsz, length, D_MODEL), BF16),
        compiler_params=_params(48),
        name="fourier_seq",
    )(cs_seq, ab)


def _head_rms(r, g):
    return r * lax.rsqrt(jnp.mean(r * r, axis=-1, keepdims=True) + NORM_EPS) * g


def _store_vt_ext(vt_ref, t):
    tokens = t.shape[0]
    first_row = lax.broadcasted_iota(jnp.int32, (BF16_ROWS, tokens), 0) == 0
    for e in range(N_KV_HEADS):
        vt_ref[e, 0:HEAD_DIM, :] = t[:, e * HEAD_DIM:(e + 1) * HEAD_DIM].T.astype(BF16)
        vt_ref[e, HEAD_DIM:HEAD_DIM + BF16_ROWS, :] = first_row.astype(BF16)


def _qkv_kernel(x_ref, mod_ref, ng_ref, w_ref, qg_ref, kg_ref, mean_ref, cos_ref, sin_ref,
                q_ref, k_ref, vt_ref, h_s):
    _modulated_norm_to(h_s, x_ref, mod_ref, ng_ref)
    cosf = cos_ref[...]
    sinf = sin_ref[...]

    qg = qg_ref[...] * (HEAD_DIM ** -0.5 * LOG2_E)
    gains = [jnp.concatenate([qg, qg], axis=1)] * (N_HEADS // 2)
    gains.append(jnp.concatenate([kg_ref[...], kg_ref[...]], axis=1))
    outs = [(q_ref, 2 * p) for p in range(N_HEADS // 2)] + [(k_ref, 0)]
    n_pairs = len(outs)

    def project(p):
        return _dot(h_s[...], w_ref[:, p * MXU_COLS:(p + 1) * MXU_COLS])

    def normalise(t, p):
        sq = t * t
        hi = sq.astype(BF16)
        lo = (sq - hi.astype(F32)).astype(BF16)
        ms = _dot(hi, mean_ref[...]) + _dot(lo, mean_ref[...])
        return t * lax.rsqrt(ms + NORM_EPS) * gains[p]

    def rope_store(rn, p):
        ref, first = outs[p]
        for e in range(2):
            r = rn[:, e * HEAD_DIM:(e + 1) * HEAD_DIM]
            ref[first + e] = (r * cosf + pltpu.roll(r, HEAD_DIM // 2, 1) * sinf).astype(BF16)

    t_next = project(0)
    rn_prev = None
    for p in range(n_pairs + 1):
        t = t_next
        if p < n_pairs:
            t_next = project(p + 1)
        rn = normalise(t, p) if p < n_pairs else None
        if rn_prev is not None:
            rope_store(rn_prev, p - 1)
        rn_prev = rn
    _store_vt_ext(vt_ref, t)


def _kv_ctx_kernel(x_ref, mod_ref, ng_ref, w_ref, kg_ref, k_ref, vt_ref, h_s):
    _modulated_norm_to(h_s, x_ref, mod_ref, ng_ref)
    t = _dot(h_s[...], w_ref[:, 0:MXU_COLS])
    for e in range(N_KV_HEADS):
        k_ref[e] = _head_rms(t[:, e * HEAD_DIM:(e + 1) * HEAD_DIM], kg_ref[...]).astype(BF16)
    _store_vt_ext(vt_ref, _dot(h_s[...], w_ref[:, MXU_COLS:2 * MXU_COLS]))


def _rope_tables(length):
    t = np.arange(length)
    row = (t // GRID_W).astype(np.float64)
    col = (t % GRID_W).astype(np.float64)
    n_freq = HEAD_DIM // 4
    inv = ROPE_THETA ** (-np.arange(n_freq, dtype=np.float64) / n_freq)
    ang = np.concatenate([row[:, None] * inv, col[:, None] * inv], axis=-1)
    cos, sin = np.cos(ang), np.sin(ang)
    return (jnp.asarray(np.concatenate([cos, cos], axis=-1), F32),
            jnp.asarray(np.concatenate([-sin, sin], axis=-1), F32))


def _qkv_latent(x, mods, ng, w_qkv, q_g, k_g):
    bsz, length, _ = x.shape
    tm = 512
    cosf, sinf = _rope_tables(length)
    n_qkv = w_qkv.shape[-1]
    lane_head = np.arange(MXU_COLS) // HEAD_DIM
    head_mean = _bf16_const((lane_head[:, None] == lane_head[None, :]) * (1.0 / HEAD_DIM))
    kv_shape = jax.ShapeDtypeStruct((bsz, N_KV_HEADS, length, HEAD_DIM), BF16)
    kv_spec = pl.BlockSpec((None, N_KV_HEADS, tm, HEAD_DIM), lambda b, t: (b, 0, t, 0))
    vt_shape = jax.ShapeDtypeStruct((bsz, N_KV_HEADS, HEAD_DIM + BF16_ROWS, length), BF16)
    vt_spec = pl.BlockSpec((None, N_KV_HEADS, HEAD_DIM + BF16_ROWS, tm), lambda b, t: (b, 0, 0, t))
    return pl.pallas_call(
        _qkv_kernel,
        grid=(bsz, length // tm),
        in_specs=[
            pl.BlockSpec((None, tm, D_MODEL), lambda b, t: (b, t, 0)),
            _mod_spec(False, bsz),
            _const_spec((4, D_MODEL)),
            _const_spec((D_MODEL, n_qkv)),
            _const_spec((1, HEAD_DIM)),
            _const_spec((1, HEAD_DIM)),
            _const_spec((MXU_COLS, MXU_COLS)),
            pl.BlockSpec((tm, HEAD_DIM), lambda b, t: (t, 0)),
            pl.BlockSpec((tm, HEAD_DIM), lambda b, t: (t, 0)),
        ],
        out_specs=[
            pl.BlockSpec((None, N_HEADS, tm, HEAD_DIM), lambda b, t: (b, 0, t, 0)),
            kv_spec, vt_spec,
        ],
        out_shape=[
            jax.ShapeDtypeStruct((bsz, N_HEADS, length, HEAD_DIM), BF16),
            kv_shape, vt_shape,
        ],
        scratch_shapes=[pltpu.VMEM((tm, D_MODEL), BF16)],
        compiler_params=_params(32),
        name="qkv_latent",
    )(x, mods, ng, w_qkv, q_g, k_g, head_mean, cosf, sinf)


def _kv_ctx(ctx, mods, ng, w_kv, k_g):
    bsz, length, _ = ctx.shape
    tm = length
    kv_shape = jax.ShapeDtypeStruct((bsz, N_KV_HEADS, length, HEAD_DIM), BF16)
    kv_spec = pl.BlockSpec((None, N_KV_HEADS, tm, HEAD_DIM), lambda b, t: (b, 0, t, 0))
    vt_shape = jax.ShapeDtypeStruct((bsz, N_KV_HEADS, HEAD_DIM + BF16_ROWS, length), BF16)
    vt_spec = pl.BlockSpec((None, N_KV_HEADS, HEAD_DIM + BF16_ROWS, tm), lambda b, t: (b, 0, 0, t))
    return pl.pallas_call(
        _kv_ctx_kernel,
        grid=(bsz, length // tm),
        in_specs=[
            pl.BlockSpec((None, tm, D_MODEL), lambda b, t: (b, t, 0)),
            _mod_spec(True, bsz),
            _const_spec((4, D_MODEL)),
            _const_spec((D_MODEL, 2 * MXU_COLS)),
            _const_spec((1, HEAD_DIM)),
        ],
        out_specs=[kv_spec, vt_spec],
        out_shape=[kv_shape, vt_shape],
        scratch_shapes=[pltpu.VMEM((tm, D_MODEL), BF16)],
        compiler_params=_params(32),
        name="kv_ctx",
    )(ctx, mods, ng, w_kv, k_g)


def _region(index, fn):
    pl.when(pl.program_id(0) >= -index)(fn)


def _attn_kernel(qa_ref, qb_ref, ka_ref, kca_ref, kb_ref, kcb_ref, vt_ref, vtc_ref, o_ref, *scratch):
    tq = qa_ref.shape[1] // 2
    n_latent = ka_ref.shape[1]
    n_keys = n_latent + kca_ref.shape[1]
    nq = HEADS_PER_KV * tq
    q4_s, s_s, p_s, m_s = (scratch[i * N_KV_HEADS:(i + 1) * N_KV_HEADS] for i in range(4))

    def chunks_of(k_ref, kc_ref):
        chunks = [(k_ref, vt_ref, slice(r, r + MXU_COLS), slice(r, r + MXU_COLS))
                  for r in range(0, n_latent, MXU_COLS)]
        chunks += [(kc_ref, vtc_ref, slice(r, r + MXU_COLS),
                    slice(n_latent + r, n_latent + r + MXU_COLS))
                   for r in range(0, n_keys - n_latent, MXU_COLS)]
        return chunks

    @pl.when(pl.program_id(0) == 0)
    def _():
        s_s[1][...] = jnp.zeros(s_s[1].shape, F32)
        m_s[1][...] = jnp.zeros(m_s[1].shape, F32)
        p_s[0][...] = jnp.ones(p_s[0].shape, BF16)

    def stage(j_scores, j_probs, j_pv, q_ref, q_rows, key_chunks, out_rows):
        for h in range(HEADS_PER_KV):
            q4_s[j_scores][h * tq:(h + 1) * tq, :] = q_ref[HEADS_PER_KV * j_scores + h, q_rows, :]
        m_probs = m_s[j_probs][...]
        m = None
        acc = None

        def store_head(acc, h):
            cols = slice(h * tq, (h + 1) * tq)
            o = acc[:HEAD_DIM, cols] / acc[HEAD_DIM:HEAD_DIM + 1, cols]
            head = HEADS_PER_KV * j_pv + h
            o_ref[out_rows, head * HEAD_DIM:(head + 1) * HEAD_DIM] = o.T.astype(BF16)

        pv_chunks = list(key_chunks)
        heads_left = list(range(HEADS_PER_KV))
        tiles = [slice(r, r + BF16_ROWS) for r in range(0, n_keys, BF16_ROWS)]
        pv_rows_per_score_row = vt_ref.shape[1] / MXU_COLS
        work_total = len(key_chunks) * (1.0 + pv_rows_per_score_row)
        work_done = 0.0
        tiles_done = 0
        for k_src, _, local, rows in key_chunks:
            s = lax.dot_general(k_src[j_scores, local, :], q4_s[j_scores][...],
                                (((1,), (1,)), ((), ())), preferred_element_type=F32)
            s_s[j_scores][rows, :] = s
            cm = jnp.max(s, axis=0, keepdims=True)
            m = cm if m is None else jnp.maximum(m, cm)
            work_done += 1.0
            if pv_chunks:
                for _, vt_src, pv_local, pv_rows in (
                        pv_chunks.pop(0) for _ in range(min(2, len(pv_chunks)))):
                    d = _dot(vt_src[j_pv, :, pv_local], p_s[j_pv][pv_rows, :])
                    acc = d if acc is None else acc + d
                    work_done += pv_rows_per_score_row
            elif heads_left:
                store_head(acc, heads_left.pop(0))
            tiles_until = round(len(tiles) * work_done / work_total)
            for tile in tiles[tiles_done:tiles_until]:
                p_s[j_probs][tile, :] = jnp.exp2(s_s[j_probs][tile, :] - m_probs).astype(BF16)
            tiles_done = tiles_until
        assert not pv_chunks and tiles_done == len(tiles)
        for h in heads_left:
            store_head(acc, h)
        m_s[j_scores][...] = jnp.broadcast_to(m, (BF16_ROWS, nq))

    first, second = slice(0, tq), slice(tq, 2 * tq)
    chunks_a, chunks_b = chunks_of(ka_ref, kca_ref), chunks_of(kb_ref, kcb_ref)
    _region(0, lambda: stage(0, 1, 0, qa_ref, second, chunks_a, first))
    _region(1, lambda: stage(1, 0, 1, qa_ref, second, chunks_a, first))
    _region(2, lambda: stage(0, 1, 0, qb_ref, first, chunks_b, second))
    _region(3, lambda: stage(1, 0, 1, qb_ref, first, chunks_b, second))


def _attention(q, k, kc, vt, vtc):
    bsz, _, length, _ = q.shape
    n_ctx = kc.shape[2]
    n_keys = length + n_ctx
    vt_rows = vt.shape[2]
    tq = 128
    nq = HEADS_PER_KV * tq
    per_batch = length // (2 * tq)
    n_blocks = bsz * per_batch

    def cur(s):
        return jnp.minimum(s, n_blocks - 1)

    def prev(s):
        return jnp.maximum(s - 1, 0)

    def q_spec(block):
        return pl.BlockSpec((None, N_HEADS, 2 * tq, HEAD_DIM),
                            lambda s: (block(s) // per_batch, 0, block(s) % per_batch, 0))

    def kv_spec(rows, cols, block):
        return pl.BlockSpec((None, N_KV_HEADS, rows, cols), lambda s: (block(s) // per_batch, 0, 0, 0))

    return pl.pallas_call(
        _attn_kernel,
        grid=(n_blocks + 1,),
        in_specs=[
            q_spec(prev), q_spec(cur),
            kv_spec(length, HEAD_DIM, prev), kv_spec(n_ctx, HEAD_DIM, prev),
            kv_spec(length, HEAD_DIM, cur), kv_spec(n_ctx, HEAD_DIM, cur),
            kv_spec(vt_rows, length, prev), kv_spec(vt_rows, n_ctx, prev),
        ],
        out_specs=pl.BlockSpec((None, 2 * tq, D_MODEL),
                               lambda s: (prev(s) // per_batch, prev(s) % per_batch, 0)),
        out_shape=jax.ShapeDtypeStruct((bsz, length, D_MODEL), BF16),
        scratch_shapes=(
            [pltpu.VMEM((nq, HEAD_DIM), BF16)] * N_KV_HEADS
            + [pltpu.VMEM((n_keys, nq), F32)] * N_KV_HEADS
            + [pltpu.VMEM((n_keys, nq), BF16)] * N_KV_HEADS
            + [pltpu.VMEM((BF16_ROWS, nq), F32)] * N_KV_HEADS),
        compiler_params=pltpu.CompilerParams(
            dimension_semantics=("arbitrary",), vmem_limit_bytes=54 * MIB),
        name="attention",
    )(q, q, k, kc, k, kc, vt, vtc)


def kernel(x, c, ctx, c_ctx, ada_w, ada_b, norm_g, mlp_w1, mlp_w2, a_w_in, a_ln_g, a_w_s, a_b_s, a_w_out,
           b_w_out, c_w_qkv, c_q_g, c_k_g, c_w_o):
    bsz = x.shape[0]
    attn_layers = [i for i in range(DEPTH) if i % N_MIXERS == 2]
    last_ctx_read = attn_layers[-1] if attn_layers else -1

    cond = jnp.zeros((COND_ROWS, D_MODEL), F32).at[:bsz].set(c).at[bsz].set(c_ctx)
    mods_all = _ada_all(cond, ada_w, ada_b).reshape(DEPTH, COND_ROWS, 6, D_MODEL)

    def big_weights(i):
        pairs = [(mlp_w1, i), (mlp_w2, i)]
        if i % N_MIXERS == 0:
            pairs += [(a_w_in, i // N_MIXERS), (a_w_out, i // N_MIXERS)]
        return pairs

    weights = [w_all[index].astype(BF16) for w_all, index in big_weights(0)]

    for i in range(DEPTH):
        kind, j = i % N_MIXERS, i // N_MIXERS
        ctx_in = i <= last_ctx_read
        ctx_out = i < last_ctx_read
        mods = mods_all[i]
        ng = norm_g[i]
        w1, w2 = weights[:2]

        if kind == 0:
            w_in, w_post = weights[2:]
            ln_g = a_ln_g[j].reshape(1, A_WIDTH)
            w_s = a_w_s[j].astype(BF16)
            b_s = a_b_s[j].reshape(A_GROUPS, CHUNK, 1)
            z = _gmlp_head(x, mods, ng, w_in, ln_g, w_s, b_s, False)
            zc = _gmlp_head(ctx, mods, ng, w_in, ln_g, w_s, b_s, True) if ctx_out else None
        elif kind == 1:
            w_post = b_w_out[j].astype(BF16)
            z = _fourier_head_folded(x, mods, ng)
            zc = _fourier_head(ctx, mods, ng, True) if ctx_out else None
        else:
            w_qkv = c_w_qkv[j].astype(BF16)
            w_post = c_w_o[j].astype(BF16)
            q_g = c_q_g[j].reshape(1, HEAD_DIM)
            k_g = c_k_g[j].reshape(1, HEAD_DIM)
            assert ctx_in and not ctx_out
            q, k, vt = _qkv_latent(x, mods, ng, w_qkv, q_g, k_g)
            kc, vtc = _kv_ctx(ctx, mods, ng, w_qkv[:, N_HEADS * HEAD_DIM:], k_g)
            z = _attention(q, k, kc, vt, vtc)
            zc = None

        cast_next = big_weights(i + 1) if i + 1 < DEPTH else ()
        x, next_weights = _tail(x, z, mods, ng, w_post, w1, w2, False, z_mirrored=(kind == 1),
                                cast_next=cast_next)
        if ctx_out:
            ctx, _ = _tail(ctx, zc, mods, ng, w_post, w1, w2, True)
        weights = next_weights

    return x
```

```python
import functools

import numpy as np
import jax
import jax.numpy as jnp
from jax import lax
from jax.experimental import pallas as pl
from jax.experimental.pallas import tpu as pltpu

D_MODEL = 1024
DEPTH = 4
GRID_W = 64
N_MIXERS = 3
CHUNK = 128
A_WIDTH = 2 * D_MODEL
A_GROUPS = 8
A_GROUP_W = A_WIDTH // A_GROUPS
B_GROUPS = 4
B_GROUP_W = D_MODEL // B_GROUPS
HEAD_DIM = 128
N_HEADS = D_MODEL // HEAD_DIM
N_KV_HEADS = 2
HEADS_PER_KV = N_HEADS // N_KV_HEADS
ROPE_THETA = 10000.0
D_FF = 4 * D_MODEL
NORM_EPS = 1e-6
LN_EPS = 1e-5
LOG2_E = 1.4426950408889634

F32 = jnp.float32
BF16 = jnp.bfloat16

MXU_COLS = 256
ROW_BLOCK = 64
BF16_ROWS = 16
LANES = 128
SUBLANES = 8
COND_ROWS = 16
TOKEN_TILE = 512
TOKEN_BLOCK = 1024
QUERY_TILE = 128
ADA_COLS = 1536
MIB = 1024 * 1024
VMEM_MIB = {
    "ada_mod": 32, "gmlp_head": 56, "tail": 58, "fourier_fold": 32, "fourier_seq_fold": 40,
    "fourier_chan": 32, "fourier_seq": 48, "qkv_latent": 32, "kv_ctx": 32, "attention": 54,
}


def _dot(a, b):
    return jnp.dot(a, b, preferred_element_type=F32)


def _rms(x, g):
    ms = jnp.mean(x * x, axis=-1, keepdims=True)
    return x * lax.rsqrt(ms + NORM_EPS) * g


def _gelu_tanh(x):
    c = 2.0 * 0.7978845608028654 * LOG2_E
    z = x * (-c - (c * 0.044715) * (x * x))
    return x / (1.0 + jnp.exp2(z))


def _slab_specs(w, width, layer=None):
    k, n = w.shape[-2:]
    if layer is None:
        return [pl.BlockSpec((k, width), lambda b, t, c=c: (0, c), pipeline_mode=pl.Buffered(1))
                for c in range(n // width)]
    return [pl.BlockSpec((None, k, width), lambda b, t, c=c: (layer, 0, c), pipeline_mode=pl.Buffered(1))
            for c in range(n // width)]


def _const_spec(shape):
    zeros = (0,) * len(shape)
    return pl.BlockSpec(shape, lambda b, t: zeros, pipeline_mode=pl.Buffered(1))


def _mod_spec(is_ctx, batch):
    if is_ctx:
        return pl.BlockSpec((None, 6, D_MODEL), lambda b, t: (batch, 0, 0))
    return pl.BlockSpec((None, 6, D_MODEL), lambda b, t: (b, 0, 0))


def _call_options(name, grid_rank=2):
    return dict(
        name=name,
        compiler_params=pltpu.CompilerParams(
            dimension_semantics=("arbitrary",) * grid_rank,
            vmem_limit_bytes=VMEM_MIB[name] * MIB))


def _modulated_norm_to(h_s, x_ref, mod_ref, ng_ref, start=0, stop=None):
    sh1 = mod_ref[0:1, :]
    sc1p = 1.0 + mod_ref[1:2, :]
    g0 = ng_ref[0:1, :]
    stop = x_ref.shape[0] if stop is None else stop
    for r in range(start, stop, ROW_BLOCK):
        rows = slice(r, r + ROW_BLOCK)
        h_s[rows, :] = (_rms(x_ref[rows, :], g0) * sc1p + sh1).astype(BF16)


def _ada_kernel(cond_ref, w_ref, b_ref, o_ref):
    a = cond_ref[...]
    a = a * jax.nn.sigmoid(a)
    o_ref[...] = _dot(a.astype(BF16), w_ref[...].astype(BF16)) + b_ref[...]


def _ada_all(cond, ada_w, ada_b):
    tn = ADA_COLS
    n_out = 6 * D_MODEL
    return pl.pallas_call(
        _ada_kernel,
        grid=(DEPTH, n_out // tn),
        in_specs=[
            pl.BlockSpec((COND_ROWS, D_MODEL), lambda i, n: (0, 0)),
            pl.BlockSpec((None, D_MODEL, tn), lambda i, n: (i, 0, n)),
            pl.BlockSpec((None, 1, tn), lambda i, n: (i, 0, n)),
        ],
        out_specs=pl.BlockSpec((None, COND_ROWS, tn), lambda i, n: (i, 0, n)),
        out_shape=jax.ShapeDtypeStruct((DEPTH, COND_ROWS, n_out), F32),
        **_call_options("ada_mod"),
    )(cond, ada_w, ada_b.reshape(DEPTH, 1, n_out))


def _gmlp_head_kernel(x_ref, mod_ref, ng_ref, lng_ref, ws_ref, bs_ref, *refs):
    width = 2 * MXU_COLS
    n_half = A_WIDTH // width
    win = refs[:2 * n_half]
    z_ref, h_s, u_s, v_s, vb_s, sum_s, sq_s = refs[2 * n_half:]
    tm = x_ref.shape[0]
    lng = lng_ref[...]
    half = tm // 2
    lanes = sum_s.shape[1]

    def proj(unit):
        c, a = unit
        return _dot(h_s[a * half:(a + 1) * half, :], win[c][...])

    def gelu_store(d, unit):
        c, a = unit
        is_v = c >= n_half
        dst, c = (v_s, c - n_half) if is_v else (u_s, c)
        for r in range(0, half, ROW_BLOCK):
            rows = slice(a * half + r, a * half + r + ROW_BLOCK)
            g = _gelu_tanh(d[r:r + ROW_BLOCK, :])
            dst[rows, c * width:(c + 1) * width] = g
            if is_v:
                parts = [g[:, i:i + lanes] for i in range(0, width, lanes)]
                p1 = sum(parts[1:], parts[0])
                p2 = sum([p * p for p in parts[1:]], parts[0] * parts[0])
                if c == 0:
                    sum_s[rows, :] = p1
                    sq_s[rows, :] = p2
                else:
                    sum_s[rows, :] += p1
                    sq_s[rows, :] += p2

    def layer_norm_rows(a, part):
        start = a * half + part * half // n_half
        for r in range(start, start + half // n_half, ROW_BLOCK // 2):
            rows = slice(r, r + ROW_BLOCK // 2)
            mu = jnp.sum(sum_s[rows, :], axis=-1, keepdims=True) * (1.0 / A_WIDTH)
            ex2 = jnp.sum(sq_s[rows, :], axis=-1, keepdims=True) * (1.0 / A_WIDTH)
            rstd = lax.rsqrt(jnp.maximum(ex2 - mu * mu, 0.0) + LN_EPS)
            vb_s[rows, :] = ((v_s[rows, :] - mu) * rstd * lng).astype(BF16)

    def gate_group(g):
        w = ws_ref[g]
        b = jnp.broadcast_to(bs_ref[g], (CHUNK, A_GROUP_W))
        cols = slice(g * A_GROUP_W, (g + 1) * A_GROUP_W)
        for n in range(tm // CHUNK):
            rows = slice(n * CHUNK, (n + 1) * CHUNK)
            sv = _dot(w, vb_s[rows, cols]) + b
            z_ref[rows, cols] = (u_s[rows, cols] * sv).astype(BF16)

    chunks = list(range(n_half, 2 * n_half)) + list(range(n_half))
    order = [(c, a) for c in chunks for a in range(2)]
    norms =[functools.partial(layer_norm_rows, a, part) for a in range(2) for part in range(n_half)]
    fillers = {2 * n_half + 1 + k: norms[k:k + 2] for k in range(0, len(norms), 2)}

    _modulated_norm_to(h_s, x_ref, mod_ref, ng_ref, 0, half)
    d = proj(order[0])
    _modulated_norm_to(h_s, x_ref, mod_ref, ng_ref, half, tm)
    for i in range(1, len(order) + 1):
        d_next = proj(order[i]) if i < len(order) else None
        gelu_store(d, order[i - 1])
        for filler in fillers.get(i - 1, []):
            filler()
        d = d_next
    for g in range(A_GROUPS):
        gate_group(g)


def _gmlp_head(x, mods, ng, w_in, ln_g, w_s, b_s, is_ctx):
    bsz, length, _ = x.shape
    tm = min(TOKEN_BLOCK, length)
    w_in_specs = _slab_specs(w_in, 2 * MXU_COLS)
    return pl.pallas_call(
        _gmlp_head_kernel,
        grid=(bsz, length // tm),
        in_specs=[
            pl.BlockSpec((None, tm, D_MODEL), lambda b, t: (b, t, 0)),
            _mod_spec(is_ctx, bsz),
            _const_spec((4, D_MODEL)),
            _const_spec((1, A_WIDTH)),
            _const_spec((A_GROUPS, CHUNK, CHUNK)),
            _const_spec((A_GROUPS, CHUNK, 1)),
        ] + w_in_specs,
        out_specs=pl.BlockSpec((None, tm, A_WIDTH), lambda b, t: (b, t, 0)),
        out_shape=jax.ShapeDtypeStruct((bsz, length, A_WIDTH), BF16),
        scratch_shapes=[
            pltpu.VMEM((tm, D_MODEL), BF16),
            pltpu.VMEM((tm, A_WIDTH), F32),
            pltpu.VMEM((tm, A_WIDTH), F32),
            pltpu.VMEM((tm, A_WIDTH), BF16),
            pltpu.VMEM((tm, LANES), F32),
            pltpu.VMEM((tm, LANES), F32),
        ],
        **_call_options("gmlp_head"),
    )(x, mods, ng, ln_g, w_s, b_s, *([w_in] * len(w_in_specs)))


def _interleave(dots, fillers):
    per = -(-len(fillers) // len(dots))
    for i, dot_fn in enumerate(dots):
        dot_fn()
        for filler in fillers[i * per:(i + 1) * per]:
            filler()


def _tail_kernel(x_ref, z_ref, mod_ref, ng_ref, *refs, sub, z_mirrored, n_cast):
    n_out = D_MODEL // MXU_COLS
    n_ff = D_FF // (2 * MXU_COLS)
    n_w = 2 * n_out + n_ff
    wp, w1, w2 = refs[:n_out], refs[n_out:n_out + n_ff], refs[n_out + n_ff:n_w]
    cast_src = refs[n_w:n_w + n_cast]
    o_ref = refs[n_w + n_cast]
    cast_dst = refs[n_w + n_cast + 1:n_w + 2 * n_cast + 1]
    scratch = refs[n_w + 2 * n_cast + 1:]
    for src, dst in zip(cast_src, cast_dst):
        dst[...] = src[...].astype(BF16)
    n_sub = x_ref.shape[0] // sub
    weights = (mod_ref, ng_ref, wp, w1, w2)
    pending = []
    for i in range(n_sub):
        zi = i
        if z_mirrored and n_sub > 1:
            second_half = pl.program_id(1) >= pl.num_programs(1) // 2
            zi = jnp.where(second_half, n_sub - 1 - i, i)
        rows = pl.ds(i * sub, sub)
        z_rows = rows if isinstance(zi, int) else pl.ds(pl.multiple_of(zi * sub, sub), sub)
        pending = _tail_subtile(x_ref.at[rows, :], z_ref.at[z_rows, :], *weights, o_ref.at[rows, :],
                                *scratch, pending=pending, defer_last=i + 1 < n_sub)


def _tail_subtile(x_ref, z_ref, mod_ref, ng_ref, wp, w1, w2, o_ref, *scratch, pending, defer_last):
    half = x_ref.shape[0] // 2
    y_s, hm_s, hid_s = scratch[0:2], scratch[2:4], scratch[4:6]
    g1 = mod_ref[2:3, :]
    sh2 = mod_ref[3:4, :]
    sc2p = 1.0 + mod_ref[4:5, :]
    g2 = mod_ref[5:6, :]
    ff_cols = 2 * MXU_COLS
    out_chunks = list(range(D_MODEL // MXU_COLS))
    ff_chunks = list(range(D_FF // ff_cols))
    blocks = list(range(0, half, ROW_BLOCK))

    def post_dot(a, n):
        y_s[a][:, n * MXU_COLS:(n + 1) * MXU_COLS] = _dot(z_ref[a * half:(a + 1) * half, :], wp[n][...])

    def norm1(a, r):
        rows = slice(a * half + r, a * half + r + ROW_BLOCK)
        x1 = x_ref[rows, :] + g1 * _rms(y_s[a][r:r + ROW_BLOCK, :], ng_ref[1:2, :])
        o_ref[rows, :] = x1
        hm_s[a][r:r + ROW_BLOCK, :] = (_rms(x1, ng_ref[2:3, :]) * sc2p + sh2).astype(BF16)

    def up_dot(a, c):
        t = jnp.maximum(_dot(hm_s[a][...], w1[c][...]), 0.0)
        hid_s[a][:, c * ff_cols:(c + 1) * ff_cols] = (t * t).astype(BF16)

    def down_dot(a, n):
        y_s[a][:, n * MXU_COLS:(n + 1) * MXU_COLS] = _dot(hid_s[a][...], w2[n][...])

    def norm3(a, r):
        rows = slice(a * half + r, a * half + r + ROW_BLOCK)
        o_ref[rows, :] = o_ref[rows, :] + g2 * _rms(y_s[a][r:r + ROW_BLOCK, :], ng_ref[3:4, :])

    def each(fn, a, items):
        return [functools.partial(fn, a, item) for item in items]

    _interleave(each(post_dot, 0, out_chunks), list(pending))
    _interleave(each(post_dot, 1, out_chunks), each(norm1, 0, blocks))
    _interleave(each(up_dot, 0, ff_chunks), each(norm1, 1, blocks))
    _interleave(each(up_dot, 1, ff_chunks), [])
    _interleave(each(down_dot, 0, out_chunks), [])
    _interleave(each(down_dot, 1, out_chunks), each(norm3, 0, blocks))
    last = each(norm3, 1, blocks)
    if defer_last:
        return last
    _interleave(last, [])
    return []


def _tail(x, z, mods, ng, w_post, w1, w2, is_ctx, z_mirrored=False, cast_next=()):
    bsz, length, _ = x.shape
    kz = z.shape[-1]
    sub = min(TOKEN_TILE, length)
    tm = min(TOKEN_BLOCK, length)
    w_post, post_index = w_post if isinstance(w_post, tuple) else (w_post, None)
    wp_specs = _slab_specs(w_post, MXU_COLS, post_index)
    w1_specs = _slab_specs(w1, 2 * MXU_COLS)
    w2_specs = _slab_specs(w2, MXU_COLS)
    n_steps = bsz * (length // tm)
    per_batch = length // tm
    cast_in_specs, cast_out_specs, cast_out_shapes = [], [], []
    for w_all, index in cast_next:
        k, n = w_all.shape[-2:]
        rows = k // n_steps
        cast_in_specs.append(pl.BlockSpec(
            (None, rows, n), lambda b, t, index=index: (index, b * per_batch + t, 0)))
        cast_out_specs.append(pl.BlockSpec((rows, n), lambda b, t: (b * per_batch + t, 0)))
        cast_out_shapes.append(jax.ShapeDtypeStruct((k, n), BF16))
    if z_mirrored:
        nt = length // 2 // tm
        z_spec = pl.BlockSpec(
            (None, None, tm, kz),
            lambda b, t: (b, t // nt, jnp.where(t < nt, t, 2 * nt - 1 - t), 0))
    else:
        z_spec = pl.BlockSpec((None, tm, kz), lambda b, t: (b, t, 0))
    out, *casts = pl.pallas_call(
        functools.partial(_tail_kernel, sub=sub, z_mirrored=z_mirrored, n_cast=len(cast_next)),
        grid=(bsz, length // tm),
        in_specs=[
            pl.BlockSpec((None, tm, D_MODEL), lambda b, t: (b, t, 0)),
            z_spec,
            _mod_spec(is_ctx, bsz),
            _const_spec((4, D_MODEL)),
        ] + wp_specs + w1_specs + w2_specs + cast_in_specs,
        out_specs=[pl.BlockSpec((None, tm, D_MODEL), lambda b, t: (b, t, 0))] + cast_out_specs,
        out_shape=[jax.ShapeDtypeStruct((bsz, length, D_MODEL), F32)] + cast_out_shapes,
        scratch_shapes=(
            [pltpu.VMEM((sub // 2, D_MODEL), F32)] * 2
            + [pltpu.VMEM((sub // 2, D_MODEL), BF16)] * 2
            + [pltpu.VMEM((sub // 2, D_FF), BF16)] * 2),
        **_call_options("tail"),
    )(x, z, mods, ng, *([w_post] * len(wp_specs)), *([w1] * len(w1_specs)),
      *([w2] * len(w2_specs)), *[w_all for w_all, _ in cast_next])
    return out, casts


def _fourier_chan_kernel(x_ref, mod_ref, ng_ref, cs_ref, ab_ref, h_s):
    _modulated_norm_to(h_s, x_ref, mod_ref, ng_ref)
    for g in range(B_GROUPS):
        cols = slice(g * B_GROUP_W, (g + 1) * B_GROUP_W)
        t = _dot(h_s[:, cols], cs_ref[...])
        ab_ref[0, :, cols] = t[:, :B_GROUP_W].astype(BF16)
        ab_ref[1, :, cols] = t[:, B_GROUP_W:].astype(BF16)


def _fourier_seq_kernel(cs_ref, ab_ref, y_ref):
    for n in range(D_MODEL // MXU_COLS):
        cols = slice(n * MXU_COLS, (n + 1) * MXU_COLS)
        y_ref[:, cols] = _dot(cs_ref[...], ab_ref[:, cols]).astype(BF16)


def _bf16_const(a):
    return jnp.asarray(np.asarray(a, np.float32)).astype(BF16)


def _dft_angles(n):
    idx = np.arange(n, dtype=np.int64)
    return (2.0 * np.pi / n) * ((idx[:, None] * idx[None, :]) % n)


def _dft_tables(length):
    ang_c = _dft_angles(B_GROUP_W)
    cs_chan = np.concatenate([np.cos(ang_c), np.sin(ang_c)], axis=1) * B_GROUP_W ** -0.5
    ang_l = _dft_angles(length)
    cs_seq = np.concatenate([np.cos(ang_l), -np.sin(ang_l)], axis=1) * length ** -0.5
    return _bf16_const(cs_chan), _bf16_const(cs_seq)


def _fourier_fold_kernel(xp_ref, xm_ref, x0_ref, mod_ref, ng_ref, cc_ref, sc_ref, rev_ref,
                         ab_ref, a0_ref, hp_s, hm_s):
    tm = xm_ref.shape[0]
    sh1 = mod_ref[0:1, :]
    sc1p = 1.0 + mod_ref[1:2, :]
    g0 = ng_ref[0:1, :]
    for r in range(0, tm, ROW_BLOCK):
        rows = slice(r, r + ROW_BLOCK)
        hp_s[rows, :] = _rms(xp_ref[0, r + 1:r + 1 + ROW_BLOCK, :], g0) * sc1p + sh1
        hm_s[rows, :] = (_rms(xm_ref[rows, :], g0) * sc1p + sh1).astype(BF16)
    h0 = (_rms(x0_ref[...], g0) * sc1p + sh1).astype(BF16)
    for g in range(B_GROUPS):
        cols = slice(g * B_GROUP_W, (g + 1) * B_GROUP_W)
        hm_rev = _dot(rev_ref[...], hm_s[:, cols])
        hp = hp_s[:, cols]
        ab_ref[0, :, cols] = _dot((hp + hm_rev).astype(BF16), cc_ref[...]).astype(BF16)
        ab_ref[1, :, cols] = _dot((hp - hm_rev).astype(BF16), sc_ref[...]).astype(BF16)
        a0_ref[:, cols] = _dot(h0[:, cols], cc_ref[...])


def _fourier_seq_fold_kernel(ct_ref, st_ref, ab_ref, a0_ref, perm_ref, y_ref, w_s, *, scale):
    tk = y_ref.shape[1]
    half = ct_ref.shape[1]
    for n in range(D_MODEL // MXU_COLS):
        cols = slice(n * MXU_COLS, (n + 1) * MXU_COLS)
        yc = _dot(ct_ref[...], ab_ref[0:half, cols]) + a0_ref[0:1, cols] * scale
        ys = _dot(st_ref[...], ab_ref[half:2 * half, cols])
        y_ref[0, :, cols] = (yc[:tk] - ys[:tk]).astype(BF16)
        w_s[:, cols] = (yc + ys).astype(BF16)
    for n in range(D_MODEL // MXU_COLS):
        cols = slice(n * MXU_COLS, (n + 1) * MXU_COLS)
        y_ref[1, :, cols] = _dot(perm_ref[...], w_s[:, cols]).astype(BF16)


def _seq_fold_tables(length, tk):
    half = length // 2
    nt = half // tk
    rows = tk + BF16_ROWS
    t = np.arange(1, half + 1, dtype=np.int64)
    k = np.arange(nt, dtype=np.int64)[:, None] * tk + np.arange(rows, dtype=np.int64)[None, :]
    ang = (2.0 * np.pi / length) * ((k[..., None] * t) % length)
    weight = np.where(t == half, 0.5, 1.0) * length ** -0.5
    return _bf16_const(np.cos(ang) * weight), _bf16_const(np.sin(ang) * weight)


def _fourier_head_folded(x, mods, ng):
    bsz, length, _ = x.shape
    tm = TOKEN_TILE
    half = length // 2
    nt = half // tm
    rows = tm + BF16_ROWS
    ang_c = _dft_angles(B_GROUP_W)
    cc = _bf16_const(np.cos(ang_c) * B_GROUP_W ** -0.5)
    sc = _bf16_const(np.sin(ang_c) * B_GROUP_W ** -0.5)
    rev = _bf16_const(np.arange(tm)[:, None] + np.arange(tm)[None, :] == tm - 1)
    ab, a0 = pl.pallas_call(
        _fourier_fold_kernel,
        grid=(bsz, nt),
        in_specs=[
            pl.BlockSpec((pl.Element(1), pl.Element(tm + SUBLANES), pl.Element(D_MODEL)),
                         lambda b, t: (b, t * tm, 0)),
            pl.BlockSpec((None, tm, D_MODEL), lambda b, t: (b, 2 * nt - 1 - t, 0)),
            pl.BlockSpec((None, SUBLANES, D_MODEL), lambda b, t: (b, 0, 0)),
            _mod_spec(False, bsz),
            _const_spec((4, D_MODEL)),
            _const_spec((B_GROUP_W, B_GROUP_W)),
            _const_spec((B_GROUP_W, B_GROUP_W)),
            _const_spec((tm, tm)),
        ],
        out_specs=[
            pl.BlockSpec((None, 2, tm, D_MODEL), lambda b, t: (b, 0, t, 0)),
            pl.BlockSpec((None, SUBLANES, D_MODEL), lambda b, t: (b, 0, 0)),
        ],
        out_shape=[
            jax.ShapeDtypeStruct((bsz, 2, half, D_MODEL), BF16),
            jax.ShapeDtypeStruct((bsz, SUBLANES, D_MODEL), F32),
        ],
        scratch_shapes=[pltpu.VMEM((tm, D_MODEL), F32), pltpu.VMEM((tm, D_MODEL), BF16)],
        **_call_options("fourier_fold"),
    )(x, x, x, mods, ng, cc, sc, rev)
    ct, st = _seq_fold_tables(length, tm)
    perm = _bf16_const(np.arange(tm)[:, None] + np.arange(rows)[None, :] == tm)
    return pl.pallas_call(
        functools.partial(_fourier_seq_fold_kernel, scale=length ** -0.5),
        grid=(bsz, nt),
        in_specs=[
            pl.BlockSpec((None, rows, half), lambda b, t: (t, 0, 0)),
            pl.BlockSpec((None, rows, half), lambda b, t: (t, 0, 0)),
            pl.BlockSpec((None, 2 * half, D_MODEL), lambda b, t: (b, 0, 0)),
            pl.BlockSpec((None, SUBLANES, D_MODEL), lambda b, t: (b, 0, 0)),
            _const_spec((tm, rows)),
        ],
        out_specs=pl.BlockSpec((None, 2, tm, D_MODEL), lambda b, t: (b, 0, t, 0)),
        out_shape=jax.ShapeDtypeStruct((bsz, 2, half, D_MODEL), BF16),
        scratch_shapes=[pltpu.VMEM((rows, D_MODEL), BF16)],
        **_call_options("fourier_seq_fold"),
    )(ct, st, ab.reshape(bsz, 2 * half, D_MODEL), a0, perm)


def _fourier_head(x, mods, ng, is_ctx):
    bsz, length, _ = x.shape
    tm = min(TOKEN_TILE, length)
    cs_chan, cs_seq = _dft_tables(length)
    ab = pl.pallas_call(
        _fourier_chan_kernel,
        grid=(bsz, length // tm),
        in_specs=[
            pl.BlockSpec((None, tm, D_MODEL), lambda b, t: (b, t, 0)),
            _mod_spec(is_ctx, bsz),
            _const_spec((4, D_MODEL)),
            _const_spec((B_GROUP_W, 2 * B_GROUP_W)),
        ],
        out_specs=pl.BlockSpec((None, 2, tm, D_MODEL), lambda b, t: (b, 0, t, 0)),
        out_shape=jax.ShapeDtypeStruct((bsz, 2, length, D_MODEL), BF16),
        scratch_shapes=[pltpu.VMEM((tm, D_MODEL), BF16)],
        **_call_options("fourier_chan"),
    )(x, mods, ng, cs_chan)
    ab = ab.reshape(bsz, 2 * length, D_MODEL)
    return pl.pallas_call(
        _fourier_seq_kernel,
        grid=(bsz, length // tm),
        in_specs=[
            pl.BlockSpec((tm, 2 * length), lambda b, t: (t, 0)),
            pl.BlockSpec((None, 2 * length, D_MODEL), lambda b, t: (b, 0, 0),
                         pipeline_mode=pl.Buffered(1)),
        ],
        out_specs=pl.BlockSpec((None, tm, D_MODEL), lambda b, t: (b, t, 0)),
        out_shape=jax.ShapeDtypeStruct((bsz, length, D_MODEL), BF16),
        **_call_options("fourier_seq"),
    )(cs_seq, ab)


def _head_rms(r, g):
    return r * lax.rsqrt(jnp.mean(r * r, axis=-1, keepdims=True) + NORM_EPS) * g


def _store_vt_ext(vt_ref, t):
    tokens = t.shape[0]
    first_row = lax.broadcasted_iota(jnp.int32, (BF16_ROWS, tokens), 0) == 0
    for e in range(N_KV_HEADS):
        vt_ref[e, 0:HEAD_DIM, :] = t[:, e * HEAD_DIM:(e + 1) * HEAD_DIM].T.astype(BF16)
        vt_ref[e, HEAD_DIM:HEAD_DIM + BF16_ROWS, :] = first_row.astype(BF16)


def _qkv_kernel(x_ref, mod_ref, ng_ref, w_ref, qg_ref, kg_ref, mean_ref, cos_ref, sin_ref,
                q_ref, k_ref, vt_ref, h_s):
    _modulated_norm_to(h_s, x_ref, mod_ref, ng_ref)
    cosf = cos_ref[...]
    sinf = sin_ref[...]

    qg = qg_ref[...] * (HEAD_DIM ** -0.5 * LOG2_E)
    gains = [jnp.concatenate([qg, qg], axis=1)] * (N_HEADS // 2)
    gains.append(jnp.concatenate([kg_ref[...], kg_ref[...]], axis=1))
    outs = [(q_ref, 2 * p) for p in range(N_HEADS // 2)] + [(k_ref, 0)]
    n_pairs = len(outs)

    def project(p):
        return _dot(h_s[...], w_ref[:, p * MXU_COLS:(p + 1) * MXU_COLS])

    def normalise(t, p):
        sq = t * t
        hi = sq.astype(BF16)
        lo = (sq - hi.astype(F32)).astype(BF16)
        ms = _dot(hi, mean_ref[...]) + _dot(lo, mean_ref[...])
        return t * lax.rsqrt(ms + NORM_EPS) * gains[p]

    def rope_store(rn, p):
        ref, first = outs[p]
        for e in range(2):
            r = rn[:, e * HEAD_DIM:(e + 1) * HEAD_DIM]
            ref[first + e] = (r * cosf + pltpu.roll(r, HEAD_DIM // 2, 1) * sinf).astype(BF16)

    t_next = project(0)
    rn_prev = None
    for p in range(n_pairs + 1):
        t = t_next
        if p < n_pairs:
            t_next = project(p + 1)
        rn = normalise(t, p) if p < n_pairs else None
        if rn_prev is not None:
            rope_store(rn_prev, p - 1)
        rn_prev = rn
    _store_vt_ext(vt_ref, t)


def _kv_ctx_kernel(x_ref, mod_ref, ng_ref, w_ref, kg_ref, k_ref, vt_ref, h_s):
    _modulated_norm_to(h_s, x_ref, mod_ref, ng_ref)
    t = _dot(h_s[...], w_ref[:, 0:MXU_COLS])
    for e in range(N_KV_HEADS):
        k_ref[e] = _head_rms(t[:, e * HEAD_DIM:(e + 1) * HEAD_DIM], kg_ref[...]).astype(BF16)
    _store_vt_ext(vt_ref, _dot(h_s[...], w_ref[:, MXU_COLS:2 * MXU_COLS]))


def _rope_tables(length):
    t = np.arange(length)
    row = (t // GRID_W).astype(np.float64)
    col = (t % GRID_W).astype(np.float64)
    n_freq = HEAD_DIM // 4
    inv = ROPE_THETA ** (-np.arange(n_freq, dtype=np.float64) / n_freq)
    ang = np.concatenate([row[:, None] * inv, col[:, None] * inv], axis=-1)
    cos, sin = np.cos(ang), np.sin(ang)
    return (jnp.asarray(np.concatenate([cos, cos], axis=-1), F32),
            jnp.asarray(np.concatenate([-sin, sin], axis=-1), F32))


def _qkv_latent(x, mods, ng, w_qkv, q_g, k_g):
    bsz, length, _ = x.shape
    tm = TOKEN_TILE
    cosf, sinf = _rope_tables(length)
    n_qkv = w_qkv.shape[-1]
    lane_head = np.arange(MXU_COLS) // HEAD_DIM
    head_mean = _bf16_const((lane_head[:, None] == lane_head[None, :]) * (1.0 / HEAD_DIM))
    kv_shape = jax.ShapeDtypeStruct((bsz, N_KV_HEADS, length, HEAD_DIM), BF16)
    kv_spec = pl.BlockSpec((None, N_KV_HEADS, tm, HEAD_DIM), lambda b, t: (b, 0, t, 0))
    vt_shape = jax.ShapeDtypeStruct((bsz, N_KV_HEADS, HEAD_DIM + BF16_ROWS, length), BF16)
    vt_spec = pl.BlockSpec((None, N_KV_HEADS, HEAD_DIM + BF16_ROWS, tm), lambda b, t: (b, 0, 0, t))
    return pl.pallas_call(
        _qkv_kernel,
        grid=(bsz, length // tm),
        in_specs=[
            pl.BlockSpec((None, tm, D_MODEL), lambda b, t: (b, t, 0)),
            _mod_spec(False, bsz),
            _const_spec((4, D_MODEL)),
            _const_spec((D_MODEL, n_qkv)),
            _const_spec((1, HEAD_DIM)),
            _const_spec((1, HEAD_DIM)),
            _const_spec((MXU_COLS, MXU_COLS)),
            pl.BlockSpec((tm, HEAD_DIM), lambda b, t: (t, 0)),
            pl.BlockSpec((tm, HEAD_DIM), lambda b, t: (t, 0)),
        ],
        out_specs=[
            pl.BlockSpec((None, N_HEADS, tm, HEAD_DIM), lambda b, t: (b, 0, t, 0)),
            kv_spec, vt_spec,
        ],
        out_shape=[
            jax.ShapeDtypeStruct((bsz, N_HEADS, length, HEAD_DIM), BF16),
            kv_shape, vt_shape,
        ],
        scratch_shapes=[pltpu.VMEM((tm, D_MODEL), BF16)],
        **_call_options("qkv_latent"),
    )(x, mods, ng, w_qkv, q_g, k_g, head_mean, cosf, sinf)


def _kv_ctx(ctx, mods, ng, w_kv, k_g):
    bsz, length, _ = ctx.shape
    tm = length
    kv_shape = jax.ShapeDtypeStruct((bsz, N_KV_HEADS, length, HEAD_DIM), BF16)
    kv_spec = pl.BlockSpec((None, N_KV_HEADS, tm, HEAD_DIM), lambda b, t: (b, 0, t, 0))
    vt_shape = jax.ShapeDtypeStruct((bsz, N_KV_HEADS, HEAD_DIM + BF16_ROWS, length), BF16)
    vt_spec = pl.BlockSpec((None, N_KV_HEADS, HEAD_DIM + BF16_ROWS, tm), lambda b, t: (b, 0, 0, t))
    return pl.pallas_call(
        _kv_ctx_kernel,
        grid=(bsz, length // tm),
        in_specs=[
            pl.BlockSpec((None, tm, D_MODEL), lambda b, t: (b, t, 0)),
            _mod_spec(True, bsz),
            _const_spec((4, D_MODEL)),
            _const_spec((D_MODEL, 2 * MXU_COLS)),
            _const_spec((1, HEAD_DIM)),
        ],
        out_specs=[kv_spec, vt_spec],
        out_shape=[kv_shape, vt_shape],
        scratch_shapes=[pltpu.VMEM((tm, D_MODEL), BF16)],
        **_call_options("kv_ctx"),
    )(ctx, mods, ng, w_kv, k_g)


def _region(index, fn):
    pl.when(pl.program_id(0) >= -index)(fn)


def _attn_kernel(qa_ref, qb_ref, ka_ref, kca_ref, kb_ref, kcb_ref, vt_ref, vtc_ref, o_ref, *scratch):
    tq = qa_ref.shape[1] // 2
    n_latent = ka_ref.shape[1]
    n_keys = n_latent + kca_ref.shape[1]
    nq = HEADS_PER_KV * tq
    q4_s, s_s, p_s, m_s = (scratch[i * N_KV_HEADS:(i + 1) * N_KV_HEADS] for i in range(4))

    def chunks_of(k_ref, kc_ref):
        chunks = [(k_ref, vt_ref, slice(r, r + MXU_COLS), slice(r, r + MXU_COLS))
                  for r in range(0, n_latent, MXU_COLS)]
        chunks += [(kc_ref, vtc_ref, slice(r, r + MXU_COLS),
                    slice(n_latent + r, n_latent + r + MXU_COLS))
                   for r in range(0, n_keys - n_latent, MXU_COLS)]
        return chunks

    @pl.when(pl.program_id(0) == 0)
    def _():
        s_s[1][...] = jnp.zeros(s_s[1].shape, F32)
        m_s[1][...] = jnp.zeros(m_s[1].shape, F32)
        p_s[0][...] = jnp.ones(p_s[0].shape, BF16)

    def stage(j_scores, j_probs, j_pv, q_ref, q_rows, key_chunks, out_rows):
        for h in range(HEADS_PER_KV):
            q4_s[j_scores][h * tq:(h + 1) * tq, :] = q_ref[HEADS_PER_KV * j_scores + h, q_rows, :]
        m_probs = m_s[j_probs][...]
        m = None
        acc = None

        def store_head(acc, h):
            cols = slice(h * tq, (h + 1) * tq)
            o = acc[:HEAD_DIM, cols] / acc[HEAD_DIM:HEAD_DIM + 1, cols]
            head = HEADS_PER_KV * j_pv + h
            o_ref[out_rows, head * HEAD_DIM:(head + 1) * HEAD_DIM] = o.T.astype(BF16)

        pv_chunks = list(key_chunks)
        heads_left = list(range(HEADS_PER_KV))
        tiles = [slice(r, r + BF16_ROWS) for r in range(0, n_keys, BF16_ROWS)]
        pv_rows_per_score_row = vt_ref.shape[1] / MXU_COLS
        work_total = len(key_chunks) * (1.0 + pv_rows_per_score_row)
        work_done = 0.0
        tiles_done = 0
        for k_src, _, local, rows in key_chunks:
            s = lax.dot_general(k_src[j_scores, local, :], q4_s[j_scores][...],
                                (((1,), (1,)), ((), ())), preferred_element_type=F32)
            s_s[j_scores][rows, :] = s
            cm = jnp.max(s, axis=0, keepdims=True)
            m = cm if m is None else jnp.maximum(m, cm)
            work_done += 1.0
            if pv_chunks:
                for _, vt_src, pv_local, pv_rows in (
                        pv_chunks.pop(0) for _ in range(min(2, len(pv_chunks)))):
                    d = _dot(vt_src[j_pv, :, pv_local], p_s[j_pv][pv_rows, :])
                    acc = d if acc is None else acc + d
                    work_done += pv_rows_per_score_row
            elif heads_left:
                store_head(acc, heads_left.pop(0))
            tiles_until = round(len(tiles) * work_done / work_total)
            for tile in tiles[tiles_done:tiles_until]:
                p_s[j_probs][tile, :] = jnp.exp2(s_s[j_probs][tile, :] - m_probs).astype(BF16)
            tiles_done = tiles_until
        assert not pv_chunks and tiles_done == len(tiles)
        for h in heads_left:
            store_head(acc, h)
        m_s[j_scores][...] = jnp.broadcast_to(m, (BF16_ROWS, nq))

    first, second = slice(0, tq), slice(tq, 2 * tq)
    chunks_a, chunks_b = chunks_of(ka_ref, kca_ref), chunks_of(kb_ref, kcb_ref)
    _region(0, lambda: stage(0, 1, 0, qa_ref, second, chunks_a, first))
    _region(1, lambda: stage(1, 0, 1, qa_ref, second, chunks_a, first))
    _region(2, lambda: stage(0, 1, 0, qb_ref, first, chunks_b, second))
    _region(3, lambda: stage(1, 0, 1, qb_ref, first, chunks_b, second))


def _attention(q, k, kc, vt, vtc):
    bsz, _, length, _ = q.shape
    n_ctx = kc.shape[2]
    n_keys = length + n_ctx
    vt_rows = vt.shape[2]
    tq = QUERY_TILE
    nq = HEADS_PER_KV * tq
    per_batch = length // (2 * tq)
    n_blocks = bsz * per_batch

    def cur(s):
        return jnp.minimum(s, n_blocks - 1)

    def prev(s):
        return jnp.maximum(s - 1, 0)

    def q_spec(block):
        return pl.BlockSpec((None, N_HEADS, 2 * tq, HEAD_DIM),
                            lambda s: (block(s) // per_batch, 0, block(s) % per_batch, 0))

    def kv_spec(rows, cols, block):
        return pl.BlockSpec((None, N_KV_HEADS, rows, cols), lambda s: (block(s) // per_batch, 0, 0, 0))

    return pl.pallas_call(
        _attn_kernel,
        grid=(n_blocks + 1,),
        in_specs=[
            q_spec(prev), q_spec(cur),
            kv_spec(length, HEAD_DIM, prev), kv_spec(n_ctx, HEAD_DIM, prev),
            kv_spec(length, HEAD_DIM, cur), kv_spec(n_ctx, HEAD_DIM, cur),
            kv_spec(vt_rows, length, prev), kv_spec(vt_rows, n_ctx, prev),
        ],
        out_specs=pl.BlockSpec((None, 2 * tq, D_MODEL),
                               lambda s: (prev(s) // per_batch, prev(s) % per_batch, 0)),
        out_shape=jax.ShapeDtypeStruct((bsz, length, D_MODEL), BF16),
        scratch_shapes=(
            [pltpu.VMEM((nq, HEAD_DIM), BF16)] * N_KV_HEADS
            + [pltpu.VMEM((n_keys, nq), F32)] * N_KV_HEADS
            + [pltpu.VMEM((n_keys, nq), BF16)] * N_KV_HEADS
            + [pltpu.VMEM((BF16_ROWS, nq), F32)] * N_KV_HEADS),
        **_call_options("attention", grid_rank=1),
    )(q, q, k, kc, k, kc, vt, vtc)


def kernel(x, c, ctx, c_ctx, ada_w, ada_b, norm_g, mlp_w1, mlp_w2, a_w_in, a_ln_g, a_w_s, a_b_s, a_w_out,
           b_w_out, c_w_qkv, c_q_g, c_k_g, c_w_o):
    bsz = x.shape[0]
    attn_layers = [i for i in range(DEPTH) if i % N_MIXERS == 2]
    last_ctx_read = attn_layers[-1] if attn_layers else -1

    cond = jnp.zeros((COND_ROWS, D_MODEL), F32).at[:bsz].set(c).at[bsz].set(c_ctx)
    mods_all = _ada_all(cond, ada_w, ada_b).reshape(DEPTH, COND_ROWS, 6, D_MODEL)

    def big_weights(i):
        pairs = [(mlp_w1, i), (mlp_w2, i)]
        if i % N_MIXERS == 0:
            pairs += [(a_w_in, i // N_MIXERS), (a_w_out, i // N_MIXERS)]
        return pairs

    weights = [w_all[index].astype(BF16) for w_all, index in big_weights(0)]

    for i in range(DEPTH):
        kind, j = i % N_MIXERS, i // N_MIXERS
        ctx_in = i <= last_ctx_read
        ctx_out = i < last_ctx_read
        mods = mods_all[i]
        ng = norm_g[i]
        w1, w2 = weights[:2]

        if kind == 0:
            w_in, w_post = weights[2:]
            ln_g = a_ln_g[j].reshape(1, A_WIDTH)
            w_s = a_w_s[j].astype(BF16)
            b_s = a_b_s[j].reshape(A_GROUPS, CHUNK, 1)
            z = _gmlp_head(x, mods, ng, w_in, ln_g, w_s, b_s, False)
            zc = _gmlp_head(ctx, mods, ng, w_in, ln_g, w_s, b_s, True) if ctx_out else None
        elif kind == 1:
            w_post = b_w_out[j].astype(BF16)
            z = _fourier_head_folded(x, mods, ng)
            zc = _fourier_head(ctx, mods, ng, True) if ctx_out else None
        else:
            w_qkv = c_w_qkv[j].astype(BF16)
            w_post = c_w_o[j].astype(BF16)
            q_g = c_q_g[j].reshape(1, HEAD_DIM)
            k_g = c_k_g[j].reshape(1, HEAD_DIM)
            assert ctx_in and not ctx_out
            q, k, vt = _qkv_latent(x, mods, ng, w_qkv, q_g, k_g)
            kc, vtc = _kv_ctx(ctx, mods, ng, w_qkv[:, N_HEADS * HEAD_DIM:], k_g)
            z = _attention(q, k, kc, vt, vtc)
            zc = None

        cast_next = big_weights(i + 1) if i + 1 < DEPTH else ()
        x, next_weights = _tail(x, z, mods, ng, w_post, w1, w2, False, z_mirrored=(kind == 1),
                                cast_next=cast_next)
        if ctx_out:
            ctx, _ = _tail(ctx, zc, mods, ng, w_post, w1, w2, True)
        weights = next_weights

    return x
```

```python
import functools

import numpy as np
import jax
import jax.numpy as jnp
from jax import lax
from jax.experimental import pallas as pl
from jax.experimental.pallas import tpu as pltpu

D_MODEL = 1024
DEPTH = 4
GRID_W = 64
N_MIXERS = 3
CHUNK = 128
A_WIDTH = 2 * D_MODEL
A_GROUPS = 8
A_GROUP_W = A_WIDTH // A_GROUPS
B_GROUPS = 4
B_GROUP_W = D_MODEL // B_GROUPS
HEAD_DIM = 128
N_HEADS = D_MODEL // HEAD_DIM
N_KV_HEADS = 2
HEADS_PER_KV = N_HEADS // N_KV_HEADS
ROPE_THETA = 10000.0
D_FF = 4 * D_MODEL
NORM_EPS = 1e-6
LN_EPS = 1e-5
LOG2_E = 1.4426950408889634

F32 = jnp.float32
BF16 = jnp.bfloat16

MXU_COLS = 256
ROW_BLOCK = 64
BF16_ROWS = 16
LANES = 128
SUBLANES = 8
COND_ROWS = 16
TOKEN_TILE = 512
TOKEN_BLOCK = 1024
QUERY_TILE = 128
ADA_COLS = 1536
MIB = 1024 * 1024
VMEM_MIB = {
    "ada_mod": 32, "gmlp_head": 56, "tail": 58, "fourier_fold": 32, "fourier_seq_fold": 40,
    "fourier_chan": 32, "fourier_seq": 48, "qkv_latent": 32, "kv_ctx": 32, "attention": 54,
}


def _dot(a, b):
    return jnp.dot(a, b, preferred_element_type=F32)


def _rms(x, g):
    ms = jnp.mean(x * x, axis=-1, keepdims=True)
    return x * lax.rsqrt(ms + NORM_EPS) * g


def _gelu_tanh(x):
    c = 2.0 * 0.7978845608028654 * LOG2_E
    z = x * (-c - (c * 0.044715) * (x * x))
    return x / (1.0 + jnp.exp2(z))


def _slab_specs(w, width, layer=None):
    k, n = w.shape[-2:]
    if layer is None:
        return [pl.BlockSpec((k, width), lambda b, t, c=c: (0, c), pipeline_mode=pl.Buffered(1))
                for c in range(n // width)]
    return [pl.BlockSpec((None, k, width), lambda b, t, c=c: (layer, 0, c), pipeline_mode=pl.Buffered(1))
            for c in range(n // width)]


def _const_spec(shape):
    zeros = (0,) * len(shape)
    return pl.BlockSpec(shape, lambda b, t: zeros, pipeline_mode=pl.Buffered(1))


def _mod_spec(ctx_row):
    if ctx_row is not None:
        return pl.BlockSpec((None, 6, D_MODEL), lambda b, t: (ctx_row, 0, 0))
    return pl.BlockSpec((None, 6, D_MODEL), lambda b, t: (b, 0, 0))


def _call_options(name, grid_rank=2):
    return dict(
        name=name,
        compiler_params=pltpu.CompilerParams(
            dimension_semantics=("arbitrary",) * grid_rank,
            vmem_limit_bytes=VMEM_MIB[name] * MIB))


def _modulated_norm_to(h_s, x_ref, mod_ref, ng_ref, start=0, stop=None):
    sh1 = mod_ref[0:1, :]
    sc1p = 1.0 + mod_ref[1:2, :]
    g0 = ng_ref[0:1, :]
    stop = x_ref.shape[0] if stop is None else stop
    for r in range(start, stop, ROW_BLOCK):
        rows = slice(r, r + ROW_BLOCK)
        h_s[rows, :] = (_rms(x_ref[rows, :], g0) * sc1p + sh1).astype(BF16)


def _ada_kernel(cond_ref, w_ref, b_ref, o_ref):
    a = cond_ref[...]
    a = a * jax.nn.sigmoid(a)
    o_ref[...] = _dot(a.astype(BF16), w_ref[...].astype(BF16)) + b_ref[...]


def _ada_all(cond, ada_w, ada_b):
    tn = ADA_COLS
    n_out = 6 * D_MODEL
    return pl.pallas_call(
        _ada_kernel,
        grid=(DEPTH, n_out // tn),
        in_specs=[
            pl.BlockSpec((COND_ROWS, D_MODEL), lambda i, n: (0, 0)),
            pl.BlockSpec((None, D_MODEL, tn), lambda i, n: (i, 0, n)),
            pl.BlockSpec((None, 1, tn), lambda i, n: (i, 0, n)),
        ],
        out_specs=pl.BlockSpec((None, COND_ROWS, tn), lambda i, n: (i, 0, n)),
        out_shape=jax.ShapeDtypeStruct((DEPTH, COND_ROWS, n_out), F32),
        **_call_options("ada_mod"),
    )(cond, ada_w, ada_b.reshape(DEPTH, 1, n_out))


def _gmlp_head_kernel(x_ref, mod_ref, ng_ref, lng_ref, ws_ref, bs_ref, *refs):
    width = 2 * MXU_COLS
    n_half = A_WIDTH // width
    win = refs[:2 * n_half]
    z_ref, h_s, u_s, v_s, vb_s, sum_s, sq_s = refs[2 * n_half:]
    tm = x_ref.shape[0]
    lng = lng_ref[...]
    half = tm // 2
    lanes = sum_s.shape[1]

    def proj(unit):
        c, a = unit
        return _dot(h_s[a * half:(a + 1) * half, :], win[c][...])

    def gelu_store(d, unit):
        c, a = unit
        is_v = c >= n_half
        dst, c = (v_s, c - n_half) if is_v else (u_s, c)
        for r in range(0, half, ROW_BLOCK):
            rows = slice(a * half + r, a * half + r + ROW_BLOCK)
            g = _gelu_tanh(d[r:r + ROW_BLOCK, :])
            dst[rows, c * width:(c + 1) * width] = g
            if is_v:
                parts = [g[:, i:i + lanes] for i in range(0, width, lanes)]
                p1 = sum(parts[1:], parts[0])
                p2 = sum([p * p for p in parts[1:]], parts[0] * parts[0])
                if c == 0:
                    sum_s[rows, :] = p1
                    sq_s[rows, :] = p2
                else:
                    sum_s[rows, :] += p1
                    sq_s[rows, :] += p2

    def layer_norm_rows(a, part):
        start = a * half + part * half // n_half
        for r in range(start, start + half // n_half, ROW_BLOCK // 2):
            rows = slice(r, r + ROW_BLOCK // 2)
            mu = jnp.sum(sum_s[rows, :], axis=-1, keepdims=True) * (1.0 / A_WIDTH)
            ex2 = jnp.sum(sq_s[rows, :], axis=-1, keepdims=True) * (1.0 / A_WIDTH)
            rstd = lax.rsqrt(jnp.maximum(ex2 - mu * mu, 0.0) + LN_EPS)
            vb_s[rows, :] = ((v_s[rows, :] - mu) * rstd * lng).astype(BF16)

    def gate_group(g):
        w = ws_ref[g]
        b = jnp.broadcast_to(bs_ref[g], (CHUNK, A_GROUP_W))
        cols = slice(g * A_GROUP_W, (g + 1) * A_GROUP_W)
        for n in range(tm // CHUNK):
            rows = slice(n * CHUNK, (n + 1) * CHUNK)
            sv = _dot(w, vb_s[rows, cols]) + b
            z_ref[rows, cols] = (u_s[rows, cols] * sv).astype(BF16)

    chunks = list(range(n_half, 2 * n_half)) + list(range(n_half))
    order = [(c, a) for c in chunks for a in range(2)]
    norms =[functools.partial(layer_norm_rows, a, part) for a in range(2) for part in range(n_half)]
    fillers = {2 * n_half + 1 + k: norms[k:k + 2] for k in range(0, len(norms), 2)}

    _modulated_norm_to(h_s, x_ref, mod_ref, ng_ref, 0, half)
    d = proj(order[0])
    _modulated_norm_to(h_s, x_ref, mod_ref, ng_ref, half, tm)
    for i in range(1, len(order) + 1):
        d_next = proj(order[i]) if i < len(order) else None
        gelu_store(d, order[i - 1])
        for filler in fillers.get(i - 1, []):
            filler()
        d = d_next
    for g in range(A_GROUPS):
        gate_group(g)


def _gmlp_head(x, mods, ng, w_in, ln_g, w_s, b_s, ctx_row=None):
    bsz, length, _ = x.shape
    tm = min(TOKEN_BLOCK, length)
    w_in_specs = _slab_specs(w_in, 2 * MXU_COLS)
    return pl.pallas_call(
        _gmlp_head_kernel,
        grid=(bsz, length // tm),
        in_specs=[
            pl.BlockSpec((None, tm, D_MODEL), lambda b, t: (b, t, 0)),
            _mod_spec(ctx_row),
            _const_spec((4, D_MODEL)),
            _const_spec((1, A_WIDTH)),
            _const_spec((A_GROUPS, CHUNK, CHUNK)),
            _const_spec((A_GROUPS, CHUNK, 1)),
        ] + w_in_specs,
        out_specs=pl.BlockSpec((None, tm, A_WIDTH), lambda b, t: (b, t, 0)),
        out_shape=jax.ShapeDtypeStruct((bsz, length, A_WIDTH), BF16),
        scratch_shapes=[
            pltpu.VMEM((tm, D_MODEL), BF16),
            pltpu.VMEM((tm, A_WIDTH), F32),
            pltpu.VMEM((tm, A_WIDTH), F32),
            pltpu.VMEM((tm, A_WIDTH), BF16),
            pltpu.VMEM((tm, LANES), F32),
            pltpu.VMEM((tm, LANES), F32),
        ],
        **_call_options("gmlp_head"),
    )(x, mods, ng, ln_g, w_s, b_s, *([w_in] * len(w_in_specs)))


def _interleave(dots, fillers):
    per = -(-len(fillers) // len(dots))
    for i, dot_fn in enumerate(dots):
        dot_fn()
        for filler in fillers[i * per:(i + 1) * per]:
            filler()


def _tail_kernel(x_ref, z_ref, mod_ref, ng_ref, *refs, sub, z_mirrored, n_cast):
    n_out = D_MODEL // MXU_COLS
    n_ff = D_FF // (2 * MXU_COLS)
    n_w = 2 * n_out + n_ff
    wp, w1, w2 = refs[:n_out], refs[n_out:n_out + n_ff], refs[n_out + n_ff:n_w]
    cast_src = refs[n_w:n_w + n_cast]
    o_ref = refs[n_w + n_cast]
    cast_dst = refs[n_w + n_cast + 1:n_w + 2 * n_cast + 1]
    scratch = refs[n_w + 2 * n_cast + 1:]
    for src, dst in zip(cast_src, cast_dst):
        dst[...] = src[...].astype(BF16)
    n_sub = x_ref.shape[0] // sub
    weights = (mod_ref, ng_ref, wp, w1, w2)
    pending = []
    for i in range(n_sub):
        zi = i
        if z_mirrored and n_sub > 1:
            second_half = pl.program_id(1) >= pl.num_programs(1) // 2
            zi = jnp.where(second_half, n_sub - 1 - i, i)
        rows = pl.ds(i * sub, sub)
        z_rows = rows if isinstance(zi, int) else pl.ds(pl.multiple_of(zi * sub, sub), sub)
        pending = _tail_subtile(x_ref.at[rows, :], z_ref.at[z_rows, :], *weights, o_ref.at[rows, :],
                                *scratch, pending=pending, defer_last=i + 1 < n_sub)


def _tail_subtile(x_ref, z_ref, mod_ref, ng_ref, wp, w1, w2, o_ref, *scratch, pending, defer_last):
    half = x_ref.shape[0] // 2
    y_s, hm_s, hid_s = scratch[0:2], scratch[2:4], scratch[4:6]
    g1 = mod_ref[2:3, :]
    sh2 = mod_ref[3:4, :]
    sc2p = 1.0 + mod_ref[4:5, :]
    g2 = mod_ref[5:6, :]
    ff_cols = 2 * MXU_COLS
    out_chunks = list(range(D_MODEL // MXU_COLS))
    ff_chunks = list(range(D_FF // ff_cols))
    blocks = list(range(0, half, ROW_BLOCK))

    def post_dot(a, n):
        y_s[a][:, n * MXU_COLS:(n + 1) * MXU_COLS] = _dot(z_ref[a * half:(a + 1) * half, :], wp[n][...])

    def norm1(a, r):
        rows = slice(a * half + r, a * half + r + ROW_BLOCK)
        x1 = x_ref[rows, :] + g1 * _rms(y_s[a][r:r + ROW_BLOCK, :], ng_ref[1:2, :])
        o_ref[rows, :] = x1
        hm_s[a][r:r + ROW_BLOCK, :] = (_rms(x1, ng_ref[2:3, :]) * sc2p + sh2).astype(BF16)

    def up_dot(a, c):
        t = jnp.maximum(_dot(hm_s[a][...], w1[c][...]), 0.0)
        hid_s[a][:, c * ff_cols:(c + 1) * ff_cols] = (t * t).astype(BF16)

    def down_dot(a, n):
        y_s[a][:, n * MXU_COLS:(n + 1) * MXU_COLS] = _dot(hid_s[a][...], w2[n][...])

    def norm3(a, r):
        rows = slice(a * half + r, a * half + r + ROW_BLOCK)
        o_ref[rows, :] = o_ref[rows, :] + g2 * _rms(y_s[a][r:r + ROW_BLOCK, :], ng_ref[3:4, :])

    def each(fn, a, items):
        return [functools.partial(fn, a, item) for item in items]

    _interleave(each(post_dot, 0, out_chunks), list(pending))
    _interleave(each(post_dot, 1, out_chunks), each(norm1, 0, blocks))
    _interleave(each(up_dot, 0, ff_chunks), each(norm1, 1, blocks))
    _interleave(each(up_dot, 1, ff_chunks), [])
    _interleave(each(down_dot, 0, out_chunks), [])
    _interleave(each(down_dot, 1, out_chunks), each(norm3, 0, blocks))
    last = each(norm3, 1, blocks)
    if defer_last:
        return last
    _interleave(last, [])
    return []


def _tail(x, z, mods, ng, w_post, w1, w2, ctx_row=None, z_mirrored=False, cast_next=()):
    bsz, length, _ = x.shape
    kz = z.shape[-1]
    sub = min(TOKEN_TILE, length)
    tm = min(TOKEN_BLOCK, length)
    w_post, post_index = w_post if isinstance(w_post, tuple) else (w_post, None)
    wp_specs = _slab_specs(w_post, MXU_COLS, post_index)
    w1_specs = _slab_specs(w1, 2 * MXU_COLS)
    w2_specs = _slab_specs(w2, MXU_COLS)
    n_steps = bsz * (length // tm)
    per_batch = length // tm
    cast_in_specs, cast_out_specs, cast_out_shapes = [], [], []
    for w_all, index in cast_next:
        k, n = w_all.shape[-2:]
        rows = k // n_steps
        cast_in_specs.append(pl.BlockSpec(
            (None, rows, n), lambda b, t, index=index: (index, b * per_batch + t, 0)))
        cast_out_specs.append(pl.BlockSpec((rows, n), lambda b, t: (b * per_batch + t, 0)))
        cast_out_shapes.append(jax.ShapeDtypeStruct((k, n), BF16))
    if z_mirrored:
        nt = length // 2 // tm
        z_spec = pl.BlockSpec(
            (None, None, tm, kz),
            lambda b, t: (b, t // nt, jnp.where(t < nt, t, 2 * nt - 1 - t), 0))
    else:
        z_spec = pl.BlockSpec((None, tm, kz), lambda b, t: (b, t, 0))
    out, *casts = pl.pallas_call(
        functools.partial(_tail_kernel, sub=sub, z_mirrored=z_mirrored, n_cast=len(cast_next)),
        grid=(bsz, length // tm),
        in_specs=[
            pl.BlockSpec((None, tm, D_MODEL), lambda b, t: (b, t, 0)),
            z_spec,
            _mod_spec(ctx_row),
            _const_spec((4, D_MODEL)),
        ] + wp_specs + w1_specs + w2_specs + cast_in_specs,
        out_specs=[pl.BlockSpec((None, tm, D_MODEL), lambda b, t: (b, t, 0))] + cast_out_specs,
        out_shape=[jax.ShapeDtypeStruct((bsz, length, D_MODEL), F32)] + cast_out_shapes,
        scratch_shapes=(
            [pltpu.VMEM((sub // 2, D_MODEL), F32)] * 2
            + [pltpu.VMEM((sub // 2, D_MODEL), BF16)] * 2
            + [pltpu.VMEM((sub // 2, D_FF), BF16)] * 2),
        **_call_options("tail"),
    )(x, z, mods, ng, *([w_post] * len(wp_specs)), *([w1] * len(w1_specs)),
      *([w2] * len(w2_specs)), *[w_all for w_all, _ in cast_next])
    return out, casts


def _fourier_chan_kernel(x_ref, mod_ref, ng_ref, cs_ref, ab_ref, h_s):
    _modulated_norm_to(h_s, x_ref, mod_ref, ng_ref)
    for g in range(B_GROUPS):
        cols = slice(g * B_GROUP_W, (g + 1) * B_GROUP_W)
        t = _dot(h_s[:, cols], cs_ref[...])
        ab_ref[0, :, cols] = t[:, :B_GROUP_W].astype(BF16)
        ab_ref[1, :, cols] = t[:, B_GROUP_W:].astype(BF16)


def _fourier_seq_kernel(cs_ref, ab_ref, y_ref):
    for n in range(D_MODEL // MXU_COLS):
        cols = slice(n * MXU_COLS, (n + 1) * MXU_COLS)
        y_ref[:, cols] = _dot(cs_ref[...], ab_ref[:, cols]).astype(BF16)


def _bf16_const(a):
    return jnp.asarray(np.asarray(a, np.float32)).astype(BF16)


def _dft_angles(n):
    idx = np.arange(n, dtype=np.int64)
    return (2.0 * np.pi / n) * ((idx[:, None] * idx[None, :]) % n)


def _dft_tables(length):
    ang_c = _dft_angles(B_GROUP_W)
    cs_chan = np.concatenate([np.cos(ang_c), np.sin(ang_c)], axis=1) * B_GROUP_W ** -0.5
    ang_l = _dft_angles(length)
    cs_seq = np.concatenate([np.cos(ang_l), -np.sin(ang_l)], axis=1) * length ** -0.5
    return _bf16_const(cs_chan), _bf16_const(cs_seq)


def _fourier_fold_kernel(xp_ref, xm_ref, x0_ref, mod_ref, ng_ref, cc_ref, sc_ref, rev_ref,
                         ab_ref, a0_ref, hp_s, hm_s):
    tm = xm_ref.shape[0]
    sh1 = mod_ref[0:1, :]
    sc1p = 1.0 + mod_ref[1:2, :]
    g0 = ng_ref[0:1, :]
    for r in range(0, tm, ROW_BLOCK):
        rows = slice(r, r + ROW_BLOCK)
        hp_s[rows, :] = _rms(xp_ref[0, r + 1:r + 1 + ROW_BLOCK, :], g0) * sc1p + sh1
        hm_s[rows, :] = (_rms(xm_ref[rows, :], g0) * sc1p + sh1).astype(BF16)
    h0 = (_rms(x0_ref[...], g0) * sc1p + sh1).astype(BF16)
    for g in range(B_GROUPS):
        cols = slice(g * B_GROUP_W, (g + 1) * B_GROUP_W)
        hm_rev = _dot(rev_ref[...], hm_s[:, cols])
        hp = hp_s[:, cols]
        ab_ref[0, :, cols] = _dot((hp + hm_rev).astype(BF16), cc_ref[...]).astype(BF16)
        ab_ref[1, :, cols] = _dot((hp - hm_rev).astype(BF16), sc_ref[...]).astype(BF16)
        a0_ref[:, cols] = _dot(h0[:, cols], cc_ref[...])


def _fourier_seq_fold_kernel(ct_ref, st_ref, ab_ref, a0_ref, perm_ref, y_ref, w_s, *, scale):
    tk = y_ref.shape[1]
    half = ct_ref.shape[1]
    for n in range(D_MODEL // MXU_COLS):
        cols = slice(n * MXU_COLS, (n + 1) * MXU_COLS)
        yc = _dot(ct_ref[...], ab_ref[0:half, cols]) + a0_ref[0:1, cols] * scale
        ys = _dot(st_ref[...], ab_ref[half:2 * half, cols])
        y_ref[0, :, cols] = (yc[:tk] - ys[:tk]).astype(BF16)
        w_s[:, cols] = (yc + ys).astype(BF16)
    for n in range(D_MODEL // MXU_COLS):
        cols = slice(n * MXU_COLS, (n + 1) * MXU_COLS)
        y_ref[1, :, cols] = _dot(perm_ref[...], w_s[:, cols]).astype(BF16)


def _seq_fold_tables(length, tk):
    half = length // 2
    nt = half // tk
    rows = tk + BF16_ROWS
    t = np.arange(1, half + 1, dtype=np.int64)
    k = np.arange(nt, dtype=np.int64)[:, None] * tk + np.arange(rows, dtype=np.int64)[None, :]
    ang = (2.0 * np.pi / length) * ((k[..., None] * t) % length)
    weight = np.where(t == half, 0.5, 1.0) * length ** -0.5
    return _bf16_const(np.cos(ang) * weight), _bf16_const(np.sin(ang) * weight)


def _fourier_head_folded(x, mods, ng):
    bsz, length, _ = x.shape
    tm = TOKEN_TILE
    half = length // 2
    nt = half // tm
    rows = tm + BF16_ROWS
    ang_c = _dft_angles(B_GROUP_W)
    cc = _bf16_const(np.cos(ang_c) * B_GROUP_W ** -0.5)
    sc = _bf16_const(np.sin(ang_c) * B_GROUP_W ** -0.5)
    rev = _bf16_const(np.arange(tm)[:, None] + np.arange(tm)[None, :] == tm - 1)
    ab, a0 = pl.pallas_call(
        _fourier_fold_kernel,
        grid=(bsz, nt),
        in_specs=[
            pl.BlockSpec((pl.Element(1), pl.Element(tm + SUBLANES), pl.Element(D_MODEL)),
                         lambda b, t: (b, t * tm, 0)),
            pl.BlockSpec((None, tm, D_MODEL), lambda b, t: (b, 2 * nt - 1 - t, 0)),
            pl.BlockSpec((None, SUBLANES, D_MODEL), lambda b, t: (b, 0, 0)),
            _mod_spec(None),
            _const_spec((4, D_MODEL)),
            _const_spec((B_GROUP_W, B_GROUP_W)),
            _const_spec((B_GROUP_W, B_GROUP_W)),
            _const_spec((tm, tm)),
        ],
        out_specs=[
            pl.BlockSpec((None, 2, tm, D_MODEL), lambda b, t: (b, 0, t, 0)),
            pl.BlockSpec((None, SUBLANES, D_MODEL), lambda b, t: (b, 0, 0)),
        ],
        out_shape=[
            jax.ShapeDtypeStruct((bsz, 2, half, D_MODEL), BF16),
            jax.ShapeDtypeStruct((bsz, SUBLANES, D_MODEL), F32),
        ],
        scratch_shapes=[pltpu.VMEM((tm, D_MODEL), F32), pltpu.VMEM((tm, D_MODEL), BF16)],
        **_call_options("fourier_fold"),
    )(x, x, x, mods, ng, cc, sc, rev)
    ct, st = _seq_fold_tables(length, tm)
    perm = _bf16_const(np.arange(tm)[:, None] + np.arange(rows)[None, :] == tm)
    return pl.pallas_call(
        functools.partial(_fourier_seq_fold_kernel, scale=length ** -0.5),
        grid=(bsz, nt),
        in_specs=[
            pl.BlockSpec((None, rows, half), lambda b, t: (t, 0, 0)),
            pl.BlockSpec((None, rows, half), lambda b, t: (t, 0, 0)),
            pl.BlockSpec((None, 2 * half, D_MODEL), lambda b, t: (b, 0, 0)),
            pl.BlockSpec((None, SUBLANES, D_MODEL), lambda b, t: (b, 0, 0)),
            _const_spec((tm, rows)),
        ],
        out_specs=pl.BlockSpec((None, 2, tm, D_MODEL), lambda b, t: (b, 0, t, 0)),
        out_shape=jax.ShapeDtypeStruct((bsz, 2, half, D_MODEL), BF16),
        scratch_shapes=[pltpu.VMEM((rows, D_MODEL), BF16)],
        **_call_options("fourier_seq_fold"),
    )(ct, st, ab.reshape(bsz, 2 * half, D_MODEL), a0, perm)


def _fourier_head(x, mods, ng, ctx_row):
    bsz, length, _ = x.shape
    tm = min(TOKEN_TILE, length)
    cs_chan, cs_seq = _dft_tables(length)
    ab = pl.pallas_call(
        _fourier_chan_kernel,
        grid=(bsz, length // tm),
        in_specs=[
            pl.BlockSpec((None, tm, D_MODEL), lambda b, t: (b, t, 0)),
            _mod_spec(ctx_row),
            _const_spec((4, D_MODEL)),
            _const_spec((B_GROUP_W, 2 * B_GROUP_W)),
        ],
        out_specs=pl.BlockSpec((None, 2, tm, D_MODEL), lambda b, t: (b, 0, t, 0)),
        out_shape=jax.ShapeDtypeStruct((bsz, 2, length, D_MODEL), BF16),
        scratch_shapes=[pltpu.VMEM((tm, D_MODEL), BF16)],
        **_call_options("fourier_chan"),
    )(x, mods, ng, cs_chan)
    ab = ab.reshape(bsz, 2 * length, D_MODEL)
    return pl.pallas_call(
        _fourier_seq_kernel,
        grid=(bsz, length // tm),
        in_specs=[
            pl.BlockSpec((tm, 2 * length), lambda b, t: (t, 0)),
            pl.BlockSpec((None, 2 * length, D_MODEL), lambda b, t: (b, 0, 0),
                         pipeline_mode=pl.Buffered(1)),
        ],
        out_specs=pl.BlockSpec((None, tm, D_MODEL), lambda b, t: (b, t, 0)),
        out_shape=jax.ShapeDtypeStruct((bsz, length, D_MODEL), BF16),
        **_call_options("fourier_seq"),
    )(cs_seq, ab)


def _head_rms(r, g):
    return r * lax.rsqrt(jnp.mean(r * r, axis=-1, keepdims=True) + NORM_EPS) * g


def _store_vt_ext(vt_ref, t):
    tokens = t.shape[0]
    first_row = lax.broadcasted_iota(jnp.int32, (BF16_ROWS, tokens), 0) == 0
    for e in range(N_KV_HEADS):
        vt_ref[e, 0:HEAD_DIM, :] = t[:, e * HEAD_DIM:(e + 1) * HEAD_DIM].T.astype(BF16)
        vt_ref[e, HEAD_DIM:HEAD_DIM + BF16_ROWS, :] = first_row.astype(BF16)


def _qkv_kernel(x_ref, mod_ref, ng_ref, w_ref, qg_ref, kg_ref, mean_ref, cos_ref, sin_ref,
                q_ref, k_ref, vt_ref, h_s):
    _modulated_norm_to(h_s, x_ref, mod_ref, ng_ref)
    cosf = cos_ref[...]
    sinf = sin_ref[...]

    qg = qg_ref[...] * (HEAD_DIM ** -0.5 * LOG2_E)
    gains = [jnp.concatenate([qg, qg], axis=1)] * (N_HEADS // 2)
    gains.append(jnp.concatenate([kg_ref[...], kg_ref[...]], axis=1))
    outs = [(q_ref, 2 * p) for p in range(N_HEADS // 2)] + [(k_ref, 0)]
    n_pairs = len(outs)

    def project(p):
        return _dot(h_s[...], w_ref[:, p * MXU_COLS:(p + 1) * MXU_COLS])

    def normalise(t, p):
        sq = t * t
        hi = sq.astype(BF16)
        lo = (sq - hi.astype(F32)).astype(BF16)
        ms = _dot(hi, mean_ref[...]) + _dot(lo, mean_ref[...])
        return t * lax.rsqrt(ms + NORM_EPS) * gains[p]

    def rope_store(rn, p):
        ref, first = outs[p]
        for e in range(2):
            r = rn[:, e * HEAD_DIM:(e + 1) * HEAD_DIM]
            ref[first + e] = (r * cosf + pltpu.roll(r, HEAD_DIM // 2, 1) * sinf).astype(BF16)

    t_next = project(0)
    rn_prev = None
    for p in range(n_pairs + 1):
        t = t_next
        if p < n_pairs:
            t_next = project(p + 1)
        rn = normalise(t, p) if p < n_pairs else None
        if rn_prev is not None:
            rope_store(rn_prev, p - 1)
        rn_prev = rn
    _store_vt_ext(vt_ref, t)


def _kv_ctx_kernel(x_ref, mod_ref, ng_ref, w_ref, kg_ref, k_ref, vt_ref, h_s):
    _modulated_norm_to(h_s, x_ref, mod_ref, ng_ref)
    t = _dot(h_s[...], w_ref[:, 0:MXU_COLS])
    for e in range(N_KV_HEADS):
        k_ref[e] = _head_rms(t[:, e * HEAD_DIM:(e + 1) * HEAD_DIM], kg_ref[...]).astype(BF16)
    _store_vt_ext(vt_ref, _dot(h_s[...], w_ref[:, MXU_COLS:2 * MXU_COLS]))


def _rope_tables(length):
    t = np.arange(length)
    row = (t // GRID_W).astype(np.float64)
    col = (t % GRID_W).astype(np.float64)
    n_freq = HEAD_DIM // 4
    inv = ROPE_THETA ** (-np.arange(n_freq, dtype=np.float64) / n_freq)
    ang = np.concatenate([row[:, None] * inv, col[:, None] * inv], axis=-1)
    cos, sin = np.cos(ang), np.sin(ang)
    return (jnp.asarray(np.concatenate([cos, cos], axis=-1), F32),
            jnp.asarray(np.concatenate([-sin, sin], axis=-1), F32))


def _qkv_latent(x, mods, ng, w_qkv, q_g, k_g):
    bsz, length, _ = x.shape
    tm = TOKEN_TILE
    cosf, sinf = _rope_tables(length)
    n_qkv = w_qkv.shape[-1]
    lane_head = np.arange(MXU_COLS) // HEAD_DIM
    head_mean = _bf16_const((lane_head[:, None] == lane_head[None, :]) * (1.0 / HEAD_DIM))
    kv_shape = jax.ShapeDtypeStruct((bsz, N_KV_HEADS, length, HEAD_DIM), BF16)
    kv_spec = pl.BlockSpec((None, N_KV_HEADS, tm, HEAD_DIM), lambda b, t: (b, 0, t, 0))
    vt_shape = jax.ShapeDtypeStruct((bsz, N_KV_HEADS, HEAD_DIM + BF16_ROWS, length), BF16)
    vt_spec = pl.BlockSpec((None, N_KV_HEADS, HEAD_DIM + BF16_ROWS, tm), lambda b, t: (b, 0, 0, t))
    return pl.pallas_call(
        _qkv_kernel,
        grid=(bsz, length // tm),
        in_specs=[
            pl.BlockSpec((None, tm, D_MODEL), lambda b, t: (b, t, 0)),
            _mod_spec(None),
            _const_spec((4, D_MODEL)),
            _const_spec((D_MODEL, n_qkv)),
            _const_spec((1, HEAD_DIM)),
            _const_spec((1, HEAD_DIM)),
            _const_spec((MXU_COLS, MXU_COLS)),
            pl.BlockSpec((tm, HEAD_DIM), lambda b, t: (t, 0)),
            pl.BlockSpec((tm, HEAD_DIM), lambda b, t: (t, 0)),
        ],
        out_specs=[
            pl.BlockSpec((None, N_HEADS, tm, HEAD_DIM), lambda b, t: (b, 0, t, 0)),
            kv_spec, vt_spec,
        ],
        out_shape=[
            jax.ShapeDtypeStruct((bsz, N_HEADS, length, HEAD_DIM), BF16),
            kv_shape, vt_shape,
        ],
        scratch_shapes=[pltpu.VMEM((tm, D_MODEL), BF16)],
        **_call_options("qkv_latent"),
    )(x, mods, ng, w_qkv, q_g, k_g, head_mean, cosf, sinf)


def _kv_ctx(ctx, mods, ng, w_kv, k_g):
    bsz, length, _ = ctx.shape
    tm = length
    kv_shape = jax.ShapeDtypeStruct((bsz, N_KV_HEADS, length, HEAD_DIM), BF16)
    kv_spec = pl.BlockSpec((None, N_KV_HEADS, tm, HEAD_DIM), lambda b, t: (b, 0, t, 0))
    vt_shape = jax.ShapeDtypeStruct((bsz, N_KV_HEADS, HEAD_DIM + BF16_ROWS, length), BF16)
    vt_spec = pl.BlockSpec((None, N_KV_HEADS, HEAD_DIM + BF16_ROWS, tm), lambda b, t: (b, 0, 0, t))
    return pl.pallas_call(
        _kv_ctx_kernel,
        grid=(bsz, length // tm),
        in_specs=[
            pl.BlockSpec((None, tm, D_MODEL), lambda b, t: (b, t, 0)),
            _mod_spec(bsz),
            _const_spec((4, D_MODEL)),
            _const_spec((D_MODEL, 2 * MXU_COLS)),
            _const_spec((1, HEAD_DIM)),
        ],
        out_specs=[kv_spec, vt_spec],
        out_shape=[kv_shape, vt_shape],
        scratch_shapes=[pltpu.VMEM((tm, D_MODEL), BF16)],
        **_call_options("kv_ctx"),
    )(ctx, mods, ng, w_kv, k_g)


def _region(index, fn):
    pl.when(pl.program_id(0) >= -index)(fn)


def _attn_kernel(qa_ref, qb_ref, ka_ref, kca_ref, kb_ref, kcb_ref, vt_ref, vtc_ref, o_ref, *scratch):
    tq = qa_ref.shape[1] // 2
    n_latent = ka_ref.shape[1]
    n_keys = n_latent + kca_ref.shape[1]
    nq = HEADS_PER_KV * tq
    q4_s, s_s, p_s, m_s = (scratch[i * N_KV_HEADS:(i + 1) * N_KV_HEADS] for i in range(4))

    def chunks_of(k_ref, kc_ref):
        chunks = [(k_ref, vt_ref, slice(r, r + MXU_COLS), slice(r, r + MXU_COLS))
                  for r in range(0, n_latent, MXU_COLS)]
        chunks += [(kc_ref, vtc_ref, slice(r, r + MXU_COLS),
                    slice(n_latent + r, n_latent + r + MXU_COLS))
                   for r in range(0, n_keys - n_latent, MXU_COLS)]
        return chunks

    @pl.when(pl.program_id(0) == 0)
    def _():
        s_s[1][...] = jnp.zeros(s_s[1].shape, F32)
        m_s[1][...] = jnp.zeros(m_s[1].shape, F32)
        p_s[0][...] = jnp.ones(p_s[0].shape, BF16)

    def stage(j_scores, j_probs, j_pv, q_ref, q_rows, key_chunks, out_rows):
        for h in range(HEADS_PER_KV):
            q4_s[j_scores][h * tq:(h + 1) * tq, :] = q_ref[HEADS_PER_KV * j_scores + h, q_rows, :]
        m_probs = m_s[j_probs][...]
        m = None
        acc = None

        def store_head(acc, h):
            cols = slice(h * tq, (h + 1) * tq)
            o = acc[:HEAD_DIM, cols] / acc[HEAD_DIM:HEAD_DIM + 1, cols]
            head = HEADS_PER_KV * j_pv + h
            o_ref[out_rows, head * HEAD_DIM:(head + 1) * HEAD_DIM] = o.T.astype(BF16)

        pv_chunks = list(key_chunks)
        heads_left = list(range(HEADS_PER_KV))
        tiles = [slice(r, r + BF16_ROWS) for r in range(0, n_keys, BF16_ROWS)]
        pv_rows_per_score_row = vt_ref.shape[1] / MXU_COLS
        work_total = len(key_chunks) * (1.0 + pv_rows_per_score_row)
        work_done = 0.0
        tiles_done = 0
        for k_src, _, local, rows in key_chunks:
            s = lax.dot_general(k_src[j_scores, local, :], q4_s[j_scores][...],
                                (((1,), (1,)), ((), ())), preferred_element_type=F32)
            s_s[j_scores][rows, :] = s
            cm = jnp.max(s, axis=0, keepdims=True)
            m = cm if m is None else jnp.maximum(m, cm)
            work_done += 1.0
            if pv_chunks:
                for _, vt_src, pv_local, pv_rows in (
                        pv_chunks.pop(0) for _ in range(min(2, len(pv_chunks)))):
                    d = _dot(vt_src[j_pv, :, pv_local], p_s[j_pv][pv_rows, :])
                    acc = d if acc is None else acc + d
                    work_done += pv_rows_per_score_row
            elif heads_left:
                store_head(acc, heads_left.pop(0))
            tiles_until = round(len(tiles) * work_done / work_total)
            for tile in tiles[tiles_done:tiles_until]:
                p_s[j_probs][tile, :] = jnp.exp2(s_s[j_probs][tile, :] - m_probs).astype(BF16)
            tiles_done = tiles_until
        assert not pv_chunks and tiles_done == len(tiles)
        for h in heads_left:
            store_head(acc, h)
        m_s[j_scores][...] = jnp.broadcast_to(m, (BF16_ROWS, nq))

    first, second = slice(0, tq), slice(tq, 2 * tq)
    chunks_a, chunks_b = chunks_of(ka_ref, kca_ref), chunks_of(kb_ref, kcb_ref)
    _region(0, lambda: stage(0, 1, 0, qa_ref, second, chunks_a, first))
    _region(1, lambda: stage(1, 0, 1, qa_ref, second, chunks_a, first))
    _region(2, lambda: stage(0, 1, 0, qb_ref, first, chunks_b, second))
    _region(3, lambda: stage(1, 0, 1, qb_ref, first, chunks_b, second))


def _attention(q, k, kc, vt, vtc):
    bsz, _, length, _ = q.shape
    n_ctx = kc.shape[2]
    n_keys = length + n_ctx
    vt_rows = vt.shape[2]
    tq = QUERY_TILE
    nq = HEADS_PER_KV * tq
    per_batch = length // (2 * tq)
    n_blocks = bsz * per_batch

    def cur(s):
        return jnp.minimum(s, n_blocks - 1)

    def prev(s):
        return jnp.maximum(s - 1, 0)

    def q_spec(block):
        return pl.BlockSpec((None, N_HEADS, 2 * tq, HEAD_DIM),
                            lambda s: (block(s) // per_batch, 0, block(s) % per_batch, 0))

    def kv_spec(rows, cols, block):
        return pl.BlockSpec((None, N_KV_HEADS, rows, cols), lambda s: (block(s) // per_batch, 0, 0, 0))

    return pl.pallas_call(
        _attn_kernel,
        grid=(n_blocks + 1,),
        in_specs=[
            q_spec(prev), q_spec(cur),
            kv_spec(length, HEAD_DIM, prev), kv_spec(n_ctx, HEAD_DIM, prev),
            kv_spec(length, HEAD_DIM, cur), kv_spec(n_ctx, HEAD_DIM, cur),
            kv_spec(vt_rows, length, prev), kv_spec(vt_rows, n_ctx, prev),
        ],
        out_specs=pl.BlockSpec((None, 2 * tq, D_MODEL),
                               lambda s: (prev(s) // per_batch, prev(s) % per_batch, 0)),
        out_shape=jax.ShapeDtypeStruct((bsz, length, D_MODEL), BF16),
        scratch_shapes=(
            [pltpu.VMEM((nq, HEAD_DIM), BF16)] * N_KV_HEADS
            + [pltpu.VMEM((n_keys, nq), F32)] * N_KV_HEADS
            + [pltpu.VMEM((n_keys, nq), BF16)] * N_KV_HEADS
            + [pltpu.VMEM((BF16_ROWS, nq), F32)] * N_KV_HEADS),
        **_call_options("attention", grid_rank=1),
    )(q, q, k, kc, k, kc, vt, vtc)


def kernel(x, c, ctx, c_ctx, ada_w, ada_b, norm_g, mlp_w1, mlp_w2, a_w_in, a_ln_g, a_w_s, a_b_s, a_w_out,
           b_w_out, c_w_qkv, c_q_g, c_k_g, c_w_o):
    bsz = x.shape[0]
    attn_layers = [i for i in range(DEPTH) if i % N_MIXERS == 2]
    last_ctx_read = attn_layers[-1] if attn_layers else -1

    cond = jnp.zeros((COND_ROWS, D_MODEL), F32).at[:bsz].set(c).at[bsz].set(c_ctx)
    mods_all = _ada_all(cond, ada_w, ada_b).reshape(DEPTH, COND_ROWS, 6, D_MODEL)

    def big_weights(i):
        pairs = [(mlp_w1, i), (mlp_w2, i)]
        if i % N_MIXERS == 0:
            pairs += [(a_w_in, i // N_MIXERS), (a_w_out, i // N_MIXERS)]
        return pairs

    weights = [w_all[index].astype(BF16) for w_all, index in big_weights(0)]

    def ctx_tiles(a):
        return a.reshape(-1, TOKEN_TILE, a.shape[-1])

    for i in range(DEPTH):
        kind, j = i % N_MIXERS, i // N_MIXERS
        ctx_in = i <= last_ctx_read
        ctx_out = i < last_ctx_read
        mods = mods_all[i]
        ng = norm_g[i]
        w1, w2 = weights[:2]

        if kind == 0:
            w_in, w_post = weights[2:]
            ln_g = a_ln_g[j].reshape(1, A_WIDTH)
            w_s = a_w_s[j].astype(BF16)
            b_s = a_b_s[j].reshape(A_GROUPS, CHUNK, 1)
            z = _gmlp_head(x, mods, ng, w_in, ln_g, w_s, b_s)
            zc = _gmlp_head(ctx_tiles(ctx), mods, ng, w_in, ln_g, w_s, b_s, ctx_row=bsz) if ctx_out else None
        elif kind == 1:
            w_post = b_w_out[j].astype(BF16)
            z = _fourier_head_folded(x, mods, ng)
            zc = ctx_tiles(_fourier_head(ctx, mods, ng, bsz)) if ctx_out else None
        else:
            w_qkv = c_w_qkv[j].astype(BF16)
            w_post = c_w_o[j].astype(BF16)
            q_g = c_q_g[j].reshape(1, HEAD_DIM)
            k_g = c_k_g[j].reshape(1, HEAD_DIM)
            assert ctx_in and not ctx_out
            q, k, vt = _qkv_latent(x, mods, ng, w_qkv, q_g, k_g)
            kc, vtc = _kv_ctx(ctx, mods, ng, w_qkv[:, N_HEADS * HEAD_DIM:], k_g)
            z = _attention(q, k, kc, vt, vtc)
            zc = None

        cast_next = big_weights(i + 1) if i + 1 < DEPTH else ()
        x, next_weights = _tail(x, z, mods, ng, w_post, w1, w2, z_mirrored=(kind == 1),
                                cast_next=cast_next)
        if ctx_out:
            new_ctx, _ = _tail(ctx_tiles(ctx), zc, mods, ng, w_post, w1, w2, ctx_row=bsz)
            ctx = new_ctx.reshape(ctx.shape)
        weights = next_weights

    return x
```

```python
import functools

import numpy as np
import jax
import jax.numpy as jnp
from jax import lax
from jax.experimental import pallas as pl
from jax.experimental.pallas import tpu as pltpu

D_MODEL = 1024
DEPTH = 4
GRID_W = 64
N_MIXERS = 3
CHUNK = 128
A_WIDTH = 2 * D_MODEL
A_GROUPS = 8
A_GROUP_W = A_WIDTH // A_GROUPS
B_GROUPS = 4
B_GROUP_W = D_MODEL // B_GROUPS
HEAD_DIM = 128
N_HEADS = D_MODEL // HEAD_DIM
N_KV_HEADS = 2
HEADS_PER_KV = N_HEADS // N_KV_HEADS
ROPE_THETA = 10000.0
D_FF = 4 * D_MODEL
NORM_EPS = 1e-6
LN_EPS = 1e-5
LOG2_E = 1.4426950408889634

F32 = jnp.float32
BF16 = jnp.bfloat16

MXU_COLS = 256
ROW_BLOCK = 64
BF16_ROWS = 16
LANES = 128
SUBLANES = 8
COND_ROWS = 16
TOKEN_TILE = 512
TOKEN_BLOCK = 1024
QUERY_TILE = 128
ADA_COLS = 1536
MIB = 1024 * 1024
VMEM_MIB = {
    "ada_mod": 32, "gmlp_head": 56, "tail": 58, "fourier_fold": 32, "fourier_seq_fold": 40,
    "fourier_chan": 32, "fourier_seq": 48, "qkv_latent": 32, "kv_ctx": 32, "attention": 54,
}


def _dot(a, b):
    return jnp.dot(a, b, preferred_element_type=F32)


def _rms(x, g):
    ms = jnp.mean(x * x, axis=-1, keepdims=True)
    return x * lax.rsqrt(ms + NORM_EPS) * g


def _gelu_tanh(x):
    c = 2.0 * 0.7978845608028654 * LOG2_E
    z = x * (-c - (c * 0.044715) * (x * x))
    return x / (1.0 + jnp.exp2(z))


def _slab_specs(w, width, layer=None):
    k, n = w.shape[-2:]
    if layer is None:
        return [pl.BlockSpec((k, width), lambda b, t, c=c: (0, c), pipeline_mode=pl.Buffered(1))
                for c in range(n // width)]
    return [pl.BlockSpec((None, k, width), lambda b, t, c=c: (layer, 0, c), pipeline_mode=pl.Buffered(1))
            for c in range(n // width)]


def _const_spec(shape):
    zeros = (0,) * len(shape)
    return pl.BlockSpec(shape, lambda b, t: zeros, pipeline_mode=pl.Buffered(1))


def _mod_spec(ctx_row):
    if ctx_row is not None:
        return pl.BlockSpec((None, 6, D_MODEL), lambda b, t: (ctx_row, 0, 0))
    return pl.BlockSpec((None, 6, D_MODEL), lambda b, t: (b, 0, 0))


def _call_options(name, grid_rank=2):
    return dict(
        name=name,
        compiler_params=pltpu.CompilerParams(
            dimension_semantics=("arbitrary",) * grid_rank,
            vmem_limit_bytes=VMEM_MIB[name] * MIB))


def _modulated_norm_to(h_s, x_ref, mod_ref, ng_ref, start=0, stop=None):
    sh1 = mod_ref[0:1, :]
    sc1p = 1.0 + mod_ref[1:2, :]
    g0 = ng_ref[0:1, :]
    stop = x_ref.shape[0] if stop is None else stop
    for r in range(start, stop, ROW_BLOCK):
        rows = slice(r, r + ROW_BLOCK)
        h_s[rows, :] = (_rms(x_ref[rows, :], g0) * sc1p + sh1).astype(BF16)


def _ada_kernel(cond_ref, w_ref, b_ref, o_ref):
    a = cond_ref[...]
    a = a * jax.nn.sigmoid(a)
    o_ref[...] = _dot(a.astype(BF16), w_ref[...].astype(BF16)) + b_ref[...]


def _ada_all(cond, ada_w, ada_b):
    tn = ADA_COLS
    n_out = 6 * D_MODEL
    return pl.pallas_call(
        _ada_kernel,
        grid=(DEPTH, n_out // tn),
        in_specs=[
            pl.BlockSpec((COND_ROWS, D_MODEL), lambda i, n: (0, 0)),
            pl.BlockSpec((None, D_MODEL, tn), lambda i, n: (i, 0, n)),
            pl.BlockSpec((None, 1, tn), lambda i, n: (i, 0, n)),
        ],
        out_specs=pl.BlockSpec((None, COND_ROWS, tn), lambda i, n: (i, 0, n)),
        out_shape=jax.ShapeDtypeStruct((DEPTH, COND_ROWS, n_out), F32),
        **_call_options("ada_mod"),
    )(cond, ada_w, ada_b.reshape(DEPTH, 1, n_out))


def _gmlp_head_kernel(x_ref, mod_ref, ng_ref, lng_ref, ws_ref, bs_ref, *refs):
    width = 2 * MXU_COLS
    n_half = A_WIDTH // width
    win = refs[:2 * n_half]
    z_ref, h_s, u_s, v_s, vb_s, sum_s, sq_s = refs[2 * n_half:]
    tm = x_ref.shape[0]
    lng = lng_ref[...]
    half = tm // 2
    lanes = sum_s.shape[1]

    def proj(unit):
        c, a = unit
        return _dot(h_s[a * half:(a + 1) * half, :], win[c][...])

    def gelu_store(d, unit):
        c, a = unit
        is_v = c >= n_half
        dst, c = (v_s, c - n_half) if is_v else (u_s, c)
        for r in range(0, half, ROW_BLOCK):
            rows = slice(a * half + r, a * half + r + ROW_BLOCK)
            g = _gelu_tanh(d[r:r + ROW_BLOCK, :])
            dst[rows, c * width:(c + 1) * width] = g
            if is_v:
                parts = [g[:, i:i + lanes] for i in range(0, width, lanes)]
                p1 = sum(parts[1:], parts[0])
                p2 = sum([p * p for p in parts[1:]], parts[0] * parts[0])
                if c == 0:
                    sum_s[rows, :] = p1
                    sq_s[rows, :] = p2
                else:
                    sum_s[rows, :] += p1
                    sq_s[rows, :] += p2

    def layer_norm_rows(a, part):
        start = a * half + part * half // n_half
        for r in range(start, start + half // n_half, ROW_BLOCK // 2):
            rows = slice(r, r + ROW_BLOCK // 2)
            mu = jnp.sum(sum_s[rows, :], axis=-1, keepdims=True) * (1.0 / A_WIDTH)
            ex2 = jnp.sum(sq_s[rows, :], axis=-1, keepdims=True) * (1.0 / A_WIDTH)
            rstd = lax.rsqrt(jnp.maximum(ex2 - mu * mu, 0.0) + LN_EPS)
            vb_s[rows, :] = ((v_s[rows, :] - mu) * rstd * lng).astype(BF16)

    def gate_group(g):
        w = ws_ref[g]
        b = jnp.broadcast_to(bs_ref[g], (CHUNK, A_GROUP_W))
        cols = slice(g * A_GROUP_W, (g + 1) * A_GROUP_W)
        for n in range(tm // CHUNK):
            rows = slice(n * CHUNK, (n + 1) * CHUNK)
            sv = _dot(w, vb_s[rows, cols]) + b
            z_ref[rows, cols] = (u_s[rows, cols] * sv).astype(BF16)

    chunks = list(range(n_half, 2 * n_half)) + list(range(n_half))
    order = [(c, a) for c in chunks for a in range(2)]
    norms =[functools.partial(layer_norm_rows, a, part) for a in range(2) for part in range(n_half)]
    fillers = {2 * n_half + 1 + k: norms[k:k + 2] for k in range(0, len(norms), 2)}

    _modulated_norm_to(h_s, x_ref, mod_ref, ng_ref, 0, half)
    d = proj(order[0])
    _modulated_norm_to(h_s, x_ref, mod_ref, ng_ref, half, tm)
    for i in range(1, len(order) + 1):
        d_next = proj(order[i]) if i < len(order) else None
        gelu_store(d, order[i - 1])
        for filler in fillers.get(i - 1, []):
            filler()
        d = d_next
    for g in range(A_GROUPS):
        gate_group(g)


def _gmlp_head(x, mods, ng, w_in, ln_g, w_s, b_s, ctx_row=None):
    bsz, length, _ = x.shape
    tm = min(TOKEN_BLOCK, length)
    w_in_specs = _slab_specs(w_in, 2 * MXU_COLS)
    return pl.pallas_call(
        _gmlp_head_kernel,
        grid=(bsz, length // tm),
        in_specs=[
            pl.BlockSpec((None, tm, D_MODEL), lambda b, t: (b, t, 0)),
            _mod_spec(ctx_row),
            _const_spec((4, D_MODEL)),
            _const_spec((1, A_WIDTH)),
            _const_spec((A_GROUPS, CHUNK, CHUNK)),
            _const_spec((A_GROUPS, CHUNK, 1)),
        ] + w_in_specs,
        out_specs=pl.BlockSpec((None, tm, A_WIDTH), lambda b, t: (b, t, 0)),
        out_shape=jax.ShapeDtypeStruct((bsz, length, A_WIDTH), BF16),
        scratch_shapes=[
            pltpu.VMEM((tm, D_MODEL), BF16),
            pltpu.VMEM((tm, A_WIDTH), F32),
            pltpu.VMEM((tm, A_WIDTH), F32),
            pltpu.VMEM((tm, A_WIDTH), BF16),
            pltpu.VMEM((tm, LANES), F32),
            pltpu.VMEM((tm, LANES), F32),
        ],
        **_call_options("gmlp_head"),
    )(x, mods, ng, ln_g, w_s, b_s, *([w_in] * len(w_in_specs)))


def _interleave(dots, fillers):
    per = -(-len(fillers) // len(dots))
    for i, dot_fn in enumerate(dots):
        dot_fn()
        for filler in fillers[i * per:(i + 1) * per]:
            filler()


def _tail_kernel(x_ref, z_ref, mod_ref, ng_ref, *refs, sub, z_mirrored, n_cast):
    n_out = D_MODEL // MXU_COLS
    n_ff = D_FF // (2 * MXU_COLS)
    n_w = 2 * n_out + n_ff
    wp, w1, w2 = refs[:n_out], refs[n_out:n_out + n_ff], refs[n_out + n_ff:n_w]
    cast_src = refs[n_w:n_w + n_cast]
    o_ref = refs[n_w + n_cast]
    cast_dst = refs[n_w + n_cast + 1:n_w + 2 * n_cast + 1]
    scratch = refs[n_w + 2 * n_cast + 1:]
    for src, dst in zip(cast_src, cast_dst):
        dst[...] = src[...].astype(BF16)
    n_sub = x_ref.shape[0] // sub
    weights = (mod_ref, ng_ref, wp, w1, w2)
    pending = []
    for i in range(n_sub):
        zi = i
        if z_mirrored and n_sub > 1:
            second_half = pl.program_id(1) >= pl.num_programs(1) // 2
            zi = jnp.where(second_half, n_sub - 1 - i, i)
        rows = pl.ds(i * sub, sub)
        z_rows = rows if isinstance(zi, int) else pl.ds(pl.multiple_of(zi * sub, sub), sub)
        pending = _tail_subtile(x_ref.at[rows, :], z_ref.at[z_rows, :], *weights, o_ref.at[rows, :],
                                *scratch, pending=pending, defer_last=i + 1 < n_sub)


def _tail_subtile(x_ref, z_ref, mod_ref, ng_ref, wp, w1, w2, o_ref, *scratch, pending, defer_last):
    half = x_ref.shape[0] // 2
    y_s, hm_s, hid_s = scratch[0:2], scratch[2:4], scratch[4:6]
    g1 = mod_ref[2:3, :]
    sh2 = mod_ref[3:4, :]
    sc2p = 1.0 + mod_ref[4:5, :]
    g2 = mod_ref[5:6, :]
    ff_cols = 2 * MXU_COLS
    out_chunks = list(range(D_MODEL // MXU_COLS))
    ff_chunks = list(range(D_FF // ff_cols))
    blocks = list(range(0, half, ROW_BLOCK))

    def post_dot(a, n):
        y_s[a][:, n * MXU_COLS:(n + 1) * MXU_COLS] = _dot(z_ref[a * half:(a + 1) * half, :], wp[n][...])

    def norm1(a, r):
        rows = slice(a * half + r, a * half + r + ROW_BLOCK)
        x1 = x_ref[rows, :] + g1 * _rms(y_s[a][r:r + ROW_BLOCK, :], ng_ref[1:2, :])
        o_ref[rows, :] = x1
        hm_s[a][r:r + ROW_BLOCK, :] = (_rms(x1, ng_ref[2:3, :]) * sc2p + sh2).astype(BF16)

    def up_dot(a, c):
        t = jnp.maximum(_dot(hm_s[a][...], w1[c][...]), 0.0)
        hid_s[a][:, c * ff_cols:(c + 1) * ff_cols] = (t * t).astype(BF16)

    def down_dot(a, n):
        y_s[a][:, n * MXU_COLS:(n + 1) * MXU_COLS] = _dot(hid_s[a][...], w2[n][...])

    def norm3(a, r):
        rows = slice(a * half + r, a * half + r + ROW_BLOCK)
        o_ref[rows, :] = o_ref[rows, :] + g2 * _rms(y_s[a][r:r + ROW_BLOCK, :], ng_ref[3:4, :])

    def each(fn, a, items):
        return [functools.partial(fn, a, item) for item in items]

    _interleave(each(post_dot, 0, out_chunks), list(pending))
    _interleave(each(post_dot, 1, out_chunks), each(norm1, 0, blocks))
    _interleave(each(up_dot, 0, ff_chunks), each(norm1, 1, blocks))
    _interleave(each(up_dot, 1, ff_chunks), [])
    _interleave(each(down_dot, 0, out_chunks), [])
    _interleave(each(down_dot, 1, out_chunks), each(norm3, 0, blocks))
    last = each(norm3, 1, blocks)
    if defer_last:
        return last
    _interleave(last, [])
    return []


def _tail(x, z, mods, ng, w_post, w1, w2, ctx_row=None, z_mirrored=False, cast_next=()):
    bsz, length, _ = x.shape
    kz = z.shape[-1]
    sub = min(TOKEN_TILE, length)
    tm = min(TOKEN_BLOCK, length)
    w_post, post_index = w_post if isinstance(w_post, tuple) else (w_post, None)
    wp_specs = _slab_specs(w_post, MXU_COLS, post_index)
    w1_specs = _slab_specs(w1, 2 * MXU_COLS)
    w2_specs = _slab_specs(w2, MXU_COLS)
    n_steps = bsz * (length // tm)
    per_batch = length // tm
    cast_in_specs, cast_out_specs, cast_out_shapes = [], [], []
    for w_all, index in cast_next:
        k, n = w_all.shape[-2:]
        rows = k // n_steps
        cast_in_specs.append(pl.BlockSpec(
            (None, rows, n), lambda b, t, index=index: (index, b * per_batch + t, 0)))
        cast_out_specs.append(pl.BlockSpec((rows, n), lambda b, t: (b * per_batch + t, 0)))
        cast_out_shapes.append(jax.ShapeDtypeStruct((k, n), BF16))
    if z_mirrored:
        nt = length // 2 // tm
        z_spec = pl.BlockSpec(
            (None, None, tm, kz),
            lambda b, t: (b, t // nt, jnp.where(t < nt, t, 2 * nt - 1 - t), 0))
    else:
        z_spec = pl.BlockSpec((None, tm, kz), lambda b, t: (b, t, 0))
    out, *casts = pl.pallas_call(
        functools.partial(_tail_kernel, sub=sub, z_mirrored=z_mirrored, n_cast=len(cast_next)),
        grid=(bsz, length // tm),
        in_specs=[
            pl.BlockSpec((None, tm, D_MODEL), lambda b, t: (b, t, 0)),
            z_spec,
            _mod_spec(ctx_row),
            _const_spec((4, D_MODEL)),
        ] + wp_specs + w1_specs + w2_specs + cast_in_specs,
        out_specs=[pl.BlockSpec((None, tm, D_MODEL), lambda b, t: (b, t, 0))] + cast_out_specs,
        out_shape=[jax.ShapeDtypeStruct((bsz, length, D_MODEL), F32)] + cast_out_shapes,
        scratch_shapes=(
            [pltpu.VMEM((sub // 2, D_MODEL), F32)] * 2
            + [pltpu.VMEM((sub // 2, D_MODEL), BF16)] * 2
            + [pltpu.VMEM((sub // 2, D_FF), BF16)] * 2),
        **_call_options("tail"),
    )(x, z, mods, ng, *([w_post] * len(wp_specs)), *([w1] * len(w1_specs)),
      *([w2] * len(w2_specs)), *[w_all for w_all, _ in cast_next])
    return out, casts


def _fourier_chan_kernel(x_ref, mod_ref, ng_ref, cs_ref, ab_ref, h_s):
    _modulated_norm_to(h_s, x_ref, mod_ref, ng_ref)
    for g in range(B_GROUPS):
        cols = slice(g * B_GROUP_W, (g + 1) * B_GROUP_W)
        t = _dot(h_s[:, cols], cs_ref[...])
        ab_ref[0, :, cols] = t[:, :B_GROUP_W].astype(BF16)
        ab_ref[1, :, cols] = t[:, B_GROUP_W:].astype(BF16)


def _fourier_seq_kernel(cs_ref, ab_ref, y_ref):
    for n in range(D_MODEL // MXU_COLS):
        cols = slice(n * MXU_COLS, (n + 1) * MXU_COLS)
        y_ref[:, cols] = _dot(cs_ref[...], ab_ref[:, cols]).astype(BF16)


def _bf16_const(a):
    return jnp.asarray(np.asarray(a, np.float32)).astype(BF16)


def _dft_angles(n):
    idx = np.arange(n, dtype=np.int64)
    return (2.0 * np.pi / n) * ((idx[:, None] * idx[None, :]) % n)


def _dft_tables(length):
    ang_c = _dft_angles(B_GROUP_W)
    cs_chan = np.concatenate([np.cos(ang_c), np.sin(ang_c)], axis=1) * B_GROUP_W ** -0.5
    ang_l = _dft_angles(length)
    cs_seq = np.concatenate([np.cos(ang_l), -np.sin(ang_l)], axis=1) * length ** -0.5
    return _bf16_const(cs_chan), _bf16_const(cs_seq)


def _fourier_fold_kernel(xp_ref, xm_ref, x0_ref, mod_ref, ng_ref, cc_ref, sc_ref, rev_ref,
                         ab_ref, a0_ref, hp_s, hm_s):
    tm = xm_ref.shape[0]
    sh1 = mod_ref[0:1, :]
    sc1p = 1.0 + mod_ref[1:2, :]
    g0 = ng_ref[0:1, :]
    for r in range(0, tm, ROW_BLOCK):
        rows = slice(r, r + ROW_BLOCK)
        hp_s[rows, :] = _rms(xp_ref[0, r + 1:r + 1 + ROW_BLOCK, :], g0) * sc1p + sh1
        hm_s[rows, :] = (_rms(xm_ref[rows, :], g0) * sc1p + sh1).astype(BF16)
    h0 = (_rms(x0_ref[...], g0) * sc1p + sh1).astype(BF16)
    for g in range(B_GROUPS):
        cols = slice(g * B_GROUP_W, (g + 1) * B_GROUP_W)
        hm_rev = _dot(rev_ref[...], hm_s[:, cols])
        hp = hp_s[:, cols]
        ab_ref[0, :, cols] = _dot((hp + hm_rev).astype(BF16), cc_ref[...]).astype(BF16)
        ab_ref[1, :, cols] = _dot((hp - hm_rev).astype(BF16), sc_ref[...]).astype(BF16)
        a0_ref[:, cols] = _dot(h0[:, cols], cc_ref[...])


def _fourier_seq_fold_kernel(ct_ref, st_ref, ab_ref, a0_ref, perm_ref, y_ref, w_s, *, scale):
    tk = y_ref.shape[1]
    half = ct_ref.shape[1]
    for n in range(D_MODEL // MXU_COLS):
        cols = slice(n * MXU_COLS, (n + 1) * MXU_COLS)
        yc = _dot(ct_ref[...], ab_ref[0:half, cols]) + a0_ref[0:1, cols] * scale
        ys = _dot(st_ref[...], ab_ref[half:2 * half, cols])
        y_ref[0, :, cols] = (yc[:tk] - ys[:tk]).astype(BF16)
        w_s[:, cols] = (yc + ys).astype(BF16)
    first_row = (lax.broadcasted_iota(jnp.int32, (tk, 1), 0) == 0).astype(F32)
    for n in range(D_MODEL // MXU_COLS):
        cols = slice(n * MXU_COLS, (n + 1) * MXU_COLS)
        mirrored = _dot(perm_ref[...], w_s[0:tk, cols]) + first_row * w_s[tk:tk + 1, cols].astype(F32)
        y_ref[1, :, cols] = mirrored.astype(BF16)


def _seq_fold_tables(length, tk):
    half = length // 2
    nt = half // tk
    rows = tk + BF16_ROWS
    t = np.arange(1, half + 1, dtype=np.int64)
    k = np.arange(nt, dtype=np.int64)[:, None] * tk + np.arange(rows, dtype=np.int64)[None, :]
    ang = (2.0 * np.pi / length) * ((k[..., None] * t) % length)
    weight = np.where(t == half, 0.5, 1.0) * length ** -0.5
    return _bf16_const(np.cos(ang) * weight), _bf16_const(np.sin(ang) * weight)


def _fourier_head_folded(x, mods, ng):
    bsz, length, _ = x.shape
    tm = TOKEN_TILE
    half = length // 2
    nt = half // tm
    rows = tm + BF16_ROWS
    ang_c = _dft_angles(B_GROUP_W)
    cc = _bf16_const(np.cos(ang_c) * B_GROUP_W ** -0.5)
    sc = _bf16_const(np.sin(ang_c) * B_GROUP_W ** -0.5)
    rev = _bf16_const(np.arange(tm)[:, None] + np.arange(tm)[None, :] == tm - 1)
    ab, a0 = pl.pallas_call(
        _fourier_fold_kernel,
        grid=(bsz, nt),
        in_specs=[
            pl.BlockSpec((pl.Element(1), pl.Element(tm + SUBLANES), pl.Element(D_MODEL)),
                         lambda b, t: (b, t * tm, 0)),
            pl.BlockSpec((None, tm, D_MODEL), lambda b, t: (b, 2 * nt - 1 - t, 0)),
            pl.BlockSpec((None, SUBLANES, D_MODEL), lambda b, t: (b, 0, 0)),
            _mod_spec(None),
            _const_spec((4, D_MODEL)),
            _const_spec((B_GROUP_W, B_GROUP_W)),
            _const_spec((B_GROUP_W, B_GROUP_W)),
            _const_spec((tm, tm)),
        ],
        out_specs=[
            pl.BlockSpec((None, 2, tm, D_MODEL), lambda b, t: (b, 0, t, 0)),
            pl.BlockSpec((None, SUBLANES, D_MODEL), lambda b, t: (b, 0, 0)),
        ],
        out_shape=[
            jax.ShapeDtypeStruct((bsz, 2, half, D_MODEL), BF16),
            jax.ShapeDtypeStruct((bsz, SUBLANES, D_MODEL), F32),
        ],
        scratch_shapes=[pltpu.VMEM((tm, D_MODEL), F32), pltpu.VMEM((tm, D_MODEL), BF16)],
        **_call_options("fourier_fold"),
    )(x, x, x, mods, ng, cc, sc, rev)
    ct, st = _seq_fold_tables(length, tm)
    perm = _bf16_const(np.arange(tm)[:, None] + np.arange(tm)[None, :] == tm)
    return pl.pallas_call(
        functools.partial(_fourier_seq_fold_kernel, scale=length ** -0.5),
        grid=(bsz, nt),
        in_specs=[
            pl.BlockSpec((None, rows, half), lambda b, t: (t, 0, 0)),
            pl.BlockSpec((None, rows, half), lambda b, t: (t, 0, 0)),
            pl.BlockSpec((None, 2 * half, D_MODEL), lambda b, t: (b, 0, 0)),
            pl.BlockSpec((None, SUBLANES, D_MODEL), lambda b, t: (b, 0, 0)),
            _const_spec((tm, tm)),
        ],
        out_specs=pl.BlockSpec((None, 2, tm, D_MODEL), lambda b, t: (b, 0, t, 0)),
        out_shape=jax.ShapeDtypeStruct((bsz, 2, half, D_MODEL), BF16),
        scratch_shapes=[pltpu.VMEM((rows, D_MODEL), BF16)],
        **_call_options("fourier_seq_fold"),
    )(ct, st, ab.reshape(bsz, 2 * half, D_MODEL), a0, perm)


def _fourier_head(x, mods, ng, ctx_row):
    bsz, length, _ = x.shape
    tm = min(TOKEN_TILE, length)
    cs_chan, cs_seq = _dft_tables(length)
    ab = pl.pallas_call(
        _fourier_chan_kernel,
        grid=(bsz, length // tm),
        in_specs=[
            pl.BlockSpec((None, tm, D_MODEL), lambda b, t: (b, t, 0)),
            _mod_spec(ctx_row),
            _const_spec((4, D_MODEL)),
            _const_spec((B_GROUP_W, 2 * B_GROUP_W)),
        ],
        out_specs=pl.BlockSpec((None, 2, tm, D_MODEL), lambda b, t: (b, 0, t, 0)),
        out_shape=jax.ShapeDtypeStruct((bsz, 2, length, D_MODEL), BF16),
        scratch_shapes=[pltpu.VMEM((tm, D_MODEL), BF16)],
        **_call_options("fourier_chan"),
    )(x, mods, ng, cs_chan)
    ab = ab.reshape(bsz, 2 * length, D_MODEL)
    return pl.pallas_call(
        _fourier_seq_kernel,
        grid=(bsz, length // tm),
        in_specs=[
            pl.BlockSpec((tm, 2 * length), lambda b, t: (t, 0)),
            pl.BlockSpec((None, 2 * length, D_MODEL), lambda b, t: (b, 0, 0),
                         pipeline_mode=pl.Buffered(1)),
        ],
        out_specs=pl.BlockSpec((None, tm, D_MODEL), lambda b, t: (b, t, 0)),
        out_shape=jax.ShapeDtypeStruct((bsz, length, D_MODEL), BF16),
        **_call_options("fourier_seq"),
    )(cs_seq, ab)


def _head_rms(r, g):
    return r * lax.rsqrt(jnp.mean(r * r, axis=-1, keepdims=True) + NORM_EPS) * g


def _store_vt_ext(vt_ref, t, cols=slice(None)):
    tokens = t.shape[0]
    first_row = lax.broadcasted_iota(jnp.int32, (BF16_ROWS, tokens), 0) == 0
    for e in range(N_KV_HEADS):
        vt_ref[e, 0:HEAD_DIM, cols] = t[:, e * HEAD_DIM:(e + 1) * HEAD_DIM].T.astype(BF16)
        vt_ref[e, HEAD_DIM:HEAD_DIM + BF16_ROWS, cols] = first_row.astype(BF16)


def _qkv_kernel(x_ref, mod_ref, ng_ref, w_ref, qg_ref, kg_ref, mean_ref, cos_ref, sin_ref,
                q_ref, k_ref, vt_ref, h_s):
    sub = min(TOKEN_TILE, x_ref.shape[0])
    n_sub = x_ref.shape[0] // sub
    qg = qg_ref[...] * (HEAD_DIM ** -0.5 * LOG2_E)
    gains = [jnp.concatenate([qg, qg], axis=1)] * (N_HEADS // 2)
    gains.append(jnp.concatenate([kg_ref[...], kg_ref[...]], axis=1))
    outs = [(q_ref, 2 * p) for p in range(N_HEADS // 2)] + [(k_ref, 0)]
    n_pairs = len(outs)

    def normalise(t, p):
        sq = t * t
        hi = sq.astype(BF16)
        lo = (sq - hi.astype(F32)).astype(BF16)
        ms = _dot(hi, mean_ref[...]) + _dot(lo, mean_ref[...])
        return t * lax.rsqrt(ms + NORM_EPS) * gains[p]

    _modulated_norm_to(h_s, x_ref, mod_ref, ng_ref, 0, sub)
    for i in range(n_sub):
        rows = slice(i * sub, (i + 1) * sub)
        cosf = cos_ref[rows, :]
        sinf = sin_ref[rows, :]
        next_blocks = list(range((i + 1) * sub, (i + 2) * sub, ROW_BLOCK)) if i + 1 < n_sub else []
        per_pair = -(-len(next_blocks) // (n_pairs + 1))

        def project(p):
            return _dot(h_s[rows, :], w_ref[:, p * MXU_COLS:(p + 1) * MXU_COLS])

        def rope_store(rn, p):
            ref, first = outs[p]
            for e in range(2):
                r = rn[:, e * HEAD_DIM:(e + 1) * HEAD_DIM]
                ref[first + e, rows, :] = (r * cosf + pltpu.roll(r, HEAD_DIM // 2, 1) * sinf).astype(BF16)

        t_next = project(0)
        rn_prev = None
        for p in range(n_pairs + 1):
            t = t_next
            if p < n_pairs:
                t_next = project(p + 1)
            rn = normalise(t, p) if p < n_pairs else None
            if rn_prev is not None:
                rope_store(rn_prev, p - 1)
            rn_prev = rn
            for r in next_blocks[p * per_pair:(p + 1) * per_pair]:
                _modulated_norm_to(h_s, x_ref, mod_ref, ng_ref, r, r + ROW_BLOCK)
        _store_vt_ext(vt_ref, t, rows)


def _kv_ctx_kernel(x_ref, mod_ref, ng_ref, w_ref, kg_ref, k_ref, vt_ref, h_s):
    _modulated_norm_to(h_s, x_ref, mod_ref, ng_ref)
    t = _dot(h_s[...], w_ref[:, 0:MXU_COLS])
    for e in range(N_KV_HEADS):
        k_ref[e] = _head_rms(t[:, e * HEAD_DIM:(e + 1) * HEAD_DIM], kg_ref[...]).astype(BF16)
    _store_vt_ext(vt_ref, _dot(h_s[...], w_ref[:, MXU_COLS:2 * MXU_COLS]))


def _rope_tables(length):
    t = np.arange(length)
    row = (t // GRID_W).astype(np.float64)
    col = (t % GRID_W).astype(np.float64)
    n_freq = HEAD_DIM // 4
    inv = ROPE_THETA ** (-np.arange(n_freq, dtype=np.float64) / n_freq)
    ang = np.concatenate([row[:, None] * inv, col[:, None] * inv], axis=-1)
    cos, sin = np.cos(ang), np.sin(ang)
    return (jnp.asarray(np.concatenate([cos, cos], axis=-1), F32),
            jnp.asarray(np.concatenate([-sin, sin], axis=-1), F32))


def _qkv_latent(x, mods, ng, w_qkv, q_g, k_g):
    bsz, length, _ = x.shape
    tm = TOKEN_BLOCK
    cosf, sinf = _rope_tables(length)
    n_qkv = w_qkv.shape[-1]
    lane_head = np.arange(MXU_COLS) // HEAD_DIM
    head_mean = _bf16_const((lane_head[:, None] == lane_head[None, :]) * (1.0 / HEAD_DIM))
    kv_shape = jax.ShapeDtypeStruct((bsz, N_KV_HEADS, length, HEAD_DIM), BF16)
    kv_spec = pl.BlockSpec((None, N_KV_HEADS, tm, HEAD_DIM), lambda b, t: (b, 0, t, 0))
    vt_shape = jax.ShapeDtypeStruct((bsz, N_KV_HEADS, HEAD_DIM + BF16_ROWS, length), BF16)
    vt_spec = pl.BlockSpec((None, N_KV_HEADS, HEAD_DIM + BF16_ROWS, tm), lambda b, t: (b, 0, 0, t))
    return pl.pallas_call(
        _qkv_kernel,
        grid=(bsz, length // tm),
        in_specs=[
            pl.BlockSpec((None, tm, D_MODEL), lambda b, t: (b, t, 0)),
            _mod_spec(None),
            _const_spec((4, D_MODEL)),
            _const_spec((D_MODEL, n_qkv)),
            _const_spec((1, HEAD_DIM)),
            _const_spec((1, HEAD_DIM)),
            _const_spec((MXU_COLS, MXU_COLS)),
            pl.BlockSpec((tm, HEAD_DIM), lambda b, t: (t, 0)),
            pl.BlockSpec((tm, HEAD_DIM), lambda b, t: (t, 0)),
        ],
        out_specs=[
            pl.BlockSpec((None, N_HEADS, tm, HEAD_DIM), lambda b, t: (b, 0, t, 0)),
            kv_spec, vt_spec,
        ],
        out_shape=[
            jax.ShapeDtypeStruct((bsz, N_HEADS, length, HEAD_DIM), BF16),
            kv_shape, vt_shape,
        ],
        scratch_shapes=[pltpu.VMEM((tm, D_MODEL), BF16)],
        **_call_options("qkv_latent"),
    )(x, mods, ng, w_qkv, q_g, k_g, head_mean, cosf, sinf)


def _kv_ctx(ctx, mods, ng, w_kv, k_g):
    bsz, length, _ = ctx.shape
    tm = length
    kv_shape = jax.ShapeDtypeStruct((bsz, N_KV_HEADS, length, HEAD_DIM), BF16)
    kv_spec = pl.BlockSpec((None, N_KV_HEADS, tm, HEAD_DIM), lambda b, t: (b, 0, t, 0))
    vt_shape = jax.ShapeDtypeStruct((bsz, N_KV_HEADS, HEAD_DIM + BF16_ROWS, length), BF16)
    vt_spec = pl.BlockSpec((None, N_KV_HEADS, HEAD_DIM + BF16_ROWS, tm), lambda b, t: (b, 0, 0, t))
    return pl.pallas_call(
        _kv_ctx_kernel,
        grid=(bsz, length // tm),
        in_specs=[
            pl.BlockSpec((None, tm, D_MODEL), lambda b, t: (b, t, 0)),
            _mod_spec(bsz),
            _const_spec((4, D_MODEL)),
            _const_spec((D_MODEL, 2 * MXU_COLS)),
            _const_spec((1, HEAD_DIM)),
        ],
        out_specs=[kv_spec, vt_spec],
        out_shape=[kv_shape, vt_shape],
        scratch_shapes=[pltpu.VMEM((tm, D_MODEL), BF16)],
        **_call_options("kv_ctx"),
    )(ctx, mods, ng, w_kv, k_g)


def _region(index, fn):
    pl.when(pl.program_id(0) >= -index)(fn)


def _attn_kernel(qa_ref, qb_ref, ka_ref, kca_ref, kb_ref, kcb_ref, vt_ref, vtc_ref, o_ref, *scratch):
    tq = qa_ref.shape[1] // 2
    n_latent = ka_ref.shape[1]
    n_keys = n_latent + kca_ref.shape[1]
    nq = HEADS_PER_KV * tq
    q4_s, s_s, p_s, m_s = (scratch[i * N_KV_HEADS:(i + 1) * N_KV_HEADS] for i in range(4))

    def chunks_of(k_ref, kc_ref):
        chunks = [(k_ref, vt_ref, slice(r, r + MXU_COLS), slice(r, r + MXU_COLS))
                  for r in range(0, n_latent, MXU_COLS)]
        chunks += [(kc_ref, vtc_ref, slice(r, r + MXU_COLS),
                    slice(n_latent + r, n_latent + r + MXU_COLS))
                   for r in range(0, n_keys - n_latent, MXU_COLS)]
        return chunks

    @pl.when(pl.program_id(0) == 0)
    def _():
        s_s[1][...] = jnp.zeros(s_s[1].shape, F32)
        m_s[1][...] = jnp.zeros(m_s[1].shape, F32)
        p_s[0][...] = jnp.ones(p_s[0].shape, BF16)

    def stage(j_scores, j_probs, j_pv, q_ref, q_rows, key_chunks, out_rows):
        for h in range(HEADS_PER_KV):
            q4_s[j_scores][h * tq:(h + 1) * tq, :] = q_ref[HEADS_PER_KV * j_scores + h, q_rows, :]
        m_probs = m_s[j_probs][...]
        m = None
        acc = None

        def store_head(acc, h):
            cols = slice(h * tq, (h + 1) * tq)
            o = acc[:HEAD_DIM, cols] / acc[HEAD_DIM:HEAD_DIM + 1, cols]
            head = HEADS_PER_KV * j_pv + h
            o_ref[out_rows, head * HEAD_DIM:(head + 1) * HEAD_DIM] = o.T.astype(BF16)

        pv_chunks = list(key_chunks)
        heads_left = list(range(HEADS_PER_KV))
        tiles = [slice(r, r + BF16_ROWS) for r in range(0, n_keys, BF16_ROWS)]
        pv_rows_per_score_row = vt_ref.shape[1] / MXU_COLS
        work_total = len(key_chunks) * (1.0 + pv_rows_per_score_row)
        work_done = 0.0
        tiles_done = 0
        for k_src, _, local, rows in key_chunks:
            s = lax.dot_general(k_src[j_scores, local, :], q4_s[j_scores][...],
                                (((1,), (1,)), ((), ())), preferred_element_type=F32)
            s_s[j_scores][rows, :] = s
            cm = jnp.max(s, axis=0, keepdims=True)
            m = cm if m is None else jnp.maximum(m, cm)
            work_done += 1.0
            if pv_chunks:
                for _, vt_src, pv_local, pv_rows in (
                        pv_chunks.pop(0) for _ in range(min(2, len(pv_chunks)))):
                    d = _dot(vt_src[j_pv, :, pv_local], p_s[j_pv][pv_rows, :])
                    acc = d if acc is None else acc + d
                    work_done += pv_rows_per_score_row
            elif heads_left:
                store_head(acc, heads_left.pop(0))
            tiles_until = round(len(tiles) * work_done / work_total)
            for tile in tiles[tiles_done:tiles_until]:
                p_s[j_probs][tile, :] = jnp.exp2(s_s[j_probs][tile, :] - m_probs).astype(BF16)
            tiles_done = tiles_until
        assert not pv_chunks and tiles_done == len(tiles)
        for h in heads_left:
            store_head(acc, h)
        m_s[j_scores][...] = jnp.broadcast_to(m, (BF16_ROWS, nq))

    first, second = slice(0, tq), slice(tq, 2 * tq)
    chunks_a, chunks_b = chunks_of(ka_ref, kca_ref), chunks_of(kb_ref, kcb_ref)
    _region(0, lambda: stage(0, 1, 0, qa_ref, second, chunks_a, first))
    _region(1, lambda: stage(1, 0, 1, qa_ref, second, chunks_a, first))
    _region(2, lambda: stage(0, 1, 0, qb_ref, first, chunks_b, second))
    _region(3, lambda: stage(1, 0, 1, qb_ref, first, chunks_b, second))


def _attention(q, k, kc, vt, vtc):
    bsz, _, length, _ = q.shape
    n_ctx = kc.shape[2]
    n_keys = length + n_ctx
    vt_rows = vt.shape[2]
    tq = QUERY_TILE
    nq = HEADS_PER_KV * tq
    per_batch = length // (2 * tq)
    n_blocks = bsz * per_batch

    def cur(s):
        return jnp.minimum(s, n_blocks - 1)

    def prev(s):
        return jnp.maximum(s - 1, 0)

    def q_spec(block):
        return pl.BlockSpec((None, N_HEADS, 2 * tq, HEAD_DIM),
                            lambda s: (block(s) // per_batch, 0, block(s) % per_batch, 0))

    def kv_spec(rows, cols, block):
        return pl.BlockSpec((None, N_KV_HEADS, rows, cols), lambda s: (block(s) // per_batch, 0, 0, 0))

    return pl.pallas_call(
        _attn_kernel,
        grid=(n_blocks + 1,),
        in_specs=[
            q_spec(prev), q_spec(cur),
            kv_spec(length, HEAD_DIM, prev), kv_spec(n_ctx, HEAD_DIM, prev),
            kv_spec(length, HEAD_DIM, cur), kv_spec(n_ctx, HEAD_DIM, cur),
            kv_spec(vt_rows, length, prev), kv_spec(vt_rows, n_ctx, prev),
        ],
        out_specs=pl.BlockSpec((None, 2 * tq, D_MODEL),
                               lambda s: (prev(s) // per_batch, prev(s) % per_batch, 0)),
        out_shape=jax.ShapeDtypeStruct((bsz, length, D_MODEL), BF16),
        scratch_shapes=(
            [pltpu.VMEM((nq, HEAD_DIM), BF16)] * N_KV_HEADS
            + [pltpu.VMEM((n_keys, nq), F32)] * N_KV_HEADS
            + [pltpu.VMEM((n_keys, nq), BF16)] * N_KV_HEADS
            + [pltpu.VMEM((BF16_ROWS, nq), F32)] * N_KV_HEADS),
        **_call_options("attention", grid_rank=1),
    )(q, q, k, kc, k, kc, vt, vtc)


def kernel(x, c, ctx, c_ctx, ada_w, ada_b, norm_g, mlp_w1, mlp_w2, a_w_in, a_ln_g, a_w_s, a_b_s, a_w_out,
           b_w_out, c_w_qkv, c_q_g, c_k_g, c_w_o):
    bsz = x.shape[0]
    attn_layers = [i for i in range(DEPTH) if i % N_MIXERS == 2]
    last_ctx_read = attn_layers[-1] if attn_layers else -1

    cond = jnp.zeros((COND_ROWS, D_MODEL), F32).at[:bsz].set(c).at[bsz].set(c_ctx)
    mods_all = _ada_all(cond, ada_w, ada_b).reshape(DEPTH, COND_ROWS, 6, D_MODEL)

    def big_weights(i):
        pairs = [(mlp_w1, i), (mlp_w2, i)]
        if i % N_MIXERS == 0:
            pairs += [(a_w_in, i // N_MIXERS), (a_w_out, i // N_MIXERS)]
        return pairs

    weights = [w_all[index].astype(BF16) for w_all, index in big_weights(0)]

    def ctx_tiles(a):
        return a.reshape(-1, TOKEN_TILE, a.shape[-1])

    for i in range(DEPTH):
        kind, j = i % N_MIXERS, i // N_MIXERS
        ctx_in = i <= last_ctx_read
        ctx_out = i < last_ctx_read
        mods = mods_all[i]
        ng = norm_g[i]
        w1, w2 = weights[:2]

        if kind == 0:
            w_in, w_post = weights[2:]
            ln_g = a_ln_g[j].reshape(1, A_WIDTH)
            w_s = a_w_s[j].astype(BF16)
            b_s = a_b_s[j].reshape(A_GROUPS, CHUNK, 1)
            z = _gmlp_head(x, mods, ng, w_in, ln_g, w_s, b_s)
            zc = _gmlp_head(ctx_tiles(ctx), mods, ng, w_in, ln_g, w_s, b_s, ctx_row=bsz) if ctx_out else None
        elif kind == 1:
            w_post = b_w_out[j].astype(BF16)
            z = _fourier_head_folded(x, mods, ng)
            zc = ctx_tiles(_fourier_head(ctx, mods, ng, bsz)) if ctx_out else None
        else:
            w_qkv = c_w_qkv[j].astype(BF16)
            w_post = c_w_o[j].astype(BF16)
            q_g = c_q_g[j].reshape(1, HEAD_DIM)
            k_g = c_k_g[j].reshape(1, HEAD_DIM)
            assert ctx_in and not ctx_out
            q, k, vt = _qkv_latent(x, mods, ng, w_qkv, q_g, k_g)
            kc, vtc = _kv_ctx(ctx, mods, ng, w_qkv[:, N_HEADS * HEAD_DIM:], k_g)
            z = _attention(q, k, kc, vt, vtc)
            zc = None

        cast_next = big_weights(i + 1) if i + 1 < DEPTH else ()
        x, next_weights = _tail(x, z, mods, ng, w_post, w1, w2, z_mirrored=(kind == 1),
                                cast_next=cast_next)
        if ctx_out:
            new_ctx, _ = _tail(ctx_tiles(ctx), zc, mods, ng, w_post, w1, w2, ctx_row=bsz)
            ctx = new_ctx.reshape(ctx.shape)
        weights = next_weights

    return x
```

```python
import functools

import numpy as np
import jax
import jax.numpy as jnp
from jax import lax
from jax.experimental import pallas as pl
from jax.experimental.pallas import tpu as pltpu

D_MODEL = 1024
DEPTH = 4
GRID_W = 64
N_MIXERS = 3
CHUNK = 128
A_WIDTH = 2 * D_MODEL
A_GROUPS = 8
A_GROUP_W = A_WIDTH // A_GROUPS
B_GROUPS = 4
B_GROUP_W = D_MODEL // B_GROUPS
HEAD_DIM = 128
N_HEADS = D_MODEL // HEAD_DIM
N_KV_HEADS = 2
HEADS_PER_KV = N_HEADS // N_KV_HEADS
ROPE_THETA = 10000.0
D_FF = 4 * D_MODEL
NORM_EPS = 1e-6
LN_EPS = 1e-5
LOG2_E = 1.4426950408889634

F32 = jnp.float32
BF16 = jnp.bfloat16

MXU_COLS = 256
ROW_BLOCK = 64
BF16_ROWS = 16
LANES = 128
SUBLANES = 8
COND_ROWS = 16
TOKEN_TILE = 512
TOKEN_BLOCK = 1024
QUERY_TILE = 128
ADA_COLS = 1536
MIB = 1024 * 1024
VMEM_MIB = {
    "ada_mod": 32, "gmlp_head": 56, "tail": 58, "fourier_fold": 40, "fourier_seq_fold": 40,
    "fourier_chan": 32, "fourier_seq": 48, "qkv_latent": 32, "kv_ctx": 32, "attention": 54,
}


def _dot(a, b):
    return jnp.dot(a, b, preferred_element_type=F32)


def _rms(x, g):
    ms = jnp.mean(x * x, axis=-1, keepdims=True)
    return x * lax.rsqrt(ms + NORM_EPS) * g


def _gelu_tanh(x):
    c = 2.0 * 0.7978845608028654 * LOG2_E
    z = x * (-c - (c * 0.044715) * (x * x))
    return x / (1.0 + jnp.exp2(z))


def _slab_specs(w, width, layer=None):
    k, n = w.shape[-2:]
    if layer is None:
        return [pl.BlockSpec((k, width), lambda b, t, c=c: (0, c), pipeline_mode=pl.Buffered(1))
                for c in range(n // width)]
    return [pl.BlockSpec((None, k, width), lambda b, t, c=c: (layer, 0, c), pipeline_mode=pl.Buffered(1))
            for c in range(n // width)]


def _const_spec(shape):
    zeros = (0,) * len(shape)
    return pl.BlockSpec(shape, lambda b, t: zeros, pipeline_mode=pl.Buffered(1))


def _mod_spec(ctx_row):
    if ctx_row is not None:
        return pl.BlockSpec((None, 6, D_MODEL), lambda b, t: (ctx_row, 0, 0))
    return pl.BlockSpec((None, 6, D_MODEL), lambda b, t: (b, 0, 0))


def _call_options(name, grid_rank=2):
    return dict(
        name=name,
        compiler_params=pltpu.CompilerParams(
            dimension_semantics=("arbitrary",) * grid_rank,
            vmem_limit_bytes=VMEM_MIB[name] * MIB))


def _modulated_norm_to(h_s, x_ref, mod_ref, ng_ref, start=0, stop=None):
    sh1 = mod_ref[0:1, :]
    sc1p = 1.0 + mod_ref[1:2, :]
    g0 = ng_ref[0:1, :]
    stop = x_ref.shape[0] if stop is None else stop
    for r in range(start, stop, ROW_BLOCK):
        rows = slice(r, r + ROW_BLOCK)
        h_s[rows, :] = (_rms(x_ref[rows, :], g0) * sc1p + sh1).astype(BF16)


def _ada_kernel(cond_ref, w_ref, b_ref, o_ref):
    a = cond_ref[...]
    a = a * jax.nn.sigmoid(a)
    o_ref[...] = _dot(a.astype(BF16), w_ref[...].astype(BF16)) + b_ref[...]


def _ada_all(cond, ada_w, ada_b):
    tn = ADA_COLS
    n_out = 6 * D_MODEL
    return pl.pallas_call(
        _ada_kernel,
        grid=(DEPTH, n_out // tn),
        in_specs=[
            pl.BlockSpec((COND_ROWS, D_MODEL), lambda i, n: (0, 0)),
            pl.BlockSpec((None, D_MODEL, tn), lambda i, n: (i, 0, n)),
            pl.BlockSpec((None, 1, tn), lambda i, n: (i, 0, n)),
        ],
        out_specs=pl.BlockSpec((None, COND_ROWS, tn), lambda i, n: (i, 0, n)),
        out_shape=jax.ShapeDtypeStruct((DEPTH, COND_ROWS, n_out), F32),
        **_call_options("ada_mod"),
    )(cond, ada_w, ada_b.reshape(DEPTH, 1, n_out))


def _gmlp_head_kernel(x_ref, mod_ref, ng_ref, lng_ref, ws_ref, bs_ref, *refs):
    width = 2 * MXU_COLS
    n_half = A_WIDTH // width
    win = refs[:2 * n_half]
    z_ref, h_s, u_s, v_s, vb_s, sum_s, sq_s = refs[2 * n_half:]
    tm = x_ref.shape[0]
    lng = lng_ref[...]
    half = tm // 2
    lanes = sum_s.shape[1]

    def proj(unit):
        c, a = unit
        return _dot(h_s[a * half:(a + 1) * half, :], win[c][...])

    def gelu_store(d, unit):
        c, a = unit
        is_v = c >= n_half
        dst, c = (v_s, c - n_half) if is_v else (u_s, c)
        for r in range(0, half, ROW_BLOCK):
            rows = slice(a * half + r, a * half + r + ROW_BLOCK)
            g = _gelu_tanh(d[r:r + ROW_BLOCK, :])
            dst[rows, c * width:(c + 1) * width] = g
            if is_v:
                parts = [g[:, i:i + lanes] for i in range(0, width, lanes)]
                p1 = sum(parts[1:], parts[0])
                p2 = sum([p * p for p in parts[1:]], parts[0] * parts[0])
                if c == 0:
                    sum_s[rows, :] = p1
                    sq_s[rows, :] = p2
                else:
                    sum_s[rows, :] += p1
                    sq_s[rows, :] += p2

    def layer_norm_rows(a, part):
        start = a * half + part * half // n_half
        for r in range(start, start + half // n_half, ROW_BLOCK // 2):
            rows = slice(r, r + ROW_BLOCK // 2)
            mu = jnp.sum(sum_s[rows, :], axis=-1, keepdims=True) * (1.0 / A_WIDTH)
            ex2 = jnp.sum(sq_s[rows, :], axis=-1, keepdims=True) * (1.0 / A_WIDTH)
            rstd = lax.rsqrt(jnp.maximum(ex2 - mu * mu, 0.0) + LN_EPS)
            vb_s[rows, :] = ((v_s[rows, :] - mu) * rstd * lng).astype(BF16)

    def gate_group(g):
        w = ws_ref[g]
        b = jnp.broadcast_to(bs_ref[g], (CHUNK, A_GROUP_W))
        cols = slice(g * A_GROUP_W, (g + 1) * A_GROUP_W)
        for n in range(tm // CHUNK):
            rows = slice(n * CHUNK, (n + 1) * CHUNK)
            sv = _dot(w, vb_s[rows, cols]) + b
            z_ref[rows, cols] = (u_s[rows, cols] * sv).astype(BF16)

    chunks = list(range(n_half, 2 * n_half)) + list(range(n_half))
    order = [(c, a) for c in chunks for a in range(2)]
    norms =[functools.partial(layer_norm_rows, a, part) for a in range(2) for part in range(n_half)]
    fillers = {2 * n_half + 1 + k: norms[k:k + 2] for k in range(0, len(norms), 2)}

    _modulated_norm_to(h_s, x_ref, mod_ref, ng_ref, 0, half)
    d = proj(order[0])
    _modulated_norm_to(h_s, x_ref, mod_ref, ng_ref, half, tm)
    for i in range(1, len(order) + 1):
        d_next = proj(order[i]) if i < len(order) else None
        gelu_store(d, order[i - 1])
        for filler in fillers.get(i - 1, []):
            filler()
        d = d_next
    for g in range(A_GROUPS):
        gate_group(g)


def _gmlp_head(x, mods, ng, w_in, ln_g, w_s, b_s, ctx_row=None):
    bsz, length, _ = x.shape
    tm = min(TOKEN_BLOCK, length)
    w_in_specs = _slab_specs(w_in, 2 * MXU_COLS)
    return pl.pallas_call(
        _gmlp_head_kernel,
        grid=(bsz, length // tm),
        in_specs=[
            pl.BlockSpec((None, tm, D_MODEL), lambda b, t: (b, t, 0)),
            _mod_spec(ctx_row),
            _const_spec((4, D_MODEL)),
            _const_spec((1, A_WIDTH)),
            _const_spec((A_GROUPS, CHUNK, CHUNK)),
            _const_spec((A_GROUPS, CHUNK, 1)),
        ] + w_in_specs,
        out_specs=pl.BlockSpec((None, tm, A_WIDTH), lambda b, t: (b, t, 0)),
        out_shape=jax.ShapeDtypeStruct((bsz, length, A_WIDTH), BF16),
        scratch_shapes=[
            pltpu.VMEM((tm, D_MODEL), BF16),
            pltpu.VMEM((tm, A_WIDTH), F32),
            pltpu.VMEM((tm, A_WIDTH), F32),
            pltpu.VMEM((tm, A_WIDTH), BF16),
            pltpu.VMEM((tm, LANES), F32),
            pltpu.VMEM((tm, LANES), F32),
        ],
        **_call_options("gmlp_head"),
    )(x, mods, ng, ln_g, w_s, b_s, *([w_in] * len(w_in_specs)))


def _interleave(dots, fillers):
    per = -(-len(fillers) // len(dots))
    for i, dot_fn in enumerate(dots):
        dot_fn()
        for filler in fillers[i * per:(i + 1) * per]:
            filler()


def _tail_kernel(x_ref, z_ref, mod_ref, ng_ref, *refs, sub, z_mirrored, n_cast):
    n_out = D_MODEL // MXU_COLS
    n_ff = D_FF // (2 * MXU_COLS)
    n_w = 2 * n_out + n_ff
    wp, w1, w2 = refs[:n_out], refs[n_out:n_out + n_ff], refs[n_out + n_ff:n_w]
    cast_src = refs[n_w:n_w + n_cast]
    o_ref = refs[n_w + n_cast]
    cast_dst = refs[n_w + n_cast + 1:n_w + 2 * n_cast + 1]
    scratch = refs[n_w + 2 * n_cast + 1:]
    for src, dst in zip(cast_src, cast_dst):
        dst[...] = src[...].astype(BF16)
    n_sub = x_ref.shape[0] // sub
    weights = (mod_ref, ng_ref, wp, w1, w2)
    pending = []
    for i in range(n_sub):
        zi = i
        if z_mirrored and n_sub > 1:
            second_half = pl.program_id(1) >= pl.num_programs(1) // 2
            zi = jnp.where(second_half, n_sub - 1 - i, i)
        rows = pl.ds(i * sub, sub)
        z_rows = rows if isinstance(zi, int) else pl.ds(pl.multiple_of(zi * sub, sub), sub)
        pending = _tail_subtile(x_ref.at[rows, :], z_ref.at[z_rows, :], *weights, o_ref.at[rows, :],
                                *scratch, pending=pending, defer_last=i + 1 < n_sub)


def _tail_subtile(x_ref, z_ref, mod_ref, ng_ref, wp, w1, w2, o_ref, *scratch, pending, defer_last):
    half = x_ref.shape[0] // 2
    y_s, hm_s, hid_s = scratch[0:2], scratch[2:4], scratch[4:6]
    g1 = mod_ref[2:3, :]
    sh2 = mod_ref[3:4, :]
    sc2p = 1.0 + mod_ref[4:5, :]
    g2 = mod_ref[5:6, :]
    ff_cols = 2 * MXU_COLS
    out_chunks = list(range(D_MODEL // MXU_COLS))
    ff_chunks = list(range(D_FF // ff_cols))
    blocks = list(range(0, half, ROW_BLOCK))

    def post_dot(a, n):
        y_s[a][:, n * MXU_COLS:(n + 1) * MXU_COLS] = _dot(z_ref[a * half:(a + 1) * half, :], wp[n][...])

    def norm1(a, r):
        rows = slice(a * half + r, a * half + r + ROW_BLOCK)
        x1 = x_ref[rows, :] + g1 * _rms(y_s[a][r:r + ROW_BLOCK, :], ng_ref[1:2, :])
        o_ref[rows, :] = x1
        hm_s[a][r:r + ROW_BLOCK, :] = (_rms(x1, ng_ref[2:3, :]) * sc2p + sh2).astype(BF16)

    def up_dot(a, c):
        t = jnp.maximum(_dot(hm_s[a][...], w1[c][...]), 0.0)
        hid_s[a][:, c * ff_cols:(c + 1) * ff_cols] = (t * t).astype(BF16)

    def down_dot(a, n):
        y_s[a][:, n * MXU_COLS:(n + 1) * MXU_COLS] = _dot(hid_s[a][...], w2[n][...])

    def norm3(a, r):
        rows = slice(a * half + r, a * half + r + ROW_BLOCK)
        o_ref[rows, :] = o_ref[rows, :] + g2 * _rms(y_s[a][r:r + ROW_BLOCK, :], ng_ref[3:4, :])

    def each(fn, a, items):
        return [functools.partial(fn, a, item) for item in items]

    _interleave(each(post_dot, 0, out_chunks), list(pending))
    _interleave(each(post_dot, 1, out_chunks), each(norm1, 0, blocks))
    _interleave(each(up_dot, 0, ff_chunks), each(norm1, 1, blocks))
    _interleave(each(up_dot, 1, ff_chunks), [])
    _interleave(each(down_dot, 0, out_chunks), [])
    _interleave(each(down_dot, 1, out_chunks), each(norm3, 0, blocks))
    last = each(norm3, 1, blocks)
    if defer_last:
        return last
    _interleave(last, [])
    return []


def _tail(x, z, mods, ng, w_post, w1, w2, ctx_row=None, z_mirrored=False, cast_next=()):
    bsz, length, _ = x.shape
    kz = z.shape[-1]
    sub = min(TOKEN_TILE, length)
    tm = min(TOKEN_BLOCK, length)
    w_post, post_index = w_post if isinstance(w_post, tuple) else (w_post, None)
    wp_specs = _slab_specs(w_post, MXU_COLS, post_index)
    w1_specs = _slab_specs(w1, 2 * MXU_COLS)
    w2_specs = _slab_specs(w2, MXU_COLS)
    n_steps = bsz * (length // tm)
    per_batch = length // tm
    cast_in_specs, cast_out_specs, cast_out_shapes = [], [], []
    for w_all, index in cast_next:
        k, n = w_all.shape[-2:]
        rows = k // n_steps
        cast_in_specs.append(pl.BlockSpec(
            (None, rows, n), lambda b, t, index=index: (index, b * per_batch + t, 0)))
        cast_out_specs.append(pl.BlockSpec((rows, n), lambda b, t: (b * per_batch + t, 0)))
        cast_out_shapes.append(jax.ShapeDtypeStruct((k, n), BF16))
    if z_mirrored:
        nt = length // 2 // tm
        z_spec = pl.BlockSpec(
            (None, None, tm, kz),
            lambda b, t: (b, t // nt, jnp.where(t < nt, t, 2 * nt - 1 - t), 0))
    else:
        z_spec = pl.BlockSpec((None, tm, kz), lambda b, t: (b, t, 0))
    out, *casts = pl.pallas_call(
        functools.partial(_tail_kernel, sub=sub, z_mirrored=z_mirrored, n_cast=len(cast_next)),
        grid=(bsz, length // tm),
        in_specs=[
            pl.BlockSpec((None, tm, D_MODEL), lambda b, t: (b, t, 0)),
            z_spec,
            _mod_spec(ctx_row),
            _const_spec((4, D_MODEL)),
        ] + wp_specs + w1_specs + w2_specs + cast_in_specs,
        out_specs=[pl.BlockSpec((None, tm, D_MODEL), lambda b, t: (b, t, 0))] + cast_out_specs,
        out_shape=[jax.ShapeDtypeStruct((bsz, length, D_MODEL), F32)] + cast_out_shapes,
        scratch_shapes=(
            [pltpu.VMEM((sub // 2, D_MODEL), F32)] * 2
            + [pltpu.VMEM((sub // 2, D_MODEL), BF16)] * 2
            + [pltpu.VMEM((sub // 2, D_FF), BF16)] * 2),
        **_call_options("tail"),
    )(x, z, mods, ng, *([w_post] * len(wp_specs)), *([w1] * len(w1_specs)),
      *([w2] * len(w2_specs)), *[w_all for w_all, _ in cast_next])
    return out, casts


def _fourier_chan_kernel(x_ref, mod_ref, ng_ref, cs_ref, ab_ref, h_s):
    _modulated_norm_to(h_s, x_ref, mod_ref, ng_ref)
    for g in range(B_GROUPS):
        cols = slice(g * B_GROUP_W, (g + 1) * B_GROUP_W)
        t = _dot(h_s[:, cols], cs_ref[...])
        ab_ref[0, :, cols] = t[:, :B_GROUP_W].astype(BF16)
        ab_ref[1, :, cols] = t[:, B_GROUP_W:].astype(BF16)


def _fourier_seq_kernel(cs_ref, ab_ref, y_ref):
    for n in range(D_MODEL // MXU_COLS):
        cols = slice(n * MXU_COLS, (n + 1) * MXU_COLS)
        y_ref[:, cols] = _dot(cs_ref[...], ab_ref[:, cols]).astype(BF16)


def _bf16_const(a):
    return jnp.asarray(np.asarray(a, np.float32)).astype(BF16)


def _dft_angles(n):
    idx = np.arange(n, dtype=np.int64)
    return (2.0 * np.pi / n) * ((idx[:, None] * idx[None, :]) % n)


def _dft_tables(length):
    ang_c = _dft_angles(B_GROUP_W)
    cs_chan = np.concatenate([np.cos(ang_c), np.sin(ang_c)], axis=1) * B_GROUP_W ** -0.5
    ang_l = _dft_angles(length)
    cs_seq = np.concatenate([np.cos(ang_l), -np.sin(ang_l)], axis=1) * length ** -0.5
    return _bf16_const(cs_chan), _bf16_const(cs_seq)


def _fourier_fold_kernel(xp_ref, xm_ref, x0_ref, mod_ref, ng_ref, cc_ref, sc_ref, rev_ref,
                         ab_ref, a0_ref, hp_s, hm_s):
    tm = rev_ref.shape[0]
    n_tiles = xm_ref.shape[0] // tm
    sh1 = mod_ref[0:1, :]
    sc1p = 1.0 + mod_ref[1:2, :]
    g0 = ng_ref[0:1, :]

    def norm_block(i, r):
        rows = slice(i * tm + r, i * tm + r + ROW_BLOCK)
        hp_s[rows, :] = _rms(xp_ref[0, i * tm + r + 1:i * tm + r + 1 + ROW_BLOCK, :], g0) * sc1p + sh1
        mirror = (n_tiles - 1 - i) * tm + r
        hm_s[rows, :] = (_rms(xm_ref[mirror:mirror + ROW_BLOCK, :], g0) * sc1p + sh1).astype(BF16)

    blocks = list(range(0, tm, ROW_BLOCK))
    for r in blocks:
        norm_block(0, r)
    h0 = (_rms(x0_ref[...], g0) * sc1p + sh1).astype(BF16)
    per_group = -(-len(blocks) // B_GROUPS)
    for i in range(n_tiles):
        rows = slice(i * tm, (i + 1) * tm)
        for g in range(B_GROUPS):
            cols = slice(g * B_GROUP_W, (g + 1) * B_GROUP_W)
            hm_rev = _dot(rev_ref[...], hm_s[rows, cols])
            hp = hp_s[rows, cols]
            ab_ref[0, rows, cols] = _dot((hp + hm_rev).astype(BF16), cc_ref[...]).astype(BF16)
            ab_ref[1, rows, cols] = _dot((hp - hm_rev).astype(BF16), sc_ref[...]).astype(BF16)
            if i + 1 < n_tiles:
                for r in blocks[g * per_group:(g + 1) * per_group]:
                    norm_block(i + 1, r)
    for g in range(B_GROUPS):
        cols = slice(g * B_GROUP_W, (g + 1) * B_GROUP_W)
        a0_ref[:, cols] = _dot(h0[:, cols], cc_ref[...])


def _fourier_seq_fold_kernel(ct_ref, st_ref, ab_ref, a0_ref, perm_ref, y_ref, w_s, *, scale):
    tk = y_ref.shape[1]
    half = ct_ref.shape[1]
    for n in range(D_MODEL // MXU_COLS):
        cols = slice(n * MXU_COLS, (n + 1) * MXU_COLS)
        yc = _dot(ct_ref[...], ab_ref[0:half, cols]) + a0_ref[0:1, cols] * scale
        ys = _dot(st_ref[...], ab_ref[half:2 * half, cols])
        y_ref[0, :, cols] = (yc[:tk] - ys[:tk]).astype(BF16)
        w_s[:, cols] = (yc + ys).astype(BF16)
    first_row = (lax.broadcasted_iota(jnp.int32, (tk, 1), 0) == 0).astype(F32)
    for n in range(D_MODEL // MXU_COLS):
        cols = slice(n * MXU_COLS, (n + 1) * MXU_COLS)
        mirrored = _dot(perm_ref[...], w_s[0:tk, cols]) + first_row * w_s[tk:tk + 1, cols].astype(F32)
        y_ref[1, :, cols] = mirrored.astype(BF16)


def _seq_fold_tables(length, tk):
    half = length // 2
    nt = half // tk
    rows = tk + BF16_ROWS
    t = np.arange(1, half + 1, dtype=np.int64)
    k = np.arange(nt, dtype=np.int64)[:, None] * tk + np.arange(rows, dtype=np.int64)[None, :]
    ang = (2.0 * np.pi / length) * ((k[..., None] * t) % length)
    weight = np.where(t == half, 0.5, 1.0) * length ** -0.5
    return _bf16_const(np.cos(ang) * weight), _bf16_const(np.sin(ang) * weight)


def _fourier_head_folded(x, mods, ng):
    bsz, length, _ = x.shape
    tm = TOKEN_TILE
    half = length // 2
    nt = half // tm
    rows = tm + BF16_ROWS
    ang_c = _dft_angles(B_GROUP_W)
    cc = _bf16_const(np.cos(ang_c) * B_GROUP_W ** -0.5)
    sc = _bf16_const(np.sin(ang_c) * B_GROUP_W ** -0.5)
    rev = _bf16_const(np.arange(tm)[:, None] + np.arange(tm)[None, :] == tm - 1)
    fold_rows = TOKEN_BLOCK
    n_fold = half // fold_rows
    ab, a0 = pl.pallas_call(
        _fourier_fold_kernel,
        grid=(bsz, n_fold),
        in_specs=[
            pl.BlockSpec((pl.Element(1), pl.Element(fold_rows + SUBLANES), pl.Element(D_MODEL)),
                         lambda b, t: (b, t * fold_rows, 0)),
            pl.BlockSpec((None, fold_rows, D_MODEL), lambda b, t: (b, 2 * n_fold - 1 - t, 0)),
            pl.BlockSpec((None, SUBLANES, D_MODEL), lambda b, t: (b, 0, 0)),
            _mod_spec(None),
            _const_spec((4, D_MODEL)),
            _const_spec((B_GROUP_W, B_GROUP_W)),
            _const_spec((B_GROUP_W, B_GROUP_W)),
            _const_spec((tm, tm)),
        ],
        out_specs=[
            pl.BlockSpec((None, 2, fold_rows, D_MODEL), lambda b, t: (b, 0, t, 0)),
            pl.BlockSpec((None, SUBLANES, D_MODEL), lambda b, t: (b, 0, 0)),
        ],
        out_shape=[
            jax.ShapeDtypeStruct((bsz, 2, half, D_MODEL), BF16),
            jax.ShapeDtypeStruct((bsz, SUBLANES, D_MODEL), F32),
        ],
        scratch_shapes=[pltpu.VMEM((fold_rows, D_MODEL), F32), pltpu.VMEM((fold_rows, D_MODEL), BF16)],
        **_call_options("fourier_fold"),
    )(x, x, x, mods, ng, cc, sc, rev)
    ct, st = _seq_fold_tables(length, tm)
    perm = _bf16_const(np.arange(tm)[:, None] + np.arange(tm)[None, :] == tm)
    return pl.pallas_call(
        functools.partial(_fourier_seq_fold_kernel, scale=length ** -0.5),
        grid=(bsz, nt),
        in_specs=[
            pl.BlockSpec((None, rows, half), lambda b, t: (t, 0, 0)),
            pl.BlockSpec((None, rows, half), lambda b, t: (t, 0, 0)),
            pl.BlockSpec((None, 2 * half, D_MODEL), lambda b, t: (b, 0, 0)),
            pl.BlockSpec((None, SUBLANES, D_MODEL), lambda b, t: (b, 0, 0)),
            _const_spec((tm, tm)),
        ],
        out_specs=pl.BlockSpec((None, 2, tm, D_MODEL), lambda b, t: (b, 0, t, 0)),
        out_shape=jax.ShapeDtypeStruct((bsz, 2, half, D_MODEL), BF16),
        scratch_shapes=[pltpu.VMEM((rows, D_MODEL), BF16)],
        **_call_options("fourier_seq_fold"),
    )(ct, st, ab.reshape(bsz, 2 * half, D_MODEL), a0, perm)


def _fourier_head(x, mods, ng, ctx_row):
    bsz, length, _ = x.shape
    tm = min(TOKEN_TILE, length)
    cs_chan, cs_seq = _dft_tables(length)
    ab = pl.pallas_call(
        _fourier_chan_kernel,
        grid=(bsz, length // tm),
        in_specs=[
            pl.BlockSpec((None, tm, D_MODEL), lambda b, t: (b, t, 0)),
            _mod_spec(ctx_row),
            _const_spec((4, D_MODEL)),
            _const_spec((B_GROUP_W, 2 * B_GROUP_W)),
        ],
        out_specs=pl.BlockSpec((None, 2, tm, D_MODEL), lambda b, t: (b, 0, t, 0)),
        out_shape=jax.ShapeDtypeStruct((bsz, 2, length, D_MODEL), BF16),
        scratch_shapes=[pltpu.VMEM((tm, D_MODEL), BF16)],
        **_call_options("fourier_chan"),
    )(x, mods, ng, cs_chan)
    ab = ab.reshape(bsz, 2 * length, D_MODEL)
    return pl.pallas_call(
        _fourier_seq_kernel,
        grid=(bsz, length // tm),
        in_specs=[
            pl.BlockSpec((tm, 2 * length), lambda b, t: (t, 0)),
            pl.BlockSpec((None, 2 * length, D_MODEL), lambda b, t: (b, 0, 0),
                         pipeline_mode=pl.Buffered(1)),
        ],
        out_specs=pl.BlockSpec((None, tm, D_MODEL), lambda b, t: (b, t, 0)),
        out_shape=jax.ShapeDtypeStruct((bsz, length, D_MODEL), BF16),
        **_call_options("fourier_seq"),
    )(cs_seq, ab)


def _head_rms(r, g):
    return r * lax.rsqrt(jnp.mean(r * r, axis=-1, keepdims=True) + NORM_EPS) * g


def _store_vt_ext(vt_ref, t, cols=slice(None)):
    tokens = t.shape[0]
    first_row = lax.broadcasted_iota(jnp.int32, (BF16_ROWS, tokens), 0) == 0
    for e in range(N_KV_HEADS):
        vt_ref[e, 0:HEAD_DIM, cols] = t[:, e * HEAD_DIM:(e + 1) * HEAD_DIM].T.astype(BF16)
        vt_ref[e, HEAD_DIM:HEAD_DIM + BF16_ROWS, cols] = first_row.astype(BF16)


def _qkv_kernel(x_ref, mod_ref, ng_ref, w_ref, qg_ref, kg_ref, mean_ref, cos_ref, sin_ref,
                q_ref, k_ref, vt_ref, h_s):
    sub = min(TOKEN_TILE, x_ref.shape[0])
    n_sub = x_ref.shape[0] // sub
    qg = qg_ref[...] * (HEAD_DIM ** -0.5 * LOG2_E)
    gains = [jnp.concatenate([qg, qg], axis=1)] * (N_HEADS // 2)
    gains.append(jnp.concatenate([kg_ref[...], kg_ref[...]], axis=1))
    outs = [(q_ref, 2 * p) for p in range(N_HEADS // 2)] + [(k_ref, 0)]
    n_pairs = len(outs)

    def normalise(t, p):
        sq = t * t
        hi = sq.astype(BF16)
        lo = (sq - hi.astype(F32)).astype(BF16)
        ms = _dot(hi, mean_ref[...]) + _dot(lo, mean_ref[...])
        return t * lax.rsqrt(ms + NORM_EPS) * gains[p]

    _modulated_norm_to(h_s, x_ref, mod_ref, ng_ref, 0, sub)
    for i in range(n_sub):
        rows = slice(i * sub, (i + 1) * sub)
        cosf = cos_ref[rows, :]
        sinf = sin_ref[rows, :]
        next_blocks = list(range((i + 1) * sub, (i + 2) * sub, ROW_BLOCK)) if i + 1 < n_sub else []
        per_pair = -(-len(next_blocks) // (n_pairs + 1))

        def project(p):
            return _dot(h_s[rows, :], w_ref[:, p * MXU_COLS:(p + 1) * MXU_COLS])

        def rope_store(rn, p):
            ref, first = outs[p]
            for e in range(2):
                r = rn[:, e * HEAD_DIM:(e + 1) * HEAD_DIM]
                ref[first + e, rows, :] = (r * cosf + pltpu.roll(r, HEAD_DIM // 2, 1) * sinf).astype(BF16)

        t_next = project(0)
        rn_prev = None
        for p in range(n_pairs + 1):
            t = t_next
            if p < n_pairs:
                t_next = project(p + 1)
            rn = normalise(t, p) if p < n_pairs else None
            if rn_prev is not None:
                rope_store(rn_prev, p - 1)
            rn_prev = rn
            for r in next_blocks[p * per_pair:(p + 1) * per_pair]:
                _modulated_norm_to(h_s, x_ref, mod_ref, ng_ref, r, r + ROW_BLOCK)
        _store_vt_ext(vt_ref, t, rows)


def _kv_ctx_kernel(x_ref, mod_ref, ng_ref, w_ref, kg_ref, k_ref, vt_ref, h_s):
    _modulated_norm_to(h_s, x_ref, mod_ref, ng_ref)
    t = _dot(h_s[...], w_ref[:, 0:MXU_COLS])
    for e in range(N_KV_HEADS):
        k_ref[e] = _head_rms(t[:, e * HEAD_DIM:(e + 1) * HEAD_DIM], kg_ref[...]).astype(BF16)
    _store_vt_ext(vt_ref, _dot(h_s[...], w_ref[:, MXU_COLS:2 * MXU_COLS]))


def _rope_tables(length):
    t = np.arange(length)
    row = (t // GRID_W).astype(np.float64)
    col = (t % GRID_W).astype(np.float64)
    n_freq = HEAD_DIM // 4
    inv = ROPE_THETA ** (-np.arange(n_freq, dtype=np.float64) / n_freq)
    ang = np.concatenate([row[:, None] * inv, col[:, None] * inv], axis=-1)
    cos, sin = np.cos(ang), np.sin(ang)
    return (jnp.asarray(np.concatenate([cos, cos], axis=-1), F32),
            jnp.asarray(np.concatenate([-sin, sin], axis=-1), F32))


def _qkv_latent(x, mods, ng, w_qkv, q_g, k_g):
    bsz, length, _ = x.shape
    tm = TOKEN_BLOCK
    cosf, sinf = _rope_tables(length)
    n_qkv = w_qkv.shape[-1]
    lane_head = np.arange(MXU_COLS) // HEAD_DIM
    head_mean = _bf16_const((lane_head[:, None] == lane_head[None, :]) * (1.0 / HEAD_DIM))
    kv_shape = jax.ShapeDtypeStruct((bsz, N_KV_HEADS, length, HEAD_DIM), BF16)
    kv_spec = pl.BlockSpec((None, N_KV_HEADS, tm, HEAD_DIM), lambda b, t: (b, 0, t, 0))
    vt_shape = jax.ShapeDtypeStruct((bsz, N_KV_HEADS, HEAD_DIM + BF16_ROWS, length), BF16)
    vt_spec = pl.BlockSpec((None, N_KV_HEADS, HEAD_DIM + BF16_ROWS, tm), lambda b, t: (b, 0, 0, t))
    return pl.pallas_call(
        _qkv_kernel,
        grid=(bsz, length // tm),
        in_specs=[
            pl.BlockSpec((None, tm, D_MODEL), lambda b, t: (b, t, 0)),
            _mod_spec(None),
            _const_spec((4, D_MODEL)),
            _const_spec((D_MODEL, n_qkv)),
            _const_spec((1, HEAD_DIM)),
            _const_spec((1, HEAD_DIM)),
            _const_spec((MXU_COLS, MXU_COLS)),
            pl.BlockSpec((tm, HEAD_DIM), lambda b, t: (t, 0)),
            pl.BlockSpec((tm, HEAD_DIM), lambda b, t: (t, 0)),
        ],
        out_specs=[
            pl.BlockSpec((None, N_HEADS, tm, HEAD_DIM), lambda b, t: (b, 0, t, 0)),
            kv_spec, vt_spec,
        ],
        out_shape=[
            jax.ShapeDtypeStruct((bsz, N_HEADS, length, HEAD_DIM), BF16),
            kv_shape, vt_shape,
        ],
        scratch_shapes=[pltpu.VMEM((tm, D_MODEL), BF16)],
        **_call_options("qkv_latent"),
    )(x, mods, ng, w_qkv, q_g, k_g, head_mean, cosf, sinf)


def _kv_ctx(ctx, mods, ng, w_kv, k_g):
    bsz, length, _ = ctx.shape
    tm = length
    kv_shape = jax.ShapeDtypeStruct((bsz, N_KV_HEADS, length, HEAD_DIM), BF16)
    kv_spec = pl.BlockSpec((None, N_KV_HEADS, tm, HEAD_DIM), lambda b, t: (b, 0, t, 0))
    vt_shape = jax.ShapeDtypeStruct((bsz, N_KV_HEADS, HEAD_DIM + BF16_ROWS, length), BF16)
    vt_spec = pl.BlockSpec((None, N_KV_HEADS, HEAD_DIM + BF16_ROWS, tm), lambda b, t: (b, 0, 0, t))
    return pl.pallas_call(
        _kv_ctx_kernel,
        grid=(bsz, length // tm),
        in_specs=[
            pl.BlockSpec((None, tm, D_MODEL), lambda b, t: (b, t, 0)),
            _mod_spec(bsz),
            _const_spec((4, D_MODEL)),
            _const_spec((D_MODEL, 2 * MXU_COLS)),
            _const_spec((1, HEAD_DIM)),
        ],
        out_specs=[kv_spec, vt_spec],
        out_shape=[kv_shape, vt_shape],
        scratch_shapes=[pltpu.VMEM((tm, D_MODEL), BF16)],
        **_call_options("kv_ctx"),
    )(ctx, mods, ng, w_kv, k_g)


def _region(index, fn):
    pl.when(pl.program_id(0) >= -index)(fn)


def _attn_kernel(qa_ref, qb_ref, ka_ref, kca_ref, kb_ref, kcb_ref, vt_ref, vtc_ref, o_ref, *scratch):
    tq = qa_ref.shape[1] // 2
    n_latent = ka_ref.shape[1]
    n_keys = n_latent + kca_ref.shape[1]
    nq = HEADS_PER_KV * tq
    q4_s, s_s, p_s, m_s = (scratch[i * N_KV_HEADS:(i + 1) * N_KV_HEADS] for i in range(4))

    def chunks_of(k_ref, kc_ref):
        chunks = [(k_ref, vt_ref, slice(r, r + MXU_COLS), slice(r, r + MXU_COLS))
                  for r in range(0, n_latent, MXU_COLS)]
        chunks += [(kc_ref, vtc_ref, slice(r, r + MXU_COLS),
                    slice(n_latent + r, n_latent + r + MXU_COLS))
                   for r in range(0, n_keys - n_latent, MXU_COLS)]
        return chunks

    @pl.when(pl.program_id(0) == 0)
    def _():
        s_s[1][...] = jnp.zeros(s_s[1].shape, F32)
        m_s[1][...] = jnp.zeros(m_s[1].shape, F32)
        p_s[0][...] = jnp.ones(p_s[0].shape, BF16)

    def stage(j_scores, j_probs, j_pv, q_ref, q_rows, key_chunks, out_rows):
        for h in range(HEADS_PER_KV):
            q4_s[j_scores][h * tq:(h + 1) * tq, :] = q_ref[HEADS_PER_KV * j_scores + h, q_rows, :]
        m_probs = m_s[j_probs][...]
        m = None
        acc = None

        def store_head(acc, h):
            cols = slice(h * tq, (h + 1) * tq)
            o = acc[:HEAD_DIM, cols] / acc[HEAD_DIM:HEAD_DIM + 1, cols]
            head = HEADS_PER_KV * j_pv + h
            o_ref[out_rows, head * HEAD_DIM:(head + 1) * HEAD_DIM] = o.T.astype(BF16)

        pv_chunks = list(key_chunks)
        heads_left = list(range(HEADS_PER_KV))
        tiles = [slice(r, r + BF16_ROWS) for r in range(0, n_keys, BF16_ROWS)]
        pv_rows_per_score_row = vt_ref.shape[1] / MXU_COLS
        work_total = len(key_chunks) * (1.0 + pv_rows_per_score_row)
        work_done = 0.0
        tiles_done = 0
        for k_src, _, local, rows in key_chunks:
            s = lax.dot_general(k_src[j_scores, local, :], q4_s[j_scores][...],
                                (((1,), (1,)), ((), ())), preferred_element_type=F32)
            s_s[j_scores][rows, :] = s
            cm = jnp.max(s, axis=0, keepdims=True)
            m = cm if m is None else jnp.maximum(m, cm)
            work_done += 1.0
            if pv_chunks:
                for _, vt_src, pv_local, pv_rows in (
                        pv_chunks.pop(0) for _ in range(min(2, len(pv_chunks)))):
                    d = _dot(vt_src[j_pv, :, pv_local], p_s[j_pv][pv_rows, :])
                    acc = d if acc is None else acc + d
                    work_done += pv_rows_per_score_row
            elif heads_left:
                store_head(acc, heads_left.pop(0))
            tiles_until = round(len(tiles) * work_done / work_total)
            for tile in tiles[tiles_done:tiles_until]:
                p_s[j_probs][tile, :] = jnp.exp2(s_s[j_probs][tile, :] - m_probs).astype(BF16)
            tiles_done = tiles_until
        assert not pv_chunks and tiles_done == len(tiles)
        for h in heads_left:
            store_head(acc, h)
        m_s[j_scores][...] = jnp.broadcast_to(m, (BF16_ROWS, nq))

    first, second = slice(0, tq), slice(tq, 2 * tq)
    chunks_a, chunks_b = chunks_of(ka_ref, kca_ref), chunks_of(kb_ref, kcb_ref)
    _region(0, lambda: stage(0, 1, 0, qa_ref, second, chunks_a, first))
    _region(1, lambda: stage(1, 0, 1, qa_ref, second, chunks_a, first))
    _region(2, lambda: stage(0, 1, 0, qb_ref, first, chunks_b, second))
    _region(3, lambda: stage(1, 0, 1, qb_ref, first, chunks_b, second))


def _attention(q, k, kc, vt, vtc):
    bsz, _, length, _ = q.shape
    n_ctx = kc.shape[2]
    n_keys = length + n_ctx
    vt_rows = vt.shape[2]
    tq = QUERY_TILE
    nq = HEADS_PER_KV * tq
    per_batch = length // (2 * tq)
    n_blocks = bsz * per_batch

    def cur(s):
        return jnp.minimum(s, n_blocks - 1)

    def prev(s):
        return jnp.maximum(s - 1, 0)

    def q_spec(block):
        return pl.BlockSpec((None, N_HEADS, 2 * tq, HEAD_DIM),
                            lambda s: (block(s) // per_batch, 0, block(s) % per_batch, 0))

    def kv_spec(rows, cols, block):
        return pl.BlockSpec((None, N_KV_HEADS, rows, cols), lambda s: (block(s) // per_batch, 0, 0, 0))

    return pl.pallas_call(
        _attn_kernel,
        grid=(n_blocks + 1,),
        in_specs=[
            q_spec(prev), q_spec(cur),
            kv_spec(length, HEAD_DIM, prev), kv_spec(n_ctx, HEAD_DIM, prev),
            kv_spec(length, HEAD_DIM, cur), kv_spec(n_ctx, HEAD_DIM, cur),
            kv_spec(vt_rows, length, prev), kv_spec(vt_rows, n_ctx, prev),
        ],
        out_specs=pl.BlockSpec((None, 2 * tq, D_MODEL),
                               lambda s: (prev(s) // per_batch, prev(s) % per_batch, 0)),
        out_shape=jax.ShapeDtypeStruct((bsz, length, D_MODEL), BF16),
        scratch_shapes=(
            [pltpu.VMEM((nq, HEAD_DIM), BF16)] * N_KV_HEADS
            + [pltpu.VMEM((n_keys, nq), F32)] * N_KV_HEADS
            + [pltpu.VMEM((n_keys, nq), BF16)] * N_KV_HEADS
            + [pltpu.VMEM((BF16_ROWS, nq), F32)] * N_KV_HEADS),
        **_call_options("attention", grid_rank=1),
    )(q, q, k, kc, k, kc, vt, vtc)


def kernel(x, c, ctx, c_ctx, ada_w, ada_b, norm_g, mlp_w1, mlp_w2, a_w_in, a_ln_g, a_w_s, a_b_s, a_w_out,
           b_w_out, c_w_qkv, c_q_g, c_k_g, c_w_o):
    bsz = x.shape[0]
    attn_layers = [i for i in range(DEPTH) if i % N_MIXERS == 2]
    last_ctx_read = attn_layers[-1] if attn_layers else -1

    cond = jnp.zeros((COND_ROWS, D_MODEL), F32).at[:bsz].set(c).at[bsz].set(c_ctx)
    mods_all = _ada_all(cond, ada_w, ada_b).reshape(DEPTH, COND_ROWS, 6, D_MODEL)

    def big_weights(i):
        pairs = [(mlp_w1, i), (mlp_w2, i)]
        if i % N_MIXERS == 0:
            pairs += [(a_w_in, i // N_MIXERS), (a_w_out, i // N_MIXERS)]
        return pairs

    weights = [w_all[index].astype(BF16) for w_all, index in big_weights(0)]

    def ctx_tiles(a):
        return a.reshape(-1, TOKEN_TILE, a.shape[-1])

    for i in range(DEPTH):
        kind, j = i % N_MIXERS, i // N_MIXERS
        ctx_in = i <= last_ctx_read
        ctx_out = i < last_ctx_read
        mods = mods_all[i]
        ng = norm_g[i]
        w1, w2 = weights[:2]

        if kind == 0:
            w_in, w_post = weights[2:]
            ln_g = a_ln_g[j].reshape(1, A_WIDTH)
            w_s = a_w_s[j].astype(BF16)
            b_s = a_b_s[j].reshape(A_GROUPS, CHUNK, 1)
            z = _gmlp_head(x, mods, ng, w_in, ln_g, w_s, b_s)
            zc = _gmlp_head(ctx_tiles(ctx), mods, ng, w_in, ln_g, w_s, b_s, ctx_row=bsz) if ctx_out else None
        elif kind == 1:
            w_post = b_w_out[j].astype(BF16)
            z = _fourier_head_folded(x, mods, ng)
            zc = ctx_tiles(_fourier_head(ctx, mods, ng, bsz)) if ctx_out else None
        else:
            w_qkv = c_w_qkv[j].astype(BF16)
            w_post = c_w_o[j].astype(BF16)
            q_g = c_q_g[j].reshape(1, HEAD_DIM)
            k_g = c_k_g[j].reshape(1, HEAD_DIM)
            assert ctx_in and not ctx_out
            q, k, vt = _qkv_latent(x, mods, ng, w_qkv, q_g, k_g)
            kc, vtc = _kv_ctx(ctx, mods, ng, w_qkv[:, N_HEADS * HEAD_DIM:], k_g)
            z = _attention(q, k, kc, vt, vtc)
            zc = None

        cast_next = big_weights(i + 1) if i + 1 < DEPTH else ()
        x, next_weights = _tail(x, z, mods, ng, w_post, w1, w2, z_mirrored=(kind == 1),
                                cast_next=cast_next)
        if ctx_out:
            new_ctx, _ = _tail(ctx_tiles(ctx), zc, mods, ng, w_post, w1, w2, ctx_row=bsz)
            ctx = new_ctx.reshape(ctx.shape)
        weights = next_weights

    return x
```

```python
import functools

import numpy as np
import jax
import jax.numpy as jnp
from jax import lax
from jax.experimental import pallas as pl
from jax.experimental.pallas import tpu as pltpu

D_MODEL = 1024
DEPTH = 4
GRID_W = 64
N_MIXERS = 3
CHUNK = 128
A_WIDTH = 2 * D_MODEL
A_GROUPS = 8
A_GROUP_W = A_WIDTH // A_GROUPS
B_GROUPS = 4
B_GROUP_W = D_MODEL // B_GROUPS
HEAD_DIM = 128
N_HEADS = D_MODEL // HEAD_DIM
N_KV_HEADS = 2
HEADS_PER_KV = N_HEADS // N_KV_HEADS
ROPE_THETA = 10000.0
D_FF = 4 * D_MODEL
NORM_EPS = 1e-6
LN_EPS = 1e-5
LOG2_E = 1.4426950408889634

F32 = jnp.float32
BF16 = jnp.bfloat16

MXU_COLS = 256
ROW_BLOCK = 64
BF16_ROWS = 16
LANES = 128
SUBLANES = 8
COND_ROWS = 16
TOKEN_TILE = 512
TOKEN_BLOCK = 1024
QUERY_TILE = 128
ADA_COLS = 1536
MIB = 1024 * 1024
VMEM_MIB = {
    "ada_mod": 32, "gmlp_head": 56, "tail": 58, "fourier_fold": 40, "fourier_seq_fold": 40,
    "fourier_chan": 32, "fourier_seq": 48, "qkv_latent": 32, "kv_ctx": 32, "attention": 54,
}


def _dot(a, b):
    return jnp.dot(a, b, preferred_element_type=F32)


def _rms(x, g):
    ms = jnp.mean(x * x, axis=-1, keepdims=True)
    return x * lax.rsqrt(ms + NORM_EPS) * g


def _gelu_tanh(x):
    c = 2.0 * 0.7978845608028654 * LOG2_E
    z = x * (-c - (c * 0.044715) * (x * x))
    return x / (1.0 + jnp.exp2(z))


def _slab_specs(w, width, layer=None):
    k, n = w.shape[-2:]
    if layer is None:
        return [pl.BlockSpec((k, width), lambda b, t, c=c: (0, c), pipeline_mode=pl.Buffered(1))
                for c in range(n // width)]
    return [pl.BlockSpec((None, k, width), lambda b, t, c=c: (layer, 0, c), pipeline_mode=pl.Buffered(1))
            for c in range(n // width)]


def _cast_specs(cast_next, bsz, per_batch):
    n_steps = bsz * per_batch
    in_specs, out_specs, out_shapes = [], [], []
    for w_all, index in cast_next:
        k, n = w_all.shape[-2:]
        rows = k // n_steps
        in_specs.append(pl.BlockSpec(
            (None, rows, n), lambda b, t, index=index: (index, b * per_batch + t, 0)))
        out_specs.append(pl.BlockSpec((rows, n), lambda b, t: (b * per_batch + t, 0)))
        out_shapes.append(jax.ShapeDtypeStruct((k, n), BF16))
    return in_specs, out_specs, out_shapes


def _const_spec(shape):
    zeros = (0,) * len(shape)
    return pl.BlockSpec(shape, lambda b, t: zeros, pipeline_mode=pl.Buffered(1))


def _mod_spec(ctx_row):
    if ctx_row is not None:
        return pl.BlockSpec((None, 6, D_MODEL), lambda b, t: (ctx_row, 0, 0))
    return pl.BlockSpec((None, 6, D_MODEL), lambda b, t: (b, 0, 0))


def _call_options(name, grid_rank=2):
    return dict(
        name=name,
        compiler_params=pltpu.CompilerParams(
            dimension_semantics=("arbitrary",) * grid_rank,
            vmem_limit_bytes=VMEM_MIB[name] * MIB))


def _modulated_norm_to(h_s, x_ref, mod_ref, ng_ref, start=0, stop=None):
    sh1 = mod_ref[0:1, :]
    sc1p = 1.0 + mod_ref[1:2, :]
    g0 = ng_ref[0:1, :]
    stop = x_ref.shape[0] if stop is None else stop
    for r in range(start, stop, ROW_BLOCK):
        rows = slice(r, r + ROW_BLOCK)
        h_s[rows, :] = (_rms(x_ref[rows, :], g0) * sc1p + sh1).astype(BF16)


def _ada_kernel(cond_ref, w_ref, b_ref, o_ref):
    a = cond_ref[...]
    a = a * jax.nn.sigmoid(a)
    o_ref[...] = _dot(a.astype(BF16), w_ref[...].astype(BF16)) + b_ref[...]


def _ada_all(cond, ada_w, ada_b):
    tn = ADA_COLS
    n_out = 6 * D_MODEL
    return pl.pallas_call(
        _ada_kernel,
        grid=(DEPTH, n_out // tn),
        in_specs=[
            pl.BlockSpec((COND_ROWS, D_MODEL), lambda i, n: (0, 0)),
            pl.BlockSpec((None, D_MODEL, tn), lambda i, n: (i, 0, n)),
            pl.BlockSpec((None, 1, tn), lambda i, n: (i, 0, n)),
        ],
        out_specs=pl.BlockSpec((None, COND_ROWS, tn), lambda i, n: (i, 0, n)),
        out_shape=jax.ShapeDtypeStruct((DEPTH, COND_ROWS, n_out), F32),
        **_call_options("ada_mod"),
    )(cond, ada_w, ada_b.reshape(DEPTH, 1, n_out))


def _gmlp_head_kernel(x_ref, mod_ref, ng_ref, lng_ref, ws_ref, bs_ref, *refs, n_cast):
    width = 2 * MXU_COLS
    n_half = A_WIDTH // width
    win = refs[:2 * n_half]
    cast_src = refs[2 * n_half:2 * n_half + n_cast]
    z_ref = refs[2 * n_half + n_cast]
    cast_dst = refs[2 * n_half + n_cast + 1:2 * n_half + 2 * n_cast + 1]
    h_s, u_s, v_s, vb_s, sum_s, sq_s = refs[2 * n_half + 2 * n_cast + 1:]
    for src, dst in zip(cast_src, cast_dst):
        dst[...] = src[...].astype(BF16)
    tm = x_ref.shape[0]
    lng = lng_ref[...]
    half = tm // 2
    lanes = sum_s.shape[1]

    def proj(unit):
        c, a = unit
        return _dot(h_s[a * half:(a + 1) * half, :], win[c][...])

    def gelu_store(d, unit):
        c, a = unit
        is_v = c >= n_half
        dst, c = (v_s, c - n_half) if is_v else (u_s, c)
        for r in range(0, half, ROW_BLOCK):
            rows = slice(a * half + r, a * half + r + ROW_BLOCK)
            g = _gelu_tanh(d[r:r + ROW_BLOCK, :])
            dst[rows, c * width:(c + 1) * width] = g
            if is_v:
                parts = [g[:, i:i + lanes] for i in range(0, width, lanes)]
                p1 = sum(parts[1:], parts[0])
                p2 = sum([p * p for p in parts[1:]], parts[0] * parts[0])
                if c == 0:
                    sum_s[rows, :] = p1
                    sq_s[rows, :] = p2
                else:
                    sum_s[rows, :] += p1
                    sq_s[rows, :] += p2

    def layer_norm_rows(a, part):
        start = a * half + part * half // n_half
        for r in range(start, start + half // n_half, ROW_BLOCK // 2):
            rows = slice(r, r + ROW_BLOCK // 2)
            mu = jnp.sum(sum_s[rows, :], axis=-1, keepdims=True) * (1.0 / A_WIDTH)
            ex2 = jnp.sum(sq_s[rows, :], axis=-1, keepdims=True) * (1.0 / A_WIDTH)
            rstd = lax.rsqrt(jnp.maximum(ex2 - mu * mu, 0.0) + LN_EPS)
            vb_s[rows, :] = ((v_s[rows, :] - mu) * rstd * lng).astype(BF16)

    def gate_group(g):
        w = ws_ref[g]
        b = jnp.broadcast_to(bs_ref[g], (CHUNK, A_GROUP_W))
        cols = slice(g * A_GROUP_W, (g + 1) * A_GROUP_W)
        for n in range(tm // CHUNK):
            rows = slice(n * CHUNK, (n + 1) * CHUNK)
            sv = _dot(w, vb_s[rows, cols]) + b
            z_ref[rows, cols] = (u_s[rows, cols] * sv).astype(BF16)

    chunks = list(range(n_half, 2 * n_half)) + list(range(n_half))
    order = [(c, a) for c in chunks for a in range(2)]
    norms =[functools.partial(layer_norm_rows, a, part) for a in range(2) for part in range(n_half)]
    fillers = {2 * n_half + 1 + k: norms[k:k + 2] for k in range(0, len(norms), 2)}

    _modulated_norm_to(h_s, x_ref, mod_ref, ng_ref, 0, half)
    d = proj(order[0])
    _modulated_norm_to(h_s, x_ref, mod_ref, ng_ref, half, tm)
    for i in range(1, len(order) + 1):
        d_next = proj(order[i]) if i < len(order) else None
        gelu_store(d, order[i - 1])
        for filler in fillers.get(i - 1, []):
            filler()
        d = d_next
    for g in range(A_GROUPS):
        gate_group(g)


def _gmlp_head(x, mods, ng, w_in, ln_g, w_s, b_s, ctx_row=None, cast_next=()):
    bsz, length, _ = x.shape
    tm = min(TOKEN_BLOCK, length)
    w_in_specs = _slab_specs(w_in, 2 * MXU_COLS)
    cast_in_specs, cast_out_specs, cast_out_shapes = _cast_specs(cast_next, bsz, length // tm)
    z, *casts = pl.pallas_call(
        functools.partial(_gmlp_head_kernel, n_cast=len(cast_next)),
        grid=(bsz, length // tm),
        in_specs=[
            pl.BlockSpec((None, tm, D_MODEL), lambda b, t: (b, t, 0)),
            _mod_spec(ctx_row),
            _const_spec((4, D_MODEL)),
            _const_spec((1, A_WIDTH)),
            _const_spec((A_GROUPS, CHUNK, CHUNK)),
            _const_spec((A_GROUPS, CHUNK, 1)),
        ] + w_in_specs + cast_in_specs,
        out_specs=[pl.BlockSpec((None, tm, A_WIDTH), lambda b, t: (b, t, 0))] + cast_out_specs,
        out_shape=[jax.ShapeDtypeStruct((bsz, length, A_WIDTH), BF16)] + cast_out_shapes,
        scratch_shapes=[
            pltpu.VMEM((tm, D_MODEL), BF16),
            pltpu.VMEM((tm, A_WIDTH), F32),
            pltpu.VMEM((tm, A_WIDTH), F32),
            pltpu.VMEM((tm, A_WIDTH), BF16),
            pltpu.VMEM((tm, LANES), F32),
            pltpu.VMEM((tm, LANES), F32),
        ],
        **_call_options("gmlp_head"),
    )(x, mods, ng, ln_g, w_s, b_s, *([w_in] * len(w_in_specs)), *[w_all for w_all, _ in cast_next])
    return z, casts


def _interleave(dots, fillers):
    per = -(-len(fillers) // len(dots))
    for i, dot_fn in enumerate(dots):
        dot_fn()
        for filler in fillers[i * per:(i + 1) * per]:
            filler()


def _tail_kernel(x_ref, z_ref, mod_ref, ng_ref, *refs, sub, z_mirrored, n_cast):
    n_out = D_MODEL // MXU_COLS
    n_ff = D_FF // (2 * MXU_COLS)
    n_w = 2 * n_out + n_ff
    wp, w1, w2 = refs[:n_out], refs[n_out:n_out + n_ff], refs[n_out + n_ff:n_w]
    cast_src = refs[n_w:n_w + n_cast]
    o_ref = refs[n_w + n_cast]
    cast_dst = refs[n_w + n_cast + 1:n_w + 2 * n_cast + 1]
    scratch = refs[n_w + 2 * n_cast + 1:]
    for src, dst in zip(cast_src, cast_dst):
        dst[...] = src[...].astype(BF16)
    n_sub = x_ref.shape[0] // sub
    weights = (mod_ref, ng_ref, wp, w1, w2)
    pending = []
    for i in range(n_sub):
        zi = i
        if z_mirrored and n_sub > 1:
            second_half = pl.program_id(1) >= pl.num_programs(1) // 2
            zi = jnp.where(second_half, n_sub - 1 - i, i)
        rows = pl.ds(i * sub, sub)
        z_rows = rows if isinstance(zi, int) else pl.ds(pl.multiple_of(zi * sub, sub), sub)
        pending = _tail_subtile(x_ref.at[rows, :], z_ref.at[z_rows, :], *weights, o_ref.at[rows, :],
                                *scratch, pending=pending, defer_last=i + 1 < n_sub)


def _tail_subtile(x_ref, z_ref, mod_ref, ng_ref, wp, w1, w2, o_ref, *scratch, pending, defer_last):
    half = x_ref.shape[0] // 2
    y_s, hm_s, hid_s = scratch[0:2], scratch[2:4], scratch[4:6]
    g1 = mod_ref[2:3, :]
    sh2 = mod_ref[3:4, :]
    sc2p = 1.0 + mod_ref[4:5, :]
    g2 = mod_ref[5:6, :]
    ff_cols = 2 * MXU_COLS
    out_chunks = list(range(D_MODEL // MXU_COLS))
    ff_chunks = list(range(D_FF // ff_cols))
    blocks = list(range(0, half, ROW_BLOCK))

    def post_dot(a, n):
        y_s[a][:, n * MXU_COLS:(n + 1) * MXU_COLS] = _dot(z_ref[a * half:(a + 1) * half, :], wp[n][...])

    def norm1(a, r):
        rows = slice(a * half + r, a * half + r + ROW_BLOCK)
        x1 = x_ref[rows, :] + g1 * _rms(y_s[a][r:r + ROW_BLOCK, :], ng_ref[1:2, :])
        o_ref[rows, :] = x1
        hm_s[a][r:r + ROW_BLOCK, :] = (_rms(x1, ng_ref[2:3, :]) * sc2p + sh2).astype(BF16)

    def up_dot(a, c):
        t = jnp.maximum(_dot(hm_s[a][...], w1[c][...]), 0.0)
        hid_s[a][:, c * ff_cols:(c + 1) * ff_cols] = (t * t).astype(BF16)

    def down_dot(a, n):
        y_s[a][:, n * MXU_COLS:(n + 1) * MXU_COLS] = _dot(hid_s[a][...], w2[n][...])

    def norm3(a, r):
        rows = slice(a * half + r, a * half + r + ROW_BLOCK)
        o_ref[rows, :] = o_ref[rows, :] + g2 * _rms(y_s[a][r:r + ROW_BLOCK, :], ng_ref[3:4, :])

    def each(fn, a, items):
        return [functools.partial(fn, a, item) for item in items]

    _interleave(each(post_dot, 0, out_chunks), list(pending))
    _interleave(each(post_dot, 1, out_chunks), each(norm1, 0, blocks))
    _interleave(each(up_dot, 0, ff_chunks), each(norm1, 1, blocks))
    _interleave(each(up_dot, 1, ff_chunks), [])
    _interleave(each(down_dot, 0, out_chunks), [])
    _interleave(each(down_dot, 1, out_chunks), each(norm3, 0, blocks))
    last = each(norm3, 1, blocks)
    if defer_last:
        return last
    _interleave(last, [])
    return []


def _tail(x, z, mods, ng, w_post, w1, w2, ctx_row=None, z_mirrored=False, cast_next=()):
    bsz, length, _ = x.shape
    kz = z.shape[-1]
    sub = min(TOKEN_TILE, length)
    tm = min(TOKEN_BLOCK, length)
    w_post, post_index = w_post if isinstance(w_post, tuple) else (w_post, None)
    wp_specs = _slab_specs(w_post, MXU_COLS, post_index)
    w1_specs = _slab_specs(w1, 2 * MXU_COLS)
    w2_specs = _slab_specs(w2, MXU_COLS)
    cast_in_specs, cast_out_specs, cast_out_shapes = _cast_specs(cast_next, bsz, length // tm)
    if z_mirrored:
        nt = length // 2 // tm
        z_spec = pl.BlockSpec(
            (None, None, tm, kz),
            lambda b, t: (b, t // nt, jnp.where(t < nt, t, 2 * nt - 1 - t), 0))
    else:
        z_spec = pl.BlockSpec((None, tm, kz), lambda b, t: (b, t, 0))
    out, *casts = pl.pallas_call(
        functools.partial(_tail_kernel, sub=sub, z_mirrored=z_mirrored, n_cast=len(cast_next)),
        grid=(bsz, length // tm),
        in_specs=[
            pl.BlockSpec((None, tm, D_MODEL), lambda b, t: (b, t, 0)),
            z_spec,
            _mod_spec(ctx_row),
            _const_spec((4, D_MODEL)),
        ] + wp_specs + w1_specs + w2_specs + cast_in_specs,
        out_specs=[pl.BlockSpec((None, tm, D_MODEL), lambda b, t: (b, t, 0))] + cast_out_specs,
        out_shape=[jax.ShapeDtypeStruct((bsz, length, D_MODEL), F32)] + cast_out_shapes,
        scratch_shapes=(
            [pltpu.VMEM((sub // 2, D_MODEL), F32)] * 2
            + [pltpu.VMEM((sub // 2, D_MODEL), BF16)] * 2
            + [pltpu.VMEM((sub // 2, D_FF), BF16)] * 2),
        **_call_options("tail"),
    )(x, z, mods, ng, *([w_post] * len(wp_specs)), *([w1] * len(w1_specs)),
      *([w2] * len(w2_specs)), *[w_all for w_all, _ in cast_next])
    return out, casts


def _fourier_chan_kernel(x_ref, mod_ref, ng_ref, cs_ref, ab_ref, h_s):
    _modulated_norm_to(h_s, x_ref, mod_ref, ng_ref)
    for g in range(B_GROUPS):
        cols = slice(g * B_GROUP_W, (g + 1) * B_GROUP_W)
        t = _dot(h_s[:, cols], cs_ref[...])
        ab_ref[0, :, cols] = t[:, :B_GROUP_W].astype(BF16)
        ab_ref[1, :, cols] = t[:, B_GROUP_W:].astype(BF16)


def _fourier_seq_kernel(cs_ref, ab_ref, y_ref):
    for n in range(D_MODEL // MXU_COLS):
        cols = slice(n * MXU_COLS, (n + 1) * MXU_COLS)
        y_ref[:, cols] = _dot(cs_ref[...], ab_ref[:, cols]).astype(BF16)


def _bf16_const(a):
    return jnp.asarray(np.asarray(a, np.float32)).astype(BF16)


def _dft_angles(n):
    idx = np.arange(n, dtype=np.int64)
    return (2.0 * np.pi / n) * ((idx[:, None] * idx[None, :]) % n)


def _dft_tables(length):
    ang_c = _dft_angles(B_GROUP_W)
    cs_chan = np.concatenate([np.cos(ang_c), np.sin(ang_c)], axis=1) * B_GROUP_W ** -0.5
    ang_l = _dft_angles(length)
    cs_seq = np.concatenate([np.cos(ang_l), -np.sin(ang_l)], axis=1) * length ** -0.5
    return _bf16_const(cs_chan), _bf16_const(cs_seq)


def _fourier_fold_kernel(xp_ref, xm_ref, x0_ref, mod_ref, ng_ref, cc_ref, sc_ref, rev_ref,
                         ab_ref, a0_ref, hp_s, hm_s):
    tm = rev_ref.shape[0]
    n_tiles = xm_ref.shape[0] // tm
    sh1 = mod_ref[0:1, :]
    sc1p = 1.0 + mod_ref[1:2, :]
    g0 = ng_ref[0:1, :]

    def norm_block(i, r):
        rows = slice(i * tm + r, i * tm + r + ROW_BLOCK)
        hp_s[rows, :] = _rms(xp_ref[0, i * tm + r + 1:i * tm + r + 1 + ROW_BLOCK, :], g0) * sc1p + sh1
        mirror = (n_tiles - 1 - i) * tm + r
        hm_s[rows, :] = (_rms(xm_ref[mirror:mirror + ROW_BLOCK, :], g0) * sc1p + sh1).astype(BF16)

    blocks = list(range(0, tm, ROW_BLOCK))
    for r in blocks:
        norm_block(0, r)
    h0 = (_rms(x0_ref[...], g0) * sc1p + sh1).astype(BF16)
    per_group = -(-len(blocks) // B_GROUPS)
    for i in range(n_tiles):
        rows = slice(i * tm, (i + 1) * tm)
        for g in range(B_GROUPS):
            cols = slice(g * B_GROUP_W, (g + 1) * B_GROUP_W)
            hm_rev = _dot(rev_ref[...], hm_s[rows, cols])
            hp = hp_s[rows, cols]
            ab_ref[0, rows, cols] = _dot((hp + hm_rev).astype(BF16), cc_ref[...]).astype(BF16)
            ab_ref[1, rows, cols] = _dot((hp - hm_rev).astype(BF16), sc_ref[...]).astype(BF16)
            if i + 1 < n_tiles:
                for r in blocks[g * per_group:(g + 1) * per_group]:
                    norm_block(i + 1, r)
    for g in range(B_GROUPS):
        cols = slice(g * B_GROUP_W, (g + 1) * B_GROUP_W)
        a0_ref[:, cols] = _dot(h0[:, cols], cc_ref[...])


def _fourier_seq_fold_kernel(ct_ref, st_ref, ab_ref, a0_ref, perm_ref, y_ref, w_s, *, scale):
    tk = y_ref.shape[1]
    half = ct_ref.shape[1]
    for n in range(D_MODEL // MXU_COLS):
        cols = slice(n * MXU_COLS, (n + 1) * MXU_COLS)
        yc = _dot(ct_ref[...], ab_ref[0:half, cols]) + a0_ref[0:1, cols] * scale
        ys = _dot(st_ref[...], ab_ref[half:2 * half, cols])
        y_ref[0, :, cols] = (yc[:tk] - ys[:tk]).astype(BF16)
        w_s[:, cols] = (yc + ys).astype(BF16)
    first_row = (lax.broadcasted_iota(jnp.int32, (tk, 1), 0) == 0).astype(F32)
    for n in range(D_MODEL // MXU_COLS):
        cols = slice(n * MXU_COLS, (n + 1) * MXU_COLS)
        mirrored = _dot(perm_ref[...], w_s[0:tk, cols]) + first_row * w_s[tk:tk + 1, cols].astype(F32)
        y_ref[1, :, cols] = mirrored.astype(BF16)


def _seq_fold_tables(length, tk):
    half = length // 2
    nt = half // tk
    rows = tk + BF16_ROWS
    t = np.arange(1, half + 1, dtype=np.int64)
    k = np.arange(nt, dtype=np.int64)[:, None] * tk + np.arange(rows, dtype=np.int64)[None, :]
    ang = (2.0 * np.pi / length) * ((k[..., None] * t) % length)
    weight = np.where(t == half, 0.5, 1.0) * length ** -0.5
    return _bf16_const(np.cos(ang) * weight), _bf16_const(np.sin(ang) * weight)


def _fourier_head_folded(x, mods, ng):
    bsz, length, _ = x.shape
    tm = TOKEN_TILE
    half = length // 2
    nt = half // tm
    rows = tm + BF16_ROWS
    ang_c = _dft_angles(B_GROUP_W)
    cc = _bf16_const(np.cos(ang_c) * B_GROUP_W ** -0.5)
    sc = _bf16_const(np.sin(ang_c) * B_GROUP_W ** -0.5)
    rev = _bf16_const(np.arange(tm)[:, None] + np.arange(tm)[None, :] == tm - 1)
    fold_rows = TOKEN_BLOCK
    n_fold = half // fold_rows
    ab, a0 = pl.pallas_call(
        _fourier_fold_kernel,
        grid=(bsz, n_fold),
        in_specs=[
            pl.BlockSpec((pl.Element(1), pl.Element(fold_rows + SUBLANES), pl.Element(D_MODEL)),
                         lambda b, t: (b, t * fold_rows, 0)),
            pl.BlockSpec((None, fold_rows, D_MODEL), lambda b, t: (b, 2 * n_fold - 1 - t, 0)),
            pl.BlockSpec((None, SUBLANES, D_MODEL), lambda b, t: (b, 0, 0)),
            _mod_spec(None),
            _const_spec((4, D_MODEL)),
            _const_spec((B_GROUP_W, B_GROUP_W)),
            _const_spec((B_GROUP_W, B_GROUP_W)),
            _const_spec((tm, tm)),
        ],
        out_specs=[
            pl.BlockSpec((None, 2, fold_rows, D_MODEL), lambda b, t: (b, 0, t, 0)),
            pl.BlockSpec((None, SUBLANES, D_MODEL), lambda b, t: (b, 0, 0)),
        ],
        out_shape=[
            jax.ShapeDtypeStruct((bsz, 2, half, D_MODEL), BF16),
            jax.ShapeDtypeStruct((bsz, SUBLANES, D_MODEL), F32),
        ],
        scratch_shapes=[pltpu.VMEM((fold_rows, D_MODEL), F32), pltpu.VMEM((fold_rows, D_MODEL), BF16)],
        **_call_options("fourier_fold"),
    )(x, x, x, mods, ng, cc, sc, rev)
    ct, st = _seq_fold_tables(length, tm)
    perm = _bf16_const(np.arange(tm)[:, None] + np.arange(tm)[None, :] == tm)
    return pl.pallas_call(
        functools.partial(_fourier_seq_fold_kernel, scale=length ** -0.5),
        grid=(bsz, nt),
        in_specs=[
            pl.BlockSpec((None, rows, half), lambda b, t: (t, 0, 0)),
            pl.BlockSpec((None, rows, half), lambda b, t: (t, 0, 0)),
            pl.BlockSpec((None, 2 * half, D_MODEL), lambda b, t: (b, 0, 0)),
            pl.BlockSpec((None, SUBLANES, D_MODEL), lambda b, t: (b, 0, 0)),
            _const_spec((tm, tm)),
        ],
        out_specs=pl.BlockSpec((None, 2, tm, D_MODEL), lambda b, t: (b, 0, t, 0)),
        out_shape=jax.ShapeDtypeStruct((bsz, 2, half, D_MODEL), BF16),
        scratch_shapes=[pltpu.VMEM((rows, D_MODEL), BF16)],
        **_call_options("fourier_seq_fold"),
    )(ct, st, ab.reshape(bsz, 2 * half, D_MODEL), a0, perm)


def _fourier_head(x, mods, ng, ctx_row):
    bsz, length, _ = x.shape
    tm = min(TOKEN_TILE, length)
    cs_chan, cs_seq = _dft_tables(length)
    ab = pl.pallas_call(
        _fourier_chan_kernel,
        grid=(bsz, length // tm),
        in_specs=[
            pl.BlockSpec((None, tm, D_MODEL), lambda b, t: (b, t, 0)),
            _mod_spec(ctx_row),
            _const_spec((4, D_MODEL)),
            _const_spec((B_GROUP_W, 2 * B_GROUP_W)),
        ],
        out_specs=pl.BlockSpec((None, 2, tm, D_MODEL), lambda b, t: (b, 0, t, 0)),
        out_shape=jax.ShapeDtypeStruct((bsz, 2, length, D_MODEL), BF16),
        scratch_shapes=[pltpu.VMEM((tm, D_MODEL), BF16)],
        **_call_options("fourier_chan"),
    )(x, mods, ng, cs_chan)
    ab = ab.reshape(bsz, 2 * length, D_MODEL)
    return pl.pallas_call(
        _fourier_seq_kernel,
        grid=(bsz, length // tm),
        in_specs=[
            pl.BlockSpec((tm, 2 * length), lambda b, t: (t, 0)),
            pl.BlockSpec((None, 2 * length, D_MODEL), lambda b, t: (b, 0, 0),
                         pipeline_mode=pl.Buffered(1)),
        ],
        out_specs=pl.BlockSpec((None, tm, D_MODEL), lambda b, t: (b, t, 0)),
        out_shape=jax.ShapeDtypeStruct((bsz, length, D_MODEL), BF16),
        **_call_options("fourier_seq"),
    )(cs_seq, ab)


def _head_rms(r, g):
    return r * lax.rsqrt(jnp.mean(r * r, axis=-1, keepdims=True) + NORM_EPS) * g


def _store_vt_ext(vt_ref, t, cols=slice(None)):
    tokens = t.shape[0]
    first_row = lax.broadcasted_iota(jnp.int32, (BF16_ROWS, tokens), 0) == 0
    for e in range(N_KV_HEADS):
        vt_ref[e, 0:HEAD_DIM, cols] = t[:, e * HEAD_DIM:(e + 1) * HEAD_DIM].T.astype(BF16)
        vt_ref[e, HEAD_DIM:HEAD_DIM + BF16_ROWS, cols] = first_row.astype(BF16)


def _qkv_kernel(x_ref, mod_ref, ng_ref, w_ref, qg_ref, kg_ref, mean_ref, cos_ref, sin_ref,
                q_ref, k_ref, vt_ref, h_s):
    sub = min(TOKEN_TILE, x_ref.shape[0])
    n_sub = x_ref.shape[0] // sub
    qg = qg_ref[...] * (HEAD_DIM ** -0.5 * LOG2_E)
    gains = [jnp.concatenate([qg, qg], axis=1)] * (N_HEADS // 2)
    gains.append(jnp.concatenate([kg_ref[...], kg_ref[...]], axis=1))
    outs = [(q_ref, 2 * p) for p in range(N_HEADS // 2)] + [(k_ref, 0)]
    n_pairs = len(outs)

    def normalise(t, p):
        sq = t * t
        hi = sq.astype(BF16)
        lo = (sq - hi.astype(F32)).astype(BF16)
        ms = _dot(hi, mean_ref[...]) + _dot(lo, mean_ref[...])
        return t * lax.rsqrt(ms + NORM_EPS) * gains[p]

    _modulated_norm_to(h_s, x_ref, mod_ref, ng_ref, 0, sub)
    for i in range(n_sub):
        rows = slice(i * sub, (i + 1) * sub)
        cosf = cos_ref[rows, :]
        sinf = sin_ref[rows, :]
        next_blocks = list(range((i + 1) * sub, (i + 2) * sub, ROW_BLOCK)) if i + 1 < n_sub else []
        per_pair = -(-len(next_blocks) // (n_pairs + 1))

        def project(p):
            return _dot(h_s[rows, :], w_ref[:, p * MXU_COLS:(p + 1) * MXU_COLS])

        def rope_store(rn, p):
            ref, first = outs[p]
            for e in range(2):
                r = rn[:, e * HEAD_DIM:(e + 1) * HEAD_DIM]
                ref[first + e, rows, :] = (r * cosf + pltpu.roll(r, HEAD_DIM // 2, 1) * sinf).astype(BF16)

        t_next = project(0)
        rn_prev = None
        for p in range(n_pairs + 1):
            t = t_next
            if p < n_pairs:
                t_next = project(p + 1)
            rn = normalise(t, p) if p < n_pairs else None
            if rn_prev is not None:
                rope_store(rn_prev, p - 1)
            rn_prev = rn
            for r in next_blocks[p * per_pair:(p + 1) * per_pair]:
                _modulated_norm_to(h_s, x_ref, mod_ref, ng_ref, r, r + ROW_BLOCK)
        _store_vt_ext(vt_ref, t, rows)


def _kv_ctx_kernel(x_ref, mod_ref, ng_ref, w_ref, kg_ref, k_ref, vt_ref, h_s):
    _modulated_norm_to(h_s, x_ref, mod_ref, ng_ref)
    t = _dot(h_s[...], w_ref[:, 0:MXU_COLS])
    for e in range(N_KV_HEADS):
        k_ref[e] = _head_rms(t[:, e * HEAD_DIM:(e + 1) * HEAD_DIM], kg_ref[...]).astype(BF16)
    _store_vt_ext(vt_ref, _dot(h_s[...], w_ref[:, MXU_COLS:2 * MXU_COLS]))


def _rope_tables(length):
    t = np.arange(length)
    row = (t // GRID_W).astype(np.float64)
    col = (t % GRID_W).astype(np.float64)
    n_freq = HEAD_DIM // 4
    inv = ROPE_THETA ** (-np.arange(n_freq, dtype=np.float64) / n_freq)
    ang = np.concatenate([row[:, None] * inv, col[:, None] * inv], axis=-1)
    cos, sin = np.cos(ang), np.sin(ang)
    return (jnp.asarray(np.concatenate([cos, cos], axis=-1), F32),
            jnp.asarray(np.concatenate([-sin, sin], axis=-1), F32))


def _qkv_latent(x, mods, ng, w_qkv, q_g, k_g):
    bsz, length, _ = x.shape
    tm = TOKEN_BLOCK
    cosf, sinf = _rope_tables(length)
    n_qkv = w_qkv.shape[-1]
    lane_head = np.arange(MXU_COLS) // HEAD_DIM
    head_mean = _bf16_const((lane_head[:, None] == lane_head[None, :]) * (1.0 / HEAD_DIM))
    kv_shape = jax.ShapeDtypeStruct((bsz, N_KV_HEADS, length, HEAD_DIM), BF16)
    kv_spec = pl.BlockSpec((None, N_KV_HEADS, tm, HEAD_DIM), lambda b, t: (b, 0, t, 0))
    vt_shape = jax.ShapeDtypeStruct((bsz, N_KV_HEADS, HEAD_DIM + BF16_ROWS, length), BF16)
    vt_spec = pl.BlockSpec((None, N_KV_HEADS, HEAD_DIM + BF16_ROWS, tm), lambda b, t: (b, 0, 0, t))
    return pl.pallas_call(
        _qkv_kernel,
        grid=(bsz, length // tm),
        in_specs=[
            pl.BlockSpec((None, tm, D_MODEL), lambda b, t: (b, t, 0)),
            _mod_spec(None),
            _const_spec((4, D_MODEL)),
            _const_spec((D_MODEL, n_qkv)),
            _const_spec((1, HEAD_DIM)),
            _const_spec((1, HEAD_DIM)),
            _const_spec((MXU_COLS, MXU_COLS)),
            pl.BlockSpec((tm, HEAD_DIM), lambda b, t: (t, 0)),
            pl.BlockSpec((tm, HEAD_DIM), lambda b, t: (t, 0)),
        ],
        out_specs=[
            pl.BlockSpec((None, N_HEADS, tm, HEAD_DIM), lambda b, t: (b, 0, t, 0)),
            kv_spec, vt_spec,
        ],
        out_shape=[
            jax.ShapeDtypeStruct((bsz, N_HEADS, length, HEAD_DIM), BF16),
            kv_shape, vt_shape,
        ],
        scratch_shapes=[pltpu.VMEM((tm, D_MODEL), BF16)],
        **_call_options("qkv_latent"),
    )(x, mods, ng, w_qkv, q_g, k_g, head_mean, cosf, sinf)


def _kv_ctx(ctx, mods, ng, w_kv, k_g):
    bsz, length, _ = ctx.shape
    tm = length
    kv_shape = jax.ShapeDtypeStruct((bsz, N_KV_HEADS, length, HEAD_DIM), BF16)
    kv_spec = pl.BlockSpec((None, N_KV_HEADS, tm, HEAD_DIM), lambda b, t: (b, 0, t, 0))
    vt_shape = jax.ShapeDtypeStruct((bsz, N_KV_HEADS, HEAD_DIM + BF16_ROWS, length), BF16)
    vt_spec = pl.BlockSpec((None, N_KV_HEADS, HEAD_DIM + BF16_ROWS, tm), lambda b, t: (b, 0, 0, t))
    return pl.pallas_call(
        _kv_ctx_kernel,
        grid=(bsz, length // tm),
        in_specs=[
            pl.BlockSpec((None, tm, D_MODEL), lambda b, t: (b, t, 0)),
            _mod_spec(bsz),
            _const_spec((4, D_MODEL)),
            _const_spec((D_MODEL, 2 * MXU_COLS)),
            _const_spec((1, HEAD_DIM)),
        ],
        out_specs=[kv_spec, vt_spec],
        out_shape=[kv_shape, vt_shape],
        scratch_shapes=[pltpu.VMEM((tm, D_MODEL), BF16)],
        **_call_options("kv_ctx"),
    )(ctx, mods, ng, w_kv, k_g)


def _region(index, fn):
    pl.when(pl.program_id(0) >= -index)(fn)


def _attn_kernel(qa_ref, qb_ref, ka_ref, kca_ref, kb_ref, kcb_ref, vt_ref, vtc_ref, o_ref, *scratch):
    tq = qa_ref.shape[1] // 2
    n_latent = ka_ref.shape[1]
    n_keys = n_latent + kca_ref.shape[1]
    nq = HEADS_PER_KV * tq
    q4_s, s_s, p_s, m_s = (scratch[i * N_KV_HEADS:(i + 1) * N_KV_HEADS] for i in range(4))

    def chunks_of(k_ref, kc_ref):
        chunks = [(k_ref, vt_ref, slice(r, r + MXU_COLS), slice(r, r + MXU_COLS))
                  for r in range(0, n_latent, MXU_COLS)]
        chunks += [(kc_ref, vtc_ref, slice(r, r + MXU_COLS),
                    slice(n_latent + r, n_latent + r + MXU_COLS))
                   for r in range(0, n_keys - n_latent, MXU_COLS)]
        return chunks

    @pl.when(pl.program_id(0) == 0)
    def _():
        s_s[1][...] = jnp.zeros(s_s[1].shape, F32)
        m_s[1][...] = jnp.zeros(m_s[1].shape, F32)
        p_s[0][...] = jnp.ones(p_s[0].shape, BF16)

    def stage(j_scores, j_probs, j_pv, q_ref, q_rows, key_chunks, out_rows):
        for h in range(HEADS_PER_KV):
            q4_s[j_scores][h * tq:(h + 1) * tq, :] = q_ref[HEADS_PER_KV * j_scores + h, q_rows, :]
        m_probs = m_s[j_probs][...]
        m = None
        acc = None

        def store_head(acc, h):
            cols = slice(h * tq, (h + 1) * tq)
            o = acc[:HEAD_DIM, cols] / acc[HEAD_DIM:HEAD_DIM + 1, cols]
            head = HEADS_PER_KV * j_pv + h
            o_ref[out_rows, head * HEAD_DIM:(head + 1) * HEAD_DIM] = o.T.astype(BF16)

        pv_chunks = list(key_chunks)
        heads_left = list(range(HEADS_PER_KV))
        tiles = [slice(r, r + BF16_ROWS) for r in range(0, n_keys, BF16_ROWS)]
        pv_rows_per_score_row = vt_ref.shape[1] / MXU_COLS
        work_total = len(key_chunks) * (1.0 + pv_rows_per_score_row)
        work_done = 0.0
        tiles_done = 0
        for k_src, _, local, rows in key_chunks:
            s = lax.dot_general(k_src[j_scores, local, :], q4_s[j_scores][...],
                                (((1,), (1,)), ((), ())), preferred_element_type=F32)
            s_s[j_scores][rows, :] = s
            cm = jnp.max(s, axis=0, keepdims=True)
            m = cm if m is None else jnp.maximum(m, cm)
            work_done += 1.0
            if pv_chunks:
                for _, vt_src, pv_local, pv_rows in (
                        pv_chunks.pop(0) for _ in range(min(2, len(pv_chunks)))):
                    d = _dot(vt_src[j_pv, :, pv_local], p_s[j_pv][pv_rows, :])
                    acc = d if acc is None else acc + d
                    work_done += pv_rows_per_score_row
            elif heads_left:
                store_head(acc, heads_left.pop(0))
            tiles_until = round(len(tiles) * work_done / work_total)
            for tile in tiles[tiles_done:tiles_until]:
                p_s[j_probs][tile, :] = jnp.exp2(s_s[j_probs][tile, :] - m_probs).astype(BF16)
            tiles_done = tiles_until
        assert not pv_chunks and tiles_done == len(tiles)
        for h in heads_left:
            store_head(acc, h)
        m_s[j_scores][...] = jnp.broadcast_to(m, (BF16_ROWS, nq))

    first, second = slice(0, tq), slice(tq, 2 * tq)
    chunks_a, chunks_b = chunks_of(ka_ref, kca_ref), chunks_of(kb_ref, kcb_ref)
    _region(0, lambda: stage(0, 1, 0, qa_ref, second, chunks_a, first))
    _region(1, lambda: stage(1, 0, 1, qa_ref, second, chunks_a, first))
    _region(2, lambda: stage(0, 1, 0, qb_ref, first, chunks_b, second))
    _region(3, lambda: stage(1, 0, 1, qb_ref, first, chunks_b, second))


def _attention(q, k, kc, vt, vtc):
    bsz, _, length, _ = q.shape
    n_ctx = kc.shape[2]
    n_keys = length + n_ctx
    vt_rows = vt.shape[2]
    tq = QUERY_TILE
    nq = HEADS_PER_KV * tq
    per_batch = length // (2 * tq)
    n_blocks = bsz * per_batch

    def cur(s):
        return jnp.minimum(s, n_blocks - 1)

    def prev(s):
        return jnp.maximum(s - 1, 0)

    def q_spec(block):
        return pl.BlockSpec((None, N_HEADS, 2 * tq, HEAD_DIM),
                            lambda s: (block(s) // per_batch, 0, block(s) % per_batch, 0))

    def kv_spec(rows, cols, block):
        return pl.BlockSpec((None, N_KV_HEADS, rows, cols), lambda s: (block(s) // per_batch, 0, 0, 0))

    return pl.pallas_call(
        _attn_kernel,
        grid=(n_blocks + 1,),
        in_specs=[
            q_spec(prev), q_spec(cur),
            kv_spec(length, HEAD_DIM, prev), kv_spec(n_ctx, HEAD_DIM, prev),
            kv_spec(length, HEAD_DIM, cur), kv_spec(n_ctx, HEAD_DIM, cur),
            kv_spec(vt_rows, length, prev), kv_spec(vt_rows, n_ctx, prev),
        ],
        out_specs=pl.BlockSpec((None, 2 * tq, D_MODEL),
                               lambda s: (prev(s) // per_batch, prev(s) % per_batch, 0)),
        out_shape=jax.ShapeDtypeStruct((bsz, length, D_MODEL), BF16),
        scratch_shapes=(
            [pltpu.VMEM((nq, HEAD_DIM), BF16)] * N_KV_HEADS
            + [pltpu.VMEM((n_keys, nq), F32)] * N_KV_HEADS
            + [pltpu.VMEM((n_keys, nq), BF16)] * N_KV_HEADS
            + [pltpu.VMEM((BF16_ROWS, nq), F32)] * N_KV_HEADS),
        **_call_options("attention", grid_rank=1),
    )(q, q, k, kc, k, kc, vt, vtc)


def kernel(x, c, ctx, c_ctx, ada_w, ada_b, norm_g, mlp_w1, mlp_w2, a_w_in, a_ln_g, a_w_s, a_b_s, a_w_out,
           b_w_out, c_w_qkv, c_q_g, c_k_g, c_w_o):
    bsz = x.shape[0]
    attn_layers = [i for i in range(DEPTH) if i % N_MIXERS == 2]
    last_ctx_read = attn_layers[-1] if attn_layers else -1

    cond = jnp.zeros((COND_ROWS, D_MODEL), F32).at[:bsz].set(c).at[bsz].set(c_ctx)
    mods_all = _ada_all(cond, ada_w, ada_b).reshape(DEPTH, COND_ROWS, 6, D_MODEL)

    def big_weights(i):
        pairs = [(mlp_w1, i), (mlp_w2, i)]
        if i % N_MIXERS == 0:
            pairs += [(a_w_out, i // N_MIXERS), (a_w_in, i // N_MIXERS)]
        return pairs

    weights = None

    def ctx_tiles(a):
        return a.reshape(-1, TOKEN_TILE, a.shape[-1])

    for i in range(DEPTH):
        kind, j = i % N_MIXERS, i // N_MIXERS
        ctx_in = i <= last_ctx_read
        ctx_out = i < last_ctx_read
        mods = mods_all[i]
        ng = norm_g[i]
        assert weights is not None or kind == 0
        if weights is not None:
            w1, w2 = weights[:2]

        if kind == 0:
            ln_g = a_ln_g[j].reshape(1, A_WIDTH)
            w_s = a_w_s[j].astype(BF16)
            b_s = a_b_s[j].reshape(A_GROUPS, CHUNK, 1)
            if weights is None:
                w_in = a_w_in[j].astype(BF16)
                z, (w1, w2, w_post) = _gmlp_head(x, mods, ng, w_in, ln_g, w_s, b_s,
                                                 cast_next=big_weights(i)[:3])
            else:
                w_post, w_in = weights[2:]
                z, _ = _gmlp_head(x, mods, ng, w_in, ln_g, w_s, b_s)
            zc = None
            if ctx_out:
                zc, _ = _gmlp_head(ctx_tiles(ctx), mods, ng, w_in, ln_g, w_s, b_s, ctx_row=bsz)
        elif kind == 1:
            w_post = b_w_out[j].astype(BF16)
            z = _fourier_head_folded(x, mods, ng)
            zc = ctx_tiles(_fourier_head(ctx, mods, ng, bsz)) if ctx_out else None
        else:
            w_qkv = c_w_qkv[j].astype(BF16)
            w_post = c_w_o[j].astype(BF16)
            q_g = c_q_g[j].reshape(1, HEAD_DIM)
            k_g = c_k_g[j].reshape(1, HEAD_DIM)
            assert ctx_in and not ctx_out
            q, k, vt = _qkv_latent(x, mods, ng, w_qkv, q_g, k_g)
            kc, vtc = _kv_ctx(ctx, mods, ng, w_qkv[:, N_HEADS * HEAD_DIM:], k_g)
            z = _attention(q, k, kc, vt, vtc)
            zc = None

        cast_next = big_weights(i + 1) if i + 1 < DEPTH else ()
        x, next_weights = _tail(x, z, mods, ng, w_post, w1, w2, z_mirrored=(kind == 1),
                                cast_next=cast_next)
        if ctx_out:
            new_ctx, _ = _tail(ctx_tiles(ctx), zc, mods, ng, w_post, w1, w2, ctx_row=bsz)
            ctx = new_ctx.reshape(ctx.shape)
        weights = next_weights

    return x
```

```python
import functools

import numpy as np
import jax
import jax.numpy as jnp
from jax import lax
from jax.experimental import pallas as pl
from jax.experimental.pallas import tpu as pltpu

D_MODEL = 1024
DEPTH = 4
GRID_W = 64
N_MIXERS = 3
CHUNK = 128
A_WIDTH = 2 * D_MODEL
A_GROUPS = 8
A_GROUP_W = A_WIDTH // A_GROUPS
B_GROUPS = 4
B_GROUP_W = D_MODEL // B_GROUPS
HEAD_DIM = 128
N_HEADS = D_MODEL // HEAD_DIM
N_KV_HEADS = 2
HEADS_PER_KV = N_HEADS // N_KV_HEADS
ROPE_THETA = 10000.0
D_FF = 4 * D_MODEL
NORM_EPS = 1e-6
LN_EPS = 1e-5
LOG2_E = 1.4426950408889634

F32 = jnp.float32
BF16 = jnp.bfloat16

MXU_COLS = 256
ROW_BLOCK = 64
BF16_ROWS = 16
LANES = 128
SUBLANES = 8
COND_ROWS = 16
TOKEN_TILE = 512
TOKEN_BLOCK = 1024
QUERY_TILE = 128
ADA_COLS = 1536
MIB = 1024 * 1024
VMEM_MIB = {
    "ada_mod": 32, "gmlp_head": 56, "tail": 58, "fourier_fold": 40, "fourier_seq_fold": 40,
    "fourier_chan": 32, "fourier_seq": 48, "qkv_latent": 32, "kv_ctx": 32, "attention": 54,
}


def _dot(a, b):
    return jnp.dot(a, b, preferred_element_type=F32)


def _rms(x, g):
    ms = jnp.mean(x * x, axis=-1, keepdims=True)
    return x * lax.rsqrt(ms + NORM_EPS) * g


def _gelu_tanh(x):
    c = 2.0 * 0.7978845608028654 * LOG2_E
    z = x * (-c - (c * 0.044715) * (x * x))
    return x / (1.0 + jnp.exp2(z))


def _slab_specs(w, width, layer=None):
    k, n = w.shape[-2:]
    if layer is None:
        return [pl.BlockSpec((k, width), lambda b, t, c=c: (0, c), pipeline_mode=pl.Buffered(1))
                for c in range(n // width)]
    return [pl.BlockSpec((None, k, width), lambda b, t, c=c: (layer, 0, c), pipeline_mode=pl.Buffered(1))
            for c in range(n // width)]


def _cast_specs(cast_next, bsz, per_batch):
    n_steps = bsz * per_batch
    in_specs, out_specs, out_shapes = [], [], []
    for w_all, index in cast_next:
        k, n = w_all.shape[-2:]
        rows = k // n_steps
        in_specs.append(pl.BlockSpec(
            (None, rows, n), lambda b, t, index=index: (index, b * per_batch + t, 0)))
        out_specs.append(pl.BlockSpec((rows, n), lambda b, t: (b * per_batch + t, 0)))
        out_shapes.append(jax.ShapeDtypeStruct((k, n), BF16))
    return in_specs, out_specs, out_shapes


def _const_spec(shape):
    zeros = (0,) * len(shape)
    return pl.BlockSpec(shape, lambda b, t: zeros, pipeline_mode=pl.Buffered(1))


def _mod_spec(ctx_row):
    if ctx_row is not None:
        return pl.BlockSpec((None, 6, D_MODEL), lambda b, t: (ctx_row, 0, 0))
    return pl.BlockSpec((None, 6, D_MODEL), lambda b, t: (b, 0, 0))


def _call_options(name, grid_rank=2):
    return dict(
        name=name,
        compiler_params=pltpu.CompilerParams(
            dimension_semantics=("arbitrary",) * grid_rank,
            vmem_limit_bytes=VMEM_MIB[name] * MIB))


def _modulated_norm_to(h_s, x_ref, mod_ref, ng_ref, start=0, stop=None):
    sh1 = mod_ref[0:1, :]
    gain = ng_ref[0:1, :] * (1.0 + mod_ref[1:2, :])
    stop = x_ref.shape[0] if stop is None else stop
    for r in range(start, stop, ROW_BLOCK):
        rows = slice(r, r + ROW_BLOCK)
        h_s[rows, :] = (_rms(x_ref[rows, :], gain) + sh1).astype(BF16)


def _ada_kernel(cond_ref, w_ref, b_ref, o_ref):
    a = cond_ref[...]
    a = a * jax.nn.sigmoid(a)
    o_ref[...] = _dot(a.astype(BF16), w_ref[...].astype(BF16)) + b_ref[...]


def _ada_all(cond, ada_w, ada_b):
    tn = ADA_COLS
    n_out = 6 * D_MODEL
    return pl.pallas_call(
        _ada_kernel,
        grid=(DEPTH, n_out // tn),
        in_specs=[
            pl.BlockSpec((COND_ROWS, D_MODEL), lambda i, n: (0, 0)),
            pl.BlockSpec((None, D_MODEL, tn), lambda i, n: (i, 0, n)),
            pl.BlockSpec((None, 1, tn), lambda i, n: (i, 0, n)),
        ],
        out_specs=pl.BlockSpec((None, COND_ROWS, tn), lambda i, n: (i, 0, n)),
        out_shape=jax.ShapeDtypeStruct((DEPTH, COND_ROWS, n_out), F32),
        **_call_options("ada_mod"),
    )(cond, ada_w, ada_b.reshape(DEPTH, 1, n_out))


def _gmlp_head_kernel(x_ref, mod_ref, ng_ref, lng_ref, ws_ref, bs_ref, *refs, n_cast):
    width = 2 * MXU_COLS
    n_half = A_WIDTH // width
    win = refs[:2 * n_half]
    cast_src = refs[2 * n_half:2 * n_half + n_cast]
    z_ref = refs[2 * n_half + n_cast]
    cast_dst = refs[2 * n_half + n_cast + 1:2 * n_half + 2 * n_cast + 1]
    h_s, u_s, v_s, vb_s, sum_s, sq_s = refs[2 * n_half + 2 * n_cast + 1:]
    for src, dst in zip(cast_src, cast_dst):
        dst[...] = src[...].astype(BF16)
    tm = x_ref.shape[0]
    lng = lng_ref[...]
    half = tm // 2
    lanes = sum_s.shape[1]

    def proj(unit):
        c, a = unit
        return _dot(h_s[a * half:(a + 1) * half, :], win[c][...])

    def gelu_store(d, unit):
        c, a = unit
        is_v = c >= n_half
        dst, c = (v_s, c - n_half) if is_v else (u_s, c)
        for r in range(0, half, ROW_BLOCK):
            rows = slice(a * half + r, a * half + r + ROW_BLOCK)
            g = _gelu_tanh(d[r:r + ROW_BLOCK, :])
            dst[rows, c * width:(c + 1) * width] = g
            if is_v:
                parts = [g[:, i:i + lanes] for i in range(0, width, lanes)]
                p1 = sum(parts[1:], parts[0])
                p2 = sum([p * p for p in parts[1:]], parts[0] * parts[0])
                if c == 0:
                    sum_s[rows, :] = p1
                    sq_s[rows, :] = p2
                else:
                    sum_s[rows, :] += p1
                    sq_s[rows, :] += p2

    def layer_norm_rows(a, part):
        start = a * half + part * half // n_half
        for r in range(start, start + half // n_half, ROW_BLOCK // 2):
            rows = slice(r, r + ROW_BLOCK // 2)
            mu = jnp.sum(sum_s[rows, :], axis=-1, keepdims=True) * (1.0 / A_WIDTH)
            ex2 = jnp.sum(sq_s[rows, :], axis=-1, keepdims=True) * (1.0 / A_WIDTH)
            rstd = lax.rsqrt(jnp.maximum(ex2 - mu * mu, 0.0) + LN_EPS)
            vb_s[rows, :] = ((v_s[rows, :] - mu) * rstd * lng).astype(BF16)

    def gate_group(g):
        w = ws_ref[g]
        b = jnp.broadcast_to(bs_ref[g], (CHUNK, A_GROUP_W))
        cols = slice(g * A_GROUP_W, (g + 1) * A_GROUP_W)
        for n in range(tm // CHUNK):
            rows = slice(n * CHUNK, (n + 1) * CHUNK)
            sv = _dot(w, vb_s[rows, cols]) + b
            z_ref[rows, cols] = (u_s[rows, cols] * sv).astype(BF16)

    chunks = list(range(n_half, 2 * n_half)) + list(range(n_half))
    order = [(c, a) for c in chunks for a in range(2)]
    norms =[functools.partial(layer_norm_rows, a, part) for a in range(2) for part in range(n_half)]
    fillers = {2 * n_half + 1 + k: norms[k:k + 2] for k in range(0, len(norms), 2)}

    _modulated_norm_to(h_s, x_ref, mod_ref, ng_ref, 0, half)
    d = proj(order[0])
    _modulated_norm_to(h_s, x_ref, mod_ref, ng_ref, half, tm)
    for i in range(1, len(order) + 1):
        d_next = proj(order[i]) if i < len(order) else None
        gelu_store(d, order[i - 1])
        for filler in fillers.get(i - 1, []):
            filler()
        d = d_next
    for g in range(A_GROUPS):
        gate_group(g)


def _gmlp_head(x, mods, ng, w_in, ln_g, w_s, b_s, ctx_row=None, cast_next=()):
    bsz, length, _ = x.shape
    tm = min(TOKEN_BLOCK, length)
    w_in_specs = _slab_specs(w_in, 2 * MXU_COLS)
    cast_in_specs, cast_out_specs, cast_out_shapes = _cast_specs(cast_next, bsz, length // tm)
    z, *casts = pl.pallas_call(
        functools.partial(_gmlp_head_kernel, n_cast=len(cast_next)),
        grid=(bsz, length // tm),
        in_specs=[
            pl.BlockSpec((None, tm, D_MODEL), lambda b, t: (b, t, 0)),
            _mod_spec(ctx_row),
            _const_spec((4, D_MODEL)),
            _const_spec((1, A_WIDTH)),
            _const_spec((A_GROUPS, CHUNK, CHUNK)),
            _const_spec((A_GROUPS, CHUNK, 1)),
        ] + w_in_specs + cast_in_specs,
        out_specs=[pl.BlockSpec((None, tm, A_WIDTH), lambda b, t: (b, t, 0))] + cast_out_specs,
        out_shape=[jax.ShapeDtypeStruct((bsz, length, A_WIDTH), BF16)] + cast_out_shapes,
        scratch_shapes=[
            pltpu.VMEM((tm, D_MODEL), BF16),
            pltpu.VMEM((tm, A_WIDTH), F32),
            pltpu.VMEM((tm, A_WIDTH), F32),
            pltpu.VMEM((tm, A_WIDTH), BF16),
            pltpu.VMEM((tm, LANES), F32),
            pltpu.VMEM((tm, LANES), F32),
        ],
        **_call_options("gmlp_head"),
    )(x, mods, ng, ln_g, w_s, b_s, *([w_in] * len(w_in_specs)), *[w_all for w_all, _ in cast_next])
    return z, casts


def _interleave(dots, fillers):
    per = -(-len(fillers) // len(dots))
    for i, dot_fn in enumerate(dots):
        dot_fn()
        for filler in fillers[i * per:(i + 1) * per]:
            filler()


def _tail_kernel(x_ref, z_ref, mod_ref, ng_ref, *refs, sub, z_mirrored, n_cast):
    n_out = D_MODEL // MXU_COLS
    n_ff = D_FF // (2 * MXU_COLS)
    n_w = 2 * n_out + n_ff
    wp, w1, w2 = refs[:n_out], refs[n_out:n_out + n_ff], refs[n_out + n_ff:n_w]
    cast_src = refs[n_w:n_w + n_cast]
    o_ref = refs[n_w + n_cast]
    cast_dst = refs[n_w + n_cast + 1:n_w + 2 * n_cast + 1]
    scratch = refs[n_w + 2 * n_cast + 1:]
    for src, dst in zip(cast_src, cast_dst):
        dst[...] = src[...].astype(BF16)
    n_sub = x_ref.shape[0] // sub
    weights = (mod_ref, ng_ref, wp, w1, w2)
    pending = []
    for i in range(n_sub):
        zi = i
        if z_mirrored and n_sub > 1:
            second_half = pl.program_id(1) >= pl.num_programs(1) // 2
            zi = jnp.where(second_half, n_sub - 1 - i, i)
        rows = pl.ds(i * sub, sub)
        z_rows = rows if isinstance(zi, int) else pl.ds(pl.multiple_of(zi * sub, sub), sub)
        pending = _tail_subtile(x_ref.at[rows, :], z_ref.at[z_rows, :], *weights, o_ref.at[rows, :],
                                *scratch, pending=pending, defer_last=i + 1 < n_sub)


def _tail_subtile(x_ref, z_ref, mod_ref, ng_ref, wp, w1, w2, o_ref, *scratch, pending, defer_last):
    half = x_ref.shape[0] // 2
    y_s, hm_s, hid_s = scratch[0:2], scratch[2:4], scratch[4:6]
    g1 = mod_ref[2:3, :]
    sh2 = mod_ref[3:4, :]
    gate1 = g1 * ng_ref[1:2, :]
    gain2 = ng_ref[2:3, :] * (1.0 + mod_ref[4:5, :])
    gate2 = mod_ref[5:6, :] * ng_ref[3:4, :]
    ff_cols = 2 * MXU_COLS
    out_chunks = list(range(D_MODEL // MXU_COLS))
    ff_chunks = list(range(D_FF // ff_cols))
    blocks = list(range(0, half, ROW_BLOCK))

    def post_dot(a, n):
        y_s[a][:, n * MXU_COLS:(n + 1) * MXU_COLS] = _dot(z_ref[a * half:(a + 1) * half, :], wp[n][...])

    def norm1(a, r):
        rows = slice(a * half + r, a * half + r + ROW_BLOCK)
        x1 = x_ref[rows, :] + _rms(y_s[a][r:r + ROW_BLOCK, :], gate1)
        o_ref[rows, :] = x1
        hm_s[a][r:r + ROW_BLOCK, :] = (_rms(x1, gain2) + sh2).astype(BF16)

    def up_dot(a, c):
        t = jnp.maximum(_dot(hm_s[a][...], w1[c][...]), 0.0)
        hid_s[a][:, c * ff_cols:(c + 1) * ff_cols] = (t * t).astype(BF16)

    def down_dot(a, n):
        y_s[a][:, n * MXU_COLS:(n + 1) * MXU_COLS] = _dot(hid_s[a][...], w2[n][...])

    def norm3(a, r):
        rows = slice(a * half + r, a * half + r + ROW_BLOCK)
        o_ref[rows, :] = o_ref[rows, :] + _rms(y_s[a][r:r + ROW_BLOCK, :], gate2)

    def each(fn, a, items):
        return [functools.partial(fn, a, item) for item in items]

    _interleave(each(post_dot, 0, out_chunks), list(pending))
    _interleave(each(post_dot, 1, out_chunks), each(norm1, 0, blocks))
    _interleave(each(up_dot, 0, ff_chunks), each(norm1, 1, blocks))
    _interleave(each(up_dot, 1, ff_chunks), [])
    _interleave(each(down_dot, 0, out_chunks), [])
    _interleave(each(down_dot, 1, out_chunks), each(norm3, 0, blocks))
    last = each(norm3, 1, blocks)
    if defer_last:
        return last
    _interleave(last, [])
    return []


def _tail(x, z, mods, ng, w_post, w1, w2, ctx_row=None, z_mirrored=False, cast_next=()):
    bsz, length, _ = x.shape
    kz = z.shape[-1]
    sub = min(TOKEN_TILE, length)
    tm = min(TOKEN_BLOCK, length)
    w_post, post_index = w_post if isinstance(w_post, tuple) else (w_post, None)
    wp_specs = _slab_specs(w_post, MXU_COLS, post_index)
    w1_specs = _slab_specs(w1, 2 * MXU_COLS)
    w2_specs = _slab_specs(w2, MXU_COLS)
    cast_in_specs, cast_out_specs, cast_out_shapes = _cast_specs(cast_next, bsz, length // tm)
    if z_mirrored:
        nt = length // 2 // tm
        z_spec = pl.BlockSpec(
            (None, None, tm, kz),
            lambda b, t: (b, t // nt, jnp.where(t < nt, t, 2 * nt - 1 - t), 0))
    else:
        z_spec = pl.BlockSpec((None, tm, kz), lambda b, t: (b, t, 0))
    out, *casts = pl.pallas_call(
        functools.partial(_tail_kernel, sub=sub, z_mirrored=z_mirrored, n_cast=len(cast_next)),
        grid=(bsz, length // tm),
        in_specs=[
            pl.BlockSpec((None, tm, D_MODEL), lambda b, t: (b, t, 0)),
            z_spec,
            _mod_spec(ctx_row),
            _const_spec((4, D_MODEL)),
        ] + wp_specs + w1_specs + w2_specs + cast_in_specs,
        out_specs=[pl.BlockSpec((None, tm, D_MODEL), lambda b, t: (b, t, 0))] + cast_out_specs,
        out_shape=[jax.ShapeDtypeStruct((bsz, length, D_MODEL), F32)] + cast_out_shapes,
        scratch_shapes=(
            [pltpu.VMEM((sub // 2, D_MODEL), F32)] * 2
            + [pltpu.VMEM((sub // 2, D_MODEL), BF16)] * 2
            + [pltpu.VMEM((sub // 2, D_FF), BF16)] * 2),
        **_call_options("tail"),
    )(x, z, mods, ng, *([w_post] * len(wp_specs)), *([w1] * len(w1_specs)),
      *([w2] * len(w2_specs)), *[w_all for w_all, _ in cast_next])
    return out, casts


def _fourier_chan_kernel(x_ref, mod_ref, ng_ref, cs_ref, ab_ref, h_s):
    _modulated_norm_to(h_s, x_ref, mod_ref, ng_ref)
    for g in range(B_GROUPS):
        cols = slice(g * B_GROUP_W, (g + 1) * B_GROUP_W)
        t = _dot(h_s[:, cols], cs_ref[...])
        ab_ref[0, :, cols] = t[:, :B_GROUP_W].astype(BF16)
        ab_ref[1, :, cols] = t[:, B_GROUP_W:].astype(BF16)


def _fourier_seq_kernel(cs_ref, ab_ref, y_ref):
    for n in range(D_MODEL // MXU_COLS):
        cols = slice(n * MXU_COLS, (n + 1) * MXU_COLS)
        y_ref[:, cols] = _dot(cs_ref[...], ab_ref[:, cols]).astype(BF16)


def _bf16_const(a):
    return jnp.asarray(np.asarray(a, np.float32)).astype(BF16)


def _dft_angles(n):
    idx = np.arange(n, dtype=np.int64)
    return (2.0 * np.pi / n) * ((idx[:, None] * idx[None, :]) % n)


def _dft_tables(length):
    ang_c = _dft_angles(B_GROUP_W)
    cs_chan = np.concatenate([np.cos(ang_c), np.sin(ang_c)], axis=1) * B_GROUP_W ** -0.5
    ang_l = _dft_angles(length)
    cs_seq = np.concatenate([np.cos(ang_l), -np.sin(ang_l)], axis=1) * length ** -0.5
    return _bf16_const(cs_chan), _bf16_const(cs_seq)


def _fourier_fold_kernel(xp_ref, xm_ref, x0_ref, mod_ref, ng_ref, cc_ref, sc_ref, rev_ref,
                         ab_ref, a0_ref, hp_s, hm_s):
    tm = rev_ref.shape[0]
    n_tiles = xm_ref.shape[0] // tm
    sh1 = mod_ref[0:1, :]
    gain = ng_ref[0:1, :] * (1.0 + mod_ref[1:2, :])

    def norm_block(i, r):
        rows = slice(i * tm + r, i * tm + r + ROW_BLOCK)
        hp_s[rows, :] = _rms(xp_ref[0, i * tm + r + 1:i * tm + r + 1 + ROW_BLOCK, :], gain) + sh1
        mirror = (n_tiles - 1 - i) * tm + r
        hm_s[rows, :] = (_rms(xm_ref[mirror:mirror + ROW_BLOCK, :], gain) + sh1).astype(BF16)

    blocks = list(range(0, tm, ROW_BLOCK))
    for r in blocks:
        norm_block(0, r)
    h0 = (_rms(x0_ref[...], gain) + sh1).astype(BF16)
    per_group = -(-len(blocks) // B_GROUPS)
    for i in range(n_tiles):
        rows = slice(i * tm, (i + 1) * tm)
        for g in range(B_GROUPS):
            cols = slice(g * B_GROUP_W, (g + 1) * B_GROUP_W)
            hm_rev = _dot(rev_ref[...], hm_s[rows, cols])
            hp = hp_s[rows, cols]
            ab_ref[0, rows, cols] = _dot((hp + hm_rev).astype(BF16), cc_ref[...]).astype(BF16)
            ab_ref[1, rows, cols] = _dot((hp - hm_rev).astype(BF16), sc_ref[...]).astype(BF16)
            if i + 1 < n_tiles:
                for r in blocks[g * per_group:(g + 1) * per_group]:
                    norm_block(i + 1, r)
    for g in range(B_GROUPS):
        cols = slice(g * B_GROUP_W, (g + 1) * B_GROUP_W)
        a0_ref[:, cols] = _dot(h0[:, cols], cc_ref[...])


def _fourier_seq_fold_kernel(ct_ref, st_ref, ab_ref, a0_ref, perm_ref, y_ref, w_s, *, scale):
    tk = y_ref.shape[1]
    half = ct_ref.shape[1]
    for n in range(D_MODEL // MXU_COLS):
        cols = slice(n * MXU_COLS, (n + 1) * MXU_COLS)
        yc = _dot(ct_ref[...], ab_ref[0:half, cols]) + a0_ref[0:1, cols] * scale
        ys = _dot(st_ref[...], ab_ref[half:2 * half, cols])
        y_ref[0, :, cols] = (yc[:tk] - ys[:tk]).astype(BF16)
        w_s[:, cols] = (yc + ys).astype(BF16)
    first_row = (lax.broadcasted_iota(jnp.int32, (tk, 1), 0) == 0).astype(F32)
    for n in range(D_MODEL // MXU_COLS):
        cols = slice(n * MXU_COLS, (n + 1) * MXU_COLS)
        mirrored = _dot(perm_ref[...], w_s[0:tk, cols]) + first_row * w_s[tk:tk + 1, cols].astype(F32)
        y_ref[1, :, cols] = mirrored.astype(BF16)


def _seq_fold_tables(length, tk):
    half = length // 2
    nt = half // tk
    rows = tk + BF16_ROWS
    t = np.arange(1, half + 1, dtype=np.int64)
    k = np.arange(nt, dtype=np.int64)[:, None] * tk + np.arange(rows, dtype=np.int64)[None, :]
    ang = (2.0 * np.pi / length) * ((k[..., None] * t) % length)
    weight = np.where(t == half, 0.5, 1.0) * length ** -0.5
    return _bf16_const(np.cos(ang) * weight), _bf16_const(np.sin(ang) * weight)


def _fourier_head_folded(x, mods, ng):
    bsz, length, _ = x.shape
    tm = TOKEN_TILE
    half = length // 2
    nt = half // tm
    rows = tm + BF16_ROWS
    ang_c = _dft_angles(B_GROUP_W)
    cc = _bf16_const(np.cos(ang_c) * B_GROUP_W ** -0.5)
    sc = _bf16_const(np.sin(ang_c) * B_GROUP_W ** -0.5)
    rev = _bf16_const(np.arange(tm)[:, None] + np.arange(tm)[None, :] == tm - 1)
    fold_rows = TOKEN_BLOCK
    n_fold = half // fold_rows
    ab, a0 = pl.pallas_call(
        _fourier_fold_kernel,
        grid=(bsz, n_fold),
        in_specs=[
            pl.BlockSpec((pl.Element(1), pl.Element(fold_rows + SUBLANES), pl.Element(D_MODEL)),
                         lambda b, t: (b, t * fold_rows, 0)),
            pl.BlockSpec((None, fold_rows, D_MODEL), lambda b, t: (b, 2 * n_fold - 1 - t, 0)),
            pl.BlockSpec((None, SUBLANES, D_MODEL), lambda b, t: (b, 0, 0)),
            _mod_spec(None),
            _const_spec((4, D_MODEL)),
            _const_spec((B_GROUP_W, B_GROUP_W)),
            _const_spec((B_GROUP_W, B_GROUP_W)),
            _const_spec((tm, tm)),
        ],
        out_specs=[
            pl.BlockSpec((None, 2, fold_rows, D_MODEL), lambda b, t: (b, 0, t, 0)),
            pl.BlockSpec((None, SUBLANES, D_MODEL), lambda b, t: (b, 0, 0)),
        ],
        out_shape=[
            jax.ShapeDtypeStruct((bsz, 2, half, D_MODEL), BF16),
            jax.ShapeDtypeStruct((bsz, SUBLANES, D_MODEL), F32),
        ],
        scratch_shapes=[pltpu.VMEM((fold_rows, D_MODEL), F32), pltpu.VMEM((fold_rows, D_MODEL), BF16)],
        **_call_options("fourier_fold"),
    )(x, x, x, mods, ng, cc, sc, rev)
    ct, st = _seq_fold_tables(length, tm)
    perm = _bf16_const(np.arange(tm)[:, None] + np.arange(tm)[None, :] == tm)
    return pl.pallas_call(
        functools.partial(_fourier_seq_fold_kernel, scale=length ** -0.5),
        grid=(bsz, nt),
        in_specs=[
            pl.BlockSpec((None, rows, half), lambda b, t: (t, 0, 0)),
            pl.BlockSpec((None, rows, half), lambda b, t: (t, 0, 0)),
            pl.BlockSpec((None, 2 * half, D_MODEL), lambda b, t: (b, 0, 0)),
            pl.BlockSpec((None, SUBLANES, D_MODEL), lambda b, t: (b, 0, 0)),
            _const_spec((tm, tm)),
        ],
        out_specs=pl.BlockSpec((None, 2, tm, D_MODEL), lambda b, t: (b, 0, t, 0)),
        out_shape=jax.ShapeDtypeStruct((bsz, 2, half, D_MODEL), BF16),
        scratch_shapes=[pltpu.VMEM((rows, D_MODEL), BF16)],
        **_call_options("fourier_seq_fold"),
    )(ct, st, ab.reshape(bsz, 2 * half, D_MODEL), a0, perm)


def _fourier_head(x, mods, ng, ctx_row):
    bsz, length, _ = x.shape
    tm = min(TOKEN_TILE, length)
    cs_chan, cs_seq = _dft_tables(length)
    ab = pl.pallas_call(
        _fourier_chan_kernel,
        grid=(bsz, length // tm),
        in_specs=[
            pl.BlockSpec((None, tm, D_MODEL), lambda b, t: (b, t, 0)),
            _mod_spec(ctx_row),
            _const_spec((4, D_MODEL)),
            _const_spec((B_GROUP_W, 2 * B_GROUP_W)),
        ],
        out_specs=pl.BlockSpec((None, 2, tm, D_MODEL), lambda b, t: (b, 0, t, 0)),
        out_shape=jax.ShapeDtypeStruct((bsz, 2, length, D_MODEL), BF16),
        scratch_shapes=[pltpu.VMEM((tm, D_MODEL), BF16)],
        **_call_options("fourier_chan"),
    )(x, mods, ng, cs_chan)
    ab = ab.reshape(bsz, 2 * length, D_MODEL)
    return pl.pallas_call(
        _fourier_seq_kernel,
        grid=(bsz, length // tm),
        in_specs=[
            pl.BlockSpec((tm, 2 * length), lambda b, t: (t, 0)),
            pl.BlockSpec((None, 2 * length, D_MODEL), lambda b, t: (b, 0, 0),
                         pipeline_mode=pl.Buffered(1)),
        ],
        out_specs=pl.BlockSpec((None, tm, D_MODEL), lambda b, t: (b, t, 0)),
        out_shape=jax.ShapeDtypeStruct((bsz, length, D_MODEL), BF16),
        **_call_options("fourier_seq"),
    )(cs_seq, ab)


def _head_rms(r, g):
    return r * lax.rsqrt(jnp.mean(r * r, axis=-1, keepdims=True) + NORM_EPS) * g


def _store_vt_ext(vt_ref, t, cols=slice(None)):
    tokens = t.shape[0]
    first_row = lax.broadcasted_iota(jnp.int32, (BF16_ROWS, tokens), 0) == 0
    for e in range(N_KV_HEADS):
        vt_ref[e, 0:HEAD_DIM, cols] = t[:, e * HEAD_DIM:(e + 1) * HEAD_DIM].T.astype(BF16)
        vt_ref[e, HEAD_DIM:HEAD_DIM + BF16_ROWS, cols] = first_row.astype(BF16)


def _qkv_kernel(x_ref, mod_ref, ng_ref, w_ref, qg_ref, kg_ref, mean_ref, cos_ref, sin_ref,
                q_ref, k_ref, vt_ref, h_s):
    sub = min(TOKEN_TILE, x_ref.shape[0])
    n_sub = x_ref.shape[0] // sub
    qg = qg_ref[...] * (HEAD_DIM ** -0.5 * LOG2_E)
    gains = [jnp.concatenate([qg, qg], axis=1)] * (N_HEADS // 2)
    gains.append(jnp.concatenate([kg_ref[...], kg_ref[...]], axis=1))
    outs = [(q_ref, 2 * p) for p in range(N_HEADS // 2)] + [(k_ref, 0)]
    n_pairs = len(outs)

    def normalise(t, p):
        sq = t * t
        hi = sq.astype(BF16)
        lo = (sq - hi.astype(F32)).astype(BF16)
        ms = _dot(hi, mean_ref[...]) + _dot(lo, mean_ref[...])
        return t * lax.rsqrt(ms + NORM_EPS) * gains[p]

    _modulated_norm_to(h_s, x_ref, mod_ref, ng_ref, 0, sub)
    for i in range(n_sub):
        rows = slice(i * sub, (i + 1) * sub)
        cosf = cos_ref[rows, :]
        sinf = sin_ref[rows, :]
        next_blocks = list(range((i + 1) * sub, (i + 2) * sub, ROW_BLOCK)) if i + 1 < n_sub else []
        per_pair = -(-len(next_blocks) // (n_pairs + 1))

        def project(p):
            return _dot(h_s[rows, :], w_ref[:, p * MXU_COLS:(p + 1) * MXU_COLS])

        def rope_store(rn, p):
            ref, first = outs[p]
            for e in range(2):
                r = rn[:, e * HEAD_DIM:(e + 1) * HEAD_DIM]
                ref[first + e, rows, :] = (r * cosf + pltpu.roll(r, HEAD_DIM // 2, 1) * sinf).astype(BF16)

        t_next = project(0)
        rn_prev = None
        for p in range(n_pairs + 1):
            t = t_next
            if p < n_pairs:
                t_next = project(p + 1)
            rn = normalise(t, p) if p < n_pairs else None
            if rn_prev is not None:
                rope_store(rn_prev, p - 1)
            rn_prev = rn
            for r in next_blocks[p * per_pair:(p + 1) * per_pair]:
                _modulated_norm_to(h_s, x_ref, mod_ref, ng_ref, r, r + ROW_BLOCK)
        _store_vt_ext(vt_ref, t, rows)


def _kv_ctx_kernel(x_ref, mod_ref, ng_ref, w_ref, kg_ref, k_ref, vt_ref, h_s):
    _modulated_norm_to(h_s, x_ref, mod_ref, ng_ref)
    t = _dot(h_s[...], w_ref[:, 0:MXU_COLS])
    for e in range(N_KV_HEADS):
        k_ref[e] = _head_rms(t[:, e * HEAD_DIM:(e + 1) * HEAD_DIM], kg_ref[...]).astype(BF16)
    _store_vt_ext(vt_ref, _dot(h_s[...], w_ref[:, MXU_COLS:2 * MXU_COLS]))


def _rope_tables(length):
    t = np.arange(length)
    row = (t // GRID_W).astype(np.float64)
    col = (t % GRID_W).astype(np.float64)
    n_freq = HEAD_DIM // 4
    inv = ROPE_THETA ** (-np.arange(n_freq, dtype=np.float64) / n_freq)
    ang = np.concatenate([row[:, None] * inv, col[:, None] * inv], axis=-1)
    cos, sin = np.cos(ang), np.sin(ang)
    return (jnp.asarray(np.concatenate([cos, cos], axis=-1), F32),
            jnp.asarray(np.concatenate([-sin, sin], axis=-1), F32))


def _qkv_latent(x, mods, ng, w_qkv, q_g, k_g):
    bsz, length, _ = x.shape
    tm = TOKEN_BLOCK
    cosf, sinf = _rope_tables(length)
    n_qkv = w_qkv.shape[-1]
    lane_head = np.arange(MXU_COLS) // HEAD_DIM
    head_mean = _bf16_const((lane_head[:, None] == lane_head[None, :]) * (1.0 / HEAD_DIM))
    kv_shape = jax.ShapeDtypeStruct((bsz, N_KV_HEADS, length, HEAD_DIM), BF16)
    kv_spec = pl.BlockSpec((None, N_KV_HEADS, tm, HEAD_DIM), lambda b, t: (b, 0, t, 0))
    vt_shape = jax.ShapeDtypeStruct((bsz, N_KV_HEADS, HEAD_DIM + BF16_ROWS, length), BF16)
    vt_spec = pl.BlockSpec((None, N_KV_HEADS, HEAD_DIM + BF16_ROWS, tm), lambda b, t: (b, 0, 0, t))
    return pl.pallas_call(
        _qkv_kernel,
        grid=(bsz, length // tm),
        in_specs=[
            pl.BlockSpec((None, tm, D_MODEL), lambda b, t: (b, t, 0)),
            _mod_spec(None),
            _const_spec((4, D_MODEL)),
            _const_spec((D_MODEL, n_qkv)),
            _const_spec((1, HEAD_DIM)),
            _const_spec((1, HEAD_DIM)),
            _const_spec((MXU_COLS, MXU_COLS)),
            pl.BlockSpec((tm, HEAD_DIM), lambda b, t: (t, 0)),
            pl.BlockSpec((tm, HEAD_DIM), lambda b, t: (t, 0)),
        ],
        out_specs=[
            pl.BlockSpec((None, N_HEADS, tm, HEAD_DIM), lambda b, t: (b, 0, t, 0)),
            kv_spec, vt_spec,
        ],
        out_shape=[
            jax.ShapeDtypeStruct((bsz, N_HEADS, length, HEAD_DIM), BF16),
            kv_shape, vt_shape,
        ],
        scratch_shapes=[pltpu.VMEM((tm, D_MODEL), BF16)],
        **_call_options("qkv_latent"),
    )(x, mods, ng, w_qkv, q_g, k_g, head_mean, cosf, sinf)


def _kv_ctx(ctx, mods, ng, w_kv, k_g):
    bsz, length, _ = ctx.shape
    tm = length
    kv_shape = jax.ShapeDtypeStruct((bsz, N_KV_HEADS, length, HEAD_DIM), BF16)
    kv_spec = pl.BlockSpec((None, N_KV_HEADS, tm, HEAD_DIM), lambda b, t: (b, 0, t, 0))
    vt_shape = jax.ShapeDtypeStruct((bsz, N_KV_HEADS, HEAD_DIM + BF16_ROWS, length), BF16)
    vt_spec = pl.BlockSpec((None, N_KV_HEADS, HEAD_DIM + BF16_ROWS, tm), lambda b, t: (b, 0, 0, t))
    return pl.pallas_call(
        _kv_ctx_kernel,
        grid=(bsz, length // tm),
        in_specs=[
            pl.BlockSpec((None, tm, D_MODEL), lambda b, t: (b, t, 0)),
            _mod_spec(bsz),
            _const_spec((4, D_MODEL)),
            _const_spec((D_MODEL, 2 * MXU_COLS)),
            _const_spec((1, HEAD_DIM)),
        ],
        out_specs=[kv_spec, vt_spec],
        out_shape=[kv_shape, vt_shape],
        scratch_shapes=[pltpu.VMEM((tm, D_MODEL), BF16)],
        **_call_options("kv_ctx"),
    )(ctx, mods, ng, w_kv, k_g)


def _region(index, fn):
    pl.when(pl.program_id(0) >= -index)(fn)


def _attn_kernel(qa_ref, qb_ref, ka_ref, kca_ref, kb_ref, kcb_ref, vt_ref, vtc_ref, o_ref, *scratch):
    tq = qa_ref.shape[1] // 2
    n_latent = ka_ref.shape[1]
    n_keys = n_latent + kca_ref.shape[1]
    nq = HEADS_PER_KV * tq
    q4_s, s_s, p_s, m_s = (scratch[i * N_KV_HEADS:(i + 1) * N_KV_HEADS] for i in range(4))

    def chunks_of(k_ref, kc_ref):
        chunks = [(k_ref, vt_ref, slice(r, r + MXU_COLS), slice(r, r + MXU_COLS))
                  for r in range(0, n_latent, MXU_COLS)]
        chunks += [(kc_ref, vtc_ref, slice(r, r + MXU_COLS),
                    slice(n_latent + r, n_latent + r + MXU_COLS))
                   for r in range(0, n_keys - n_latent, MXU_COLS)]
        return chunks

    @pl.when(pl.program_id(0) == 0)
    def _():
        s_s[1][...] = jnp.zeros(s_s[1].shape, F32)
        m_s[1][...] = jnp.zeros(m_s[1].shape, F32)
        p_s[0][...] = jnp.ones(p_s[0].shape, BF16)

    def stage(j_scores, j_probs, j_pv, q_ref, q_rows, key_chunks, out_rows):
        for h in range(HEADS_PER_KV):
            q4_s[j_scores][h * tq:(h + 1) * tq, :] = q_ref[HEADS_PER_KV * j_scores + h, q_rows, :]
        m_probs = m_s[j_probs][...]
        m = None
        acc = None

        def store_head(acc, h):
            cols = slice(h * tq, (h + 1) * tq)
            o = acc[:HEAD_DIM, cols] / acc[HEAD_DIM:HEAD_DIM + 1, cols]
            head = HEADS_PER_KV * j_pv + h
            o_ref[out_rows, head * HEAD_DIM:(head + 1) * HEAD_DIM] = o.T.astype(BF16)

        pv_chunks = list(key_chunks)
        heads_left = list(range(HEADS_PER_KV))
        tiles = [slice(r, r + BF16_ROWS) for r in range(0, n_keys, BF16_ROWS)]
        pv_rows_per_score_row = vt_ref.shape[1] / MXU_COLS
        work_total = len(key_chunks) * (1.0 + pv_rows_per_score_row)
        work_done = 0.0
        tiles_done = 0
        for k_src, _, local, rows in key_chunks:
            s = lax.dot_general(k_src[j_scores, local, :], q4_s[j_scores][...],
                                (((1,), (1,)), ((), ())), preferred_element_type=F32)
            s_s[j_scores][rows, :] = s
            cm = jnp.max(s, axis=0, keepdims=True)
            m = cm if m is None else jnp.maximum(m, cm)
            work_done += 1.0
            if pv_chunks:
                for _, vt_src, pv_local, pv_rows in (
                        pv_chunks.pop(0) for _ in range(min(2, len(pv_chunks)))):
                    d = _dot(vt_src[j_pv, :, pv_local], p_s[j_pv][pv_rows, :])
                    acc = d if acc is None else acc + d
                    work_done += pv_rows_per_score_row
            elif heads_left:
                store_head(acc, heads_left.pop(0))
            tiles_until = round(len(tiles) * work_done / work_total)
            for tile in tiles[tiles_done:tiles_until]:
                p_s[j_probs][tile, :] = jnp.exp2(s_s[j_probs][tile, :] - m_probs).astype(BF16)
            tiles_done = tiles_until
        assert not pv_chunks and tiles_done == len(tiles)
        for h in heads_left:
            store_head(acc, h)
        m_s[j_scores][...] = jnp.broadcast_to(m, (BF16_ROWS, nq))

    first, second = slice(0, tq), slice(tq, 2 * tq)
    chunks_a, chunks_b = chunks_of(ka_ref, kca_ref), chunks_of(kb_ref, kcb_ref)
    _region(0, lambda: stage(0, 1, 0, qa_ref, second, chunks_a, first))
    _region(1, lambda: stage(1, 0, 1, qa_ref, second, chunks_a, first))
    _region(2, lambda: stage(0, 1, 0, qb_ref, first, chunks_b, second))
    _region(3, lambda: stage(1, 0, 1, qb_ref, first, chunks_b, second))


def _attention(q, k, kc, vt, vtc):
    bsz, _, length, _ = q.shape
    n_ctx = kc.shape[2]
    n_keys = length + n_ctx
    vt_rows = vt.shape[2]
    tq = QUERY_TILE
    nq = HEADS_PER_KV * tq
    per_batch = length // (2 * tq)
    n_blocks = bsz * per_batch

    def cur(s):
        return jnp.minimum(s, n_blocks - 1)

    def prev(s):
        return jnp.maximum(s - 1, 0)

    def q_spec(block):
        return pl.BlockSpec((None, N_HEADS, 2 * tq, HEAD_DIM),
                            lambda s: (block(s) // per_batch, 0, block(s) % per_batch, 0))

    def kv_spec(rows, cols, block):
        return pl.BlockSpec((None, N_KV_HEADS, rows, cols), lambda s: (block(s) // per_batch, 0, 0, 0))

    return pl.pallas_call(
        _attn_kernel,
        grid=(n_blocks + 1,),
        in_specs=[
            q_spec(prev), q_spec(cur),
            kv_spec(length, HEAD_DIM, prev), kv_spec(n_ctx, HEAD_DIM, prev),
            kv_spec(length, HEAD_DIM, cur), kv_spec(n_ctx, HEAD_DIM, cur),
            kv_spec(vt_rows, length, prev), kv_spec(vt_rows, n_ctx, prev),
        ],
        out_specs=pl.BlockSpec((None, 2 * tq, D_MODEL),
                               lambda s: (prev(s) // per_batch, prev(s) % per_batch, 0)),
        out_shape=jax.ShapeDtypeStruct((bsz, length, D_MODEL), BF16),
        scratch_shapes=(
            [pltpu.VMEM((nq, HEAD_DIM), BF16)] * N_KV_HEADS
            + [pltpu.VMEM((n_keys, nq), F32)] * N_KV_HEADS
            + [pltpu.VMEM((n_keys, nq), BF16)] * N_KV_HEADS
            + [pltpu.VMEM((BF16_ROWS, nq), F32)] * N_KV_HEADS),
        **_call_options("attention", grid_rank=1),
    )(q, q, k, kc, k, kc, vt, vtc)


def kernel(x, c, ctx, c_ctx, ada_w, ada_b, norm_g, mlp_w1, mlp_w2, a_w_in, a_ln_g, a_w_s, a_b_s, a_w_out,
           b_w_out, c_w_qkv, c_q_g, c_k_g, c_w_o):
    bsz = x.shape[0]
    attn_layers = [i for i in range(DEPTH) if i % N_MIXERS == 2]
    last_ctx_read = attn_layers[-1] if attn_layers else -1

    cond = jnp.zeros((COND_ROWS, D_MODEL), F32).at[:bsz].set(c).at[bsz].set(c_ctx)
    mods_all = _ada_all(cond, ada_w, ada_b).reshape(DEPTH, COND_ROWS, 6, D_MODEL)

    def big_weights(i):
        pairs = [(mlp_w1, i), (mlp_w2, i)]
        if i % N_MIXERS == 0:
            pairs += [(a_w_out, i // N_MIXERS), (a_w_in, i // N_MIXERS)]
        return pairs

    weights = None

    def ctx_tiles(a):
        return a.reshape(-1, TOKEN_TILE, a.shape[-1])

    for i in range(DEPTH):
        kind, j = i % N_MIXERS, i // N_MIXERS
        ctx_in = i <= last_ctx_read
        ctx_out = i < last_ctx_read
        mods = mods_all[i]
        ng = norm_g[i]
        assert weights is not None or kind == 0
        if weights is not None:
            w1, w2 = weights[:2]

        if kind == 0:
            ln_g = a_ln_g[j].reshape(1, A_WIDTH)
            w_s = a_w_s[j].astype(BF16)
            b_s = a_b_s[j].reshape(A_GROUPS, CHUNK, 1)
            if weights is None:
                w_in = a_w_in[j].astype(BF16)
                z, (w1, w2, w_post) = _gmlp_head(x, mods, ng, w_in, ln_g, w_s, b_s,
                                                 cast_next=big_weights(i)[:3])
            else:
                w_post, w_in = weights[2:]
                z, _ = _gmlp_head(x, mods, ng, w_in, ln_g, w_s, b_s)
            zc = None
            if ctx_out:
                zc, _ = _gmlp_head(ctx_tiles(ctx), mods, ng, w_in, ln_g, w_s, b_s, ctx_row=bsz)
        elif kind == 1:
            w_post = b_w_out[j].astype(BF16)
            z = _fourier_head_folded(x, mods, ng)
            zc = ctx_tiles(_fourier_head(ctx, mods, ng, bsz)) if ctx_out else None
        else:
            w_qkv = c_w_qkv[j].astype(BF16)
            w_post = c_w_o[j].astype(BF16)
            q_g = c_q_g[j].reshape(1, HEAD_DIM)
            k_g = c_k_g[j].reshape(1, HEAD_DIM)
            assert ctx_in and not ctx_out
            q, k, vt = _qkv_latent(x, mods, ng, w_qkv, q_g, k_g)
            kc, vtc = _kv_ctx(ctx, mods, ng, w_qkv[:, N_HEADS * HEAD_DIM:], k_g)
            z = _attention(q, k, kc, vt, vtc)
            zc = None

        cast_next = big_weights(i + 1) if i + 1 < DEPTH else ()
        x, next_weights = _tail(x, z, mods, ng, w_post, w1, w2, z_mirrored=(kind == 1),
                                cast_next=cast_next)
        if ctx_out:
            new_ctx, _ = _tail(ctx_tiles(ctx), zc, mods, ng, w_post, w1, w2, ctx_row=bsz)
            ctx = new_ctx.reshape(ctx.shape)
        weights = next_weights

    return x
```
